```python
import math
import jax, jax.numpy as jnp
from jax import lax
import numpy as np

D_MODEL = 2048
BATCH = 8
SEQ = 8192
DEPTH = 4

MIX_WIDTH = D_MODEL
DN_HEADS = 8
DN_HEAD_DIM = 128
DN_WIDTH = DN_HEADS * DN_HEAD_DIM
DN_CHUNK = 64
SHORT_CONV = 4
POOL_WINDOWS = (2, 4, 8, 16)
POOL_GROUPS = len(POOL_WINDOWS)
POOL_WIDTH = MIX_WIDTH - DN_WIDTH
POOL_GROUP_DIM = POOL_WIDTH // POOL_GROUPS
EVEN_IN = 4 * DN_WIDTH + 2 * DN_HEADS + POOL_WIDTH
CONF_WIDTH = D_MODEL
CONF_WIN = 31
D_FF = 4 * D_MODEL
N_EVEN = (DEPTH + 1) // 2
N_ODD = DEPTH // 2
EPS = 1e-6

kernel_name = 'hybrid_deltanet_pool_conformer_trunk'


def rms_norm(x, g):
    xf = x.astype(jnp.float32)
    y = xf * lax.rsqrt(jnp.mean(xf * xf, axis=-1, keepdims=True) + EPS)
    return (y * g.astype(jnp.float32)).astype(x.dtype)


def layer_norm(x, g, b):
    xf = x.astype(jnp.float32)
    mu = jnp.mean(xf, axis=-1, keepdims=True)
    xc = xf - mu
    y = xc * lax.rsqrt(jnp.mean(xc * xc, axis=-1, keepdims=True) + EPS)
    return (y * g.astype(jnp.float32) + b.astype(jnp.float32)).astype(x.dtype)


def l2_normalize(x):
    return x * lax.rsqrt(jnp.sum(x * x, axis=-1, keepdims=True) + EPS)


def causal_depthwise_conv(x, w):
    K, C = w.shape
    return lax.conv_general_dilated(
        x, w[:, None, :].astype(x.dtype), window_strides=(1,), padding=[(K - 1, 0)],
        dimension_numbers=('NWC', 'WIO', 'NWC'), feature_group_count=C)


def gated_delta_rule(q, k, v, g, beta):
    B_, S_, H, Dk = q.shape
    Dv = v.shape[-1]
    C = DN_CHUNK
    N = S_ // C

    def to_chunks(t):
        t = t.reshape((B_, N, C, H) + t.shape[3:])
        return jnp.moveaxis(t, 3, 1)

    q, k, v, g, beta = map(to_chunks, (q * Dk ** -0.5, k, v, g, beta))
    gc = jnp.cumsum(g, axis=-1)
    pos = jnp.arange(C)
    causal = pos[:, None] >= pos[None, :]
    strict = pos[:, None] > pos[None, :]
    gamma = jnp.exp(jnp.where(causal, gc[..., :, None] - gc[..., None, :], -jnp.inf))
    kb = k * beta[..., None]
    a_mat = jnp.where(strict, jnp.einsum('bhnck,bhnmk->bhncm', kb, k) * gamma, 0.0)
    eye = jnp.eye(C, dtype=jnp.float32)
    t_inv = lax.linalg.triangular_solve(a_mat + eye, jnp.broadcast_to(eye, a_mat.shape),
                                        left_side=True, lower=True, unit_diagonal=True)
    u = jnp.einsum('bhncm,bhnmv->bhncv', t_inv, v * beta[..., None])
    w = jnp.einsum('bhncm,bhnmk->bhnck', t_inv, kb * jnp.exp(gc)[..., None])
    attn = jnp.where(causal, jnp.einsum('bhnck,bhnmk->bhncm', q, k) * gamma, 0.0)
    qg = q * jnp.exp(gc)[..., None]
    g_last = gc[..., -1]
    kd = k * jnp.exp(g_last[..., None] - gc)[..., None]
    decay = jnp.exp(g_last)

    def step(state, inp):
        u_n, w_n, attn_n, qg_n, kd_n, decay_n = inp
        v_new = u_n - jnp.einsum('bhck,bhkv->bhcv', w_n, state)
        o_n = (jnp.einsum('bhck,bhkv->bhcv', qg_n, state)
               + jnp.einsum('bhcm,bhmv->bhcv', attn_n, v_new))
        state = state * decay_n[..., None, None] + jnp.einsum('bhck,bhcv->bhkv', kd_n, v_new)
        return state, o_n

    xs = tuple(jnp.moveaxis(t, 2, 0) for t in (u, w, attn, qg, kd, decay))
    state0 = jnp.zeros((B_, H, Dk, Dv), jnp.float32)
    _, o = lax.scan(step, state0, xs)
    return jnp.transpose(o, (1, 0, 3, 2, 4)).reshape(B_, S_, H, Dv)


def multiscale_pool(x, w_grp, scale):
    B_, S_, _ = x.shape
    xf = x.astype(jnp.float32)
    cs = jnp.cumsum(xf, axis=1)
    count = jnp.arange(1, S_ + 1, dtype=jnp.float32)[:, None]
    outs = []
    for gi, win in enumerate(POOL_WINDOWS):
        sl = slice(gi * POOL_GROUP_DIM, (gi + 1) * POOL_GROUP_DIM)
        c = cs[..., sl]
        lower = jnp.pad(c, ((0, 0), (win, 0), (0, 0)))[:, :S_]
        outs.append((c - lower) / jnp.minimum(count, win) - xf[..., sl])
    pooled = jnp.concatenate(outs, axis=-1).astype(x.dtype).reshape(B_, S_, POOL_GROUPS, POOL_GROUP_DIM)
    y = jnp.einsum('bsgc,gcd->bsgd', pooled, w_grp).reshape(B_, S_, POOL_WIDTH)
    return y * scale


def even_mixer(h, w_in, conv_w, a_log, dt_bias, dn_norm, pool_w, pool_scale, w_out):
    B_, S_, _ = h.shape
    p = h @ w_in
    o1 = 3 * DN_WIDTH
    o2 = o1 + DN_WIDTH
    o3 = o2 + DN_HEADS
    o4 = o3 + DN_HEADS
    qkv = jax.nn.silu(causal_depthwise_conv(p[..., :o1], conv_w))
    z = p[..., o1:o2].reshape(B_, S_, DN_HEADS, DN_HEAD_DIM)
    b_logit = p[..., o2:o3].astype(jnp.float32)
    a_logit = p[..., o3:o4].astype(jnp.float32)
    xp = p[..., o4:]
    q, k, v = [t.reshape(B_, S_, DN_HEADS, DN_HEAD_DIM).astype(jnp.float32)
               for t in jnp.split(qkv, 3, axis=-1)]
    q = l2_normalize(q)
    k = l2_normalize(k)
    beta = jax.nn.sigmoid(b_logit)
    g = -jnp.exp(a_log.astype(jnp.float32)) * jax.nn.softplus(a_logit + dt_bias.astype(jnp.float32))
    o = gated_delta_rule(q, k, v, g, beta).astype(h.dtype)
    o = (rms_norm(o, dn_norm) * jax.nn.silu(z)).reshape(B_, S_, DN_WIDTH)
    y_pool = multiscale_pool(xp, pool_w, pool_scale)
    return jnp.concatenate([o, y_pool], axis=-1) @ w_out


def odd_mixer(h, w_in, dw_w, dw_b, ln_g, ln_b, w_out):
    a, gate = jnp.split(h @ w_in, 2, axis=-1)
    u = a * jax.nn.sigmoid(gate)
    u = causal_depthwise_conv(u, dw_w) + dw_b
    u = jax.nn.silu(layer_norm(u, ln_g, ln_b))
    return u @ w_out


def squared_relu_mlp(h, w_up, w_down):
    return jnp.square(jax.nn.relu(h @ w_up)) @ w_down


def _fwd_setup_inputs(seed: int = 0) -> dict:
    key = jax.random.key(seed)
    ks = jax.random.split(key, 24)
    f32 = jnp.float32

    def nrm(k, shape, fan_in):
        return jax.random.normal(k, shape, f32) * fan_in ** -0.5

    def gain(k, shape):
        return 1.0 + 0.05 * jax.random.normal(k, shape, f32)

    dt = jnp.exp(jax.random.uniform(ks[9], (N_EVEN, DN_HEADS), f32, math.log(1e-3), math.log(1e-1)))
    return {
        'x': jax.random.normal(ks[0], (BATCH, SEQ, D_MODEL), f32),
        'norm_mix_pre': gain(ks[1], (DEPTH, D_MODEL)),
        'norm_mix_post': gain(ks[2], (DEPTH, D_MODEL)),
        'norm_mlp_pre': gain(ks[3], (DEPTH, D_MODEL)),
        'norm_mlp_post': gain(ks[4], (DEPTH, D_MODEL)),
        'even_w_in': nrm(ks[5], (N_EVEN, D_MODEL, EVEN_IN), D_MODEL),
        'even_conv': nrm(ks[6], (N_EVEN, SHORT_CONV, 3 * DN_WIDTH), SHORT_CONV),
        'even_a_log': jnp.log(jax.random.uniform(ks[8], (N_EVEN, DN_HEADS), f32, 1.0, 16.0)),
        'even_dt_bias': dt + jnp.log(-jnp.expm1(-dt)),
        'even_dn_norm': gain(ks[10], (N_EVEN, DN_HEAD_DIM)),
        'even_pool_w': nrm(ks[11], (N_EVEN, POOL_GROUPS, POOL_GROUP_DIM, POOL_GROUP_DIM), POOL_GROUP_DIM),
        'even_pool_scale': gain(ks[12], (N_EVEN, POOL_WIDTH)),
        'even_w_out': nrm(ks[13], (N_EVEN, MIX_WIDTH, D_MODEL), MIX_WIDTH),
        'odd_w_in': nrm(ks[14], (N_ODD, D_MODEL, 2 * CONF_WIDTH), D_MODEL),
        'odd_dw': nrm(ks[15], (N_ODD, CONF_WIN, CONF_WIDTH), CONF_WIN),
        'odd_dw_b': 0.02 * jax.random.normal(ks[16], (N_ODD, CONF_WIDTH), f32),
        'odd_ln_g': gain(ks[17], (N_ODD, CONF_WIDTH)),
        'odd_ln_b': 0.02 * jax.random.normal(ks[18], (N_ODD, CONF_WIDTH), f32),
        'odd_w_out': nrm(ks[19], (N_ODD, CONF_WIDTH, D_MODEL), CONF_WIDTH),
        'mlp_w_up': nrm(ks[20], (DEPTH, D_MODEL, D_FF), D_MODEL),
        'mlp_w_down': nrm(ks[21], (DEPTH, D_FF, D_MODEL), D_FF),
    }


def _fwd_reference(x, norm_mix_pre, norm_mix_post, norm_mlp_pre, norm_mlp_post,
              even_w_in, even_conv, even_a_log, even_dt_bias, even_dn_norm,
              even_pool_w, even_pool_scale, even_w_out,
              odd_w_in, odd_dw, odd_dw_b, odd_ln_g, odd_ln_b, odd_w_out,
              mlp_w_up, mlp_w_down):
    for i in range(DEPTH):
        j = i // 2
        h = rms_norm(x, norm_mix_pre[i])
        if i % 2 == 0:
            mix = even_mixer(h, even_w_in[j], even_conv[j], even_a_log[j], even_dt_bias[j],
                             even_dn_norm[j], even_pool_w[j], even_pool_scale[j], even_w_out[j])
        else:
            mix = odd_mixer(h, odd_w_in[j], odd_dw[j], odd_dw_b[j], odd_ln_g[j], odd_ln_b[j], odd_w_out[j])
        x = x + rms_norm(mix, norm_mix_post[i])
        ff = squared_relu_mlp(rms_norm(x, norm_mlp_pre[i]), mlp_w_up[i], mlp_w_down[i])
        x = x + rms_norm(ff, norm_mlp_post[i])
    return x


import jax as _jax
import jax.numpy as _jnp

TWIN_FORMAT = 'train_step'
FWD_PARAMS = ['x', 'norm_mix_pre', 'norm_mix_post', 'norm_mlp_pre', 'norm_mlp_post', 'even_w_in', 'even_conv', 'even_a_log', 'even_dt_bias', 'even_dn_norm', 'even_pool_w', 'even_pool_scale', 'even_w_out', 'odd_w_in', 'odd_dw', 'odd_dw_b', 'odd_ln_g', 'odd_ln_b', 'odd_w_out', 'mlp_w_up', 'mlp_w_down']
TWIN_WEIGHTS = ['norm_mix_pre', 'norm_mix_post', 'norm_mlp_pre', 'norm_mlp_post', 'even_w_in', 'even_conv', 'even_a_log', 'even_dt_bias', 'even_dn_norm', 'even_pool_w', 'even_pool_scale', 'even_w_out', 'odd_w_in', 'odd_dw', 'odd_dw_b', 'odd_ln_g', 'odd_ln_b', 'odd_w_out', 'mlp_w_up', 'mlp_w_down']
TWIN_DIFF_INPUT = 'x'
TWIN_INPUTS = ['x', 'norm_mix_pre', 'norm_mix_post', 'norm_mlp_pre', 'norm_mlp_post', 'even_w_in', 'even_conv', 'even_a_log', 'even_dt_bias', 'even_dn_norm', 'even_pool_w', 'even_pool_scale', 'even_w_out', 'odd_w_in', 'odd_dw', 'odd_dw_b', 'odd_ln_g', 'odd_ln_b', 'odd_w_out', 'mlp_w_up', 'mlp_w_down', 'loss_target', 'm_norm_mix_pre', 'm_norm_mix_post', 'm_norm_mlp_pre', 'm_norm_mlp_post', 'm_even_w_in', 'm_even_conv', 'm_even_a_log', 'm_even_dt_bias', 'm_even_dn_norm', 'm_even_pool_w', 'm_even_pool_scale', 'm_even_w_out', 'm_odd_w_in', 'm_odd_dw', 'm_odd_dw_b', 'm_odd_ln_g', 'm_odd_ln_b', 'm_odd_w_out', 'm_mlp_w_up', 'm_mlp_w_down', 'v_norm_mix_pre', 'v_norm_mix_post', 'v_norm_mlp_pre', 'v_norm_mlp_post', 'v_even_w_in', 'v_even_conv', 'v_even_a_log', 'v_even_dt_bias', 'v_even_dn_norm', 'v_even_pool_w', 'v_even_pool_scale', 'v_even_w_out', 'v_odd_w_in', 'v_odd_dw', 'v_odd_dw_b', 'v_odd_ln_g', 'v_odd_ln_b', 'v_odd_w_out', 'v_mlp_w_up', 'v_mlp_w_down']
TWIN_OUTPUTS = ['loss', 'grad_x', 'grad_norm_mix_pre', 'grad_norm_mix_post', 'grad_norm_mlp_pre', 'grad_norm_mlp_post', 'grad_even_w_in', 'grad_even_conv', 'grad_even_a_log', 'grad_even_dt_bias', 'grad_even_dn_norm', 'grad_even_pool_w', 'grad_even_pool_scale', 'grad_even_w_out', 'grad_odd_w_in', 'grad_odd_dw', 'grad_odd_dw_b', 'grad_odd_ln_g', 'grad_odd_ln_b', 'grad_odd_w_out', 'grad_mlp_w_up', 'grad_mlp_w_down', 'delta_norm_mix_pre', 'delta_norm_mix_post', 'delta_norm_mlp_pre', 'delta_norm_mlp_post', 'delta_even_w_in', 'delta_even_conv', 'delta_even_a_log', 'delta_even_dt_bias', 'delta_even_dn_norm', 'delta_even_pool_w', 'delta_even_pool_scale', 'delta_even_w_out', 'delta_odd_w_in', 'delta_odd_dw', 'delta_odd_dw_b', 'delta_odd_ln_g', 'delta_odd_ln_b', 'delta_odd_w_out', 'delta_mlp_w_up', 'delta_mlp_w_down', 'new_m_norm_mix_pre', 'new_m_norm_mix_post', 'new_m_norm_mlp_pre', 'new_m_norm_mlp_post', 'new_m_even_w_in', 'new_m_even_conv', 'new_m_even_a_log', 'new_m_even_dt_bias', 'new_m_even_dn_norm', 'new_m_even_pool_w', 'new_m_even_pool_scale', 'new_m_even_w_out', 'new_m_odd_w_in', 'new_m_odd_dw', 'new_m_odd_dw_b', 'new_m_odd_ln_g', 'new_m_odd_ln_b', 'new_m_odd_w_out', 'new_m_mlp_w_up', 'new_m_mlp_w_down', 'new_v_norm_mix_pre', 'new_v_norm_mix_post', 'new_v_norm_mlp_pre', 'new_v_norm_mlp_post', 'new_v_even_w_in', 'new_v_even_conv', 'new_v_even_a_log', 'new_v_even_dt_bias', 'new_v_even_dn_norm', 'new_v_even_pool_w', 'new_v_even_pool_scale', 'new_v_even_w_out', 'new_v_odd_w_in', 'new_v_odd_dw', 'new_v_odd_dw_b', 'new_v_odd_ln_g', 'new_v_odd_ln_b', 'new_v_odd_w_out', 'new_v_mlp_w_up', 'new_v_mlp_w_down']
TWIN_LEAF_KINDS = {'loss': 'loss', 'grad_x': 'grad_x', 'grad_norm_mix_pre': 'grad_w', 'grad_norm_mix_post': 'grad_w', 'grad_norm_mlp_pre': 'grad_w', 'grad_norm_mlp_post': 'grad_w', 'grad_even_w_in': 'grad_w', 'grad_even_conv': 'grad_w', 'grad_even_a_log': 'grad_w', 'grad_even_dt_bias': 'grad_w', 'grad_even_dn_norm': 'grad_w', 'grad_even_pool_w': 'grad_w', 'grad_even_pool_scale': 'grad_w', 'grad_even_w_out': 'grad_w', 'grad_odd_w_in': 'grad_w', 'grad_odd_dw': 'grad_w', 'grad_odd_dw_b': 'grad_w', 'grad_odd_ln_g': 'grad_w', 'grad_odd_ln_b': 'grad_w', 'grad_odd_w_out': 'grad_w', 'grad_mlp_w_up': 'grad_w', 'grad_mlp_w_down': 'grad_w', 'delta_norm_mix_pre': 'delta_w', 'delta_norm_mix_post': 'delta_w', 'delta_norm_mlp_pre': 'delta_w', 'delta_norm_mlp_post': 'delta_w', 'delta_even_w_in': 'delta_w', 'delta_even_conv': 'delta_w', 'delta_even_a_log': 'delta_w', 'delta_even_dt_bias': 'delta_w', 'delta_even_dn_norm': 'delta_w', 'delta_even_pool_w': 'delta_w', 'delta_even_pool_scale': 'delta_w', 'delta_even_w_out': 'delta_w', 'delta_odd_w_in': 'delta_w', 'delta_odd_dw': 'delta_w', 'delta_odd_dw_b': 'delta_w', 'delta_odd_ln_g': 'delta_w', 'delta_odd_ln_b': 'delta_w', 'delta_odd_w_out': 'delta_w', 'delta_mlp_w_up': 'delta_w', 'delta_mlp_w_down': 'delta_w', 'new_m_norm_mix_pre': 'new_m', 'new_m_norm_mix_post': 'new_m', 'new_m_norm_mlp_pre': 'new_m', 'new_m_norm_mlp_post': 'new_m', 'new_m_even_w_in': 'new_m', 'new_m_even_conv': 'new_m', 'new_m_even_a_log': 'new_m', 'new_m_even_dt_bias': 'new_m', 'new_m_even_dn_norm': 'new_m', 'new_m_even_pool_w': 'new_m', 'new_m_even_pool_scale': 'new_m', 'new_m_even_w_out': 'new_m', 'new_m_odd_w_in': 'new_m', 'new_m_odd_dw': 'new_m', 'new_m_odd_dw_b': 'new_m', 'new_m_odd_ln_g': 'new_m', 'new_m_odd_ln_b': 'new_m', 'new_m_odd_w_out': 'new_m', 'new_m_mlp_w_up': 'new_m', 'new_m_mlp_w_down': 'new_m', 'new_v_norm_mix_pre': 'new_v', 'new_v_norm_mix_post': 'new_v', 'new_v_norm_mlp_pre': 'new_v', 'new_v_norm_mlp_post': 'new_v', 'new_v_even_w_in': 'new_v', 'new_v_even_conv': 'new_v', 'new_v_even_a_log': 'new_v', 'new_v_even_dt_bias': 'new_v', 'new_v_even_dn_norm': 'new_v', 'new_v_even_pool_w': 'new_v', 'new_v_even_pool_scale': 'new_v', 'new_v_even_w_out': 'new_v', 'new_v_odd_w_in': 'new_v', 'new_v_odd_dw': 'new_v', 'new_v_odd_dw_b': 'new_v', 'new_v_odd_ln_g': 'new_v', 'new_v_odd_ln_b': 'new_v', 'new_v_odd_w_out': 'new_v', 'new_v_mlp_w_up': 'new_v', 'new_v_mlp_w_down': 'new_v'}


def _forward(args):
    return _fwd_reference(*[args[k] for k in FWD_PARAMS])


def _output_shape():
    def fwd():
        inp = _fwd_setup_inputs(0)
        return _fwd_reference(*[inp[k] for k in FWD_PARAMS])
    out = _jax.eval_shape(fwd)
    return out.shape, out.dtype

N_MICROBATCH = 1
ADAM_LR = 0.001
ADAM_B1 = 0.9
ADAM_B2 = 0.999
ADAM_EPS = 1e-08
ADAM_WD = 0.01
ADAM_STEP = 10
PER_EXAMPLE_BATCH_AXIS = {'x': 0, 'loss_target': 0}
SHARED_INPUTS = []
_WEIGHT_DTYPES = {'norm_mix_pre': _jnp.float32, 'norm_mix_post': _jnp.float32, 'norm_mlp_pre': _jnp.float32, 'norm_mlp_post': _jnp.float32, 'even_w_in': _jnp.float32, 'even_conv': _jnp.float32, 'even_a_log': _jnp.float32, 'even_dt_bias': _jnp.float32, 'even_dn_norm': _jnp.float32, 'even_pool_w': _jnp.float32, 'even_pool_scale': _jnp.float32, 'even_w_out': _jnp.float32, 'odd_w_in': _jnp.float32, 'odd_dw': _jnp.float32, 'odd_dw_b': _jnp.float32, 'odd_ln_g': _jnp.float32, 'odd_ln_b': _jnp.float32, 'odd_w_out': _jnp.float32, 'mlp_w_up': _jnp.float32, 'mlp_w_down': _jnp.float32}
MOMENT_SCALE = {'norm_mix_pre': 6.539153e+00, 'norm_mix_post': 3.499110e+01, 'norm_mlp_pre': 6.865249e+00, 'norm_mlp_post': 3.785666e+01, 'even_w_in': 2.162879e+00, 'even_conv': 3.728640e+00, 'even_a_log': 8.698300e+00, 'even_dt_bias': 8.278886e+00, 'even_dn_norm': 2.564709e+01, 'even_pool_w': 1.858801e+00, 'even_pool_scale': 2.243570e+00, 'even_w_out': 6.156212e+00, 'odd_w_in': 6.133329e+00, 'odd_dw': 9.603461e+00, 'odd_dw_b': 5.489758e+01, 'odd_ln_g': 2.360315e+01, 'odd_ln_b': 3.175349e+01, 'odd_w_out': 1.604078e+01, 'mlp_w_up': 3.401508e+00, 'mlp_w_down': 1.718128e+01}


def _to_microbatches(a, axis):
    t = _jnp.moveaxis(a, axis, 0)
    t = t.reshape((N_MICROBATCH, t.shape[0] // N_MICROBATCH) + t.shape[1:])
    return _jnp.moveaxis(t, 1, axis + 1)


def setup_inputs(seed: int = 0) -> dict:
    inp = _fwd_setup_inputs(seed)
    key = _jax.random.fold_in(_jax.random.key(seed), 7919)
    shape, _ = _output_shape()
    out = dict(inp)
    out["loss_target"] = _jax.random.normal(_jax.random.fold_in(key, 0), shape, _jnp.float32)
    for i, name in enumerate(TWIN_WEIGHTS):
        w = inp[name].astype(_jnp.float32)
        if MOMENT_SCALE is None:
            s = _jnp.sqrt(_jnp.mean(_jnp.square(w)) + 1e-30)
        else:
            s = MOMENT_SCALE[name]
        km, kv = _jax.random.split(_jax.random.fold_in(key, i + 1))
        out[name] = w
        out["m_" + name] = s * _jax.random.normal(km, w.shape, _jnp.float32)
        out["v_" + name] = (s * s) * _jax.random.uniform(kv, w.shape, _jnp.float32, 0.5, 1.5)
    if N_MICROBATCH > 1:
        for name, axis in PER_EXAMPLE_BATCH_AXIS.items():
            out[name] = _to_microbatches(out[name], axis)
    return {'x': out['x'], 'norm_mix_pre': out['norm_mix_pre'], 'norm_mix_post': out['norm_mix_post'], 'norm_mlp_pre': out['norm_mlp_pre'], 'norm_mlp_post': out['norm_mlp_post'], 'even_w_in': out['even_w_in'], 'even_conv': out['even_conv'], 'even_a_log': out['even_a_log'], 'even_dt_bias': out['even_dt_bias'], 'even_dn_norm': out['even_dn_norm'], 'even_pool_w': out['even_pool_w'], 'even_pool_scale': out['even_pool_scale'], 'even_w_out': out['even_w_out'], 'odd_w_in': out['odd_w_in'], 'odd_dw': out['odd_dw'], 'odd_dw_b': out['odd_dw_b'], 'odd_ln_g': out['odd_ln_g'], 'odd_ln_b': out['odd_ln_b'], 'odd_w_out': out['odd_w_out'], 'mlp_w_up': out['mlp_w_up'], 'mlp_w_down': out['mlp_w_down'], 'loss_target': out['loss_target'], 'm_norm_mix_pre': out['m_norm_mix_pre'], 'm_norm_mix_post': out['m_norm_mix_post'], 'm_norm_mlp_pre': out['m_norm_mlp_pre'], 'm_norm_mlp_post': out['m_norm_mlp_post'], 'm_even_w_in': out['m_even_w_in'], 'm_even_conv': out['m_even_conv'], 'm_even_a_log': out['m_even_a_log'], 'm_even_dt_bias': out['m_even_dt_bias'], 'm_even_dn_norm': out['m_even_dn_norm'], 'm_even_pool_w': out['m_even_pool_w'], 'm_even_pool_scale': out['m_even_pool_scale'], 'm_even_w_out': out['m_even_w_out'], 'm_odd_w_in': out['m_odd_w_in'], 'm_odd_dw': out['m_odd_dw'], 'm_odd_dw_b': out['m_odd_dw_b'], 'm_odd_ln_g': out['m_odd_ln_g'], 'm_odd_ln_b': out['m_odd_ln_b'], 'm_odd_w_out': out['m_odd_w_out'], 'm_mlp_w_up': out['m_mlp_w_up'], 'm_mlp_w_down': out['m_mlp_w_down'], 'v_norm_mix_pre': out['v_norm_mix_pre'], 'v_norm_mix_post': out['v_norm_mix_post'], 'v_norm_mlp_pre': out['v_norm_mlp_pre'], 'v_norm_mlp_post': out['v_norm_mlp_post'], 'v_even_w_in': out['v_even_w_in'], 'v_even_conv': out['v_even_conv'], 'v_even_a_log': out['v_even_a_log'], 'v_even_dt_bias': out['v_even_dt_bias'], 'v_even_dn_norm': out['v_even_dn_norm'], 'v_even_pool_w': out['v_even_pool_w'], 'v_even_pool_scale': out['v_even_pool_scale'], 'v_even_w_out': out['v_even_w_out'], 'v_odd_w_in': out['v_odd_w_in'], 'v_odd_dw': out['v_odd_dw'], 'v_odd_dw_b': out['v_odd_dw_b'], 'v_odd_ln_g': out['v_odd_ln_g'], 'v_odd_ln_b': out['v_odd_ln_b'], 'v_odd_w_out': out['v_odd_w_out'], 'v_mlp_w_up': out['v_mlp_w_up'], 'v_mlp_w_down': out['v_mlp_w_down']}


def _loss(weights, diff, rest, loss_target):
    with _jax.named_scope("forward"):
        args = {**rest, TWIN_DIFF_INPUT: diff, **{k: w.astype(_WEIGHT_DTYPES[k]) for k, w in weights.items()}}
        y = _forward(args)
    with _jax.named_scope("loss_head"):
        err = _jnp.square(y.astype(_jnp.float32) - loss_target)
        return 0.5 * _jnp.sum(_jnp.mean(err, axis=-1)) if err.ndim else 0.5 * err


def _adamw(w, g, m, v):
    m = ADAM_B1 * m + (1.0 - ADAM_B1) * g
    v = ADAM_B2 * v + (1.0 - ADAM_B2) * _jnp.square(g)
    m_hat = m / (1.0 - ADAM_B1 ** ADAM_STEP)
    v_hat = v / (1.0 - ADAM_B2 ** ADAM_STEP)
    delta = -ADAM_LR * (m_hat / (_jnp.sqrt(v_hat) + ADAM_EPS) + ADAM_WD * w)
    return delta, m, v


def reference(x, norm_mix_pre, norm_mix_post, norm_mlp_pre, norm_mlp_post, even_w_in, even_conv, even_a_log, even_dt_bias, even_dn_norm, even_pool_w, even_pool_scale, even_w_out, odd_w_in, odd_dw, odd_dw_b, odd_ln_g, odd_ln_b, odd_w_out, mlp_w_up, mlp_w_down, loss_target, m_norm_mix_pre, m_norm_mix_post, m_norm_mlp_pre, m_norm_mlp_post, m_even_w_in, m_even_conv, m_even_a_log, m_even_dt_bias, m_even_dn_norm, m_even_pool_w, m_even_pool_scale, m_even_w_out, m_odd_w_in, m_odd_dw, m_odd_dw_b, m_odd_ln_g, m_odd_ln_b, m_odd_w_out, m_mlp_w_up, m_mlp_w_down, v_norm_mix_pre, v_norm_mix_post, v_norm_mlp_pre, v_norm_mlp_post, v_even_w_in, v_even_conv, v_even_a_log, v_even_dt_bias, v_even_dn_norm, v_even_pool_w, v_even_pool_scale, v_even_w_out, v_odd_w_in, v_odd_dw, v_odd_dw_b, v_odd_ln_g, v_odd_ln_b, v_odd_w_out, v_mlp_w_up, v_mlp_w_down):
    given = dict(x=x, norm_mix_pre=norm_mix_pre, norm_mix_post=norm_mix_post, norm_mlp_pre=norm_mlp_pre, norm_mlp_post=norm_mlp_post, even_w_in=even_w_in, even_conv=even_conv, even_a_log=even_a_log, even_dt_bias=even_dt_bias, even_dn_norm=even_dn_norm, even_pool_w=even_pool_w, even_pool_scale=even_pool_scale, even_w_out=even_w_out, odd_w_in=odd_w_in, odd_dw=odd_dw, odd_dw_b=odd_dw_b, odd_ln_g=odd_ln_g, odd_ln_b=odd_ln_b, odd_w_out=odd_w_out, mlp_w_up=mlp_w_up, mlp_w_down=mlp_w_down, loss_target=loss_target, m_norm_mix_pre=m_norm_mix_pre, m_norm_mix_post=m_norm_mix_post, m_norm_mlp_pre=m_norm_mlp_pre, m_norm_mlp_post=m_norm_mlp_post, m_even_w_in=m_even_w_in, m_even_conv=m_even_conv, m_even_a_log=m_even_a_log, m_even_dt_bias=m_even_dt_bias, m_even_dn_norm=m_even_dn_norm, m_even_pool_w=m_even_pool_w, m_even_pool_scale=m_even_pool_scale, m_even_w_out=m_even_w_out, m_odd_w_in=m_odd_w_in, m_odd_dw=m_odd_dw, m_odd_dw_b=m_odd_dw_b, m_odd_ln_g=m_odd_ln_g, m_odd_ln_b=m_odd_ln_b, m_odd_w_out=m_odd_w_out, m_mlp_w_up=m_mlp_w_up, m_mlp_w_down=m_mlp_w_down, v_norm_mix_pre=v_norm_mix_pre, v_norm_mix_post=v_norm_mix_post, v_norm_mlp_pre=v_norm_mlp_pre, v_norm_mlp_post=v_norm_mlp_post, v_even_w_in=v_even_w_in, v_even_conv=v_even_conv, v_even_a_log=v_even_a_log, v_even_dt_bias=v_even_dt_bias, v_even_dn_norm=v_even_dn_norm, v_even_pool_w=v_even_pool_w, v_even_pool_scale=v_even_pool_scale, v_even_w_out=v_even_w_out, v_odd_w_in=v_odd_w_in, v_odd_dw=v_odd_dw, v_odd_dw_b=v_odd_dw_b, v_odd_ln_g=v_odd_ln_g, v_odd_ln_b=v_odd_ln_b, v_odd_w_out=v_odd_w_out, v_mlp_w_up=v_mlp_w_up, v_mlp_w_down=v_mlp_w_down)
    weights = {n: given[n] for n in TWIN_WEIGHTS}
    shared = {n: given[n] for n in SHARED_INPUTS}
    per_example = {n: given[n] for n in ['x']}
    grad_fn = _jax.value_and_grad(_loss, argnums=(0, 1))

    def one_microbatch(ex, loss_target):
        ex = dict(ex)
        diff = ex.pop(TWIN_DIFF_INPUT)
        return grad_fn(weights, diff, {**shared, **ex}, loss_target)

    if N_MICROBATCH == 1:
        loss, (grad_w, grad_x) = one_microbatch(per_example, given["loss_target"])
    else:
        def body(carry, xs):
            loss_sum, grad_sum = carry
            l_k, (gw_k, gx_k) = one_microbatch(xs[0], xs[1])
            with _jax.named_scope("update"):
                return (loss_sum + l_k, _jax.tree.map(_jnp.add, grad_sum, gw_k)), gx_k

        init = (_jnp.zeros((), _jnp.float32), _jax.tree.map(_jnp.zeros_like, weights))
        (loss, grad_w), grad_x = _jax.lax.scan(body, init, (per_example, given["loss_target"]))
    with _jax.named_scope("update"):
        delta_w, new_m, new_v = {}, {}, {}
        for n in TWIN_WEIGHTS:
            delta_w[n], new_m[n], new_v[n] = _adamw(weights[n], grad_w[n], given["m_" + n], given["v_" + n])
    return (loss, grad_x, *[grad_w[n] for n in TWIN_WEIGHTS], *[delta_w[n] for n in TWIN_WEIGHTS],
            *[new_m[n] for n in TWIN_WEIGHTS], *[new_v[n] for n in TWIN_WEIGHTS])
```

```python
import functools
import math

import jax
import jax.numpy as jnp
from jax import lax
from jax.experimental import pallas as pl
from jax.experimental.pallas import tpu as pltpu

F32 = jnp.float32
BF16 = jnp.bfloat16
EPS = 1e-6
DN_CHUNK = 64
POOL_WINDOWS = (2, 4, 8, 16)
ADAM_LR, ADAM_B1, ADAM_B2, ADAM_EPS, ADAM_WD, ADAM_STEP = 0.001, 0.9, 0.999, 1e-08, 0.01, 10
LANE = 128
FLAT_L = 1024
ADAM_ROWS = 256
VMEM_LIMIT = 56 * 1024 * 1024
N_CHIPS = 4
HI = lax.Precision.HIGHEST
MESH = pl.DeviceIdType.MESH


def _cparams(sem):
    return pltpu.CompilerParams(dimension_semantics=sem, vmem_limit_bytes=VMEM_LIMIT)


def _pick(dim, target, mult=LANE):
    if dim <= target:
        return dim
    t = (target // mult) * mult
    while t >= mult:
        if dim % t == 0:
            return t
        t -= mult
    return dim


def _sigmoid(x):
    return 1.0 / (1.0 + jnp.exp(-x))


def _silu(x):
    return x * _sigmoid(x)


def _softplus(x):
    return jnp.maximum(x, 0.0) + jnp.log(1.0 + jnp.exp(-jnp.abs(x)))


def _rms(x, g):
    return x * lax.rsqrt(jnp.mean(x * x, axis=-1, keepdims=True) + EPS) * g


def _matmul(name, a, b, mode, out_dtype=F32, tm=1024, tn=1024, tk=512):
    if mode == "nn":
        (M, K), (K2, N) = a.shape, b.shape
    elif mode == "nt":
        (M, K), (N, K2) = a.shape, b.shape
    else:
        (K, M), (K2, N) = a.shape, b.shape
    assert K == K2, (name, a.shape, b.shape, mode)
    tm, tn, tk = _pick(M, tm), _pick(N, tn), _pick(K, tk)
    nk = K // tk
    if mode == "nn":
        a_spec = pl.BlockSpec((tm, tk), lambda i, j, k: (i, k))
        b_spec = pl.BlockSpec((tk, tn), lambda i, j, k: (k, j))
        dims = (((1,), (0,)), ((), ()))
    elif mode == "nt":
        a_spec = pl.BlockSpec((tm, tk), lambda i, j, k: (i, k))
        b_spec = pl.BlockSpec((tn, tk), lambda i, j, k: (j, k))
        dims = (((1,), (1,)), ((), ()))
    else:
        a_spec = pl.BlockSpec((tk, tm), lambda i, j, k: (k, i))
        b_spec = pl.BlockSpec((tk, tn), lambda i, j, k: (k, j))
        dims = (((0,), (0,)), ((), ()))

    def body(a_ref, b_ref, o_ref, acc_ref):
        k = pl.program_id(2)

        @pl.when(k == 0)
        def _():
            acc_ref[...] = jnp.zeros_like(acc_ref)

        acc_ref[...] += lax.dot_general(a_ref[...].astype(BF16), b_ref[...].astype(BF16), dims,
                                        preferred_element_type=F32)

        @pl.when(k == nk - 1)
        def _():
            o_ref[...] = acc_ref[...].astype(o_ref.dtype)

    return pl.pallas_call(
        body, name=name, grid=(M // tm, N // tn, nk),
        in_specs=[a_spec, b_spec], out_specs=pl.BlockSpec((tm, tn), lambda i, j, k: (i, j)),
        out_shape=jax.ShapeDtypeStruct((M, N), out_dtype),
        scratch_shapes=[pltpu.VMEM((tm, tn), F32)],
        compiler_params=_cparams(("parallel", "parallel", "arbitrary")),
    )(a, b)


def _row_spec(tr, C, off):
    return pl.BlockSpec((tr, C), lambda j, i: (i, off + j))


def _par_spec(k, C, off):
    if off is None:
        return pl.BlockSpec((k, C), lambda j, i: (0, 0))
    return pl.BlockSpec((k, C), lambda j, i: (0, off + j))


def _rowwise(name, fn, rows, params, outs, reds=(), *, S, ncb=1, tr=256):
    tr = min(tr, S)
    nr, npar, no = len(rows), len(params), len(outs)

    def body(*refs):
        j, i = pl.program_id(0), pl.program_id(1)
        ins = [r[...].astype(F32) for r in refs[:nr + npar]]
        res = fn(i * tr, j, *ins)
        res = res if isinstance(res, (tuple, list)) else (res,)
        out_refs = refs[nr + npar:]
        for r, v in zip(out_refs[:no], res[:no]):
            r[...] = v.astype(r.dtype)
        for (k, C, per_j), r, v in zip(reds, out_refs[no:], res[no:]):
            first = (i == 0) if per_j else jnp.logical_and(i == 0, j == 0)

            @pl.when(first)
            def _(r=r):
                r[...] = jnp.zeros_like(r)

            r[...] += v

    in_specs = [_row_spec(tr, C, off) for (_, off, C) in rows] + [_par_spec(a.shape[0], C, off) for (a, off, C) in params]
    out_specs = [pl.BlockSpec((tr, C), lambda j, i: (i, j)) for (C, _) in outs]
    out_specs += [pl.BlockSpec((k, C), (lambda j, i: (0, j)) if per_j else (lambda j, i: (0, 0))) for (k, C, per_j) in reds]
    out_shape = [jax.ShapeDtypeStruct((S, ncb * C), dt) for (C, dt) in outs]
    out_shape += [jax.ShapeDtypeStruct((k, C * (ncb if per_j else 1)), F32) for (k, C, per_j) in reds]
    res = pl.pallas_call(
        body, name=name, grid=(ncb, S // tr), in_specs=in_specs, out_specs=out_specs, out_shape=out_shape,
        compiler_params=_cparams(("arbitrary", "arbitrary")),
    )(*[a for (a, _, _) in rows], *[a for (a, _, _) in params])
    return res


def _rowwise_bwd(name, fn, rows, params, cots, drow, adds=None, *, S, ncb=1, tr=128):
    tr = min(tr, S)
    nr, npar, nc = len(rows), len(params), len(cots)
    adds = adds or [None] * nr
    add_list = [a for a in adds if a is not None]
    na = len(add_list)

    def body(*refs):
        j, i = pl.program_id(0), pl.program_id(1)
        ins = [r[...].astype(F32) for r in refs[:nr + npar]]
        cts = [r[...].astype(F32) for r in refs[nr + npar:nr + npar + nc]]
        add_refs = list(refs[nr + npar + nc:nr + npar + nc + na])
        out_refs = list(refs[nr + npar + nc + na:])

        def f(*a):
            res = fn(i * tr, j, *a)
            return tuple(res) if isinstance(res, (tuple, list)) else (res,)

        _, vjp = jax.vjp(f, *ins)
        grads = vjp(tuple(cts))
        for idx in range(nr):
            if drow[idx] is None:
                continue
            g = grads[idx]
            if adds[idx] is not None:
                g = g + add_refs.pop(0)[...].astype(F32)
            r = out_refs.pop(0)
            r[...] = g.astype(r.dtype)
        for idx in range(npar):
            per_j = params[idx][1] is not None
            first = (i == 0) if per_j else jnp.logical_and(i == 0, j == 0)
            r = out_refs.pop(0)

            @pl.when(first)
            def _(r=r):
                r[...] = jnp.zeros_like(r)

            r[...] += grads[nr + idx]

    in_specs = [_row_spec(tr, C, off) for (_, off, C) in rows]
    in_specs += [_par_spec(a.shape[0], C, off) for (a, off, C) in params]
    in_specs += [_row_spec(tr, C, off) for (_, off, C) in cots]
    in_specs += [_row_spec(tr, C, off) for (_, off, C) in add_list]
    out_specs, out_shape = [], []
    for idx in range(nr):
        if drow[idx] is not None:
            C = rows[idx][2]
            out_specs.append(pl.BlockSpec((tr, C), lambda j, i: (i, j)))
            out_shape.append(jax.ShapeDtypeStruct((S, ncb * C), drow[idx]))
    for (a, off, C) in params:
        per_j = off is not None
        out_specs.append(pl.BlockSpec((a.shape[0], C), (lambda j, i: (0, j)) if per_j else (lambda j, i: (0, 0))))
        out_shape.append(jax.ShapeDtypeStruct((a.shape[0], C * (ncb if per_j else 1)), F32))
    return pl.pallas_call(
        body, name=name, grid=(ncb, S // tr), in_specs=in_specs, out_specs=out_specs, out_shape=out_shape,
        compiler_params=_cparams(("arbitrary", "arbitrary")),
    )(*[a for (a, _, _) in rows], *[a for (a, _, _) in params], *[a for (a, _, _) in cots], *[a for (a, _, _) in add_list])


def _halo_rows(K):
    return 8 * ((K - 1 + 7) // 8)


def _inv_count(row0, tr, win):
    t = (row0 + lax.broadcasted_iota(jnp.int32, (tr, 1), 0)).astype(F32)
    return 1.0 / jnp.minimum(t + 1.0, win)


def _dwconv_fwd(name, x, x_off, w, *, S, C, bias=None, win=None, out_dtype=F32, cb=512, tr=256):
    K = w.shape[0]
    cb, tr = _pick(C, cb), min(tr, S)
    HB = min(_halo_rows(K), tr)
    assert K - 1 <= HB and tr % HB == 0 and C % cb == 0
    nb = tr // HB
    extra = [a for a in (bias, win) if a is not None]

    def body(xh_ref, x_ref, w_ref, *rest):
        y_ref, xx = rest[-2], rest[-1]
        i = pl.program_id(1)
        xx[0:HB, :] = jnp.where(i > 0, xh_ref[...].astype(F32), 0.0)
        xx[HB:HB + tr, :] = x_ref[...].astype(F32)
        acc = jnp.zeros((tr, cb), F32)
        for jj in range(K):
            o = HB - (K - 1) + jj
            acc = acc + w_ref[jj:jj + 1, :] * xx[o:o + tr, :]
        if bias is not None:
            acc = acc + rest[0][...]
        if win is not None:
            acc = acc * _inv_count(i * tr, tr, rest[0][...]) - x_ref[...].astype(F32)
        y_ref[...] = acc.astype(y_ref.dtype)

    in_specs = [pl.BlockSpec((HB, cb), lambda j, i: (jnp.maximum(i * nb - 1, 0), x_off + j)),
                pl.BlockSpec((tr, cb), lambda j, i: (i, x_off + j)),
                pl.BlockSpec((K, cb), lambda j, i: (0, j))]
    in_specs += [pl.BlockSpec((1, cb), lambda j, i: (0, j)) for _ in extra]
    return pl.pallas_call(
        body, name=name, grid=(C // cb, S // tr), in_specs=in_specs,
        out_specs=pl.BlockSpec((tr, cb), lambda j, i: (i, j)),
        out_shape=jax.ShapeDtypeStruct((S, C), out_dtype),
        scratch_shapes=[pltpu.VMEM((HB + tr, cb), F32)],
        compiler_params=_cparams(("parallel", "arbitrary")),
    )(x, x, w, *extra)


def _dwconv_bwd(name, x, x_off, dy, w, *, S, C, win=None, want_dw=True, cb=512, tr=256):
    K = w.shape[0]
    cb, tr = _pick(C, cb), min(tr, S)
    HB = min(_halo_rows(K), tr)
    nb, nt = tr // HB, S // tr

    def body(*refs):
        if want_dw:
            xh_ref, x_ref, dy_ref, dyn_ref, w_ref = refs[:5]
            rest = refs[5:]
        else:
            dy_ref, dyn_ref, w_ref = refs[:3]
            rest = refs[3:]
        i = pl.program_id(1)
        dyt = dy_ref[...].astype(F32)
        dyn = jnp.where(i < nt - 1, dyn_ref[...].astype(F32), 0.0)
        if win is not None:
            win_v = rest[0][...]
            rest = rest[1:]
            yy_t = dyt * _inv_count(i * tr, tr, win_v)
            dyn = dyn * _inv_count((i + 1) * tr, HB, win_v)
        else:
            yy_t = dyt
        if want_dw:
            dx_ref, dw_ref, db_ref, yy, xx = rest
        else:
            dx_ref, yy = rest
        yy[0:tr, :] = yy_t
        yy[tr:tr + HB, :] = dyn
        acc = jnp.zeros((tr, cb), F32)
        for jj in range(K):
            o = K - 1 - jj
            acc = acc + w_ref[jj:jj + 1, :] * yy[o:o + tr, :]
        if win is not None:
            acc = acc - dyt
        dx_ref[...] = acc.astype(dx_ref.dtype)
        if want_dw:
            xx[0:HB, :] = jnp.where(i > 0, xh_ref[...].astype(F32), 0.0)
            xx[HB:HB + tr, :] = x_ref[...].astype(F32)

            @pl.when(i == 0)
            def _():
                dw_ref[...] = jnp.zeros_like(dw_ref)
                db_ref[...] = jnp.zeros_like(db_ref)

            for jj in range(K):
                o = HB - (K - 1) + jj
                dw_ref[jj:jj + 1, :] += jnp.sum(dyt * xx[o:o + tr, :], axis=0, keepdims=True)
            db_ref[...] += jnp.sum(dyt, axis=0, keepdims=True)

    last = S // HB - 1
    in_specs, args = [], []
    if want_dw:
        in_specs += [pl.BlockSpec((HB, cb), lambda j, i: (jnp.maximum(i * nb - 1, 0), x_off + j)),
                     pl.BlockSpec((tr, cb), lambda j, i: (i, x_off + j))]
        args += [x, x]
    in_specs += [pl.BlockSpec((tr, cb), lambda j, i: (i, j)),
                 pl.BlockSpec((HB, cb), lambda j, i: (jnp.minimum((i + 1) * nb, last), j)),
                 pl.BlockSpec((K, cb), lambda j, i: (0, j))]
    args += [dy, dy, w]
    if win is not None:
        in_specs.append(pl.BlockSpec((1, cb), lambda j, i: (0, j)))
        args.append(win)
    out_specs = [pl.BlockSpec((tr, cb), lambda j, i: (i, j))]
    out_shape = [jax.ShapeDtypeStruct((S, C), F32)]
    scratch = [pltpu.VMEM((tr + HB, cb), F32)]
    if want_dw:
        out_specs += [pl.BlockSpec((K, cb), lambda j, i: (0, j)), pl.BlockSpec((1, cb), lambda j, i: (0, j))]
        out_shape += [jax.ShapeDtypeStruct((K, C), F32), jax.ShapeDtypeStruct((1, C), F32)]
        scratch.append(pltpu.VMEM((HB + tr, cb), F32))
    return pl.pallas_call(
        body, name=name, grid=(C // cb, S // tr), in_specs=in_specs, out_specs=out_specs, out_shape=out_shape,
        scratch_shapes=scratch, compiler_params=_cparams(("parallel", "arbitrary")),
    )(*args)


def _dot(a, b, dims, hi=False):
    if hi:
        return lax.dot_general(a, b, (dims, ((), ())), precision=HI, preferred_element_type=F32)
    return lax.dot_general(a.astype(BF16), b.astype(BF16), (dims, ((), ())), preferred_element_type=F32)


_NN, _NT, _TN = ((1,), (0,)), ((1,), (1,)), ((0,), (0,))


def _col(m, idx):
    lane = lax.broadcasted_iota(jnp.int32, m.shape, 1)
    return jnp.sum(jnp.where(lane == idx, m, 0.0), axis=1, keepdims=True)


def _row(m, idx):
    sub = lax.broadcasted_iota(jnp.int32, m.shape, 0)
    return jnp.sum(jnp.where(sub == idx, m, 0.0), axis=0, keepdims=True)


def _delta_chunk(q, k, v, beta, gcc, gcr, causal, strict, eye, scale, C):
    d = {}
    gam = jnp.where(causal, jnp.exp(jnp.where(causal, gcc - gcr, 0.0)), 0.0)
    eg = jnp.exp(gcc)
    g_last = _row(gcc, C - 1)
    d["gam"], d["eg"], d["g_last"] = gam, eg, g_last
    d["ek"] = jnp.exp(g_last - gcc)
    d["decay"] = jnp.exp(g_last)
    qs = q * scale
    kb = k * beta
    d["qs"], d["kb"] = qs, kb
    d["kk"] = _dot(kb, k, _NT)
    d["A"] = jnp.where(strict, d["kk"] * gam, 0.0)
    d["qk"] = _dot(qs, k, _NT)
    d["attn"] = jnp.where(causal, d["qk"] * gam, 0.0)
    d["vb"] = v * beta
    d["kbg"] = kb * eg
    d["qg"] = qs * eg
    d["kd"] = k * d["ek"]
    return d


def _tri_inverse(A, eye):
    P = -A
    T = eye + P
    n = 1
    while 2 * n < A.shape[0]:
        P = _dot(P, P, _NN, hi=True)
        T = T + _dot(T, P, _NN, hi=True)
        n *= 2
    return T


def _delta_fwd(qk, v, gb, gT, *, S, H, dh):
    C = min(DN_CHUNK, S)
    N, W = S // C, H * dh
    scale = dh ** -0.5

    def body(qk_ref, v_ref, gb_ref, gT_ref, o_ref, sp_ref, T_ref, st):
        n = pl.program_id(0)

        @pl.when(n == 0)
        def _():
            st[...] = jnp.zeros_like(st)

        r = lax.broadcasted_iota(jnp.int32, (C, C), 0)
        c = lax.broadcasted_iota(jnp.int32, (C, C), 1)
        causal, strict = r >= c, r > c
        eye = (r == c).astype(F32)
        Lt = causal.astype(F32)
        gbv = gb_ref[...]
        gcum = _dot(Lt, gbv, _NN, hi=True)
        gcumT = _dot(gT_ref[0], Lt, _NT, hi=True)
        for h in range(H):
            sl = slice(h * dh, (h + 1) * dh)
            q, k, vv = qk_ref[:, sl], qk_ref[:, W + h * dh:W + (h + 1) * dh], v_ref[:, sl]
            d = _delta_chunk(q, k, vv, _col(gbv, h), _col(gcum, H + h), _row(gcumT, h), causal, strict, eye, scale, C)
            T = _tri_inverse(d["A"], eye)
            u = _dot(T, d["vb"], _NN)
            w = _dot(T, d["kbg"], _NN)
            s0 = st[h]
            sp_ref[0, h] = s0
            T_ref[0, h] = T
            v_new = u - _dot(w, s0, _NN)
            o_ref[:, sl] = _dot(d["qg"], s0, _NN) + _dot(d["attn"], v_new, _NN)
            st[h] = s0 * d["decay"] + _dot(d["kd"], v_new, _TN)

    return pl.pallas_call(
        body, name="delta_fwd", grid=(N,),
        in_specs=[pl.BlockSpec((C, 2 * W), lambda n: (n, 0)), pl.BlockSpec((C, W), lambda n: (n, 0)),
                  pl.BlockSpec((C, LANE), lambda n: (n, 0)), pl.BlockSpec((1, H, C), lambda n: (n, 0, 0))],
        out_specs=[pl.BlockSpec((C, W), lambda n: (n, 0)), pl.BlockSpec((1, H, dh, dh), lambda n: (n, 0, 0, 0)),
                   pl.BlockSpec((1, H, C, C), lambda n: (n, 0, 0, 0))],
        out_shape=[jax.ShapeDtypeStruct((S, W), F32), jax.ShapeDtypeStruct((N, H, dh, dh), F32),
                   jax.ShapeDtypeStruct((N, H, C, C), F32)],
        scratch_shapes=[pltpu.VMEM((H, dh, dh), F32)],
        compiler_params=_cparams(("arbitrary",)),
    )(qk, v, gb, gT)


def _delta_bwd(qk, v, gb, gT, sp, Tm, do, *, S, H, dh):
    C = min(DN_CHUNK, S)
    N, W = S // C, H * dh
    scale = dh ** -0.5

    def body(qk_ref, v_ref, gb_ref, gT_ref, sp_ref, T_ref, do_ref, dqk_ref, dv_ref, dgb_ref, ds):
        n = pl.program_id(0)

        @pl.when(n == 0)
        def _():
            ds[...] = jnp.zeros_like(ds)

        r = lax.broadcasted_iota(jnp.int32, (C, C), 0)
        c = lax.broadcasted_iota(jnp.int32, (C, C), 1)
        causal, strict = r >= c, r > c
        eye = (r == c).astype(F32)
        Lt = causal.astype(F32)
        ones = jnp.ones((C, LANE), F32)
        lane = lax.broadcasted_iota(jnp.int32, (C, LANE), 1)
        rowi = lax.broadcasted_iota(jnp.int32, (C, 1), 0)
        gbv = gb_ref[...]
        gcum = _dot(Lt, gbv, _NN, hi=True)
        gcumT = _dot(gT_ref[0], Lt, _NT, hi=True)
        dgc_all = jnp.zeros((C, LANE), F32)
        dbeta_all = jnp.zeros((C, LANE), F32)
        for h in range(H):
            sl = slice(h * dh, (h + 1) * dh)
            ksl = slice(W + h * dh, W + (h + 1) * dh)
            q, k, vv = qk_ref[:, sl], qk_ref[:, ksl], v_ref[:, sl]
            beta, gcc = _col(gbv, h), _col(gcum, H + h)
            d = _delta_chunk(q, k, vv, beta, gcc, _row(gcumT, h), causal, strict, eye, scale, C)
            T, s0, dO, dS = T_ref[0, h], sp_ref[0, h], do_ref[:, sl], ds[h]
            u = _dot(T, d["vb"], _NN)
            w = _dot(T, d["kbg"], _NN)
            v_new = u - _dot(w, s0, _NN)
            dv_new = _dot(d["attn"], dO, _TN) + _dot(d["kd"], dS, _NN)
            dattn = jnp.where(causal, _dot(dO, v_new, _NT), 0.0)
            dqg = _dot(dO, s0, _NT)
            dkd = _dot(v_new, dS, _NT)
            ddecay = jnp.sum(jnp.sum(s0 * dS, axis=1, keepdims=True), axis=0, keepdims=True)
            ds[h] = _dot(d["qg"], dO, _TN) + d["decay"] * dS - _dot(w, dv_new, _TN)
            dw = -_dot(dv_new, s0, _NT)
            dT = _dot(dv_new, d["vb"], _NT) + _dot(dw, d["kbg"], _NT)
            dvb = _dot(T, dv_new, _TN)
            dkbg = _dot(T, dw, _TN)
            dA = jnp.where(strict, -_dot(_dot(T, dT, _TN, hi=True), T, _NT, hi=True), 0.0)
            dkk = dA * d["gam"]
            dqk_m = dattn * d["gam"]
            m = dA * d["A"] + dattn * d["attn"]
            dkb = _dot(dkk, k, _NN) + dkbg * d["eg"]
            dk = _dot(dkk, d["kb"], _TN) + _dot(dqk_m, d["qs"], _TN) + dkd * d["ek"] + dkb * beta
            dqs = _dot(dqk_m, k, _NN) + dqg * d["eg"]
            r_kd = jnp.sum(dkd * d["kd"], axis=1, keepdims=True)
            dgc = (jnp.sum(m, axis=1, keepdims=True) - jnp.max(_dot(m, ones, _TN, hi=True), axis=1, keepdims=True)
                   + jnp.sum(dqg * d["qg"], axis=1, keepdims=True) - r_kd
                   + jnp.sum(dkbg * d["kbg"], axis=1, keepdims=True))
            dg_last = jnp.sum(r_kd, axis=0, keepdims=True) + ddecay * d["decay"]
            dgc = dgc + jnp.where(rowi == C - 1, dg_last, 0.0)
            dbeta = jnp.sum(dkb * k, axis=1, keepdims=True) + jnp.sum(dvb * vv, axis=1, keepdims=True)
            dqk_ref[:, sl] = dqs * scale
            dqk_ref[:, ksl] = dk
            dv_ref[:, sl] = dvb * beta
            dgc_all = dgc_all + jnp.where(lane == H + h, dgc, 0.0)
            dbeta_all = dbeta_all + jnp.where(lane == h, dbeta, 0.0)
        dgb_ref[...] = _dot(Lt, dgc_all, _TN, hi=True) + dbeta_all

    rev = lambda n: N - 1 - n
    return pl.pallas_call(
        body, name="delta_bwd", grid=(N,),
        in_specs=[pl.BlockSpec((C, 2 * W), lambda n: (rev(n), 0)), pl.BlockSpec((C, W), lambda n: (rev(n), 0)),
                  pl.BlockSpec((C, LANE), lambda n: (rev(n), 0)), pl.BlockSpec((1, H, C), lambda n: (rev(n), 0, 0)),
                  pl.BlockSpec((1, H, dh, dh), lambda n: (rev(n), 0, 0, 0)),
                  pl.BlockSpec((1, H, C, C), lambda n: (rev(n), 0, 0, 0)),
                  pl.BlockSpec((C, W), lambda n: (rev(n), 0))],
        out_specs=[pl.BlockSpec((C, 2 * W), lambda n: (rev(n), 0)), pl.BlockSpec((C, W), lambda n: (rev(n), 0)),
                   pl.BlockSpec((C, LANE), lambda n: (rev(n), 0))],
        out_shape=[jax.ShapeDtypeStruct((S, 2 * W), F32), jax.ShapeDtypeStruct((S, W), F32),
                   jax.ShapeDtypeStruct((S, LANE), F32)],
        scratch_shapes=[pltpu.VMEM((H, dh, dh), F32)],
        compiler_params=_cparams(("arbitrary",)),
    )(qk, v, gb, gT, sp, Tm, do)


_ANY = pl.BlockSpec(memory_space=pl.ANY)


def _place():
    return lax.axis_index("x"), lax.axis_index("y"), lax.axis_index("c")


def _allgather_chips(name, flat):
    R, L = flat.shape
    Rh = R // 2

    def body(x_ref, out_ref, send_sems, recv_sems, local_sem):
        x, y, c = _place()
        sibling = (x, y, 1 - c)
        chips = [(1 - x, y), (x, 1 - y), (1 - x, 1 - y)]

        def half(px, py, pc):
            return out_ref.at[2 * px + py, pl.ds(pc * Rh, Rh), :]

        def copy(k, src, dst, to):
            return pltpu.make_async_remote_copy(src_ref=src, dst_ref=dst, send_sem=send_sems.at[k],
                                                recv_sem=recv_sems.at[k], device_id=to, device_id_type=MESH)

        mine = pltpu.make_async_copy(x_ref, out_ref.at[2 * x + y], local_sem)
        mine.start()
        my_half = x_ref.at[pl.ds(c * Rh, Rh), :]
        first = [copy(k, my_half, half(x, y, c), (*chip, c)) for k, chip in enumerate(chips)]
        for cp in first:
            cp.start()
        passed = [copy(3 + k, half(*chip, c), half(*chip, c), sibling) for k, chip in enumerate(chips)]
        for k, chip in enumerate(chips):
            copy(k, my_half, half(*chip, c), (*chip, c)).wait_recv()
            passed[k].start()
        for k, chip in enumerate(chips):
            copy(3 + k, my_half, half(*chip, 1 - c), sibling).wait_recv()
        for cp in first + passed:
            cp.wait_send()
        mine.wait()

    return pl.pallas_call(
        body, name=name, in_specs=[_ANY], out_specs=_ANY,
        out_shape=jax.ShapeDtypeStruct((N_CHIPS, R, L), flat.dtype),
        scratch_shapes=[pltpu.SemaphoreType.DMA((6,)), pltpu.SemaphoreType.DMA((6,)), pltpu.SemaphoreType.DMA],
    )(flat)


def _sibling_split(name, g):
    _, R, L = g.shape
    Rh = R // 2

    def body(g_ref, own_ref, got_ref, send_sem, recv_sem, local_sem):
        x, y, c = _place()
        keep = pltpu.make_async_copy(g_ref.at[:, pl.ds(c * Rh, Rh), :], own_ref, local_sem)
        keep.start()
        cp = pltpu.make_async_remote_copy(src_ref=g_ref.at[:, pl.ds((1 - c) * Rh, Rh), :], dst_ref=got_ref,
                                          send_sem=send_sem, recv_sem=recv_sem, device_id=(x, y, 1 - c),
                                          device_id_type=MESH)
        cp.start()
        cp.wait()
        keep.wait()

    shp = jax.ShapeDtypeStruct((N_CHIPS, Rh, L), g.dtype)
    return pl.pallas_call(
        body, name=name, in_specs=[_ANY], out_specs=[_ANY, _ANY], out_shape=[shp, shp],
        scratch_shapes=[pltpu.SemaphoreType.DMA, pltpu.SemaphoreType.DMA, pltpu.SemaphoreType.DMA],
    )(g)


def _chip_exchange(name, p):
    def body(p_ref, q_ref, send_sems, recv_sems, local_sem):
        x, y, c = _place()
        me = 2 * x + y
        chips = [(1 - x, y), (x, 1 - y), (1 - x, 1 - y)]
        keep = pltpu.make_async_copy(p_ref.at[me], q_ref.at[me], local_sem)
        keep.start()
        cps = [pltpu.make_async_remote_copy(src_ref=p_ref.at[2 * cx + cy], dst_ref=q_ref.at[me],
                                            send_sem=send_sems.at[k], recv_sem=recv_sems.at[k],
                                            device_id=(cx, cy, c), device_id_type=MESH)
               for k, (cx, cy) in enumerate(chips)]
        for cp in cps:
            cp.start()
        for k, (cx, cy) in enumerate(chips):
            pltpu.make_async_remote_copy(src_ref=p_ref.at[me], dst_ref=q_ref.at[2 * cx + cy],
                                         send_sem=send_sems.at[k], recv_sem=recv_sems.at[k],
                                         device_id=(cx, cy, c), device_id_type=MESH).wait_recv()
        for cp in cps:
            cp.wait_send()
        keep.wait()

    return pl.pallas_call(
        body, name=name, in_specs=[_ANY], out_specs=_ANY, out_shape=jax.ShapeDtypeStruct(p.shape, p.dtype),
        scratch_shapes=[pltpu.SemaphoreType.DMA((3,)), pltpu.SemaphoreType.DMA((3,)), pltpu.SemaphoreType.DMA],
    )(p)


def _sibling_join(name, half):
    Rh, L = half.shape

    def body(h_ref, out_ref, send_sem, recv_sem, local_sem):
        x, y, c = _place()
        keep = pltpu.make_async_copy(h_ref, out_ref.at[pl.ds(c * Rh, Rh), :], local_sem)
        keep.start()
        cp = pltpu.make_async_remote_copy(src_ref=h_ref, dst_ref=out_ref.at[pl.ds(c * Rh, Rh), :],
                                          send_sem=send_sem, recv_sem=recv_sem, device_id=(x, y, 1 - c),
                                          device_id_type=MESH)
        cp.start()
        pltpu.make_async_remote_copy(src_ref=h_ref, dst_ref=out_ref.at[pl.ds((1 - c) * Rh, Rh), :],
                                     send_sem=send_sem, recv_sem=recv_sem, device_id=(x, y, 1 - c),
                                     device_id_type=MESH).wait_recv()
        cp.wait_send()
        keep.wait()

    return pl.pallas_call(
        body, name=name, in_specs=[_ANY], out_specs=_ANY, out_shape=jax.ShapeDtypeStruct((2 * Rh, L), half.dtype),
        scratch_shapes=[pltpu.SemaphoreType.DMA, pltpu.SemaphoreType.DMA, pltpu.SemaphoreType.DMA],
    )(half)


def _add_pairs(name, a, b, out_dtype):
    n, Rh, L = a.shape
    tr = _pick(Rh, 512, 8)

    def body(a_ref, b_ref, o_ref):
        o_ref[...] = (a_ref[...].astype(F32) + b_ref[...].astype(F32)).astype(o_ref.dtype)

    spec = pl.BlockSpec((1, tr, L), lambda j, i: (j, i, 0))
    return pl.pallas_call(body, name=name, grid=(n, Rh // tr), in_specs=[spec, spec], out_specs=spec,
                          out_shape=jax.ShapeDtypeStruct(a.shape, out_dtype),
                          compiler_params=_cparams(("parallel", "parallel")))(a, b)


def _sum_chips(name, q):
    n, Rh, L = q.shape
    tr = _pick(Rh, 512, 8)

    def body(q_ref, o_ref):
        acc = q_ref[0].astype(F32)
        for s in range(1, n):
            acc = acc + q_ref[s].astype(F32)
        o_ref[...] = acc

    return pl.pallas_call(body, name=name, grid=(Rh // tr,),
                          in_specs=[pl.BlockSpec((n, tr, L), lambda i: (0, i, 0))],
                          out_specs=pl.BlockSpec((tr, L), lambda i: (i, 0)),
                          out_shape=jax.ShapeDtypeStruct((Rh, L), F32),
                          compiler_params=_cparams(("parallel",)))(q)


def _adamw(name, w, m, v, gflat, row0):
    shape, size = w.shape, w.size
    rows = -(-size // FLAT_L)
    rows_p = -(-rows // ADAM_ROWS) * ADAM_ROWS
    pad = rows_p * FLAT_L - size

    def flat2d(a):
        a = a.reshape(-1)
        if pad:
            a = jnp.pad(a, (0, pad), constant_values=1.0)
        return a.reshape(rows_p, FLAT_L)

    blk0 = row0 // ADAM_ROWS
    c1 = 1.0 / (1.0 - ADAM_B1 ** ADAM_STEP)
    c2 = 1.0 / (1.0 - ADAM_B2 ** ADAM_STEP)

    def body(w_ref, m_ref, v_ref, g_ref, go_ref, d_ref, mo_ref, vo_ref):
        g = g_ref[...]
        wv = w_ref[...]
        mn = ADAM_B1 * m_ref[...] + (1.0 - ADAM_B1) * g
        vn = ADAM_B2 * v_ref[...] + (1.0 - ADAM_B2) * (g * g)
        go_ref[...] = g
        mo_ref[...] = mn
        vo_ref[...] = vn
        d_ref[...] = -ADAM_LR * ((mn * c1) / (jnp.sqrt(vn * c2) + ADAM_EPS) + ADAM_WD * wv)

    spec = pl.BlockSpec((ADAM_ROWS, FLAT_L), lambda i: (i, 0))
    shp = jax.ShapeDtypeStruct((rows_p, FLAT_L), F32)
    outs = pl.pallas_call(
        body, name=name, grid=(rows_p // ADAM_ROWS,),
        in_specs=[spec, spec, spec, pl.BlockSpec((ADAM_ROWS, FLAT_L), lambda i: (blk0 + i, 0))],
        out_specs=[spec] * 4, out_shape=[shp] * 4, compiler_params=_cparams(("parallel",)),
    )(flat2d(w), flat2d(m), flat2d(v), gflat)

    def back(a):
        a = a.reshape(-1)
        if pad:
            a = a[:size]
        return a.reshape(shape)

    return tuple(back(a) for a in outs)


def _seg_rows(size):
    rows = -(-size // FLAT_L)
    return -(-rows // ADAM_ROWS) * ADAM_ROWS


def _pack(pieces, dtype, row_mult):
    segs, offs, r = [], [], 0
    for a in pieces:
        rows = _seg_rows(a.size)
        flat = a.reshape(-1).astype(dtype)
        flat = jnp.pad(flat, (0, rows * FLAT_L - a.size))
        segs.append(flat.reshape(rows, FLAT_L))
        offs.append(r)
        r += rows
    tail = -r % row_mult
    if tail:
        segs.append(jnp.zeros((tail, FLAT_L), dtype))
    return jnp.concatenate(segs, axis=0), offs


def _gather_weights(name, shards, axes, dtype):
    flat, offs = _pack(shards, dtype, 2 * 8)
    full = _allgather_chips(name, flat)
    outs = []
    for a, ax, off in zip(shards, axes, offs):
        rows = _seg_rows(a.size)
        seg = full[:, off:off + rows].reshape(N_CHIPS, -1)[:, :a.size].reshape((N_CHIPS,) + a.shape)
        outs.append(jnp.concatenate([seg[j] for j in range(N_CHIPS)], axis=ax))
    return outs


def _f_rms(row0, j, x, g):
    return _rms(x, g)


def _f_mid(row0, j, x, y, g_a, g_b):
    xn = x + _rms(y, g_a)
    return xn, _rms(xn, g_b)


def _f_resid(row0, j, x, y, g):
    return x + _rms(y, g)


def _f_relu2(row0, j, u):
    r = jnp.maximum(u, 0.0)
    return r * r


def _f_l2silu(row0, j, c):
    a = _silu(c)
    return a * lax.rsqrt(jnp.sum(a * a, axis=-1, keepdims=True) + EPS)


def _f_silu(row0, j, c):
    return _silu(c)


def _f_scale(row0, j, y, s):
    return y * s


def _f_glu(row0, j, a, gate):
    return a * _sigmoid(gate)


def _f_lnsilu(row0, j, u, g, b):
    mu = jnp.mean(u, axis=-1, keepdims=True)
    uc = u - mu
    return _silu(uc * lax.rsqrt(jnp.mean(uc * uc, axis=-1, keepdims=True) + EPS) * g + b)


def _f_outgate(row0, j, o, z, g):
    return _rms(o, g) * _silu(z)


def _make_gates(H):
    def f(row0, j, ba, alog, dt):
        lane = lax.broadcasted_iota(jnp.int32, ba.shape, 1)
        beta = _sigmoid(ba)
        g = -jnp.exp(alog) * _softplus(ba + dt)
        return jnp.where(lane < H, beta, jnp.where(lane < 2 * H, g, 0.0))
    return f


def _f_loss(row0, j, y, t):
    e = y - t
    loss = 0.5 * jnp.sum(jnp.mean(e * e, axis=-1, keepdims=True), axis=0, keepdims=True)
    return e * (1.0 / y.shape[-1]), jnp.broadcast_to(loss, (1, LANE))


def _lane_row(vec, start):
    return jnp.zeros((1, LANE), F32).at[0, start:start + vec.shape[0]].set(vec.astype(F32))


def kernel(x, norm_mix_pre, norm_mix_post, norm_mlp_pre, norm_mlp_post, even_w_in, even_conv, even_a_log, even_dt_bias, even_dn_norm, even_pool_w, even_pool_scale, even_w_out, odd_w_in, odd_dw, odd_dw_b, odd_ln_g, odd_ln_b, odd_w_out, mlp_w_up, mlp_w_down, loss_target, m_norm_mix_pre, m_norm_mix_post, m_norm_mlp_pre, m_norm_mlp_post, m_even_w_in, m_even_conv, m_even_a_log, m_even_dt_bias, m_even_dn_norm, m_even_pool_w, m_even_pool_scale, m_even_w_out, m_odd_w_in, m_odd_dw, m_odd_dw_b, m_odd_ln_g, m_odd_ln_b, m_odd_w_out, m_mlp_w_up, m_mlp_w_down, v_norm_mix_pre, v_norm_mix_post, v_norm_mlp_pre, v_norm_mlp_post, v_even_w_in, v_even_conv, v_even_a_log, v_even_dt_bias, v_even_dn_norm, v_even_pool_w, v_even_pool_scale, v_even_w_out, v_odd_w_in, v_odd_dw, v_odd_dw_b, v_odd_ln_g, v_odd_ln_b, v_odd_w_out, v_mlp_w_up, v_mlp_w_down):
    names = ["norm_mix_pre", "norm_mix_post", "norm_mlp_pre", "norm_mlp_post", "even_w_in", "even_conv", "even_a_log",
             "even_dt_bias", "even_dn_norm", "even_pool_w", "even_pool_scale", "even_w_out", "odd_w_in", "odd_dw",
             "odd_dw_b", "odd_ln_g", "odd_ln_b", "odd_w_out", "mlp_w_up", "mlp_w_down"]
    W = dict(zip(names, (norm_mix_pre, norm_mix_post, norm_mlp_pre, norm_mlp_post, even_w_in, even_conv, even_a_log,
                         even_dt_bias, even_dn_norm, even_pool_w, even_pool_scale, even_w_out, odd_w_in, odd_dw,
                         odd_dw_b, odd_ln_g, odd_ln_b, odd_w_out, mlp_w_up, mlp_w_down)))
    Mo = dict(zip(names, (m_norm_mix_pre, m_norm_mix_post, m_norm_mlp_pre, m_norm_mlp_post, m_even_w_in, m_even_conv,
                          m_even_a_log, m_even_dt_bias, m_even_dn_norm, m_even_pool_w, m_even_pool_scale, m_even_w_out,
                          m_odd_w_in, m_odd_dw, m_odd_dw_b, m_odd_ln_g, m_odd_ln_b, m_odd_w_out, m_mlp_w_up,
                          m_mlp_w_down)))
    Vo = dict(zip(names, (v_norm_mix_pre, v_norm_mix_post, v_norm_mlp_pre, v_norm_mlp_post, v_even_w_in, v_even_conv,
                          v_even_a_log, v_even_dt_bias, v_even_dn_norm, v_even_pool_w, v_even_pool_scale, v_even_w_out,
                          v_odd_w_in, v_odd_dw, v_odd_dw_b, v_odd_ln_g, v_odd_ln_b, v_odd_w_out, v_mlp_w_up,
                          v_mlp_w_down)))
    shard_axis = {"even_w_in": 2, "even_conv": 2, "even_pool_w": 2, "even_w_out": 1, "odd_w_in": 2, "odd_dw": 2,
                  "odd_dw_b": 1, "odd_ln_g": 1, "odd_ln_b": 1, "odd_w_out": 1, "mlp_w_up": 2, "mlp_w_down": 1}

    S, D = x.shape[1], x.shape[2]
    depth = norm_mix_pre.shape[0]
    H = even_a_log.shape[1]
    dh = even_dn_norm.shape[1]
    DNW = H * dh
    PW = even_pool_scale.shape[1]
    G = len(POOL_WINDOWS)
    PG = PW // G
    KC = even_conv.shape[1]
    BAW = 2 * LANE
    P_COLS = 4 * DNW + PW + BAW
    x2 = x.reshape(S, D)
    tgt = loss_target.reshape(S, D)

    big = ["even_w_in", "even_pool_w", "even_w_out", "odd_w_in", "odd_w_out", "mlp_w_up", "mlp_w_down"]
    small = ["even_conv", "odd_dw", "odd_dw_b", "odd_ln_g", "odd_ln_b"]
    full = dict(zip(big, _gather_weights("gather_big", [W[n] for n in big], [shard_axis[n] for n in big], BF16)))
    full.update(zip(small, _gather_weights("gather_small", [W[n] for n in small], [shard_axis[n] for n in small], F32)))
    CW = full["odd_w_out"].shape[1]
    KD = odd_dw.shape[1]

    def even_w_in_layout(w):
        o1 = 4 * DNW
        return jnp.concatenate([w[:, :o1], w[:, o1 + 2 * H:], w[:, o1:o1 + 2 * H],
                                jnp.zeros((w.shape[0], BAW - 2 * H), w.dtype)], axis=1)

    def even_w_in_unlayout(g):
        o1 = 4 * DNW
        return jnp.concatenate([g[:, :o1], g[:, o1 + PW:o1 + PW + 2 * H], g[:, o1:o1 + PW]], axis=1)

    def pool_blockdiag(pw):
        out = jnp.zeros((PW, PW), pw.dtype)
        for gi in range(G):
            out = out.at[gi * PG:(gi + 1) * PG, gi * PG:(gi + 1) * PG].set(pw[gi])
        return out

    pool_taps = max(POOL_WINDOWS)
    tap = jnp.arange(pool_taps)[:, None]
    win_c = jnp.repeat(jnp.asarray(POOL_WINDOWS, F32), PG)[None, :]
    pool_mask = (tap >= pool_taps - win_c).astype(F32)

    grads = {n: [None] * W[n].shape[0] for n in names}
    tr_full = 128 if D > 1024 else 256

    saved = []
    xc = x2
    for i in range(depth):
        jl = i // 2
        sv = {"x_in": xc}
        g1, g2, g3, g4 = (W[n][i:i + 1] for n in ("norm_mix_pre", "norm_mix_post", "norm_mlp_pre", "norm_mlp_post"))
        (h,) = _rowwise(f"l{i}_rms_in", _f_rms, [(xc, 0, D)], [(g1, None, D)], [(D, BF16)], S=S, tr=tr_full)
        sv["h"] = h
        if i % 2 == 0:
            w_in = even_w_in_layout(full["even_w_in"][jl])
            p = _matmul(f"l{i}_w_in", h, w_in, "nn", tn=768)
            conv_w = full["even_conv"][jl]
            c = _dwconv_fwd(f"l{i}_conv", p, 0, conv_w, S=S, C=3 * DNW)
            (qk,) = _rowwise(f"l{i}_qk", _f_l2silu, [(c, 0, dh)], [], [(dh, F32)], S=S, ncb=2 * H, tr=1024)
            (vv,) = _rowwise(f"l{i}_v", _f_silu, [(c, 2 * DNW // dh, dh)], [], [(dh, F32)], S=S, ncb=H, tr=1024)
            alog = _lane_row(W["even_a_log"][jl], H)
            dtb = _lane_row(W["even_dt_bias"][jl], H)
            ba_off = (4 * DNW + PW) // LANE
            (gb,) = _rowwise(f"l{i}_gates", _make_gates(H), [(p, ba_off, LANE)], [(alog, None, LANE), (dtb, None, LANE)],
                             [(LANE, F32)], S=S, tr=1024)
            Cn = min(DN_CHUNK, S)
            gT = gb[:, H:2 * H].reshape(S // Cn, Cn, H).transpose(0, 2, 1)
            o, sp, Tm = _delta_fwd(qk, vv, gb, gT, S=S, H=H, dh=dh)
            dn = W["even_dn_norm"][jl][None, :]
            (on,) = _rowwise(f"l{i}_outgate", _f_outgate, [(o, 0, dh), (p, 3 * DNW // dh, dh)], [(dn, None, dh)],
                             [(dh, BF16)], S=S, ncb=H, tr=1024)
            pcb = _pick(PW, 512)
            pooled = _dwconv_fwd(f"l{i}_pool", p, 4 * DNW // pcb, pool_mask, S=S, C=PW, win=win_c, out_dtype=BF16, cb=pcb)
            wbd = pool_blockdiag(full["even_pool_w"][jl])
            ypre = _matmul(f"l{i}_pool_w", pooled, wbd, "nn")
            psc = W["even_pool_scale"][jl][None, :]
            (ypool,) = _rowwise(f"l{i}_pool_scale", _f_scale, [(ypre, 0, PW)], [(psc, None, PW)], [(PW, BF16)], S=S)
            mixin = jnp.concatenate([on, ypool], axis=1)
            mix = _matmul(f"l{i}_w_out", mixin, full["even_w_out"][jl], "nn")
            sv.update(p=p, c=c, qk=qk, v=vv, gb=gb, gT=gT, o=o, sp=sp, Tm=Tm, pooled=pooled, ypre=ypre, mixin=mixin,
                      w_in=w_in, wbd=wbd, alog=alog, dtb=dtb, dn=dn, psc=psc, conv_w=conv_w)
        else:
            p = _matmul(f"l{i}_w_in", h, full["odd_w_in"][jl], "nn")
            ocb = _pick(CW, 1024)
            (u0,) = _rowwise(f"l{i}_glu", _f_glu, [(p, 0, ocb), (p, CW // ocb, ocb)], [], [(ocb, F32)], S=S,
                             ncb=CW // ocb)
            dw_w, dw_b = full["odd_dw"][jl], full["odd_dw_b"][jl][None, :]
            u1 = _dwconv_fwd(f"l{i}_dwconv", u0, 0, dw_w, S=S, C=CW, bias=dw_b)
            lg, lb = full["odd_ln_g"][jl][None, :], full["odd_ln_b"][jl][None, :]
            (u2,) = _rowwise(f"l{i}_lnsilu", _f_lnsilu, [(u1, 0, CW)], [(lg, None, CW), (lb, None, CW)], [(CW, BF16)],
                             S=S, tr=tr_full)
            mix = _matmul(f"l{i}_w_out", u2, full["odd_w_out"][jl], "nn")
            sv.update(p=p, u0=u0, u1=u1, mixin=u2, dw_w=dw_w, lg=lg, lb=lb)
        x_mid, h2 = _rowwise(f"l{i}_mid", _f_mid, [(xc, 0, D), (mix, 0, D)], [(g2, None, D), (g3, None, D)],
                             [(D, F32), (D, BF16)], S=S, tr=tr_full)
        u = _matmul(f"l{i}_w_up", h2, full["mlp_w_up"][i], "nn")
        FF = u.shape[1]
        (act,) = _rowwise(f"l{i}_relu2", _f_relu2, [(u.reshape(S * FF // D, D), 0, D)], [], [(D, BF16)], S=S * FF // D,
                          tr=tr_full)
        act = act.reshape(S, FF)
        ff = _matmul(f"l{i}_w_down", act, full["mlp_w_down"][i], "nn")
        (x_out,) = _rowwise(f"l{i}_out", _f_resid, [(x_mid, 0, D), (ff, 0, D)], [(g4, None, D)], [(D, F32)], S=S,
                            tr=tr_full)
        sv.update(mix=mix, h2=h2, u=u, act=act, ff=ff, g=(g1, g2, g3, g4))
        saved.append(sv)
        xc = x_out

    dy, loss_row = _rowwise("loss", _f_loss, [(xc, 0, D), (tgt, 0, D)], [], [(D, F32)], [(1, LANE, False)], S=S,
                            tr=tr_full)
    loss = lax.psum(loss_row[0, 0], ("x", "y", "c"))

    dx = dy
    for i in reversed(range(depth)):
        jl = i // 2
        sv = saved[i]
        g1, g2, g3, g4 = sv["g"]
        FF = sv["u"].shape[1]
        d_ff, dg4 = _rowwise_bwd(f"l{i}_out_b", _f_rms, [(sv["ff"], 0, D)], [(g4, None, D)], [(dx, 0, D)], [BF16],
                                 S=S, tr=tr_full)
        grads["norm_mlp_post"][i] = dg4[0]
        d_act = _matmul(f"l{i}_w_down_bx", d_ff, full["mlp_w_down"][i], "nt")
        grads["mlp_w_down"][i] = _matmul(f"l{i}_w_down_bw", sv["act"], d_ff, "tn")
        (du,) = _rowwise_bwd(f"l{i}_relu2_b", _f_relu2, [(sv["u"].reshape(S * FF // D, D), 0, D)], [],
                             [(d_act.reshape(S * FF // D, D), 0, D)], [BF16], S=S * FF // D, tr=tr_full)
        du = du.reshape(S, FF)
        dh2 = _matmul(f"l{i}_w_up_bx", du, full["mlp_w_up"][i], "nt")
        grads["mlp_w_up"][i] = _matmul(f"l{i}_w_up_bw", sv["h2"], du, "tn")
        dx, d_mix, dg2, dg3 = _rowwise_bwd(
            f"l{i}_mid_b", _f_mid, [(sv["x_in"], 0, D), (sv["mix"], 0, D)], [(g2, None, D), (g3, None, D)],
            [(dx, 0, D), (dh2, 0, D)], [F32, BF16], S=S, tr=tr_full)
        grads["norm_mix_post"][i], grads["norm_mlp_pre"][i] = dg2[0], dg3[0]
        if i % 2 == 0:
            d_mixin = _matmul(f"l{i}_w_out_bx", d_mix, full["even_w_out"][jl], "nt")
            grads["even_w_out"][jl] = _matmul(f"l{i}_w_out_bw", sv["mixin"], d_mix, "tn")
            p = sv["p"]
            pcb = _pick(PW, 512)
            d_ypre, dpsc = _rowwise_bwd(f"l{i}_pool_scale_b", _f_scale, [(sv["ypre"], 0, PW)], [(sv["psc"], None, PW)],
                                        [(d_mixin, DNW // PW, PW)], [BF16], S=S)
            grads["even_pool_scale"][jl] = dpsc[0]
            d_pooled = _matmul(f"l{i}_pool_w_bx", d_ypre, sv["wbd"], "nt")
            dwbd = _matmul(f"l{i}_pool_w_bw", sv["pooled"], d_ypre, "tn")
            grads["even_pool_w"][jl] = jnp.stack([dwbd[gi * PG:(gi + 1) * PG, gi * PG:(gi + 1) * PG] for gi in range(G)])
            d_xp = _dwconv_bwd(f"l{i}_pool_b", None, 0, d_pooled, pool_mask, S=S, C=PW, win=win_c, want_dw=False,
                               cb=pcb)[0]
            d_o, d_z, ddn = _rowwise_bwd(f"l{i}_outgate_b", _f_outgate, [(sv["o"], 0, dh), (p, 3 * DNW // dh, dh)],
                                         [(sv["dn"], None, dh)], [(d_mixin, 0, dh)], [F32, F32], S=S, ncb=H, tr=1024)
            grads["even_dn_norm"][jl] = ddn[0]
            dqk, dv, dgb = _delta_bwd(sv["qk"], sv["v"], sv["gb"], sv["gT"], sv["sp"], sv["Tm"], d_o, S=S, H=H, dh=dh)
            ba_off = (4 * DNW + PW) // LANE
            d_ba, dalog, ddtb = _rowwise_bwd(f"l{i}_gates_b", _make_gates(H), [(p, ba_off, LANE)],
                                             [(sv["alog"], None, LANE), (sv["dtb"], None, LANE)], [(dgb, 0, LANE)],
                                             [F32], S=S, tr=1024)
            grads["even_a_log"][jl], grads["even_dt_bias"][jl] = dalog[0, H:2 * H], ddtb[0, H:2 * H]
            (dc_qk,) = _rowwise_bwd(f"l{i}_qk_b", _f_l2silu, [(sv["c"], 0, dh)], [], [(dqk, 0, dh)], [F32], S=S,
                                    ncb=2 * H, tr=1024)
            (dc_v,) = _rowwise_bwd(f"l{i}_v_b", _f_silu, [(sv["c"], 2 * DNW // dh, dh)], [], [(dv, 0, dh)], [F32], S=S,
                                   ncb=H, tr=1024)
            dc = jnp.concatenate([dc_qk, dc_v], axis=1)
            d_qkv, dconv, _ = _dwconv_bwd(f"l{i}_conv_b", p, 0, dc, sv["conv_w"], S=S, C=3 * DNW)
            grads["even_conv"][jl] = dconv
            dp = jnp.concatenate([d_qkv.astype(BF16), d_z.astype(BF16), d_xp.astype(BF16), d_ba.astype(BF16),
                                  jnp.zeros((S, BAW - LANE), BF16)], axis=1)
            dh_ = _matmul(f"l{i}_w_in_bx", dp, sv["w_in"], "nt", tk=768)
            grads["even_w_in"][jl] = even_w_in_unlayout(_matmul(f"l{i}_w_in_bw", sv["h"], dp, "tn", tn=768))
        else:
            d_u2 = _matmul(f"l{i}_w_out_bx", d_mix, full["odd_w_out"][jl], "nt")
            grads["odd_w_out"][jl] = _matmul(f"l{i}_w_out_bw", sv["mixin"], d_mix, "tn")
            d_u1, dlg, dlb = _rowwise_bwd(f"l{i}_lnsilu_b", _f_lnsilu, [(sv["u1"], 0, CW)],
                                          [(sv["lg"], None, CW), (sv["lb"], None, CW)], [(d_u2, 0, CW)], [F32], S=S,
                                          tr=tr_full)
            grads["odd_ln_g"][jl], grads["odd_ln_b"][jl] = dlg[0], dlb[0]
            d_u0, ddw, ddb = _dwconv_bwd(f"l{i}_dwconv_b", sv["u0"], 0, d_u1, sv["dw_w"], S=S, C=CW)
            grads["odd_dw"][jl], grads["odd_dw_b"][jl] = ddw, ddb[0]
            p = sv["p"]
            ocb = _pick(CW, 1024)
            da, dgate = _rowwise_bwd(f"l{i}_glu_b", _f_glu, [(p, 0, ocb), (p, CW // ocb, ocb)], [], [(d_u0, 0, ocb)],
                                     [BF16, BF16], S=S, ncb=CW // ocb)
            dp = jnp.concatenate([da, dgate], axis=1)
            dh_ = _matmul(f"l{i}_w_in_bx", dp, full["odd_w_in"][jl], "nt")
            grads["odd_w_in"][jl] = _matmul(f"l{i}_w_in_bw", sv["h"], dp, "tn")
        dx, dg1 = _rowwise_bwd(f"l{i}_rms_in_b", _f_rms, [(sv["x_in"], 0, D)], [(g1, None, D)], [(dh_, 0, D)], [F32],
                               adds=[(dx, 0, D)], S=S, tr=tr_full)
        grads["norm_mix_pre"][i] = dg1[0]
    grad_x = dx.reshape(x.shape)

    gfull = {n: jnp.stack(grads[n]) for n in names}
    pieces = [[] for _ in range(N_CHIPS)]
    for n in names:
        ax = shard_axis.get(n)
        parts = jnp.split(gfull[n], N_CHIPS, axis=ax) if ax is not None else [gfull[n]] * N_CHIPS
        for jc in range(N_CHIPS):
            pieces[jc].append(parts[jc])
    packed = [_pack(pieces[jc], BF16, 2 * ADAM_ROWS) for jc in range(N_CHIPS)]
    offs = packed[0][1]
    gsend = jnp.stack([pk[0] for pk in packed])
    own, got = _sibling_split("grad_sibling_split", gsend)
    pair = _add_pairs("grad_add_cores", own, got, BF16)
    arrived = _chip_exchange("grad_chip_exchange", pair)
    ghalf = _sum_chips("grad_sum_chips", arrived)
    gsum = _sibling_join("grad_sibling_join", ghalf)

    outs_g, outs_d, outs_m, outs_v = [], [], [], []
    for n, off in zip(names, offs):
        g_o, d_o, m_o, v_o = _adamw(f"adamw_{n}", W[n], Mo[n], Vo[n], gsum, off)
        outs_g.append(g_o)
        outs_d.append(d_o)
        outs_m.append(m_o)
        outs_v.append(v_o)
    return (loss, grad_x, *outs_g, *outs_d, *outs_m, *outs_v)
```

```python
import functools
import math

import jax
import jax.numpy as jnp
from jax import lax
from jax.experimental import pallas as pl
from jax.experimental.pallas import tpu as pltpu

F32 = jnp.float32
BF16 = jnp.bfloat16
EPS = 1e-6
DN_CHUNK = 64
POOL_WINDOWS = (2, 4, 8, 16)
ADAM_LR, ADAM_B1, ADAM_B2, ADAM_EPS, ADAM_WD, ADAM_STEP = 0.001, 0.9, 0.999, 1e-08, 0.01, 10
LANE = 128
FLAT_L = 2048
ADAM_ROWS = 128
VMEM_LIMIT = 56 * 1024 * 1024
N_CHIPS = 4
HI = lax.Precision.HIGHEST
MESH = pl.DeviceIdType.MESH


def _cparams(sem):
    return pltpu.CompilerParams(dimension_semantics=sem, vmem_limit_bytes=VMEM_LIMIT)


def _pick(dim, target, mult=LANE):
    if dim <= target:
        return dim
    t = (target // mult) * mult
    while t >= mult:
        if dim % t == 0:
            return t
        t -= mult
    return dim


def _sigmoid(x):
    return 1.0 / (1.0 + jnp.exp(-x))


def _silu(x):
    return x * _sigmoid(x)


def _softplus(x):
    return jnp.maximum(x, 0.0) + jnp.log(1.0 + jnp.exp(-jnp.abs(x)))


def _rms(x, g):
    return x * lax.rsqrt(jnp.mean(x * x, axis=-1, keepdims=True) + EPS) * g


def _matmul(name, a, b, mode, out_dtype=F32, tm=1024, tn=1024, tk=2048, epi=None, extras=(), out_dtypes=None):
    if mode == "nn":
        (M, K), (K2, N) = a.shape, b.shape
    elif mode == "nt":
        (M, K), (N, K2) = a.shape, b.shape
    else:
        (K, M), (K2, N) = a.shape, b.shape
    assert K == K2, (name, a.shape, b.shape, mode)
    tm, tn, tk = _pick(M, tm), _pick(N, tn), _pick(K, tk)
    nk = K // tk
    if mode == "nn":
        a_spec = pl.BlockSpec((tm, tk), lambda i, j, k: (i, k))
        b_spec = pl.BlockSpec((tk, tn), lambda i, j, k: (k, j))
        dims = (((1,), (0,)), ((), ()))
    elif mode == "nt":
        a_spec = pl.BlockSpec((tm, tk), lambda i, j, k: (i, k))
        b_spec = pl.BlockSpec((tn, tk), lambda i, j, k: (j, k))
        dims = (((1,), (1,)), ((), ()))
    else:
        a_spec = pl.BlockSpec((tk, tm), lambda i, j, k: (k, i))
        b_spec = pl.BlockSpec((tk, tn), lambda i, j, k: (k, j))
        dims = (((0,), (0,)), ((), ()))
    out_dtypes = list(out_dtypes) if out_dtypes is not None else [out_dtype]
    ne, no = len(extras), len(out_dtypes)
    in_place = epi is None and out_dtypes == [F32]
    use_acc = nk > 1 and not in_place

    def finish(acc, extra_refs, out_refs):
        res = acc if epi is None else epi(acc, *[r[...] for r in extra_refs])
        res = res if isinstance(res, (tuple, list)) else (res,)
        for r, v in zip(out_refs, res):
            r[...] = v.astype(r.dtype)

    def body(a_ref, b_ref, *rest):
        extra_refs, out_refs = rest[:ne], rest[ne:ne + no]
        part = lax.dot_general(a_ref[...].astype(BF16), b_ref[...].astype(BF16), dims, preferred_element_type=F32)
        if nk == 1:
            finish(part, extra_refs, out_refs)
            return
        k = pl.program_id(2)
        acc_ref = rest[-1] if use_acc else out_refs[0]

        @pl.when(k == 0)
        def _():
            acc_ref[...] = part

        @pl.when(k > 0)
        def _():
            acc_ref[...] += part

        if use_acc:
            @pl.when(k == nk - 1)
            def _():
                finish(acc_ref[...], extra_refs, out_refs)

    o_spec = pl.BlockSpec((tm, tn), lambda i, j, k: (i, j))
    res = pl.pallas_call(
        body, name=name, grid=(M // tm, N // tn, nk),
        in_specs=[a_spec, b_spec] + [o_spec] * ne, out_specs=[o_spec] * no,
        out_shape=[jax.ShapeDtypeStruct((M, N), dt) for dt in out_dtypes],
        scratch_shapes=[pltpu.VMEM((tm, tn), F32)] if use_acc else [],
        compiler_params=_cparams(("parallel", "parallel", "arbitrary")),
    )(a, b, *extras)
    return res[0] if no == 1 else res


def _row_spec(tr, C, off):
    return pl.BlockSpec((tr, C), lambda j, i: (i, off + j))


def _par_spec(k, C, off):
    if off is None:
        return pl.BlockSpec((k, C), lambda j, i: (0, 0))
    return pl.BlockSpec((k, C), lambda j, i: (0, off + j))


def _rowwise(name, fn, rows, params, outs, reds=(), *, S, ncb=1, tr=256):
    tr = min(tr, S)
    nr, npar, no = len(rows), len(params), len(outs)

    def body(*refs):
        j, i = pl.program_id(0), pl.program_id(1)
        ins = [r[...].astype(F32) for r in refs[:nr + npar]]
        res = fn(i * tr, j, *ins)
        res = res if isinstance(res, (tuple, list)) else (res,)
        out_refs = refs[nr + npar:]
        for r, v in zip(out_refs[:no], res[:no]):
            r[...] = v.astype(r.dtype)
        for (k, C, per_j), r, v in zip(reds, out_refs[no:], res[no:]):
            first = (i == 0) if per_j else jnp.logical_and(i == 0, j == 0)

            @pl.when(first)
            def _(r=r):
                r[...] = jnp.zeros_like(r)

            r[...] += v

    in_specs = [_row_spec(tr, C, off) for (_, off, C) in rows] + [_par_spec(a.shape[0], C, off) for (a, off, C) in params]
    out_specs = [pl.BlockSpec((tr, C), lambda j, i: (i, j)) for (C, _) in outs]
    out_specs += [pl.BlockSpec((k, C), (lambda j, i: (0, j)) if per_j else (lambda j, i: (0, 0))) for (k, C, per_j) in reds]
    out_shape = [jax.ShapeDtypeStruct((S, ncb * C), dt) for (C, dt) in outs]
    out_shape += [jax.ShapeDtypeStruct((k, C * (ncb if per_j else 1)), F32) for (k, C, per_j) in reds]
    res = pl.pallas_call(
        body, name=name, grid=(ncb, S // tr), in_specs=in_specs, out_specs=out_specs, out_shape=out_shape,
        compiler_params=_cparams(("arbitrary", "arbitrary")),
    )(*[a for (a, _, _) in rows], *[a for (a, _, _) in params])
    return res


def _rowwise_bwd(name, fn, rows, params, cots, drow, adds=None, *, S, ncb=1, tr=128):
    tr = min(tr, S)
    nr, npar, nc = len(rows), len(params), len(cots)
    adds = adds or [None] * nr
    add_list = [a for a in adds if a is not None]
    na = len(add_list)

    def body(*refs):
        j, i = pl.program_id(0), pl.program_id(1)
        ins = [r[...].astype(F32) for r in refs[:nr + npar]]
        cts = [r[...].astype(F32) for r in refs[nr + npar:nr + npar + nc]]
        add_refs = list(refs[nr + npar + nc:nr + npar + nc + na])
        out_refs = list(refs[nr + npar + nc + na:])

        def f(*a):
            res = fn(i * tr, j, *a)
            return tuple(res) if isinstance(res, (tuple, list)) else (res,)

        _, vjp = jax.vjp(f, *ins)
        grads = vjp(tuple(cts))
        for idx in range(nr):
            if drow[idx] is None:
                continue
            g = grads[idx]
            if adds[idx] is not None:
                g = g + add_refs.pop(0)[...].astype(F32)
            r = out_refs.pop(0)
            r[...] = g.astype(r.dtype)
        for idx in range(npar):
            per_j = params[idx][1] is not None
            first = (i == 0) if per_j else jnp.logical_and(i == 0, j == 0)
            r = out_refs.pop(0)

            @pl.when(first)
            def _(r=r):
                r[...] = jnp.zeros_like(r)

            r[...] += grads[nr + idx]

    in_specs = [_row_spec(tr, C, off) for (_, off, C) in rows]
    in_specs += [_par_spec(a.shape[0], C, off) for (a, off, C) in params]
    in_specs += [_row_spec(tr, C, off) for (_, off, C) in cots]
    in_specs += [_row_spec(tr, C, off) for (_, off, C) in add_list]
    out_specs, out_shape = [], []
    for idx in range(nr):
        if drow[idx] is not None:
            C = rows[idx][2]
            out_specs.append(pl.BlockSpec((tr, C), lambda j, i: (i, j)))
            out_shape.append(jax.ShapeDtypeStruct((S, ncb * C), drow[idx]))
    for (a, off, C) in params:
        per_j = off is not None
        out_specs.append(pl.BlockSpec((a.shape[0], C), (lambda j, i: (0, j)) if per_j else (lambda j, i: (0, 0))))
        out_shape.append(jax.ShapeDtypeStruct((a.shape[0], C * (ncb if per_j else 1)), F32))
    return pl.pallas_call(
        body, name=name, grid=(ncb, S // tr), in_specs=in_specs, out_specs=out_specs, out_shape=out_shape,
        compiler_params=_cparams(("arbitrary", "arbitrary")),
    )(*[a for (a, _, _) in rows], *[a for (a, _, _) in params], *[a for (a, _, _) in cots], *[a for (a, _, _) in add_list])


def _halo_rows(K):
    return 8 * ((K - 1 + 7) // 8)


def _inv_count(row0, tr, win):
    t = (row0 + lax.broadcasted_iota(jnp.int32, (tr, 1), 0)).astype(F32)
    return 1.0 / jnp.minimum(t + 1.0, win)


def _dwconv_fwd(name, x, x_off, w, *, S, C, bias=None, win=None, out_dtype=F32, cb=512, tr=256):
    K = w.shape[0]
    cb, tr = _pick(C, cb), min(tr, S)
    HB = min(_halo_rows(K), tr)
    assert K - 1 <= HB and tr % HB == 0 and C % cb == 0
    nb = tr // HB
    extra = [a for a in (bias, win) if a is not None]

    def body(xh_ref, x_ref, w_ref, *rest):
        y_ref, xx = rest[-2], rest[-1]
        i = pl.program_id(1)
        xx[0:HB, :] = jnp.where(i > 0, xh_ref[...].astype(F32), 0.0)
        xx[HB:HB + tr, :] = x_ref[...].astype(F32)
        acc = jnp.zeros((tr, cb), F32)
        for jj in range(K):
            o = HB - (K - 1) + jj
            acc = acc + w_ref[jj:jj + 1, :] * xx[o:o + tr, :]
        if bias is not None:
            acc = acc + rest[0][...]
        if win is not None:
            acc = acc * _inv_count(i * tr, tr, rest[0][...]) - x_ref[...].astype(F32)
        y_ref[...] = acc.astype(y_ref.dtype)

    in_specs = [pl.BlockSpec((HB, cb), lambda j, i: (jnp.maximum(i * nb - 1, 0), x_off + j)),
                pl.BlockSpec((tr, cb), lambda j, i: (i, x_off + j)),
                pl.BlockSpec((K, cb), lambda j, i: (0, j))]
    in_specs += [pl.BlockSpec((1, cb), lambda j, i: (0, j)) for _ in extra]
    return pl.pallas_call(
        body, name=name, grid=(C // cb, S // tr), in_specs=in_specs,
        out_specs=pl.BlockSpec((tr, cb), lambda j, i: (i, j)),
        out_shape=jax.ShapeDtypeStruct((S, C), out_dtype),
        scratch_shapes=[pltpu.VMEM((HB + tr, cb), F32)],
        compiler_params=_cparams(("parallel", "arbitrary")),
    )(x, x, w, *extra)


def _dwconv_bwd(name, x, x_off, dy, w, *, S, C, win=None, want_dw=True, cb=512, tr=256):
    K = w.shape[0]
    cb, tr = _pick(C, cb), min(tr, S)
    HB = min(_halo_rows(K), tr)
    nb, nt = tr // HB, S // tr

    def body(*refs):
        if want_dw:
            xh_ref, x_ref, dy_ref, dyn_ref, w_ref = refs[:5]
            rest = refs[5:]
        else:
            dy_ref, dyn_ref, w_ref = refs[:3]
            rest = refs[3:]
        i = pl.program_id(1)
        dyt = dy_ref[...].astype(F32)
        dyn = jnp.where(i < nt - 1, dyn_ref[...].astype(F32), 0.0)
        if win is not None:
            win_v = rest[0][...]
            rest = rest[1:]
            yy_t = dyt * _inv_count(i * tr, tr, win_v)
            dyn = dyn * _inv_count((i + 1) * tr, HB, win_v)
        else:
            yy_t = dyt
        if want_dw:
            dx_ref, dw_ref, db_ref, yy, xx = rest
        else:
            dx_ref, yy = rest
        yy[0:tr, :] = yy_t
        yy[tr:tr + HB, :] = dyn
        acc = jnp.zeros((tr, cb), F32)
        for jj in range(K):
            o = K - 1 - jj
            acc = acc + w_ref[jj:jj + 1, :] * yy[o:o + tr, :]
        if win is not None:
            acc = acc - dyt
        dx_ref[...] = acc.astype(dx_ref.dtype)
        if want_dw:
            xx[0:HB, :] = jnp.where(i > 0, xh_ref[...].astype(F32), 0.0)
            xx[HB:HB + tr, :] = x_ref[...].astype(F32)

            @pl.when(i == 0)
            def _():
                dw_ref[...] = jnp.zeros_like(dw_ref)
                db_ref[...] = jnp.zeros_like(db_ref)

            for jj in range(K):
                o = HB - (K - 1) + jj
                dw_ref[jj:jj + 1, :] += jnp.sum(dyt * xx[o:o + tr, :], axis=0, keepdims=True)
            db_ref[...] += jnp.sum(dyt, axis=0, keepdims=True)

    last = S // HB - 1
    in_specs, args = [], []
    if want_dw:
        in_specs += [pl.BlockSpec((HB, cb), lambda j, i: (jnp.maximum(i * nb - 1, 0), x_off + j)),
                     pl.BlockSpec((tr, cb), lambda j, i: (i, x_off + j))]
        args += [x, x]
    in_specs += [pl.BlockSpec((tr, cb), lambda j, i: (i, j)),
                 pl.BlockSpec((HB, cb), lambda j, i: (jnp.minimum((i + 1) * nb, last), j)),
                 pl.BlockSpec((K, cb), lambda j, i: (0, j))]
    args += [dy, dy, w]
    if win is not None:
        in_specs.append(pl.BlockSpec((1, cb), lambda j, i: (0, j)))
        args.append(win)
    out_specs = [pl.BlockSpec((tr, cb), lambda j, i: (i, j))]
    out_shape = [jax.ShapeDtypeStruct((S, C), F32)]
    scratch = [pltpu.VMEM((tr + HB, cb), F32)]
    if want_dw:
        out_specs += [pl.BlockSpec((K, cb), lambda j, i: (0, j)), pl.BlockSpec((1, cb), lambda j, i: (0, j))]
        out_shape += [jax.ShapeDtypeStruct((K, C), F32), jax.ShapeDtypeStruct((1, C), F32)]
        scratch.append(pltpu.VMEM((HB + tr, cb), F32))
    return pl.pallas_call(
        body, name=name, grid=(C // cb, S // tr), in_specs=in_specs, out_specs=out_specs, out_shape=out_shape,
        scratch_shapes=scratch, compiler_params=_cparams(("parallel", "arbitrary")),
    )(*args)


def _dot(a, b, dims, hi=False):
    if hi:
        return lax.dot_general(a, b, (dims, ((), ())), precision=HI, preferred_element_type=F32)
    return lax.dot_general(a.astype(BF16), b.astype(BF16), (dims, ((), ())), preferred_element_type=F32)


_NN, _NT, _TN = ((1,), (0,)), ((1,), (1,)), ((0,), (0,))


def _col(m, idx):
    lane = lax.broadcasted_iota(jnp.int32, m.shape, 1)
    return jnp.sum(jnp.where(lane == idx, m, 0.0), axis=1, keepdims=True)


def _row(m, idx):
    sub = lax.broadcasted_iota(jnp.int32, m.shape, 0)
    return jnp.sum(jnp.where(sub == idx, m, 0.0), axis=0, keepdims=True)


def _delta_chunk(q, k, v, beta, gcc, gcr, causal, strict, eye, scale, C):
    d = {}
    gam = jnp.where(causal, jnp.exp(jnp.where(causal, gcc - gcr, 0.0)), 0.0)
    eg = jnp.exp(gcc)
    g_last = _row(gcc, C - 1)
    d["gam"], d["eg"], d["g_last"] = gam, eg, g_last
    d["ek"] = jnp.exp(g_last - gcc)
    d["decay"] = jnp.exp(g_last)
    qs = q * scale
    kb = k * beta
    d["qs"], d["kb"] = qs, kb
    d["kk"] = _dot(kb, k, _NT)
    d["A"] = jnp.where(strict, d["kk"] * gam, 0.0)
    d["qk"] = _dot(qs, k, _NT)
    d["attn"] = jnp.where(causal, d["qk"] * gam, 0.0)
    d["vb"] = v * beta
    d["kbg"] = kb * eg
    d["qg"] = qs * eg
    d["kd"] = k * d["ek"]
    return d


def _tri_inverse(A, eye):
    def split(m):
        hi = m.astype(BF16)
        return hi, (m - hi.astype(F32)).astype(BF16)

    def dot3(a, b):
        return _dot(a[0], b[0], _NN) + (_dot(a[0], b[1], _NN) + _dot(a[1], b[0], _NN))

    P = -A
    T = eye + P
    n = 1
    while 2 * n < A.shape[0]:
        Ps = split(P)
        P = dot3(Ps, Ps)
        T = T + dot3(split(T), split(P))
        n *= 2
    return T


def _delta_fwd(qk, v, gb, gT, *, S, H, dh):
    C = min(DN_CHUNK, S)
    N, W = S // C, H * dh
    scale = dh ** -0.5

    def body(qk_ref, v_ref, gb_ref, gT_ref, o_ref, sp_ref, T_ref, st):
        n = pl.program_id(0)

        @pl.when(n == 0)
        def _():
            st[...] = jnp.zeros_like(st)

        r = lax.broadcasted_iota(jnp.int32, (C, C), 0)
        c = lax.broadcasted_iota(jnp.int32, (C, C), 1)
        causal, strict = r >= c, r > c
        eye = (r == c).astype(F32)
        Lt = causal.astype(F32)
        gbv = gb_ref[...]
        gcum = _dot(Lt, gbv, _NN, hi=True)
        gcumT = _dot(gT_ref[0], Lt, _NT, hi=True)
        for h in range(H):
            sl = slice(h * dh, (h + 1) * dh)
            q, k, vv = qk_ref[:, sl], qk_ref[:, W + h * dh:W + (h + 1) * dh], v_ref[:, sl]
            d = _delta_chunk(q, k, vv, _col(gbv, h), _col(gcum, H + h), _row(gcumT, h), causal, strict, eye, scale, C)
            T = _tri_inverse(d["A"], eye)
            u = _dot(T, d["vb"], _NN)
            w = _dot(T, d["kbg"], _NN)
            s0 = st[h]
            sp_ref[0, h] = s0
            T_ref[0, h] = T
            v_new = u - _dot(w, s0, _NN)
            o_ref[:, sl] = _dot(d["qg"], s0, _NN) + _dot(d["attn"], v_new, _NN)
            st[h] = s0 * d["decay"] + _dot(d["kd"], v_new, _TN)

    return pl.pallas_call(
        body, name="delta_fwd", grid=(N,),
        in_specs=[pl.BlockSpec((C, 2 * W), lambda n: (n, 0)), pl.BlockSpec((C, W), lambda n: (n, 0)),
                  pl.BlockSpec((C, LANE), lambda n: (n, 0)), pl.BlockSpec((1, H, C), lambda n: (n, 0, 0))],
        out_specs=[pl.BlockSpec((C, W), lambda n: (n, 0)), pl.BlockSpec((1, H, dh, dh), lambda n: (n, 0, 0, 0)),
                   pl.BlockSpec((1, H, C, C), lambda n: (n, 0, 0, 0))],
        out_shape=[jax.ShapeDtypeStruct((S, W), F32), jax.ShapeDtypeStruct((N, H, dh, dh), F32),
                   jax.ShapeDtypeStruct((N, H, C, C), F32)],
        scratch_shapes=[pltpu.VMEM((H, dh, dh), F32)],
        compiler_params=_cparams(("arbitrary",)),
    )(qk, v, gb, gT)


def _delta_bwd(qk, v, gb, gT, sp, Tm, do, *, S, H, dh):
    C = min(DN_CHUNK, S)
    N, W = S // C, H * dh
    scale = dh ** -0.5

    def body(qk_ref, v_ref, gb_ref, gT_ref, sp_ref, T_ref, do_ref, dqk_ref, dv_ref, dgb_ref, ds):
        n = pl.program_id(0)

        @pl.when(n == 0)
        def _():
            ds[...] = jnp.zeros_like(ds)

        r = lax.broadcasted_iota(jnp.int32, (C, C), 0)
        c = lax.broadcasted_iota(jnp.int32, (C, C), 1)
        causal, strict = r >= c, r > c
        eye = (r == c).astype(F32)
        Lt = causal.astype(F32)
        ones = jnp.ones((C, LANE), F32)
        lane = lax.broadcasted_iota(jnp.int32, (C, LANE), 1)
        rowi = lax.broadcasted_iota(jnp.int32, (C, 1), 0)
        gbv = gb_ref[...]
        gcum = _dot(Lt, gbv, _NN, hi=True)
        gcumT = _dot(gT_ref[0], Lt, _NT, hi=True)
        dgc_all = jnp.zeros((C, LANE), F32)
        dbeta_all = jnp.zeros((C, LANE), F32)
        for h in range(H):
            sl = slice(h * dh, (h + 1) * dh)
            ksl = slice(W + h * dh, W + (h + 1) * dh)
            q, k, vv = qk_ref[:, sl], qk_ref[:, ksl], v_ref[:, sl]
            beta, gcc = _col(gbv, h), _col(gcum, H + h)
            d = _delta_chunk(q, k, vv, beta, gcc, _row(gcumT, h), causal, strict, eye, scale, C)
            T, s0, dO, dS = T_ref[0, h], sp_ref[0, h], do_ref[:, sl], ds[h]
            u = _dot(T, d["vb"], _NN)
            w = _dot(T, d["kbg"], _NN)
            v_new = u - _dot(w, s0, _NN)
            dv_new = _dot(d["attn"], dO, _TN) + _dot(d["kd"], dS, _NN)
            dattn = jnp.where(causal, _dot(dO, v_new, _NT), 0.0)
            dqg = _dot(dO, s0, _NT)
            dkd = _dot(v_new, dS, _NT)
            ddecay = jnp.sum(jnp.sum(s0 * dS, axis=1, keepdims=True), axis=0, keepdims=True)
            ds[h] = _dot(d["qg"], dO, _TN) + d["decay"] * dS - _dot(w, dv_new, _TN)
            dw = -_dot(dv_new, s0, _NT)
            dT = _dot(dv_new, d["vb"], _NT) + _dot(dw, d["kbg"], _NT)
            dvb = _dot(T, dv_new, _TN)
            dkbg = _dot(T, dw, _TN)
            dA = jnp.where(strict, -_dot(_dot(T, dT, _TN, hi=True), T, _NT, hi=True), 0.0)
            dkk = dA * d["gam"]
            dqk_m = dattn * d["gam"]
            m = dA * d["A"] + dattn * d["attn"]
            dkb = _dot(dkk, k, _NN) + dkbg * d["eg"]
            dk = _dot(dkk, d["kb"], _TN) + _dot(dqk_m, d["qs"], _TN) + dkd * d["ek"] + dkb * beta
            dqs = _dot(dqk_m, k, _NN) + dqg * d["eg"]
            r_kd = jnp.sum(dkd * d["kd"], axis=1, keepdims=True)
            dgc = (jnp.sum(m, axis=1, keepdims=True) - jnp.max(_dot(m, ones, _TN, hi=True), axis=1, keepdims=True)
                   + jnp.sum(dqg * d["qg"], axis=1, keepdims=True) - r_kd
                   + jnp.sum(dkbg * d["kbg"], axis=1, keepdims=True))
            dg_last = jnp.sum(r_kd, axis=0, keepdims=True) + ddecay * d["decay"]
            dgc = dgc + jnp.where(rowi == C - 1, dg_last, 0.0)
            dbeta = jnp.sum(dkb * k, axis=1, keepdims=True) + jnp.sum(dvb * vv, axis=1, keepdims=True)
            dqk_ref[:, sl] = dqs * scale
            dqk_ref[:, ksl] = dk
            dv_ref[:, sl] = dvb * beta
            dgc_all = dgc_all + jnp.where(lane == H + h, dgc, 0.0)
            dbeta_all = dbeta_all + jnp.where(lane == h, dbeta, 0.0)
        dgb_ref[...] = _dot(Lt, dgc_all, _TN, hi=True) + dbeta_all

    rev = lambda n: N - 1 - n
    return pl.pallas_call(
        body, name="delta_bwd", grid=(N,),
        in_specs=[pl.BlockSpec((C, 2 * W), lambda n: (rev(n), 0)), pl.BlockSpec((C, W), lambda n: (rev(n), 0)),
                  pl.BlockSpec((C, LANE), lambda n: (rev(n), 0)), pl.BlockSpec((1, H, C), lambda n: (rev(n), 0, 0)),
                  pl.BlockSpec((1, H, dh, dh), lambda n: (rev(n), 0, 0, 0)),
                  pl.BlockSpec((1, H, C, C), lambda n: (rev(n), 0, 0, 0)),
                  pl.BlockSpec((C, W), lambda n: (rev(n), 0))],
        out_specs=[pl.BlockSpec((C, 2 * W), lambda n: (rev(n), 0)), pl.BlockSpec((C, W), lambda n: (rev(n), 0)),
                   pl.BlockSpec((C, LANE), lambda n: (rev(n), 0))],
        out_shape=[jax.ShapeDtypeStruct((S, 2 * W), F32), jax.ShapeDtypeStruct((S, W), F32),
                   jax.ShapeDtypeStruct((S, LANE), F32)],
        scratch_shapes=[pltpu.VMEM((H, dh, dh), F32)],
        compiler_params=_cparams(("arbitrary",)),
    )(qk, v, gb, gT, sp, Tm, do)


_ANY = pl.BlockSpec(memory_space=pl.ANY)


ICI_CHUNKS = 4
D2D_CHUNKS = 4
LOCAL_CHUNKS = 8


def _place():
    return lax.axis_index("x"), lax.axis_index("y"), lax.axis_index("c")


def _row_chunks(rows, n):
    n = max(1, min(n, rows // 8))
    while n > 1 and (rows % n or (rows // n) % 8):
        n -= 1
    return [(k * (rows // n), rows // n) for k in range(n)]


def _allgather_chips(name, flat):
    R, L = flat.shape
    Rh = R // 2
    ici = _row_chunks(Rh, ICI_CHUNKS)
    sub = _row_chunks(ici[0][1], D2D_CHUNKS)
    loc = _row_chunks(R, LOCAL_CHUNKS)
    ni, ns = len(ici), len(sub)

    def body(x_ref, out_ref, send_sems, recv_sems, local_sems):
        x, y, c = _place()
        sibling = (x, y, 1 - c)
        chips = [(1 - x, y), (x, 1 - y), (1 - x, 1 - y)]

        def rows(px, py, pc, r0, n):
            return out_ref.at[2 * px + py, pl.ds(pc * Rh + r0, n), :]

        def copy(k, src, dst, to):
            return pltpu.make_async_remote_copy(src_ref=src, dst_ref=dst, send_sem=send_sems.at[k],
                                                recv_sem=recv_sems.at[k], device_id=to, device_id_type=MESH)

        mine = [pltpu.make_async_copy(x_ref.at[pl.ds(r0, n), :], out_ref.at[2 * x + y, pl.ds(r0, n), :], local_sems.at[q])
                for q, (r0, n) in enumerate(loc)]
        for cp in mine:
            cp.start()
        first = []
        for k, chip in enumerate(chips):
            for q, (r0, n) in enumerate(ici):
                first.append(copy(k * ni + q, x_ref.at[pl.ds(c * Rh + r0, n), :], rows(x, y, c, r0, n), (*chip, c)))
        for cp in first:
            cp.start()
        passed = []
        for k, chip in enumerate(chips):
            for q, (r0, n) in enumerate(ici):
                copy(k * ni + q, x_ref.at[pl.ds(r0, n), :], rows(*chip, c, r0, n), (*chip, c)).wait_recv()
                for t, (s0, m) in enumerate(sub):
                    cp = copy(3 * ni + (k * ni + q) * ns + t, rows(*chip, c, r0 + s0, m), rows(*chip, c, r0 + s0, m), sibling)
                    cp.start()
                    passed.append(cp)
        for k, chip in enumerate(chips):
            for q, (r0, n) in enumerate(ici):
                for t, (s0, m) in enumerate(sub):
                    copy(3 * ni + (k * ni + q) * ns + t, x_ref.at[pl.ds(r0, m), :], rows(*chip, 1 - c, r0 + s0, m),
                         sibling).wait_recv()
        for cp in first + passed:
            cp.wait_send()
        for cp in mine:
            cp.wait()

    nsem = 3 * ni * (1 + ns)
    return pl.pallas_call(
        body, name=name, in_specs=[_ANY], out_specs=_ANY,
        out_shape=jax.ShapeDtypeStruct((N_CHIPS, R, L), flat.dtype),
        scratch_shapes=[pltpu.SemaphoreType.DMA((nsem,)), pltpu.SemaphoreType.DMA((nsem,)),
                        pltpu.SemaphoreType.DMA((len(loc),))],
    )(flat)


def _sibling_split(name, g):
    _, R, L = g.shape
    Rh = R // 2

    chunks = [(j, r0, n) for j in range(N_CHIPS) for (r0, n) in _row_chunks(Rh, D2D_CHUNKS)]

    def body(g_ref, own_ref, got_ref, send_sems, recv_sems, local_sems):
        x, y, c = _place()
        keep = [pltpu.make_async_copy(g_ref.at[j, pl.ds(c * Rh + r0, n), :], own_ref.at[j, pl.ds(r0, n), :], local_sems.at[k])
                for k, (j, r0, n) in enumerate(chunks)]
        cps = [pltpu.make_async_remote_copy(src_ref=g_ref.at[j, pl.ds((1 - c) * Rh + r0, n), :],
                                            dst_ref=got_ref.at[j, pl.ds(r0, n), :], send_sem=send_sems.at[k],
                                            recv_sem=recv_sems.at[k], device_id=(x, y, 1 - c), device_id_type=MESH)
               for k, (j, r0, n) in enumerate(chunks)]
        for cp in cps + keep:
            cp.start()
        for cp in cps + keep:
            cp.wait()

    shp = jax.ShapeDtypeStruct((N_CHIPS, Rh, L), g.dtype)
    sems = pltpu.SemaphoreType.DMA((len(chunks),))
    return pl.pallas_call(
        body, name=name, in_specs=[_ANY], out_specs=[_ANY, _ANY], out_shape=[shp, shp],
        scratch_shapes=[sems, sems, sems],
    )(g)


def _chip_exchange(name, p):
    Rh = p.shape[1]
    ici = _row_chunks(Rh, ICI_CHUNKS)
    loc = _row_chunks(Rh, LOCAL_CHUNKS)
    ni = len(ici)

    def body(p_ref, q_ref, send_sems, recv_sems, local_sems):
        x, y, c = _place()
        me = 2 * x + y
        chips = [(1 - x, y), (x, 1 - y), (1 - x, 1 - y)]
        keep = [pltpu.make_async_copy(p_ref.at[me, pl.ds(r0, n), :], q_ref.at[me, pl.ds(r0, n), :], local_sems.at[q])
                for q, (r0, n) in enumerate(loc)]
        cps = [pltpu.make_async_remote_copy(src_ref=p_ref.at[2 * cx + cy, pl.ds(r0, n), :],
                                            dst_ref=q_ref.at[me, pl.ds(r0, n), :],
                                            send_sem=send_sems.at[k * ni + q], recv_sem=recv_sems.at[k * ni + q],
                                            device_id=(cx, cy, c), device_id_type=MESH)
               for k, (cx, cy) in enumerate(chips) for q, (r0, n) in enumerate(ici)]
        for cp in cps + keep:
            cp.start()
        for k, (cx, cy) in enumerate(chips):
            for q, (r0, n) in enumerate(ici):
                pltpu.make_async_remote_copy(src_ref=p_ref.at[me, pl.ds(r0, n), :],
                                             dst_ref=q_ref.at[2 * cx + cy, pl.ds(r0, n), :],
                                             send_sem=send_sems.at[k * ni + q], recv_sem=recv_sems.at[k * ni + q],
                                             device_id=(cx, cy, c), device_id_type=MESH).wait_recv()
        for cp in cps:
            cp.wait_send()
        for cp in keep:
            cp.wait()

    return pl.pallas_call(
        body, name=name, in_specs=[_ANY], out_specs=_ANY, out_shape=jax.ShapeDtypeStruct(p.shape, p.dtype),
        scratch_shapes=[pltpu.SemaphoreType.DMA((3 * ni,)), pltpu.SemaphoreType.DMA((3 * ni,)),
                        pltpu.SemaphoreType.DMA((len(loc),))],
    )(p)


def _sibling_join(name, half):
    Rh, L = half.shape

    chunks = _row_chunks(Rh, 2 * D2D_CHUNKS)

    def body(h_ref, out_ref, send_sems, recv_sems, local_sems):
        x, y, c = _place()
        keep = [pltpu.make_async_copy(h_ref.at[pl.ds(r0, n), :], out_ref.at[pl.ds(c * Rh + r0, n), :], local_sems.at[k])
                for k, (r0, n) in enumerate(chunks)]
        cps = [pltpu.make_async_remote_copy(src_ref=h_ref.at[pl.ds(r0, n), :], dst_ref=out_ref.at[pl.ds(c * Rh + r0, n), :],
                                            send_sem=send_sems.at[k], recv_sem=recv_sems.at[k], device_id=(x, y, 1 - c),
                                            device_id_type=MESH)
               for k, (r0, n) in enumerate(chunks)]
        for cp in cps + keep:
            cp.start()
        for k, (r0, n) in enumerate(chunks):
            pltpu.make_async_remote_copy(src_ref=h_ref.at[pl.ds(r0, n), :],
                                         dst_ref=out_ref.at[pl.ds((1 - c) * Rh + r0, n), :], send_sem=send_sems.at[k],
                                         recv_sem=recv_sems.at[k], device_id=(x, y, 1 - c), device_id_type=MESH).wait_recv()
        for cp in cps:
            cp.wait_send()
        for cp in keep:
            cp.wait()

    sems = pltpu.SemaphoreType.DMA((len(chunks),))
    return pl.pallas_call(
        body, name=name, in_specs=[_ANY], out_specs=_ANY, out_shape=jax.ShapeDtypeStruct((2 * Rh, L), half.dtype),
        scratch_shapes=[sems, sems, sems],
    )(half)


def _add_pairs(name, a, b, out_dtype):
    n, Rh, L = a.shape
    tr = _pick(Rh, 512, 8)

    def body(a_ref, b_ref, o_ref):
        o_ref[...] = (a_ref[...].astype(F32) + b_ref[...].astype(F32)).astype(o_ref.dtype)

    spec = pl.BlockSpec((1, tr, L), lambda j, i: (j, i, 0))
    return pl.pallas_call(body, name=name, grid=(n, Rh // tr), in_specs=[spec, spec], out_specs=spec,
                          out_shape=jax.ShapeDtypeStruct(a.shape, out_dtype),
                          compiler_params=_cparams(("parallel", "parallel")))(a, b)


def _sum_chips(name, q):
    n, Rh, L = q.shape
    tr = _pick(Rh, 512, 8)

    def body(q_ref, o_ref):
        acc = q_ref[0].astype(F32)
        for s in range(1, n):
            acc = acc + q_ref[s].astype(F32)
        o_ref[...] = acc

    return pl.pallas_call(body, name=name, grid=(Rh // tr,),
                          in_specs=[pl.BlockSpec((n, tr, L), lambda i: (0, i, 0))],
                          out_specs=pl.BlockSpec((tr, L), lambda i: (i, 0)),
                          out_shape=jax.ShapeDtypeStruct((Rh, L), F32),
                          compiler_params=_cparams(("parallel",)))(q)


def _adamw(name, w, m, v, gflat, row0):
    shape, size = w.shape, w.size
    rows = -(-size // FLAT_L)
    rows_p = -(-rows // ADAM_ROWS) * ADAM_ROWS
    pad = rows_p * FLAT_L - size

    def flat2d(a):
        a = a.reshape(-1)
        if pad:
            a = jnp.pad(a, (0, pad), constant_values=1.0)
        return a.reshape(rows_p, FLAT_L)

    blk0 = row0 // ADAM_ROWS
    c1 = 1.0 / (1.0 - ADAM_B1 ** ADAM_STEP)
    c2 = 1.0 / (1.0 - ADAM_B2 ** ADAM_STEP)

    def body(w_ref, m_ref, v_ref, g_ref, go_ref, d_ref, mo_ref, vo_ref):
        g = g_ref[...]
        wv = w_ref[...]
        mn = ADAM_B1 * m_ref[...] + (1.0 - ADAM_B1) * g
        vn = ADAM_B2 * v_ref[...] + (1.0 - ADAM_B2) * (g * g)
        go_ref[...] = g
        mo_ref[...] = mn
        vo_ref[...] = vn
        d_ref[...] = -ADAM_LR * ((mn * c1) / (jnp.sqrt(vn * c2) + ADAM_EPS) + ADAM_WD * wv)

    spec = pl.BlockSpec((ADAM_ROWS, FLAT_L), lambda i: (i, 0))
    shp = jax.ShapeDtypeStruct((rows_p, FLAT_L), F32)
    outs = pl.pallas_call(
        body, name=name, grid=(rows_p // ADAM_ROWS,),
        in_specs=[spec, spec, spec, pl.BlockSpec((ADAM_ROWS, FLAT_L), lambda i: (blk0 + i, 0))],
        out_specs=[spec] * 4, out_shape=[shp] * 4, compiler_params=_cparams(("parallel",)),
    )(flat2d(w), flat2d(m), flat2d(v), gflat)

    def back(a):
        a = a.reshape(-1)
        if pad:
            a = a[:size]
        return a.reshape(shape)

    return tuple(back(a) for a in outs)


def _seg_rows(size):
    rows = -(-size // FLAT_L)
    return -(-rows // ADAM_ROWS) * ADAM_ROWS


def _pack(pieces, dtype, row_mult):
    segs, offs, r = [], [], 0
    for a in pieces:
        rows = _seg_rows(a.size)
        flat = a.reshape(-1).astype(dtype)
        flat = jnp.pad(flat, (0, rows * FLAT_L - a.size))
        segs.append(flat.reshape(rows, FLAT_L))
        offs.append(r)
        r += rows
    tail = -r % row_mult
    if tail:
        segs.append(jnp.zeros((tail, FLAT_L), dtype))
    return jnp.concatenate(segs, axis=0), offs


def _gather_weights(name, shards, axes, dtype):
    flat, offs = _pack(shards, dtype, 2 * 8)
    full = _allgather_chips(name, flat)
    outs = []
    for a, ax, off in zip(shards, axes, offs):
        rows = _seg_rows(a.size)
        seg = full[:, off:off + rows].reshape(N_CHIPS, -1)[:, :a.size].reshape((N_CHIPS,) + a.shape)
        outs.append(jnp.concatenate([seg[j] for j in range(N_CHIPS)], axis=ax))
    return outs


def _f_rms(row0, j, x, g):
    return _rms(x, g)


def _f_mid(row0, j, x, y, g_a, g_b):
    xn = x + _rms(y, g_a)
    return xn, _rms(xn, g_b)


def _f_resid(row0, j, x, y, g):
    return x + _rms(y, g)


def _relu2(u):
    r = jnp.maximum(u, 0.0)
    return r * r


def _relu2_bwd(d_act, act):
    return d_act * (2.0 * jnp.sqrt(act.astype(F32)))


def _f_l2silu(row0, j, c):
    a = _silu(c)
    return a * lax.rsqrt(jnp.sum(a * a, axis=-1, keepdims=True) + EPS)


def _f_silu(row0, j, c):
    return _silu(c)


def _f_scale(row0, j, y, s):
    return y * s


def _f_glu(row0, j, a, gate):
    return a * _sigmoid(gate)


def _f_lnsilu(row0, j, u, g, b):
    mu = jnp.mean(u, axis=-1, keepdims=True)
    uc = u - mu
    return _silu(uc * lax.rsqrt(jnp.mean(uc * uc, axis=-1, keepdims=True) + EPS) * g + b)


def _f_outgate(row0, j, o, z, g):
    return _rms(o, g) * _silu(z)


def _make_gates(H):
    def f(row0, j, ba, alog, dt):
        lane = lax.broadcasted_iota(jnp.int32, ba.shape, 1)
        beta = _sigmoid(ba)
        g = -jnp.exp(alog) * _softplus(ba + dt)
        return jnp.where(lane < H, beta, jnp.where(lane < 2 * H, g, 0.0))
    return f


def _f_loss(row0, j, y, t):
    e = y - t
    loss = 0.5 * jnp.sum(jnp.mean(e * e, axis=-1, keepdims=True), axis=0, keepdims=True)
    return e * (1.0 / y.shape[-1]), jnp.broadcast_to(loss, (1, LANE))


def _lane_row(vec, start):
    return jnp.zeros((1, LANE), F32).at[0, start:start + vec.shape[0]].set(vec.astype(F32))


def kernel(x, norm_mix_pre, norm_mix_post, norm_mlp_pre, norm_mlp_post, even_w_in, even_conv, even_a_log, even_dt_bias, even_dn_norm, even_pool_w, even_pool_scale, even_w_out, odd_w_in, odd_dw, odd_dw_b, odd_ln_g, odd_ln_b, odd_w_out, mlp_w_up, mlp_w_down, loss_target, m_norm_mix_pre, m_norm_mix_post, m_norm_mlp_pre, m_norm_mlp_post, m_even_w_in, m_even_conv, m_even_a_log, m_even_dt_bias, m_even_dn_norm, m_even_pool_w, m_even_pool_scale, m_even_w_out, m_odd_w_in, m_odd_dw, m_odd_dw_b, m_odd_ln_g, m_odd_ln_b, m_odd_w_out, m_mlp_w_up, m_mlp_w_down, v_norm_mix_pre, v_norm_mix_post, v_norm_mlp_pre, v_norm_mlp_post, v_even_w_in, v_even_conv, v_even_a_log, v_even_dt_bias, v_even_dn_norm, v_even_pool_w, v_even_pool_scale, v_even_w_out, v_odd_w_in, v_odd_dw, v_odd_dw_b, v_odd_ln_g, v_odd_ln_b, v_odd_w_out, v_mlp_w_up, v_mlp_w_down):
    names = ["norm_mix_pre", "norm_mix_post", "norm_mlp_pre", "norm_mlp_post", "even_w_in", "even_conv", "even_a_log",
             "even_dt_bias", "even_dn_norm", "even_pool_w", "even_pool_scale", "even_w_out", "odd_w_in", "odd_dw",
             "odd_dw_b", "odd_ln_g", "odd_ln_b", "odd_w_out", "mlp_w_up", "mlp_w_down"]
    W = dict(zip(names, (norm_mix_pre, norm_mix_post, norm_mlp_pre, norm_mlp_post, even_w_in, even_conv, even_a_log,
                         even_dt_bias, even_dn_norm, even_pool_w, even_pool_scale, even_w_out, odd_w_in, odd_dw,
                         odd_dw_b, odd_ln_g, odd_ln_b, odd_w_out, mlp_w_up, mlp_w_down)))
    Mo = dict(zip(names, (m_norm_mix_pre, m_norm_mix_post, m_norm_mlp_pre, m_norm_mlp_post, m_even_w_in, m_even_conv,
                          m_even_a_log, m_even_dt_bias, m_even_dn_norm, m_even_pool_w, m_even_pool_scale, m_even_w_out,
                          m_odd_w_in, m_odd_dw, m_odd_dw_b, m_odd_ln_g, m_odd_ln_b, m_odd_w_out, m_mlp_w_up,
                          m_mlp_w_down)))
    Vo = dict(zip(names, (v_norm_mix_pre, v_norm_mix_post, v_norm_mlp_pre, v_norm_mlp_post, v_even_w_in, v_even_conv,
                          v_even_a_log, v_even_dt_bias, v_even_dn_norm, v_even_pool_w, v_even_pool_scale, v_even_w_out,
                          v_odd_w_in, v_odd_dw, v_odd_dw_b, v_odd_ln_g, v_odd_ln_b, v_odd_w_out, v_mlp_w_up,
                          v_mlp_w_down)))
    shard_axis = {"even_w_in": 2, "even_conv": 2, "even_pool_w": 2, "even_w_out": 1, "odd_w_in": 2, "odd_dw": 2,
                  "odd_dw_b": 1, "odd_ln_g": 1, "odd_ln_b": 1, "odd_w_out": 1, "mlp_w_up": 2, "mlp_w_down": 1}

    S, D = x.shape[1], x.shape[2]
    depth = norm_mix_pre.shape[0]
    H = even_a_log.shape[1]
    dh = even_dn_norm.shape[1]
    DNW = H * dh
    PW = even_pool_scale.shape[1]
    G = len(POOL_WINDOWS)
    PG = PW // G
    KC = even_conv.shape[1]
    BAW = 2 * LANE
    P_COLS = 4 * DNW + PW + BAW
    x2 = x.reshape(S, D)
    tgt = loss_target.reshape(S, D)

    big = ["even_w_in", "even_pool_w", "even_w_out", "odd_w_in", "odd_w_out", "mlp_w_up", "mlp_w_down"]
    small = ["even_conv", "odd_dw", "odd_dw_b", "odd_ln_g", "odd_ln_b"]
    full = dict(zip(big, _gather_weights("gather_big", [W[n] for n in big], [shard_axis[n] for n in big], BF16)))
    full.update(zip(small, _gather_weights("gather_small", [W[n] for n in small], [shard_axis[n] for n in small], F32)))
    CW = full["odd_w_out"].shape[1]
    KD = odd_dw.shape[1]

    def even_w_in_layout(w):
        o1 = 4 * DNW
        return jnp.concatenate([w[:, :o1], w[:, o1 + 2 * H:], w[:, o1:o1 + 2 * H],
                                jnp.zeros((w.shape[0], BAW - 2 * H), w.dtype)], axis=1)

    def even_w_in_unlayout(g):
        o1 = 4 * DNW
        return jnp.concatenate([g[:, :o1], g[:, o1 + PW:o1 + PW + 2 * H], g[:, o1:o1 + PW]], axis=1)

    def pool_blockdiag(pw):
        out = jnp.zeros((PW, PW), pw.dtype)
        for gi in range(G):
            out = out.at[gi * PG:(gi + 1) * PG, gi * PG:(gi + 1) * PG].set(pw[gi])
        return out

    pool_taps = max(POOL_WINDOWS)
    tap = jnp.arange(pool_taps)[:, None]
    win_c = jnp.repeat(jnp.asarray(POOL_WINDOWS, F32), PG)[None, :]
    pool_mask = (tap >= pool_taps - win_c).astype(F32)

    grads = {n: [None] * W[n].shape[0] for n in names}
    tr_full = 128 if D > 1024 else 256

    saved = []
    xc = x2
    for i in range(depth):
        jl = i // 2
        sv = {"x_in": xc}
        g1, g2, g3, g4 = (W[n][i:i + 1] for n in ("norm_mix_pre", "norm_mix_post", "norm_mlp_pre", "norm_mlp_post"))
        (h,) = _rowwise(f"l{i}_rms_in", _f_rms, [(xc, 0, D)], [(g1, None, D)], [(D, BF16)], S=S, tr=tr_full)
        sv["h"] = h
        if i % 2 == 0:
            w_in = even_w_in_layout(full["even_w_in"][jl])
            p = _matmul(f"l{i}_w_in", h, w_in, "nn", tn=768)
            conv_w = full["even_conv"][jl]
            c = _dwconv_fwd(f"l{i}_conv", p, 0, conv_w, S=S, C=3 * DNW)
            (qk,) = _rowwise(f"l{i}_qk", _f_l2silu, [(c, 0, dh)], [], [(dh, F32)], S=S, ncb=2 * H, tr=1024)
            (vv,) = _rowwise(f"l{i}_v", _f_silu, [(c, 2 * DNW // dh, dh)], [], [(dh, F32)], S=S, ncb=H, tr=1024)
            alog = _lane_row(W["even_a_log"][jl], H)
            dtb = _lane_row(W["even_dt_bias"][jl], H)
            ba_off = (4 * DNW + PW) // LANE
            (gb,) = _rowwise(f"l{i}_gates", _make_gates(H), [(p, ba_off, LANE)], [(alog, None, LANE), (dtb, None, LANE)],
                             [(LANE, F32)], S=S, tr=1024)
            Cn = min(DN_CHUNK, S)
            gT = gb[:, H:2 * H].reshape(S // Cn, Cn, H).transpose(0, 2, 1)
            o, sp, Tm = _delta_fwd(qk, vv, gb, gT, S=S, H=H, dh=dh)
            dn = W["even_dn_norm"][jl][None, :]
            (on,) = _rowwise(f"l{i}_outgate", _f_outgate, [(o, 0, dh), (p, 3 * DNW // dh, dh)], [(dn, None, dh)],
                             [(dh, BF16)], S=S, ncb=H, tr=1024)
            pcb = _pick(PW, 512)
            pooled = _dwconv_fwd(f"l{i}_pool", p, 4 * DNW // pcb, pool_mask, S=S, C=PW, win=win_c, out_dtype=BF16, cb=pcb)
            wbd = pool_blockdiag(full["even_pool_w"][jl])
            ypre = _matmul(f"l{i}_pool_w", pooled, wbd, "nn")
            psc = W["even_pool_scale"][jl][None, :]
            (ypool,) = _rowwise(f"l{i}_pool_scale", _f_scale, [(ypre, 0, PW)], [(psc, None, PW)], [(PW, BF16)], S=S)
            mixin = jnp.concatenate([on, ypool], axis=1)
            mix = _matmul(f"l{i}_w_out", mixin, full["even_w_out"][jl], "nn")
            sv.update(p=p, c=c, qk=qk, v=vv, gb=gb, gT=gT, o=o, sp=sp, Tm=Tm, pooled=pooled, ypre=ypre, mixin=mixin,
                      w_in=w_in, wbd=wbd, alog=alog, dtb=dtb, dn=dn, psc=psc, conv_w=conv_w)
        else:
            p = _matmul(f"l{i}_w_in", h, full["odd_w_in"][jl], "nn")
            ocb = _pick(CW, 1024)
            (u0,) = _rowwise(f"l{i}_glu", _f_glu, [(p, 0, ocb), (p, CW // ocb, ocb)], [], [(ocb, F32)], S=S,
                             ncb=CW // ocb)
            dw_w, dw_b = full["odd_dw"][jl], full["odd_dw_b"][jl][None, :]
            u1 = _dwconv_fwd(f"l{i}_dwconv", u0, 0, dw_w, S=S, C=CW, bias=dw_b)
            lg, lb = full["odd_ln_g"][jl][None, :], full["odd_ln_b"][jl][None, :]
            (u2,) = _rowwise(f"l{i}_lnsilu", _f_lnsilu, [(u1, 0, CW)], [(lg, None, CW), (lb, None, CW)], [(CW, BF16)],
                             S=S, tr=tr_full)
            mix = _matmul(f"l{i}_w_out", u2, full["odd_w_out"][jl], "nn")
            sv.update(p=p, u0=u0, u1=u1, mixin=u2, dw_w=dw_w, lg=lg, lb=lb)
        x_mid, h2 = _rowwise(f"l{i}_mid", _f_mid, [(xc, 0, D), (mix, 0, D)], [(g2, None, D), (g3, None, D)],
                             [(D, F32), (D, BF16)], S=S, tr=tr_full)
        act = _matmul(f"l{i}_w_up", h2, full["mlp_w_up"][i], "nn", epi=_relu2, out_dtypes=[BF16])
        ff = _matmul(f"l{i}_w_down", act, full["mlp_w_down"][i], "nn")
        (x_out,) = _rowwise(f"l{i}_out", _f_resid, [(x_mid, 0, D), (ff, 0, D)], [(g4, None, D)], [(D, F32)], S=S,
                            tr=tr_full)
        sv.update(mix=mix, h2=h2, act=act, ff=ff, g=(g1, g2, g3, g4))
        saved.append(sv)
        xc = x_out

    dy, loss_row = _rowwise("loss", _f_loss, [(xc, 0, D), (tgt, 0, D)], [], [(D, F32)], [(1, LANE, False)], S=S,
                            tr=tr_full)
    loss = lax.psum(loss_row[0, 0], ("x", "y", "c"))

    dx = dy
    for i in reversed(range(depth)):
        jl = i // 2
        sv = saved[i]
        g1, g2, g3, g4 = sv["g"]
        d_ff, dg4 = _rowwise_bwd(f"l{i}_out_b", _f_rms, [(sv["ff"], 0, D)], [(g4, None, D)], [(dx, 0, D)], [BF16],
                                 S=S, tr=tr_full)
        grads["norm_mlp_post"][i] = dg4[0]
        du = _matmul(f"l{i}_w_down_bx", d_ff, full["mlp_w_down"][i], "nt", epi=_relu2_bwd, extras=[sv["act"]],
                     out_dtypes=[BF16])
        grads["mlp_w_down"][i] = _matmul(f"l{i}_w_down_bw", sv["act"], d_ff, "tn")
        dh2 = _matmul(f"l{i}_w_up_bx", du, full["mlp_w_up"][i], "nt")
        grads["mlp_w_up"][i] = _matmul(f"l{i}_w_up_bw", sv["h2"], du, "tn")
        dx, d_mix, dg2, dg3 = _rowwise_bwd(
            f"l{i}_mid_b", _f_mid, [(sv["x_in"], 0, D), (sv["mix"], 0, D)], [(g2, None, D), (g3, None, D)],
            [(dx, 0, D), (dh2, 0, D)], [F32, BF16], S=S, tr=tr_full)
        grads["norm_mix_post"][i], grads["norm_mlp_pre"][i] = dg2[0], dg3[0]
        if i % 2 == 0:
            d_mixin = _matmul(f"l{i}_w_out_bx", d_mix, full["even_w_out"][jl], "nt")
            grads["even_w_out"][jl] = _matmul(f"l{i}_w_out_bw", sv["mixin"], d_mix, "tn")
            p = sv["p"]
            pcb = _pick(PW, 512)
            d_ypre, dpsc = _rowwise_bwd(f"l{i}_pool_scale_b", _f_scale, [(sv["ypre"], 0, PW)], [(sv["psc"], None, PW)],
                                        [(d_mixin, DNW // PW, PW)], [BF16], S=S)
            grads["even_pool_scale"][jl] = dpsc[0]
            d_pooled = _matmul(f"l{i}_pool_w_bx", d_ypre, sv["wbd"], "nt")
            dwbd = _matmul(f"l{i}_pool_w_bw", sv["pooled"], d_ypre, "tn")
            grads["even_pool_w"][jl] = jnp.stack([dwbd[gi * PG:(gi + 1) * PG, gi * PG:(gi + 1) * PG] for gi in range(G)])
            d_xp = _dwconv_bwd(f"l{i}_pool_b", None, 0, d_pooled, pool_mask, S=S, C=PW, win=win_c, want_dw=False,
                               cb=pcb)[0]
            d_o, d_z, ddn = _rowwise_bwd(f"l{i}_outgate_b", _f_outgate, [(sv["o"], 0, dh), (p, 3 * DNW // dh, dh)],
                                         [(sv["dn"], None, dh)], [(d_mixin, 0, dh)], [F32, F32], S=S, ncb=H, tr=1024)
            grads["even_dn_norm"][jl] = ddn[0]
            dqk, dv, dgb = _delta_bwd(sv["qk"], sv["v"], sv["gb"], sv["gT"], sv["sp"], sv["Tm"], d_o, S=S, H=H, dh=dh)
            ba_off = (4 * DNW + PW) // LANE
            d_ba, dalog, ddtb = _rowwise_bwd(f"l{i}_gates_b", _make_gates(H), [(p, ba_off, LANE)],
                                             [(sv["alog"], None, LANE), (sv["dtb"], None, LANE)], [(dgb, 0, LANE)],
                                             [F32], S=S, tr=1024)
            grads["even_a_log"][jl], grads["even_dt_bias"][jl] = dalog[0, H:2 * H], ddtb[0, H:2 * H]
            (dc_qk,) = _rowwise_bwd(f"l{i}_qk_b", _f_l2silu, [(sv["c"], 0, dh)], [], [(dqk, 0, dh)], [F32], S=S,
                                    ncb=2 * H, tr=1024)
            (dc_v,) = _rowwise_bwd(f"l{i}_v_b", _f_silu, [(sv["c"], 2 * DNW // dh, dh)], [], [(dv, 0, dh)], [F32], S=S,
                                   ncb=H, tr=1024)
            dc = jnp.concatenate([dc_qk, dc_v], axis=1)
            d_qkv, dconv, _ = _dwconv_bwd(f"l{i}_conv_b", p, 0, dc, sv["conv_w"], S=S, C=3 * DNW)
            grads["even_conv"][jl] = dconv
            dp = jnp.concatenate([d_qkv.astype(BF16), d_z.astype(BF16), d_xp.astype(BF16), d_ba.astype(BF16),
                                  jnp.zeros((S, BAW - LANE), BF16)], axis=1)
            dh_ = _matmul(f"l{i}_w_in_bx", dp, sv["w_in"], "nt", tk=768)
            grads["even_w_in"][jl] = even_w_in_unlayout(_matmul(f"l{i}_w_in_bw", sv["h"], dp, "tn", tn=768))
        else:
            d_u2 = _matmul(f"l{i}_w_out_bx", d_mix, full["odd_w_out"][jl], "nt")
            grads["odd_w_out"][jl] = _matmul(f"l{i}_w_out_bw", sv["mixin"], d_mix, "tn")
            d_u1, dlg, dlb = _rowwise_bwd(f"l{i}_lnsilu_b", _f_lnsilu, [(sv["u1"], 0, CW)],
                                          [(sv["lg"], None, CW), (sv["lb"], None, CW)], [(d_u2, 0, CW)], [F32], S=S,
                                          tr=tr_full)
            grads["odd_ln_g"][jl], grads["odd_ln_b"][jl] = dlg[0], dlb[0]
            d_u0, ddw, ddb = _dwconv_bwd(f"l{i}_dwconv_b", sv["u0"], 0, d_u1, sv["dw_w"], S=S, C=CW)
            grads["odd_dw"][jl], grads["odd_dw_b"][jl] = ddw, ddb[0]
            p = sv["p"]
            ocb = _pick(CW, 1024)
            da, dgate = _rowwise_bwd(f"l{i}_glu_b", _f_glu, [(p, 0, ocb), (p, CW // ocb, ocb)], [], [(d_u0, 0, ocb)],
                                     [BF16, BF16], S=S, ncb=CW // ocb)
            dp = jnp.concatenate([da, dgate], axis=1)
            dh_ = _matmul(f"l{i}_w_in_bx", dp, full["odd_w_in"][jl], "nt")
            grads["odd_w_in"][jl] = _matmul(f"l{i}_w_in_bw", sv["h"], dp, "tn")
        dx, dg1 = _rowwise_bwd(f"l{i}_rms_in_b", _f_rms, [(sv["x_in"], 0, D)], [(g1, None, D)], [(dh_, 0, D)], [F32],
                               adds=[(dx, 0, D)], S=S, tr=tr_full)
        grads["norm_mix_pre"][i] = dg1[0]
    grad_x = dx.reshape(x.shape)

    gfull = {n: jnp.stack(grads[n]) for n in names}
    pieces = [[] for _ in range(N_CHIPS)]
    for n in names:
        ax = shard_axis.get(n)
        parts = jnp.split(gfull[n], N_CHIPS, axis=ax) if ax is not None else [gfull[n]] * N_CHIPS
        for jc in range(N_CHIPS):
            pieces[jc].append(parts[jc])
    packed = [_pack(pieces[jc], BF16, 2 * ADAM_ROWS) for jc in range(N_CHIPS)]
    offs = packed[0][1]
    gsend = jnp.stack([pk[0] for pk in packed])
    own, got = _sibling_split("grad_sibling_split", gsend)
    pair = _add_pairs("grad_add_cores", own, got, BF16)
    arrived = _chip_exchange("grad_chip_exchange", pair)
    ghalf = _sum_chips("grad_sum_chips", arrived)
    gsum = _sibling_join("grad_sibling_join", ghalf)

    outs_g, outs_d, outs_m, outs_v = [], [], [], []
    for n, off in zip(names, offs):
        g_o, d_o, m_o, v_o = _adamw(f"adamw_{n}", W[n], Mo[n], Vo[n], gsum, off)
        outs_g.append(g_o)
        outs_d.append(d_o)
        outs_m.append(m_o)
        outs_v.append(v_o)
    return (loss, grad_x, *outs_g, *outs_d, *outs_m, *outs_v)
```

```python
import functools
import math

import jax
import jax.numpy as jnp
from jax import lax
from jax.experimental import pallas as pl
from jax.experimental.pallas import tpu as pltpu

F32 = jnp.float32
BF16 = jnp.bfloat16
EPS = 1e-6
DN_CHUNK = 64
POOL_WINDOWS = (2, 4, 8, 16)
ADAM_LR, ADAM_B1, ADAM_B2, ADAM_EPS, ADAM_WD, ADAM_STEP = 0.001, 0.9, 0.999, 1e-08, 0.01, 10
LANE = 128
FLAT_L = 2048
ADAM_ROWS = 128
VMEM_LIMIT = 56 * 1024 * 1024
N_CHIPS = 4
HI = lax.Precision.HIGHEST
MESH = pl.DeviceIdType.MESH


def _cparams(sem):
    return pltpu.CompilerParams(dimension_semantics=sem, vmem_limit_bytes=VMEM_LIMIT)


def _pick(dim, target, mult=LANE):
    if dim <= target:
        return dim
    t = (target // mult) * mult
    while t >= mult:
        if dim % t == 0:
            return t
        t -= mult
    return dim


def _sigmoid(x):
    return 1.0 / (1.0 + jnp.exp(-x))


def _silu(x):
    return x * _sigmoid(x)


def _softplus(x):
    return jnp.maximum(x, 0.0) + jnp.log(1.0 + jnp.exp(-jnp.abs(x)))


def _rms(x, g):
    return x * lax.rsqrt(jnp.mean(x * x, axis=-1, keepdims=True) + EPS) * g


def _matmul(name, a, b, mode, out_dtype=F32, tm=1024, tn=1024, tk=2048, epi=None, extras=(), out_dtypes=None):
    if mode == "nn":
        (M, K), (K2, N) = a.shape, b.shape
    elif mode == "nt":
        (M, K), (N, K2) = a.shape, b.shape
    else:
        (K, M), (K2, N) = a.shape, b.shape
    assert K == K2, (name, a.shape, b.shape, mode)
    tm, tn, tk = _pick(M, tm), _pick(N, tn), _pick(K, tk)
    nk = K // tk
    if mode == "nn":
        a_spec = pl.BlockSpec((tm, tk), lambda i, j, k: (i, k))
        b_spec = pl.BlockSpec((tk, tn), lambda i, j, k: (k, j))
        dims = (((1,), (0,)), ((), ()))
    elif mode == "nt":
        a_spec = pl.BlockSpec((tm, tk), lambda i, j, k: (i, k))
        b_spec = pl.BlockSpec((tn, tk), lambda i, j, k: (j, k))
        dims = (((1,), (1,)), ((), ()))
    else:
        a_spec = pl.BlockSpec((tk, tm), lambda i, j, k: (k, i))
        b_spec = pl.BlockSpec((tk, tn), lambda i, j, k: (k, j))
        dims = (((0,), (0,)), ((), ()))
    out_dtypes = list(out_dtypes) if out_dtypes is not None else [out_dtype]
    ne, no = len(extras), len(out_dtypes)
    in_place = epi is None and out_dtypes == [F32]
    use_acc = nk > 1 and not in_place

    def finish(acc, extra_refs, out_refs):
        res = acc if epi is None else epi(acc, *[r[...] for r in extra_refs])
        res = res if isinstance(res, (tuple, list)) else (res,)
        for r, v in zip(out_refs, res):
            r[...] = v.astype(r.dtype)

    def body(a_ref, b_ref, *rest):
        extra_refs, out_refs = rest[:ne], rest[ne:ne + no]
        part = lax.dot_general(a_ref[...].astype(BF16), b_ref[...].astype(BF16), dims, preferred_element_type=F32)
        if nk == 1:
            finish(part, extra_refs, out_refs)
            return
        k = pl.program_id(2)
        acc_ref = rest[-1] if use_acc else out_refs[0]

        @pl.when(k == 0)
        def _():
            acc_ref[...] = part

        @pl.when(k > 0)
        def _():
            acc_ref[...] += part

        if use_acc:
            @pl.when(k == nk - 1)
            def _():
                finish(acc_ref[...], extra_refs, out_refs)

    o_spec = pl.BlockSpec((tm, tn), lambda i, j, k: (i, j))
    res = pl.pallas_call(
        body, name=name, grid=(M // tm, N // tn, nk),
        in_specs=[a_spec, b_spec] + [o_spec] * ne, out_specs=[o_spec] * no,
        out_shape=[jax.ShapeDtypeStruct((M, N), dt) for dt in out_dtypes],
        scratch_shapes=[pltpu.VMEM((tm, tn), F32)] if use_acc else [],
        compiler_params=_cparams(("parallel", "parallel", "arbitrary")),
    )(a, b, *extras)
    return res[0] if no == 1 else res


def _row_spec(tr, C, off):
    return pl.BlockSpec((tr, C), lambda j, i: (i, off + j))


def _par_spec(k, C, off):
    if off is None:
        return pl.BlockSpec((k, C), lambda j, i: (0, 0))
    return pl.BlockSpec((k, C), lambda j, i: (0, off + j))


def _rowwise(name, fn, rows, params, outs, reds=(), *, S, ncb=1, tr=256):
    tr = min(tr, S)
    nr, npar, no = len(rows), len(params), len(outs)

    def body(*refs):
        j, i = pl.program_id(0), pl.program_id(1)
        ins = [r[...].astype(F32) for r in refs[:nr + npar]]
        res = fn(i * tr, j, *ins)
        res = res if isinstance(res, (tuple, list)) else (res,)
        out_refs = refs[nr + npar:]
        for r, v in zip(out_refs[:no], res[:no]):
            r[...] = v.astype(r.dtype)
        for (k, C, per_j), r, v in zip(reds, out_refs[no:], res[no:]):
            first = (i == 0) if per_j else jnp.logical_and(i == 0, j == 0)

            @pl.when(first)
            def _(r=r):
                r[...] = jnp.zeros_like(r)

            r[...] += v

    in_specs = [_row_spec(tr, C, off) for (_, off, C) in rows] + [_par_spec(a.shape[0], C, off) for (a, off, C) in params]
    out_specs = [pl.BlockSpec((tr, C), lambda j, i: (i, j)) for (C, _) in outs]
    out_specs += [pl.BlockSpec((k, C), (lambda j, i: (0, j)) if per_j else (lambda j, i: (0, 0))) for (k, C, per_j) in reds]
    out_shape = [jax.ShapeDtypeStruct((S, ncb * C), dt) for (C, dt) in outs]
    out_shape += [jax.ShapeDtypeStruct((k, C * (ncb if per_j else 1)), F32) for (k, C, per_j) in reds]
    res = pl.pallas_call(
        body, name=name, grid=(ncb, S // tr), in_specs=in_specs, out_specs=out_specs, out_shape=out_shape,
        compiler_params=_cparams(("arbitrary", "arbitrary")),
    )(*[a for (a, _, _) in rows], *[a for (a, _, _) in params])
    return res


def _rowwise_bwd(name, fn, rows, params, cots, drow, adds=None, *, S, ncb=1, tr=128):
    tr = min(tr, S)
    nr, npar, nc = len(rows), len(params), len(cots)
    adds = adds or [None] * nr
    add_list = [a for a in adds if a is not None]
    na = len(add_list)

    def body(*refs):
        j, i = pl.program_id(0), pl.program_id(1)
        ins = [r[...].astype(F32) for r in refs[:nr + npar]]
        cts = [r[...].astype(F32) for r in refs[nr + npar:nr + npar + nc]]
        add_refs = list(refs[nr + npar + nc:nr + npar + nc + na])
        out_refs = list(refs[nr + npar + nc + na:])

        def f(*a):
            res = fn(i * tr, j, *a)
            return tuple(res) if isinstance(res, (tuple, list)) else (res,)

        _, vjp = jax.vjp(f, *ins)
        grads = vjp(tuple(cts))
        for idx in range(nr):
            if drow[idx] is None:
                continue
            g = grads[idx]
            if adds[idx] is not None:
                g = g + add_refs.pop(0)[...].astype(F32)
            r = out_refs.pop(0)
            r[...] = g.astype(r.dtype)
        for idx in range(npar):
            per_j = params[idx][1] is not None
            first = (i == 0) if per_j else jnp.logical_and(i == 0, j == 0)
            r = out_refs.pop(0)

            @pl.when(first)
            def _(r=r):
                r[...] = jnp.zeros_like(r)

            r[...] += grads[nr + idx]

    in_specs = [_row_spec(tr, C, off) for (_, off, C) in rows]
    in_specs += [_par_spec(a.shape[0], C, off) for (a, off, C) in params]
    in_specs += [_row_spec(tr, C, off) for (_, off, C) in cots]
    in_specs += [_row_spec(tr, C, off) for (_, off, C) in add_list]
    out_specs, out_shape = [], []
    for idx in range(nr):
        if drow[idx] is not None:
            C = rows[idx][2]
            out_specs.append(pl.BlockSpec((tr, C), lambda j, i: (i, j)))
            out_shape.append(jax.ShapeDtypeStruct((S, ncb * C), drow[idx]))
    for (a, off, C) in params:
        per_j = off is not None
        out_specs.append(pl.BlockSpec((a.shape[0], C), (lambda j, i: (0, j)) if per_j else (lambda j, i: (0, 0))))
        out_shape.append(jax.ShapeDtypeStruct((a.shape[0], C * (ncb if per_j else 1)), F32))
    return pl.pallas_call(
        body, name=name, grid=(ncb, S // tr), in_specs=in_specs, out_specs=out_specs, out_shape=out_shape,
        compiler_params=_cparams(("arbitrary", "arbitrary")),
    )(*[a for (a, _, _) in rows], *[a for (a, _, _) in params], *[a for (a, _, _) in cots], *[a for (a, _, _) in add_list])


def _halo_rows(K):
    return 8 * ((K - 1 + 7) // 8)


def _inv_count(row0, tr, win):
    t = (row0 + lax.broadcasted_iota(jnp.int32, (tr, 1), 0)).astype(F32)
    return 1.0 / jnp.minimum(t + 1.0, win)


def _dwconv_fwd(name, x, x_off, w, *, S, C, bias=None, win=None, out_dtype=F32, cb=512, tr=256):
    K = w.shape[0]
    cb, tr = _pick(C, cb), min(tr, S)
    HB = min(_halo_rows(K), tr)
    assert K - 1 <= HB and tr % HB == 0 and C % cb == 0
    nb = tr // HB
    extra = [a for a in (bias, win) if a is not None]

    def body(xh_ref, x_ref, w_ref, *rest):
        y_ref, xx = rest[-2], rest[-1]
        i = pl.program_id(1)
        xx[0:HB, :] = jnp.where(i > 0, xh_ref[...].astype(F32), 0.0)
        xx[HB:HB + tr, :] = x_ref[...].astype(F32)
        acc = jnp.zeros((tr, cb), F32)
        for jj in range(K):
            o = HB - (K - 1) + jj
            acc = acc + w_ref[jj:jj + 1, :] * xx[o:o + tr, :]
        if bias is not None:
            acc = acc + rest[0][...]
        if win is not None:
            acc = acc * _inv_count(i * tr, tr, rest[0][...]) - x_ref[...].astype(F32)
        y_ref[...] = acc.astype(y_ref.dtype)

    in_specs = [pl.BlockSpec((HB, cb), lambda j, i: (jnp.maximum(i * nb - 1, 0), x_off + j)),
                pl.BlockSpec((tr, cb), lambda j, i: (i, x_off + j)),
                pl.BlockSpec((K, cb), lambda j, i: (0, j))]
    in_specs += [pl.BlockSpec((1, cb), lambda j, i: (0, j)) for _ in extra]
    return pl.pallas_call(
        body, name=name, grid=(C // cb, S // tr), in_specs=in_specs,
        out_specs=pl.BlockSpec((tr, cb), lambda j, i: (i, j)),
        out_shape=jax.ShapeDtypeStruct((S, C), out_dtype),
        scratch_shapes=[pltpu.VMEM((HB + tr, cb), F32)],
        compiler_params=_cparams(("parallel", "arbitrary")),
    )(x, x, w, *extra)


def _dwconv_bwd(name, x, x_off, dy, w, *, S, C, win=None, want_dw=True, cb=512, tr=256):
    K = w.shape[0]
    cb, tr = _pick(C, cb), min(tr, S)
    HB = min(_halo_rows(K), tr)
    nb, nt = tr // HB, S // tr

    def body(*refs):
        if want_dw:
            xh_ref, x_ref, dy_ref, dyn_ref, w_ref = refs[:5]
            rest = refs[5:]
        else:
            dy_ref, dyn_ref, w_ref = refs[:3]
            rest = refs[3:]
        i = pl.program_id(1)
        dyt = dy_ref[...].astype(F32)
        dyn = jnp.where(i < nt - 1, dyn_ref[...].astype(F32), 0.0)
        if win is not None:
            win_v = rest[0][...]
            rest = rest[1:]
            yy_t = dyt * _inv_count(i * tr, tr, win_v)
            dyn = dyn * _inv_count((i + 1) * tr, HB, win_v)
        else:
            yy_t = dyt
        if want_dw:
            dx_ref, dw_ref, db_ref, yy, xx = rest
        else:
            dx_ref, yy = rest
        yy[0:tr, :] = yy_t
        yy[tr:tr + HB, :] = dyn
        acc = jnp.zeros((tr, cb), F32)
        for jj in range(K):
            o = K - 1 - jj
            acc = acc + w_ref[jj:jj + 1, :] * yy[o:o + tr, :]
        if win is not None:
            acc = acc - dyt
        dx_ref[...] = acc.astype(dx_ref.dtype)
        if want_dw:
            xx[0:HB, :] = jnp.where(i > 0, xh_ref[...].astype(F32), 0.0)
            xx[HB:HB + tr, :] = x_ref[...].astype(F32)

            @pl.when(i == 0)
            def _():
                dw_ref[...] = jnp.zeros_like(dw_ref)
                db_ref[...] = jnp.zeros_like(db_ref)

            for jj in range(K):
                o = HB - (K - 1) + jj
                dw_ref[jj:jj + 1, :] += jnp.sum(dyt * xx[o:o + tr, :], axis=0, keepdims=True)
            db_ref[...] += jnp.sum(dyt, axis=0, keepdims=True)

    last = S // HB - 1
    in_specs, args = [], []
    if want_dw:
        in_specs += [pl.BlockSpec((HB, cb), lambda j, i: (jnp.maximum(i * nb - 1, 0), x_off + j)),
                     pl.BlockSpec((tr, cb), lambda j, i: (i, x_off + j))]
        args += [x, x]
    in_specs += [pl.BlockSpec((tr, cb), lambda j, i: (i, j)),
                 pl.BlockSpec((HB, cb), lambda j, i: (jnp.minimum((i + 1) * nb, last), j)),
                 pl.BlockSpec((K, cb), lambda j, i: (0, j))]
    args += [dy, dy, w]
    if win is not None:
        in_specs.append(pl.BlockSpec((1, cb), lambda j, i: (0, j)))
        args.append(win)
    out_specs = [pl.BlockSpec((tr, cb), lambda j, i: (i, j))]
    out_shape = [jax.ShapeDtypeStruct((S, C), F32)]
    scratch = [pltpu.VMEM((tr + HB, cb), F32)]
    if want_dw:
        out_specs += [pl.BlockSpec((K, cb), lambda j, i: (0, j)), pl.BlockSpec((1, cb), lambda j, i: (0, j))]
        out_shape += [jax.ShapeDtypeStruct((K, C), F32), jax.ShapeDtypeStruct((1, C), F32)]
        scratch.append(pltpu.VMEM((HB + tr, cb), F32))
    return pl.pallas_call(
        body, name=name, grid=(C // cb, S // tr), in_specs=in_specs, out_specs=out_specs, out_shape=out_shape,
        scratch_shapes=scratch, compiler_params=_cparams(("parallel", "arbitrary")),
    )(*args)


def _dot(a, b, dims, hi=False):
    if hi:
        return lax.dot_general(a, b, (dims, ((), ())), precision=HI, preferred_element_type=F32)
    return lax.dot_general(a.astype(BF16), b.astype(BF16), (dims, ((), ())), preferred_element_type=F32)


_NN, _NT, _TN = ((1,), (0,)), ((1,), (1,)), ((0,), (0,))


def _col(m, idx):
    lane = lax.broadcasted_iota(jnp.int32, m.shape, 1)
    return jnp.sum(jnp.where(lane == idx, m, 0.0), axis=1, keepdims=True)


def _row(m, idx):
    sub = lax.broadcasted_iota(jnp.int32, m.shape, 0)
    return jnp.sum(jnp.where(sub == idx, m, 0.0), axis=0, keepdims=True)


def _delta_chunk(q, k, v, beta, gcc, gcr, causal, strict, eye, scale, C):
    d = {}
    gam = jnp.where(causal, jnp.exp(jnp.where(causal, gcc - gcr, 0.0)), 0.0)
    eg = jnp.exp(gcc)
    g_last = _row(gcc, C - 1)
    d["gam"], d["eg"], d["g_last"] = gam, eg, g_last
    d["ek"] = jnp.exp(g_last - gcc)
    d["decay"] = jnp.exp(g_last)
    qs = q * scale
    kb = k * beta
    d["qs"], d["kb"] = qs, kb
    d["kk"] = _dot(kb, k, _NT)
    d["A"] = jnp.where(strict, d["kk"] * gam, 0.0)
    d["qk"] = _dot(qs, k, _NT)
    d["attn"] = jnp.where(causal, d["qk"] * gam, 0.0)
    d["vb"] = v * beta
    d["kbg"] = kb * eg
    d["qg"] = qs * eg
    d["kd"] = k * d["ek"]
    return d


def _tri_inverse(A, eye):
    def split(m):
        hi = m.astype(BF16)
        return hi, (m - hi.astype(F32)).astype(BF16)

    def dot3(a, b):
        return _dot(a[0], b[0], _NN) + (_dot(a[0], b[1], _NN) + _dot(a[1], b[0], _NN))

    P = -A
    T = eye + P
    n = 1
    while 2 * n < A.shape[0]:
        Ps = split(P)
        P = dot3(Ps, Ps)
        T = T + dot3(split(T), split(P))
        n *= 2
    return T


def _delta_fwd(qk, v, gb, gT, *, S, H, dh):
    C = min(DN_CHUNK, S)
    N, W = S // C, H * dh
    scale = dh ** -0.5

    def body(qk_ref, v_ref, gb_ref, gT_ref, o_ref, sp_ref, T_ref, st):
        n = pl.program_id(0)

        @pl.when(n == 0)
        def _():
            st[...] = jnp.zeros_like(st)

        r = lax.broadcasted_iota(jnp.int32, (C, C), 0)
        c = lax.broadcasted_iota(jnp.int32, (C, C), 1)
        causal, strict = r >= c, r > c
        eye = (r == c).astype(F32)
        Lt = causal.astype(F32)
        gbv = gb_ref[...]
        gcum = _dot(Lt, gbv, _NN, hi=True)
        gcumT = _dot(gT_ref[0], Lt, _NT, hi=True)
        for h in range(H):
            sl = slice(h * dh, (h + 1) * dh)
            q, k, vv = qk_ref[:, sl], qk_ref[:, W + h * dh:W + (h + 1) * dh], v_ref[:, sl]
            d = _delta_chunk(q, k, vv, _col(gbv, h), _col(gcum, H + h), _row(gcumT, h), causal, strict, eye, scale, C)
            T = _tri_inverse(d["A"], eye)
            u = _dot(T, d["vb"], _NN)
            w = _dot(T, d["kbg"], _NN)
            s0 = st[h]
            sp_ref[0, h] = s0
            T_ref[0, h] = T
            v_new = u - _dot(w, s0, _NN)
            o_ref[:, sl] = _dot(d["qg"], s0, _NN) + _dot(d["attn"], v_new, _NN)
            st[h] = s0 * d["decay"] + _dot(d["kd"], v_new, _TN)

    return pl.pallas_call(
        body, name="delta_fwd", grid=(N,),
        in_specs=[pl.BlockSpec((C, 2 * W), lambda n: (n, 0)), pl.BlockSpec((C, W), lambda n: (n, 0)),
                  pl.BlockSpec((C, LANE), lambda n: (n, 0)), pl.BlockSpec((1, H, C), lambda n: (n, 0, 0))],
        out_specs=[pl.BlockSpec((C, W), lambda n: (n, 0)), pl.BlockSpec((1, H, dh, dh), lambda n: (n, 0, 0, 0)),
                   pl.BlockSpec((1, H, C, C), lambda n: (n, 0, 0, 0))],
        out_shape=[jax.ShapeDtypeStruct((S, W), F32), jax.ShapeDtypeStruct((N, H, dh, dh), F32),
                   jax.ShapeDtypeStruct((N, H, C, C), F32)],
        scratch_shapes=[pltpu.VMEM((H, dh, dh), F32)],
        compiler_params=_cparams(("arbitrary",)),
    )(qk, v, gb, gT)


def _delta_bwd(qk, v, gb, gT, sp, Tm, do, *, S, H, dh):
    C = min(DN_CHUNK, S)
    N, W = S // C, H * dh
    scale = dh ** -0.5

    def body(qk_ref, v_ref, gb_ref, gT_ref, sp_ref, T_ref, do_ref, dqk_ref, dv_ref, dgb_ref, ds):
        n = pl.program_id(0)

        @pl.when(n == 0)
        def _():
            ds[...] = jnp.zeros_like(ds)

        r = lax.broadcasted_iota(jnp.int32, (C, C), 0)
        c = lax.broadcasted_iota(jnp.int32, (C, C), 1)
        causal, strict = r >= c, r > c
        eye = (r == c).astype(F32)
        Lt = causal.astype(F32)
        ones = jnp.ones((C, LANE), F32)
        lane = lax.broadcasted_iota(jnp.int32, (C, LANE), 1)
        rowi = lax.broadcasted_iota(jnp.int32, (C, 1), 0)
        gbv = gb_ref[...]
        gcum = _dot(Lt, gbv, _NN, hi=True)
        gcumT = _dot(gT_ref[0], Lt, _NT, hi=True)
        dgc_all = jnp.zeros((C, LANE), F32)
        dbeta_all = jnp.zeros((C, LANE), F32)
        for h in range(H):
            sl = slice(h * dh, (h + 1) * dh)
            ksl = slice(W + h * dh, W + (h + 1) * dh)
            q, k, vv = qk_ref[:, sl], qk_ref[:, ksl], v_ref[:, sl]
            beta, gcc = _col(gbv, h), _col(gcum, H + h)
            d = _delta_chunk(q, k, vv, beta, gcc, _row(gcumT, h), causal, strict, eye, scale, C)
            T, s0, dO, dS = T_ref[0, h], sp_ref[0, h], do_ref[:, sl], ds[h]
            u = _dot(T, d["vb"], _NN)
            w = _dot(T, d["kbg"], _NN)
            v_new = u - _dot(w, s0, _NN)
            dv_new = _dot(d["attn"], dO, _TN) + _dot(d["kd"], dS, _NN)
            dattn = jnp.where(causal, _dot(dO, v_new, _NT), 0.0)
            dqg = _dot(dO, s0, _NT)
            dkd = _dot(v_new, dS, _NT)
            ddecay = jnp.sum(jnp.sum(s0 * dS, axis=1, keepdims=True), axis=0, keepdims=True)
            ds[h] = _dot(d["qg"], dO, _TN) + d["decay"] * dS - _dot(w, dv_new, _TN)
            dw = -_dot(dv_new, s0, _NT)
            dT = _dot(dv_new, d["vb"], _NT) + _dot(dw, d["kbg"], _NT)
            dvb = _dot(T, dv_new, _TN)
            dkbg = _dot(T, dw, _TN)
            dA = jnp.where(strict, -_dot(_dot(T, dT, _TN, hi=True), T, _NT, hi=True), 0.0)
            dkk = dA * d["gam"]
            dqk_m = dattn * d["gam"]
            m = dA * d["A"] + dattn * d["attn"]
            dkb = _dot(dkk, k, _NN) + dkbg * d["eg"]
            dk = _dot(dkk, d["kb"], _TN) + _dot(dqk_m, d["qs"], _TN) + dkd * d["ek"] + dkb * beta
            dqs = _dot(dqk_m, k, _NN) + dqg * d["eg"]
            r_kd = jnp.sum(dkd * d["kd"], axis=1, keepdims=True)
            dgc = (jnp.sum(m, axis=1, keepdims=True) - jnp.max(_dot(m, ones, _TN, hi=True), axis=1, keepdims=True)
                   + jnp.sum(dqg * d["qg"], axis=1, keepdims=True) - r_kd
                   + jnp.sum(dkbg * d["kbg"], axis=1, keepdims=True))
            dg_last = jnp.sum(r_kd, axis=0, keepdims=True) + ddecay * d["decay"]
            dgc = dgc + jnp.where(rowi == C - 1, dg_last, 0.0)
            dbeta = jnp.sum(dkb * k, axis=1, keepdims=True) + jnp.sum(dvb * vv, axis=1, keepdims=True)
            dqk_ref[:, sl] = dqs * scale
            dqk_ref[:, ksl] = dk
            dv_ref[:, sl] = dvb * beta
            dgc_all = dgc_all + jnp.where(lane == H + h, dgc, 0.0)
            dbeta_all = dbeta_all + jnp.where(lane == h, dbeta, 0.0)
        dgb_ref[...] = _dot(Lt, dgc_all, _TN, hi=True) + dbeta_all

    rev = lambda n: N - 1 - n
    return pl.pallas_call(
        body, name="delta_bwd", grid=(N,),
        in_specs=[pl.BlockSpec((C, 2 * W), lambda n: (rev(n), 0)), pl.BlockSpec((C, W), lambda n: (rev(n), 0)),
                  pl.BlockSpec((C, LANE), lambda n: (rev(n), 0)), pl.BlockSpec((1, H, C), lambda n: (rev(n), 0, 0)),
                  pl.BlockSpec((1, H, dh, dh), lambda n: (rev(n), 0, 0, 0)),
                  pl.BlockSpec((1, H, C, C), lambda n: (rev(n), 0, 0, 0)),
                  pl.BlockSpec((C, W), lambda n: (rev(n), 0))],
        out_specs=[pl.BlockSpec((C, 2 * W), lambda n: (rev(n), 0)), pl.BlockSpec((C, W), lambda n: (rev(n), 0)),
                   pl.BlockSpec((C, LANE), lambda n: (rev(n), 0))],
        out_shape=[jax.ShapeDtypeStruct((S, 2 * W), F32), jax.ShapeDtypeStruct((S, W), F32),
                   jax.ShapeDtypeStruct((S, LANE), F32)],
        scratch_shapes=[pltpu.VMEM((H, dh, dh), F32)],
        compiler_params=_cparams(("arbitrary",)),
    )(qk, v, gb, gT, sp, Tm, do)


_ANY = pl.BlockSpec(memory_space=pl.ANY)


ICI_CHUNKS = 4
D2D_CHUNKS = 4


def _place():
    return lax.axis_index("x"), lax.axis_index("y"), lax.axis_index("c")


def _row_chunks(rows, n):
    n = max(1, min(n, rows // 8))
    while n > 1 and (rows % n or (rows // n) % 8):
        n -= 1
    return [(k * (rows // n), rows // n) for k in range(n)]


def _allgather_chips(name, flat):
    R, L = flat.shape
    Rh = R // 2
    ici = _row_chunks(Rh, ICI_CHUNKS)
    sub = _row_chunks(ici[0][1], D2D_CHUNKS)
    ni, ns = len(ici), len(sub)

    def body(x_ref, out_ref, send_sems, recv_sems):
        x, y, c = _place()
        sibling = (x, y, 1 - c)
        chips = [(1 - x, y), (x, 1 - y), (1 - x, 1 - y)]

        def rows(px, py, pc, r0, n):
            return out_ref.at[2 * px + py, pl.ds(pc * Rh + r0, n), :]

        def copy(k, src, dst, to):
            return pltpu.make_async_remote_copy(src_ref=src, dst_ref=dst, send_sem=send_sems.at[k],
                                                recv_sem=recv_sems.at[k], device_id=to, device_id_type=MESH)

        first = []
        for k, chip in enumerate(chips):
            for q, (r0, n) in enumerate(ici):
                first.append(copy(k * ni + q, x_ref.at[pl.ds(c * Rh + r0, n), :], rows(x, y, c, r0, n), (*chip, c)))
        for cp in first:
            cp.start()
        passed = []
        for k, chip in enumerate(chips):
            for q, (r0, n) in enumerate(ici):
                copy(k * ni + q, x_ref.at[pl.ds(r0, n), :], rows(*chip, c, r0, n), (*chip, c)).wait_recv()
                for t, (s0, m) in enumerate(sub):
                    cp = copy(3 * ni + (k * ni + q) * ns + t, rows(*chip, c, r0 + s0, m), rows(*chip, c, r0 + s0, m), sibling)
                    cp.start()
                    passed.append(cp)
        for k, chip in enumerate(chips):
            for q, (r0, n) in enumerate(ici):
                for t, (s0, m) in enumerate(sub):
                    copy(3 * ni + (k * ni + q) * ns + t, x_ref.at[pl.ds(r0, m), :], rows(*chip, 1 - c, r0 + s0, m),
                         sibling).wait_recv()
        for cp in first + passed:
            cp.wait_send()

    nsem = 3 * ni * (1 + ns)
    out = pl.pallas_call(
        body, name=name, in_specs=[_ANY], out_specs=_ANY,
        out_shape=jax.ShapeDtypeStruct((N_CHIPS, R, L), flat.dtype),
        scratch_shapes=[pltpu.SemaphoreType.DMA((nsem,)), pltpu.SemaphoreType.DMA((nsem,))],
    )(flat)
    return lax.dynamic_update_slice(out, flat[None], (2 * lax.axis_index("x") + lax.axis_index("y"), 0, 0))


def _sibling_split(name, g):
    _, R, L = g.shape
    Rh = R // 2

    chunks = [(j, r0, n) for j in range(N_CHIPS) for (r0, n) in _row_chunks(Rh, D2D_CHUNKS)]

    def body(g_ref, got_ref, send_sems, recv_sems):
        x, y, c = _place()
        cps = [pltpu.make_async_remote_copy(src_ref=g_ref.at[j, pl.ds((1 - c) * Rh + r0, n), :],
                                            dst_ref=got_ref.at[j, pl.ds(r0, n), :], send_sem=send_sems.at[k],
                                            recv_sem=recv_sems.at[k], device_id=(x, y, 1 - c), device_id_type=MESH)
               for k, (j, r0, n) in enumerate(chunks)]
        for cp in cps:
            cp.start()
        for cp in cps:
            cp.wait()

    sems = pltpu.SemaphoreType.DMA((len(chunks),))
    got = pl.pallas_call(
        body, name=name, in_specs=[_ANY], out_specs=_ANY, out_shape=jax.ShapeDtypeStruct((N_CHIPS, Rh, L), g.dtype),
        scratch_shapes=[sems, sems],
    )(g)
    own = lax.dynamic_slice(g, (0, lax.axis_index("c") * Rh, 0), (N_CHIPS, Rh, L))
    return own, got


def _chip_exchange(name, p):
    Rh = p.shape[1]
    ici = _row_chunks(Rh, ICI_CHUNKS)
    ni = len(ici)

    def body(p_ref, q_ref, send_sems, recv_sems):
        x, y, c = _place()
        me = 2 * x + y
        chips = [(1 - x, y), (x, 1 - y), (1 - x, 1 - y)]
        cps = [pltpu.make_async_remote_copy(src_ref=p_ref.at[2 * cx + cy, pl.ds(r0, n), :],
                                            dst_ref=q_ref.at[me, pl.ds(r0, n), :],
                                            send_sem=send_sems.at[k * ni + q], recv_sem=recv_sems.at[k * ni + q],
                                            device_id=(cx, cy, c), device_id_type=MESH)
               for k, (cx, cy) in enumerate(chips) for q, (r0, n) in enumerate(ici)]
        for cp in cps:
            cp.start()
        for k, (cx, cy) in enumerate(chips):
            for q, (r0, n) in enumerate(ici):
                pltpu.make_async_remote_copy(src_ref=p_ref.at[me, pl.ds(r0, n), :],
                                             dst_ref=q_ref.at[2 * cx + cy, pl.ds(r0, n), :],
                                             send_sem=send_sems.at[k * ni + q], recv_sem=recv_sems.at[k * ni + q],
                                             device_id=(cx, cy, c), device_id_type=MESH).wait_recv()
        for cp in cps:
            cp.wait_send()

    q = pl.pallas_call(
        body, name=name, in_specs=[_ANY], out_specs=_ANY, out_shape=jax.ShapeDtypeStruct(p.shape, p.dtype),
        scratch_shapes=[pltpu.SemaphoreType.DMA((3 * ni,)), pltpu.SemaphoreType.DMA((3 * ni,))],
    )(p)
    me = 2 * lax.axis_index("x") + lax.axis_index("y")
    return lax.dynamic_update_slice(q, lax.dynamic_slice(p, (me, 0, 0), (1,) + p.shape[1:]), (me, 0, 0))


def _sibling_join(name, half):
    Rh, L = half.shape

    chunks = _row_chunks(Rh, 2 * D2D_CHUNKS)

    def body(h_ref, out_ref, send_sems, recv_sems):
        x, y, c = _place()
        cps = [pltpu.make_async_remote_copy(src_ref=h_ref.at[pl.ds(r0, n), :], dst_ref=out_ref.at[pl.ds(c * Rh + r0, n), :],
                                            send_sem=send_sems.at[k], recv_sem=recv_sems.at[k], device_id=(x, y, 1 - c),
                                            device_id_type=MESH)
               for k, (r0, n) in enumerate(chunks)]
        for cp in cps:
            cp.start()
        for k, (r0, n) in enumerate(chunks):
            pltpu.make_async_remote_copy(src_ref=h_ref.at[pl.ds(r0, n), :],
                                         dst_ref=out_ref.at[pl.ds((1 - c) * Rh + r0, n), :], send_sem=send_sems.at[k],
                                         recv_sem=recv_sems.at[k], device_id=(x, y, 1 - c), device_id_type=MESH).wait_recv()
        for cp in cps:
            cp.wait_send()

    sems = pltpu.SemaphoreType.DMA((len(chunks),))
    out = pl.pallas_call(
        body, name=name, in_specs=[_ANY], out_specs=_ANY, out_shape=jax.ShapeDtypeStruct((2 * Rh, L), half.dtype),
        scratch_shapes=[sems, sems],
    )(half)
    return lax.dynamic_update_slice(out, half, (lax.axis_index("c") * Rh, 0))


def _add_pairs(name, a, b, out_dtype):
    n, Rh, L = a.shape
    tr = _pick(Rh, 512, 8)

    def body(a_ref, b_ref, o_ref):
        o_ref[...] = (a_ref[...].astype(F32) + b_ref[...].astype(F32)).astype(o_ref.dtype)

    spec = pl.BlockSpec((1, tr, L), lambda j, i: (j, i, 0))
    return pl.pallas_call(body, name=name, grid=(n, Rh // tr), in_specs=[spec, spec], out_specs=spec,
                          out_shape=jax.ShapeDtypeStruct(a.shape, out_dtype),
                          compiler_params=_cparams(("parallel", "parallel")))(a, b)


def _sum_chips(name, q):
    n, Rh, L = q.shape
    tr = _pick(Rh, 512, 8)

    def body(q_ref, o_ref):
        acc = q_ref[0].astype(F32)
        for s in range(1, n):
            acc = acc + q_ref[s].astype(F32)
        o_ref[...] = acc

    return pl.pallas_call(body, name=name, grid=(Rh // tr,),
                          in_specs=[pl.BlockSpec((n, tr, L), lambda i: (0, i, 0))],
                          out_specs=pl.BlockSpec((tr, L), lambda i: (i, 0)),
                          out_shape=jax.ShapeDtypeStruct((Rh, L), F32),
                          compiler_params=_cparams(("parallel",)))(q)


def _adamw(name, w, m, v, gflat, row0):
    shape, size = w.shape, w.size
    rows = -(-size // FLAT_L)
    rows_p = -(-rows // ADAM_ROWS) * ADAM_ROWS
    pad = rows_p * FLAT_L - size

    def flat2d(a):
        a = a.reshape(-1)
        if pad:
            a = jnp.pad(a, (0, pad), constant_values=1.0)
        return a.reshape(rows_p, FLAT_L)

    blk0 = row0 // ADAM_ROWS
    c1 = 1.0 / (1.0 - ADAM_B1 ** ADAM_STEP)
    c2 = 1.0 / (1.0 - ADAM_B2 ** ADAM_STEP)

    def body(w_ref, m_ref, v_ref, g_ref, go_ref, d_ref, mo_ref, vo_ref):
        g = g_ref[...]
        wv = w_ref[...]
        mn = ADAM_B1 * m_ref[...] + (1.0 - ADAM_B1) * g
        vn = ADAM_B2 * v_ref[...] + (1.0 - ADAM_B2) * (g * g)
        go_ref[...] = g
        mo_ref[...] = mn
        vo_ref[...] = vn
        d_ref[...] = -ADAM_LR * ((mn * c1) / (jnp.sqrt(vn * c2) + ADAM_EPS) + ADAM_WD * wv)

    spec = pl.BlockSpec((ADAM_ROWS, FLAT_L), lambda i: (i, 0))
    shp = jax.ShapeDtypeStruct((rows_p, FLAT_L), F32)
    outs = pl.pallas_call(
        body, name=name, grid=(rows_p // ADAM_ROWS,),
        in_specs=[spec, spec, spec, pl.BlockSpec((ADAM_ROWS, FLAT_L), lambda i: (blk0 + i, 0))],
        out_specs=[spec] * 4, out_shape=[shp] * 4, compiler_params=_cparams(("parallel",)),
    )(flat2d(w), flat2d(m), flat2d(v), gflat)

    def back(a):
        a = a.reshape(-1)
        if pad:
            a = a[:size]
        return a.reshape(shape)

    return tuple(back(a) for a in outs)


def _seg_rows(size):
    rows = -(-size // FLAT_L)
    return -(-rows // ADAM_ROWS) * ADAM_ROWS


def _pack(pieces, dtype, row_mult):
    segs, offs, r = [], [], 0
    for a in pieces:
        rows = _seg_rows(a.size)
        flat = a.reshape(-1).astype(dtype)
        flat = jnp.pad(flat, (0, rows * FLAT_L - a.size))
        segs.append(flat.reshape(rows, FLAT_L))
        offs.append(r)
        r += rows
    tail = -r % row_mult
    if tail:
        segs.append(jnp.zeros((tail, FLAT_L), dtype))
    return jnp.concatenate(segs, axis=0), offs


def _gather_weights(name, shards, axes, dtype):
    flat, offs = _pack(shards, dtype, 2 * 8)
    full = _allgather_chips(name, flat)
    outs = []
    for a, ax, off in zip(shards, axes, offs):
        rows = _seg_rows(a.size)
        seg = full[:, off:off + rows].reshape(N_CHIPS, -1)[:, :a.size].reshape((N_CHIPS,) + a.shape)
        outs.append(jnp.concatenate([seg[j] for j in range(N_CHIPS)], axis=ax))
    return outs


def _f_rms(row0, j, x, g):
    return _rms(x, g)


def _f_mid(row0, j, x, y, g_a, g_b):
    xn = x + _rms(y, g_a)
    return xn, _rms(xn, g_b)


def _f_resid(row0, j, x, y, g):
    return x + _rms(y, g)


def _relu2(u):
    r = jnp.maximum(u, 0.0)
    return r * r


def _relu2_bwd(d_act, act):
    return d_act * (2.0 * jnp.sqrt(act.astype(F32)))


def _f_l2silu(row0, j, c):
    a = _silu(c)
    return a * lax.rsqrt(jnp.sum(a * a, axis=-1, keepdims=True) + EPS)


def _f_silu(row0, j, c):
    return _silu(c)


def _f_scale(row0, j, y, s):
    return y * s


def _f_glu(row0, j, a, gate):
    return a * _sigmoid(gate)


def _f_lnsilu(row0, j, u, g, b):
    mu = jnp.mean(u, axis=-1, keepdims=True)
    uc = u - mu
    return _silu(uc * lax.rsqrt(jnp.mean(uc * uc, axis=-1, keepdims=True) + EPS) * g + b)


def _f_outgate(row0, j, o, z, g):
    return _rms(o, g) * _silu(z)


def _make_gates(H):
    def f(row0, j, ba, alog, dt):
        lane = lax.broadcasted_iota(jnp.int32, ba.shape, 1)
        beta = _sigmoid(ba)
        g = -jnp.exp(alog) * _softplus(ba + dt)
        return jnp.where(lane < H, beta, jnp.where(lane < 2 * H, g, 0.0))
    return f


def _f_loss(row0, j, y, t):
    e = y - t
    loss = 0.5 * jnp.sum(jnp.mean(e * e, axis=-1, keepdims=True), axis=0, keepdims=True)
    return e * (1.0 / y.shape[-1]), jnp.broadcast_to(loss, (1, LANE))


def _lane_row(vec, start):
    return jnp.pad(vec.astype(F32)[None, :], ((0, 0), (start, LANE - start - vec.shape[0])))


def kernel(x, norm_mix_pre, norm_mix_post, norm_mlp_pre, norm_mlp_post, even_w_in, even_conv, even_a_log, even_dt_bias, even_dn_norm, even_pool_w, even_pool_scale, even_w_out, odd_w_in, odd_dw, odd_dw_b, odd_ln_g, odd_ln_b, odd_w_out, mlp_w_up, mlp_w_down, loss_target, m_norm_mix_pre, m_norm_mix_post, m_norm_mlp_pre, m_norm_mlp_post, m_even_w_in, m_even_conv, m_even_a_log, m_even_dt_bias, m_even_dn_norm, m_even_pool_w, m_even_pool_scale, m_even_w_out, m_odd_w_in, m_odd_dw, m_odd_dw_b, m_odd_ln_g, m_odd_ln_b, m_odd_w_out, m_mlp_w_up, m_mlp_w_down, v_norm_mix_pre, v_norm_mix_post, v_norm_mlp_pre, v_norm_mlp_post, v_even_w_in, v_even_conv, v_even_a_log, v_even_dt_bias, v_even_dn_norm, v_even_pool_w, v_even_pool_scale, v_even_w_out, v_odd_w_in, v_odd_dw, v_odd_dw_b, v_odd_ln_g, v_odd_ln_b, v_odd_w_out, v_mlp_w_up, v_mlp_w_down):
    names = ["norm_mix_pre", "norm_mix_post", "norm_mlp_pre", "norm_mlp_post", "even_w_in", "even_conv", "even_a_log",
             "even_dt_bias", "even_dn_norm", "even_pool_w", "even_pool_scale", "even_w_out", "odd_w_in", "odd_dw",
             "odd_dw_b", "odd_ln_g", "odd_ln_b", "odd_w_out", "mlp_w_up", "mlp_w_down"]
    W = dict(zip(names, (norm_mix_pre, norm_mix_post, norm_mlp_pre, norm_mlp_post, even_w_in, even_conv, even_a_log,
                         even_dt_bias, even_dn_norm, even_pool_w, even_pool_scale, even_w_out, odd_w_in, odd_dw,
                         odd_dw_b, odd_ln_g, odd_ln_b, odd_w_out, mlp_w_up, mlp_w_down)))
    Mo = dict(zip(names, (m_norm_mix_pre, m_norm_mix_post, m_norm_mlp_pre, m_norm_mlp_post, m_even_w_in, m_even_conv,
                          m_even_a_log, m_even_dt_bias, m_even_dn_norm, m_even_pool_w, m_even_pool_scale, m_even_w_out,
                          m_odd_w_in, m_odd_dw, m_odd_dw_b, m_odd_ln_g, m_odd_ln_b, m_odd_w_out, m_mlp_w_up,
                          m_mlp_w_down)))
    Vo = dict(zip(names, (v_norm_mix_pre, v_norm_mix_post, v_norm_mlp_pre, v_norm_mlp_post, v_even_w_in, v_even_conv,
                          v_even_a_log, v_even_dt_bias, v_even_dn_norm, v_even_pool_w, v_even_pool_scale, v_even_w_out,
                          v_odd_w_in, v_odd_dw, v_odd_dw_b, v_odd_ln_g, v_odd_ln_b, v_odd_w_out, v_mlp_w_up,
                          v_mlp_w_down)))
    shard_axis = {"even_w_in": 2, "even_conv": 2, "even_pool_w": 2, "even_w_out": 1, "odd_w_in": 2, "odd_dw": 2,
                  "odd_dw_b": 1, "odd_ln_g": 1, "odd_ln_b": 1, "odd_w_out": 1, "mlp_w_up": 2, "mlp_w_down": 1}

    S, D = x.shape[1], x.shape[2]
    depth = norm_mix_pre.shape[0]
    H = even_a_log.shape[1]
    dh = even_dn_norm.shape[1]
    DNW = H * dh
    PW = even_pool_scale.shape[1]
    G = len(POOL_WINDOWS)
    PG = PW // G
    KC = even_conv.shape[1]
    BAW = 2 * LANE
    P_COLS = 4 * DNW + PW + BAW
    x2 = x.reshape(S, D)
    tgt = loss_target.reshape(S, D)

    big = ["even_w_in", "even_pool_w", "even_w_out", "odd_w_in", "odd_w_out", "mlp_w_up", "mlp_w_down"]
    small = ["even_conv", "odd_dw", "odd_dw_b", "odd_ln_g", "odd_ln_b"]
    full = dict(zip(big, _gather_weights("gather_big", [W[n] for n in big], [shard_axis[n] for n in big], BF16)))
    full.update(zip(small, _gather_weights("gather_small", [W[n] for n in small], [shard_axis[n] for n in small], F32)))
    CW = full["odd_w_out"].shape[1]
    KD = odd_dw.shape[1]

    def even_w_in_layout(w):
        o1 = 4 * DNW
        return jnp.concatenate([w[:, :o1], w[:, o1 + 2 * H:], w[:, o1:o1 + 2 * H],
                                jnp.zeros((w.shape[0], BAW - 2 * H), w.dtype)], axis=1)

    def even_w_in_unlayout(g):
        o1 = 4 * DNW
        return jnp.concatenate([g[:, :o1], g[:, o1 + PW:o1 + PW + 2 * H], g[:, o1:o1 + PW]], axis=1)

    def pool_blockdiag(pw):
        return jnp.concatenate([jnp.pad(pw[gi], ((0, 0), (gi * PG, PW - (gi + 1) * PG))) for gi in range(G)], axis=0)

    pool_taps = max(POOL_WINDOWS)
    tap = jnp.arange(pool_taps)[:, None]
    win_c = jnp.repeat(jnp.asarray(POOL_WINDOWS, F32), PG)[None, :]
    pool_mask = (tap >= pool_taps - win_c).astype(F32)

    grads = {n: [None] * W[n].shape[0] for n in names}
    tr_full = 128 if D > 1024 else 256

    saved = []
    xc = x2
    for i in range(depth):
        jl = i // 2
        sv = {"x_in": xc}
        g1, g2, g3, g4 = (W[n][i:i + 1] for n in ("norm_mix_pre", "norm_mix_post", "norm_mlp_pre", "norm_mlp_post"))
        (h,) = _rowwise(f"l{i}_rms_in", _f_rms, [(xc, 0, D)], [(g1, None, D)], [(D, BF16)], S=S, tr=tr_full)
        sv["h"] = h
        if i % 2 == 0:
            w_in = even_w_in_layout(full["even_w_in"][jl])
            p = _matmul(f"l{i}_w_in", h, w_in, "nn", tn=768)
            conv_w = full["even_conv"][jl]
            c = _dwconv_fwd(f"l{i}_conv", p, 0, conv_w, S=S, C=3 * DNW)
            (qk,) = _rowwise(f"l{i}_qk", _f_l2silu, [(c, 0, dh)], [], [(dh, F32)], S=S, ncb=2 * H, tr=1024)
            (vv,) = _rowwise(f"l{i}_v", _f_silu, [(c, 2 * DNW // dh, dh)], [], [(dh, F32)], S=S, ncb=H, tr=1024)
            alog = _lane_row(W["even_a_log"][jl], H)
            dtb = _lane_row(W["even_dt_bias"][jl], H)
            ba_off = (4 * DNW + PW) // LANE
            (gb,) = _rowwise(f"l{i}_gates", _make_gates(H), [(p, ba_off, LANE)], [(alog, None, LANE), (dtb, None, LANE)],
                             [(LANE, F32)], S=S, tr=1024)
            Cn = min(DN_CHUNK, S)
            gT = gb[:, H:2 * H].reshape(S // Cn, Cn, H).transpose(0, 2, 1)
            o, sp, Tm = _delta_fwd(qk, vv, gb, gT, S=S, H=H, dh=dh)
            dn = W["even_dn_norm"][jl][None, :]
            (on,) = _rowwise(f"l{i}_outgate", _f_outgate, [(o, 0, dh), (p, 3 * DNW // dh, dh)], [(dn, None, dh)],
                             [(dh, BF16)], S=S, ncb=H, tr=1024)
            pcb = _pick(PW, 512)
            pooled = _dwconv_fwd(f"l{i}_pool", p, 4 * DNW // pcb, pool_mask, S=S, C=PW, win=win_c, out_dtype=BF16, cb=pcb)
            wbd = pool_blockdiag(full["even_pool_w"][jl])
            ypre = _matmul(f"l{i}_pool_w", pooled, wbd, "nn")
            psc = W["even_pool_scale"][jl][None, :]
            (ypool,) = _rowwise(f"l{i}_pool_scale", _f_scale, [(ypre, 0, PW)], [(psc, None, PW)], [(PW, BF16)], S=S)
            mixin = jnp.concatenate([on, ypool], axis=1)
            mix = _matmul(f"l{i}_w_out", mixin, full["even_w_out"][jl], "nn")
            sv.update(p=p, c=c, qk=qk, v=vv, gb=gb, gT=gT, o=o, sp=sp, Tm=Tm, pooled=pooled, ypre=ypre, mixin=mixin,
                      w_in=w_in, wbd=wbd, alog=alog, dtb=dtb, dn=dn, psc=psc, conv_w=conv_w)
        else:
            p = _matmul(f"l{i}_w_in", h, full["odd_w_in"][jl], "nn")
            ocb = _pick(CW, 1024)
            (u0,) = _rowwise(f"l{i}_glu", _f_glu, [(p, 0, ocb), (p, CW // ocb, ocb)], [], [(ocb, F32)], S=S,
                             ncb=CW // ocb)
            dw_w, dw_b = full["odd_dw"][jl], full["odd_dw_b"][jl][None, :]
            u1 = _dwconv_fwd(f"l{i}_dwconv", u0, 0, dw_w, S=S, C=CW, bias=dw_b)
            lg, lb = full["odd_ln_g"][jl][None, :], full["odd_ln_b"][jl][None, :]
            (u2,) = _rowwise(f"l{i}_lnsilu", _f_lnsilu, [(u1, 0, CW)], [(lg, None, CW), (lb, None, CW)], [(CW, BF16)],
                             S=S, tr=tr_full)
            mix = _matmul(f"l{i}_w_out", u2, full["odd_w_out"][jl], "nn")
            sv.update(p=p, u0=u0, u1=u1, mixin=u2, dw_w=dw_w, lg=lg, lb=lb)
        x_mid, h2 = _rowwise(f"l{i}_mid", _f_mid, [(xc, 0, D), (mix, 0, D)], [(g2, None, D), (g3, None, D)],
                             [(D, F32), (D, BF16)], S=S, tr=tr_full)
        act = _matmul(f"l{i}_w_up", h2, full["mlp_w_up"][i], "nn", epi=_relu2, out_dtypes=[BF16])
        ff = _matmul(f"l{i}_w_down", act, full["mlp_w_down"][i], "nn")
        (x_out,) = _rowwise(f"l{i}_out", _f_resid, [(x_mid, 0, D), (ff, 0, D)], [(g4, None, D)], [(D, F32)], S=S,
                            tr=tr_full)
        sv.update(mix=mix, h2=h2, act=act, ff=ff, g=(g1, g2, g3, g4))
        saved.append(sv)
        xc = x_out

    dy, loss_row = _rowwise("loss", _f_loss, [(xc, 0, D), (tgt, 0, D)], [], [(D, F32)], [(1, LANE, False)], S=S,
                            tr=tr_full)
    loss = lax.psum(loss_row[0, 0], ("x", "y", "c"))

    dx = dy
    for i in reversed(range(depth)):
        jl = i // 2
        sv = saved[i]
        g1, g2, g3, g4 = sv["g"]
        d_ff, dg4 = _rowwise_bwd(f"l{i}_out_b", _f_rms, [(sv["ff"], 0, D)], [(g4, None, D)], [(dx, 0, D)], [BF16],
                                 S=S, tr=tr_full)
        grads["norm_mlp_post"][i] = dg4[0]
        du = _matmul(f"l{i}_w_down_bx", d_ff, full["mlp_w_down"][i], "nt", epi=_relu2_bwd, extras=[sv["act"]],
                     out_dtypes=[BF16])
        grads["mlp_w_down"][i] = _matmul(f"l{i}_w_down_bw", sv["act"], d_ff, "tn")
        dh2 = _matmul(f"l{i}_w_up_bx", du, full["mlp_w_up"][i], "nt")
        grads["mlp_w_up"][i] = _matmul(f"l{i}_w_up_bw", sv["h2"], du, "tn")
        dx, d_mix, dg2, dg3 = _rowwise_bwd(
            f"l{i}_mid_b", _f_mid, [(sv["x_in"], 0, D), (sv["mix"], 0, D)], [(g2, None, D), (g3, None, D)],
            [(dx, 0, D), (dh2, 0, D)], [F32, BF16], S=S, tr=tr_full)
        grads["norm_mix_post"][i], grads["norm_mlp_pre"][i] = dg2[0], dg3[0]
        if i % 2 == 0:
            d_mixin = _matmul(f"l{i}_w_out_bx", d_mix, full["even_w_out"][jl], "nt")
            grads["even_w_out"][jl] = _matmul(f"l{i}_w_out_bw", sv["mixin"], d_mix, "tn")
            p = sv["p"]
            pcb = _pick(PW, 512)
            d_ypre, dpsc = _rowwise_bwd(f"l{i}_pool_scale_b", _f_scale, [(sv["ypre"], 0, PW)], [(sv["psc"], None, PW)],
                                        [(d_mixin, DNW // PW, PW)], [BF16], S=S)
            grads["even_pool_scale"][jl] = dpsc[0]
            d_pooled = _matmul(f"l{i}_pool_w_bx", d_ypre, sv["wbd"], "nt")
            dwbd = _matmul(f"l{i}_pool_w_bw", sv["pooled"], d_ypre, "tn")
            grads["even_pool_w"][jl] = jnp.stack([dwbd[gi * PG:(gi + 1) * PG, gi * PG:(gi + 1) * PG] for gi in range(G)])
            d_xp = _dwconv_bwd(f"l{i}_pool_b", None, 0, d_pooled, pool_mask, S=S, C=PW, win=win_c, want_dw=False,
                               cb=pcb)[0]
            d_o, d_z, ddn = _rowwise_bwd(f"l{i}_outgate_b", _f_outgate, [(sv["o"], 0, dh), (p, 3 * DNW // dh, dh)],
                                         [(sv["dn"], None, dh)], [(d_mixin, 0, dh)], [F32, F32], S=S, ncb=H, tr=1024)
            grads["even_dn_norm"][jl] = ddn[0]
            dqk, dv, dgb = _delta_bwd(sv["qk"], sv["v"], sv["gb"], sv["gT"], sv["sp"], sv["Tm"], d_o, S=S, H=H, dh=dh)
            ba_off = (4 * DNW + PW) // LANE
            d_ba, dalog, ddtb = _rowwise_bwd(f"l{i}_gates_b", _make_gates(H), [(p, ba_off, LANE)],
                                             [(sv["alog"], None, LANE), (sv["dtb"], None, LANE)], [(dgb, 0, LANE)],
                                             [F32], S=S, tr=1024)
            grads["even_a_log"][jl], grads["even_dt_bias"][jl] = dalog[0, H:2 * H], ddtb[0, H:2 * H]
            (dc_qk,) = _rowwise_bwd(f"l{i}_qk_b", _f_l2silu, [(sv["c"], 0, dh)], [], [(dqk, 0, dh)], [F32], S=S,
                                    ncb=2 * H, tr=1024)
            (dc_v,) = _rowwise_bwd(f"l{i}_v_b", _f_silu, [(sv["c"], 2 * DNW // dh, dh)], [], [(dv, 0, dh)], [F32], S=S,
                                   ncb=H, tr=1024)
            dc = jnp.concatenate([dc_qk, dc_v], axis=1)
            d_qkv, dconv, _ = _dwconv_bwd(f"l{i}_conv_b", p, 0, dc, sv["conv_w"], S=S, C=3 * DNW)
            grads["even_conv"][jl] = dconv
            dp = jnp.concatenate([d_qkv.astype(BF16), d_z.astype(BF16), d_xp.astype(BF16), d_ba.astype(BF16),
                                  jnp.zeros((S, BAW - LANE), BF16)], axis=1)
            dh_ = _matmul(f"l{i}_w_in_bx", dp, sv["w_in"], "nt", tk=768)
            grads["even_w_in"][jl] = even_w_in_unlayout(_matmul(f"l{i}_w_in_bw", sv["h"], dp, "tn", tn=768))
        else:
            d_u2 = _matmul(f"l{i}_w_out_bx", d_mix, full["odd_w_out"][jl], "nt")
            grads["odd_w_out"][jl] = _matmul(f"l{i}_w_out_bw", sv["mixin"], d_mix, "tn")
            d_u1, dlg, dlb = _rowwise_bwd(f"l{i}_lnsilu_b", _f_lnsilu, [(sv["u1"], 0, CW)],
                                          [(sv["lg"], None, CW), (sv["lb"], None, CW)], [(d_u2, 0, CW)], [F32], S=S,
                                          tr=tr_full)
            grads["odd_ln_g"][jl], grads["odd_ln_b"][jl] = dlg[0], dlb[0]
            d_u0, ddw, ddb = _dwconv_bwd(f"l{i}_dwconv_b", sv["u0"], 0, d_u1, sv["dw_w"], S=S, C=CW)
            grads["odd_dw"][jl], grads["odd_dw_b"][jl] = ddw, ddb[0]
            p = sv["p"]
            ocb = _pick(CW, 1024)
            da, dgate = _rowwise_bwd(f"l{i}_glu_b", _f_glu, [(p, 0, ocb), (p, CW // ocb, ocb)], [], [(d_u0, 0, ocb)],
                                     [BF16, BF16], S=S, ncb=CW // ocb)
            dp = jnp.concatenate([da, dgate], axis=1)
            dh_ = _matmul(f"l{i}_w_in_bx", dp, full["odd_w_in"][jl], "nt")
            grads["odd_w_in"][jl] = _matmul(f"l{i}_w_in_bw", sv["h"], dp, "tn")
        dx, dg1 = _rowwise_bwd(f"l{i}_rms_in_b", _f_rms, [(sv["x_in"], 0, D)], [(g1, None, D)], [(dh_, 0, D)], [F32],
                               adds=[(dx, 0, D)], S=S, tr=tr_full)
        grads["norm_mix_pre"][i] = dg1[0]
    grad_x = dx.reshape(x.shape)

    gfull = {n: jnp.stack(grads[n]) for n in names}
    pieces = [[] for _ in range(N_CHIPS)]
    for n in names:
        ax = shard_axis.get(n)
        parts = jnp.split(gfull[n], N_CHIPS, axis=ax) if ax is not None else [gfull[n]] * N_CHIPS
        for jc in range(N_CHIPS):
            pieces[jc].append(parts[jc])
    packed = [_pack(pieces[jc], BF16, 2 * ADAM_ROWS) for jc in range(N_CHIPS)]
    offs = packed[0][1]
    gsend = jnp.stack([pk[0] for pk in packed])
    own, got = _sibling_split("grad_sibling_split", gsend)
    pair = _add_pairs("grad_add_cores", own, got, BF16)
    arrived = _chip_exchange("grad_chip_exchange", pair)
    ghalf = _sum_chips("grad_sum_chips", arrived)
    gsum = _sibling_join("grad_sibling_join", ghalf)

    outs_g, outs_d, outs_m, outs_v = [], [], [], []
    for n, off in zip(names, offs):
        g_o, d_o, m_o, v_o = _adamw(f"adamw_{n}", W[n], Mo[n], Vo[n], gsum, off)
        outs_g.append(g_o)
        outs_d.append(d_o)
        outs_m.append(m_o)
        outs_v.append(v_o)
    return (loss, grad_x, *outs_g, *outs_d, *outs_m, *outs_v)
```

```python
import functools
import math

import jax
import jax.numpy as jnp
from jax import lax
from jax.experimental import pallas as pl
from jax.experimental.pallas import tpu as pltpu

F32 = jnp.float32
BF16 = jnp.bfloat16
EPS = 1e-6
DN_CHUNK = 64
POOL_WINDOWS = (2, 4, 8, 16)
ADAM_LR, ADAM_B1, ADAM_B2, ADAM_EPS, ADAM_WD, ADAM_STEP = 0.001, 0.9, 0.999, 1e-08, 0.01, 10
LANE = 128
FLAT_L = 2048
ADAM_ROWS = 128
VMEM_LIMIT = 56 * 1024 * 1024
N_CHIPS = 4
HI = lax.Precision.HIGHEST
MESH = pl.DeviceIdType.MESH


def _cparams(sem):
    return pltpu.CompilerParams(dimension_semantics=sem, vmem_limit_bytes=VMEM_LIMIT)


def _pallas(body, args, *, name, grid, in_specs, out_specs, out_shape, scratch_shapes=(), sem=None, comm=()):
    n_in, n_out, n_scr = len(in_specs), len(out_specs), len(scratch_shapes)
    if not comm:
        return pl.pallas_call(body, name=name, grid=grid, in_specs=in_specs, out_specs=out_specs, out_shape=out_shape,
                              scratch_shapes=list(scratch_shapes), compiler_params=_cparams(sem))(*args)
    ci = [len(p.ins) for p in comm]
    co = [len(p.outs) for p in comm]
    cs = [len(p.sems) for p in comm]

    def wrapped(*refs):
        ins, pos = refs[:n_in], n_in
        cins = refs[pos:pos + sum(ci)]
        pos += sum(ci)
        outs = refs[pos:pos + n_out]
        pos += n_out
        couts = refs[pos:pos + sum(co)]
        pos += sum(co)
        scr = refs[pos:pos + n_scr]
        csems = refs[pos + n_scr:]
        ids = [pl.program_id(a) for a in range(len(grid))]
        first, last = ids[0] == 0, ids[0] == grid[0] - 1
        for a in range(1, len(grid)):
            first = jnp.logical_and(first, ids[a] == 0)
            last = jnp.logical_and(last, ids[a] == grid[a] - 1)
        parts, a, b, c = [], 0, 0, 0
        for p, na, nb, nc in zip(comm, ci, co, cs):
            parts.append((p, cins[a:a + na], couts[b:b + nb], csems[c:c + nc]))
            a, b, c = a + na, b + nb, c + nc

        @pl.when(first)
        def _():
            for p, pi, po, ps in parts:
                p.start(pi, po, ps)

        body(*ins, *outs, *scr)

        @pl.when(last)
        def _():
            for p, pi, po, ps in parts:
                p.finish(pi, po, ps)

    any_spec = pl.BlockSpec(memory_space=pl.ANY)
    res = pl.pallas_call(
        wrapped, name=name, grid=grid,
        in_specs=list(in_specs) + [any_spec] * sum(ci), out_specs=list(out_specs) + [any_spec] * sum(co),
        out_shape=list(out_shape) + [s for p in comm for s in p.outs],
        scratch_shapes=list(scratch_shapes) + [s for p in comm for s in p.sems],
        compiler_params=_cparams(("arbitrary",) * len(grid)),
    )(*args, *[a for p in comm for a in p.ins])
    return res


def _pick(dim, target, mult=LANE):
    if dim <= target:
        return dim
    t = (target // mult) * mult
    while t >= mult:
        if dim % t == 0:
            return t
        t -= mult
    return dim


def _sigmoid(x):
    return 1.0 / (1.0 + jnp.exp(-x))


def _silu(x):
    return x * _sigmoid(x)


def _softplus(x):
    return jnp.maximum(x, 0.0) + jnp.log(1.0 + jnp.exp(-jnp.abs(x)))


def _rms(x, g):
    return x * lax.rsqrt(jnp.mean(x * x, axis=-1, keepdims=True) + EPS) * g


def _matmul(name, a, b, mode, out_dtype=F32, tm=1024, tn=1024, tk=2048, epi=None, extras=(), out_dtypes=None):
    if mode == "nn":
        (M, K), (K2, N) = a.shape, b.shape
    elif mode == "nt":
        (M, K), (N, K2) = a.shape, b.shape
    else:
        (K, M), (K2, N) = a.shape, b.shape
    assert K == K2, (name, a.shape, b.shape, mode)
    tm, tn, tk = _pick(M, tm), _pick(N, tn), _pick(K, tk)
    nk = K // tk
    if mode == "nn":
        a_spec = pl.BlockSpec((tm, tk), lambda i, j, k: (i, k))
        b_spec = pl.BlockSpec((tk, tn), lambda i, j, k: (k, j))
        dims = (((1,), (0,)), ((), ()))
    elif mode == "nt":
        a_spec = pl.BlockSpec((tm, tk), lambda i, j, k: (i, k))
        b_spec = pl.BlockSpec((tn, tk), lambda i, j, k: (j, k))
        dims = (((1,), (1,)), ((), ()))
    else:
        a_spec = pl.BlockSpec((tk, tm), lambda i, j, k: (k, i))
        b_spec = pl.BlockSpec((tk, tn), lambda i, j, k: (k, j))
        dims = (((0,), (0,)), ((), ()))
    out_dtypes = list(out_dtypes) if out_dtypes is not None else [out_dtype]
    ne, no = len(extras), len(out_dtypes)
    in_place = epi is None and out_dtypes == [F32]
    use_acc = nk > 1 and not in_place

    def finish(acc, extra_refs, out_refs):
        res = acc if epi is None else epi(acc, *[r[...] for r in extra_refs])
        res = res if isinstance(res, (tuple, list)) else (res,)
        for r, v in zip(out_refs, res):
            r[...] = v.astype(r.dtype)

    def body(a_ref, b_ref, *rest):
        extra_refs, out_refs = rest[:ne], rest[ne:ne + no]
        part = lax.dot_general(a_ref[...].astype(BF16), b_ref[...].astype(BF16), dims, preferred_element_type=F32)
        if nk == 1:
            finish(part, extra_refs, out_refs)
            return
        k = pl.program_id(2)
        acc_ref = rest[-1] if use_acc else out_refs[0]

        @pl.when(k == 0)
        def _():
            acc_ref[...] = part

        @pl.when(k > 0)
        def _():
            acc_ref[...] += part

        if use_acc:
            @pl.when(k == nk - 1)
            def _():
                finish(acc_ref[...], extra_refs, out_refs)

    o_spec = pl.BlockSpec((tm, tn), lambda i, j, k: (i, j))
    res = pl.pallas_call(
        body, name=name, grid=(M // tm, N // tn, nk),
        in_specs=[a_spec, b_spec] + [o_spec] * ne, out_specs=[o_spec] * no,
        out_shape=[jax.ShapeDtypeStruct((M, N), dt) for dt in out_dtypes],
        scratch_shapes=[pltpu.VMEM((tm, tn), F32)] if use_acc else [],
        compiler_params=_cparams(("parallel", "parallel", "arbitrary")),
    )(a, b, *extras)
    return res[0] if no == 1 else res


def _row_spec(tr, C, off):
    return pl.BlockSpec((tr, C), lambda j, i: (i, off + j))


def _par_spec(k, C, off):
    if off is None:
        return pl.BlockSpec((k, C), lambda j, i: (0, 0))
    return pl.BlockSpec((k, C), lambda j, i: (0, off + j))


def _rowwise(name, fn, rows, params, outs, reds=(), *, S, ncb=1, tr=256):
    tr = min(tr, S)
    nr, npar, no = len(rows), len(params), len(outs)

    def body(*refs):
        j, i = pl.program_id(0), pl.program_id(1)
        ins = [r[...].astype(F32) for r in refs[:nr + npar]]
        res = fn(i * tr, j, *ins)
        res = res if isinstance(res, (tuple, list)) else (res,)
        out_refs = refs[nr + npar:]
        for r, v in zip(out_refs[:no], res[:no]):
            r[...] = v.astype(r.dtype)
        for (k, C, per_j), r, v in zip(reds, out_refs[no:], res[no:]):
            first = (i == 0) if per_j else jnp.logical_and(i == 0, j == 0)

            @pl.when(first)
            def _(r=r):
                r[...] = jnp.zeros_like(r)

            r[...] += v

    in_specs = [_row_spec(tr, C, off) for (_, off, C) in rows] + [_par_spec(a.shape[0], C, off) for (a, off, C) in params]
    out_specs = [pl.BlockSpec((tr, C), lambda j, i: (i, j)) for (C, _) in outs]
    out_specs += [pl.BlockSpec((k, C), (lambda j, i: (0, j)) if per_j else (lambda j, i: (0, 0))) for (k, C, per_j) in reds]
    out_shape = [jax.ShapeDtypeStruct((S, ncb * C), dt) for (C, dt) in outs]
    out_shape += [jax.ShapeDtypeStruct((k, C * (ncb if per_j else 1)), F32) for (k, C, per_j) in reds]
    res = pl.pallas_call(
        body, name=name, grid=(ncb, S // tr), in_specs=in_specs, out_specs=out_specs, out_shape=out_shape,
        compiler_params=_cparams(("arbitrary", "arbitrary")),
    )(*[a for (a, _, _) in rows], *[a for (a, _, _) in params])
    return res


def _rowwise_bwd(name, fn, rows, params, cots, drow, adds=None, *, S, ncb=1, tr=128):
    tr = min(tr, S)
    nr, npar, nc = len(rows), len(params), len(cots)
    adds = adds or [None] * nr
    add_list = [a for a in adds if a is not None]
    na = len(add_list)

    def body(*refs):
        j, i = pl.program_id(0), pl.program_id(1)
        ins = [r[...].astype(F32) for r in refs[:nr + npar]]
        cts = [r[...].astype(F32) for r in refs[nr + npar:nr + npar + nc]]
        add_refs = list(refs[nr + npar + nc:nr + npar + nc + na])
        out_refs = list(refs[nr + npar + nc + na:])

        def f(*a):
            res = fn(i * tr, j, *a)
            return tuple(res) if isinstance(res, (tuple, list)) else (res,)

        _, vjp = jax.vjp(f, *ins)
        grads = vjp(tuple(cts))
        for idx in range(nr):
            if drow[idx] is None:
                continue
            g = grads[idx]
            if adds[idx] is not None:
                g = g + add_refs.pop(0)[...].astype(F32)
            r = out_refs.pop(0)
            r[...] = g.astype(r.dtype)
        for idx in range(npar):
            per_j = params[idx][1] is not None
            first = (i == 0) if per_j else jnp.logical_and(i == 0, j == 0)
            r = out_refs.pop(0)

            @pl.when(first)
            def _(r=r):
                r[...] = jnp.zeros_like(r)

            r[...] += grads[nr + idx]

    in_specs = [_row_spec(tr, C, off) for (_, off, C) in rows]
    in_specs += [_par_spec(a.shape[0], C, off) for (a, off, C) in params]
    in_specs += [_row_spec(tr, C, off) for (_, off, C) in cots]
    in_specs += [_row_spec(tr, C, off) for (_, off, C) in add_list]
    out_specs, out_shape = [], []
    for idx in range(nr):
        if drow[idx] is not None:
            C = rows[idx][2]
            out_specs.append(pl.BlockSpec((tr, C), lambda j, i: (i, j)))
            out_shape.append(jax.ShapeDtypeStruct((S, ncb * C), drow[idx]))
    for (a, off, C) in params:
        per_j = off is not None
        out_specs.append(pl.BlockSpec((a.shape[0], C), (lambda j, i: (0, j)) if per_j else (lambda j, i: (0, 0))))
        out_shape.append(jax.ShapeDtypeStruct((a.shape[0], C * (ncb if per_j else 1)), F32))
    return pl.pallas_call(
        body, name=name, grid=(ncb, S // tr), in_specs=in_specs, out_specs=out_specs, out_shape=out_shape,
        compiler_params=_cparams(("arbitrary", "arbitrary")),
    )(*[a for (a, _, _) in rows], *[a for (a, _, _) in params], *[a for (a, _, _) in cots], *[a for (a, _, _) in add_list])


def _halo_rows(K):
    return 8 * ((K - 1 + 7) // 8)


def _inv_count(row0, tr, win):
    t = (row0 + lax.broadcasted_iota(jnp.int32, (tr, 1), 0)).astype(F32)
    return 1.0 / jnp.minimum(t + 1.0, win)


def _dwconv_fwd(name, x, x_off, w, *, S, C, bias=None, win=None, out_dtype=F32, cb=512, tr=256, comm=()):
    K = w.shape[0]
    cb, tr = _pick(C, cb), min(tr, S)
    HB = min(_halo_rows(K), tr)
    assert K - 1 <= HB and tr % HB == 0 and C % cb == 0
    nb = tr // HB
    extra = [a for a in (bias, win) if a is not None]

    def body(xh_ref, x_ref, w_ref, *rest):
        y_ref, xx = rest[-2], rest[-1]
        i = pl.program_id(1)
        xx[0:HB, :] = jnp.where(i > 0, xh_ref[...].astype(F32), 0.0)
        xx[HB:HB + tr, :] = x_ref[...].astype(F32)
        acc = jnp.zeros((tr, cb), F32)
        for jj in range(K):
            o = HB - (K - 1) + jj
            acc = acc + w_ref[jj:jj + 1, :] * xx[o:o + tr, :]
        if bias is not None:
            acc = acc + rest[0][...]
        if win is not None:
            acc = acc * _inv_count(i * tr, tr, rest[0][...]) - x_ref[...].astype(F32)
        y_ref[...] = acc.astype(y_ref.dtype)

    in_specs = [pl.BlockSpec((HB, cb), lambda j, i: (jnp.maximum(i * nb - 1, 0), x_off + j)),
                pl.BlockSpec((tr, cb), lambda j, i: (i, x_off + j)),
                pl.BlockSpec((K, cb), lambda j, i: (0, j))]
    in_specs += [pl.BlockSpec((1, cb), lambda j, i: (0, j)) for _ in extra]
    res = _pallas(body, (x, x, w, *extra), name=name, grid=(C // cb, S // tr), in_specs=in_specs,
                  out_specs=[pl.BlockSpec((tr, cb), lambda j, i: (i, j))],
                  out_shape=[jax.ShapeDtypeStruct((S, C), out_dtype)],
                  scratch_shapes=[pltpu.VMEM((HB + tr, cb), F32)], sem=("parallel", "arbitrary"), comm=comm)
    return (res[0], res[1:]) if comm else res[0]


def _dwconv_bwd(name, x, x_off, dy, w, *, S, C, win=None, want_dw=True, cb=512, tr=256, comm=()):
    K = w.shape[0]
    cb, tr = _pick(C, cb), min(tr, S)
    HB = min(_halo_rows(K), tr)
    nb, nt = tr // HB, S // tr

    def body(*refs):
        if want_dw:
            xh_ref, x_ref, dy_ref, dyn_ref, w_ref = refs[:5]
            rest = refs[5:]
        else:
            dy_ref, dyn_ref, w_ref = refs[:3]
            rest = refs[3:]
        i = pl.program_id(1)
        dyt = dy_ref[...].astype(F32)
        dyn = jnp.where(i < nt - 1, dyn_ref[...].astype(F32), 0.0)
        if win is not None:
            win_v = rest[0][...]
            rest = rest[1:]
            yy_t = dyt * _inv_count(i * tr, tr, win_v)
            dyn = dyn * _inv_count((i + 1) * tr, HB, win_v)
        else:
            yy_t = dyt
        if want_dw:
            dx_ref, dw_ref, db_ref, yy, xx = rest
        else:
            dx_ref, yy = rest
        yy[0:tr, :] = yy_t
        yy[tr:tr + HB, :] = dyn
        acc = jnp.zeros((tr, cb), F32)
        for jj in range(K):
            o = K - 1 - jj
            acc = acc + w_ref[jj:jj + 1, :] * yy[o:o + tr, :]
        if win is not None:
            acc = acc - dyt
        dx_ref[...] = acc.astype(dx_ref.dtype)
        if want_dw:
            xx[0:HB, :] = jnp.where(i > 0, xh_ref[...].astype(F32), 0.0)
            xx[HB:HB + tr, :] = x_ref[...].astype(F32)

            @pl.when(i == 0)
            def _():
                dw_ref[...] = jnp.zeros_like(dw_ref)
                db_ref[...] = jnp.zeros_like(db_ref)

            for jj in range(K):
                o = HB - (K - 1) + jj
                dw_ref[jj:jj + 1, :] += jnp.sum(dyt * xx[o:o + tr, :], axis=0, keepdims=True)
            db_ref[...] += jnp.sum(dyt, axis=0, keepdims=True)

    last = S // HB - 1
    in_specs, args = [], []
    if want_dw:
        in_specs += [pl.BlockSpec((HB, cb), lambda j, i: (jnp.maximum(i * nb - 1, 0), x_off + j)),
                     pl.BlockSpec((tr, cb), lambda j, i: (i, x_off + j))]
        args += [x, x]
    in_specs += [pl.BlockSpec((tr, cb), lambda j, i: (i, j)),
                 pl.BlockSpec((HB, cb), lambda j, i: (jnp.minimum((i + 1) * nb, last), j)),
                 pl.BlockSpec((K, cb), lambda j, i: (0, j))]
    args += [dy, dy, w]
    if win is not None:
        in_specs.append(pl.BlockSpec((1, cb), lambda j, i: (0, j)))
        args.append(win)
    out_specs = [pl.BlockSpec((tr, cb), lambda j, i: (i, j))]
    out_shape = [jax.ShapeDtypeStruct((S, C), F32)]
    scratch = [pltpu.VMEM((tr + HB, cb), F32)]
    if want_dw:
        out_specs += [pl.BlockSpec((K, cb), lambda j, i: (0, j)), pl.BlockSpec((1, cb), lambda j, i: (0, j))]
        out_shape += [jax.ShapeDtypeStruct((K, C), F32), jax.ShapeDtypeStruct((1, C), F32)]
        scratch.append(pltpu.VMEM((HB + tr, cb), F32))
    res = _pallas(body, args, name=name, grid=(C // cb, S // tr), in_specs=in_specs, out_specs=out_specs,
                  out_shape=out_shape, scratch_shapes=scratch, sem=("parallel", "arbitrary"), comm=comm)
    return (res[:len(out_specs)], res[len(out_specs):]) if comm else res


def _dot(a, b, dims, hi=False):
    if hi:
        return lax.dot_general(a, b, (dims, ((), ())), precision=HI, preferred_element_type=F32)
    return lax.dot_general(a.astype(BF16), b.astype(BF16), (dims, ((), ())), preferred_element_type=F32)


_NN, _NT, _TN = ((1,), (0,)), ((1,), (1,)), ((0,), (0,))


def _col(m, idx):
    lane = lax.broadcasted_iota(jnp.int32, m.shape, 1)
    return jnp.sum(jnp.where(lane == idx, m, 0.0), axis=1, keepdims=True)


def _row(m, idx):
    sub = lax.broadcasted_iota(jnp.int32, m.shape, 0)
    return jnp.sum(jnp.where(sub == idx, m, 0.0), axis=0, keepdims=True)


def _delta_chunk(q, k, v, beta, gcc, gcr, causal, strict, eye, scale, C):
    d = {}
    gam = jnp.where(causal, jnp.exp(jnp.where(causal, gcc - gcr, 0.0)), 0.0)
    eg = jnp.exp(gcc)
    g_last = _row(gcc, C - 1)
    d["gam"], d["eg"], d["g_last"] = gam, eg, g_last
    d["ek"] = jnp.exp(g_last - gcc)
    d["decay"] = jnp.exp(g_last)
    qs = q * scale
    kb = k * beta
    d["qs"], d["kb"] = qs, kb
    d["kk"] = _dot(kb, k, _NT)
    d["A"] = jnp.where(strict, d["kk"] * gam, 0.0)
    d["qk"] = _dot(qs, k, _NT)
    d["attn"] = jnp.where(causal, d["qk"] * gam, 0.0)
    d["vb"] = v * beta
    d["kbg"] = kb * eg
    d["qg"] = qs * eg
    d["kd"] = k * d["ek"]
    return d


def _tri_inverse(A, eye):
    def split(m):
        hi = m.astype(BF16)
        return hi, (m - hi.astype(F32)).astype(BF16)

    def dot3(a, b):
        return _dot(a[0], b[0], _NN) + (_dot(a[0], b[1], _NN) + _dot(a[1], b[0], _NN))

    P = -A
    T = eye + P
    n = 1
    while 2 * n < A.shape[0]:
        Ps = split(P)
        P = dot3(Ps, Ps)
        T = T + dot3(split(T), split(P))
        n *= 2
    return T


def _delta_fwd(name, qk, v, gb, gT, *, S, H, dh, comm=()):
    C = min(DN_CHUNK, S)
    N, W = S // C, H * dh
    scale = dh ** -0.5

    def body(qk_ref, v_ref, gb_ref, gT_ref, o_ref, sp_ref, T_ref, st):
        n = pl.program_id(0)

        @pl.when(n == 0)
        def _():
            st[...] = jnp.zeros_like(st)

        r = lax.broadcasted_iota(jnp.int32, (C, C), 0)
        c = lax.broadcasted_iota(jnp.int32, (C, C), 1)
        causal, strict = r >= c, r > c
        eye = (r == c).astype(F32)
        Lt = causal.astype(F32)
        gbv = gb_ref[...]
        gcum = _dot(Lt, gbv, _NN, hi=True)
        gcumT = _dot(gT_ref[0], Lt, _NT, hi=True)
        for h in range(H):
            sl = slice(h * dh, (h + 1) * dh)
            q, k, vv = qk_ref[:, sl], qk_ref[:, W + h * dh:W + (h + 1) * dh], v_ref[:, sl]
            d = _delta_chunk(q, k, vv, _col(gbv, h), _col(gcum, H + h), _row(gcumT, h), causal, strict, eye, scale, C)
            T = _tri_inverse(d["A"], eye)
            u = _dot(T, d["vb"], _NN)
            w = _dot(T, d["kbg"], _NN)
            s0 = st[h]
            sp_ref[0, h] = s0
            T_ref[0, h] = T
            v_new = u - _dot(w, s0, _NN)
            o_ref[:, sl] = _dot(d["qg"], s0, _NN) + _dot(d["attn"], v_new, _NN)
            st[h] = s0 * d["decay"] + _dot(d["kd"], v_new, _TN)

    res = _pallas(
        body, (qk, v, gb, gT), name=name, grid=(N,),
        in_specs=[pl.BlockSpec((C, 2 * W), lambda n: (n, 0)), pl.BlockSpec((C, W), lambda n: (n, 0)),
                  pl.BlockSpec((C, LANE), lambda n: (n, 0)), pl.BlockSpec((1, H, C), lambda n: (n, 0, 0))],
        out_specs=[pl.BlockSpec((C, W), lambda n: (n, 0)), pl.BlockSpec((1, H, dh, dh), lambda n: (n, 0, 0, 0)),
                   pl.BlockSpec((1, H, C, C), lambda n: (n, 0, 0, 0))],
        out_shape=[jax.ShapeDtypeStruct((S, W), F32), jax.ShapeDtypeStruct((N, H, dh, dh), F32),
                   jax.ShapeDtypeStruct((N, H, C, C), F32)],
        scratch_shapes=[pltpu.VMEM((H, dh, dh), F32)], sem=("arbitrary",), comm=comm)
    return res[:3], res[3:]


def _delta_bwd(name, qk, v, gb, gT, sp, Tm, do, *, S, H, dh, comm=()):
    C = min(DN_CHUNK, S)
    N, W = S // C, H * dh
    scale = dh ** -0.5

    def body(qk_ref, v_ref, gb_ref, gT_ref, sp_ref, T_ref, do_ref, dqk_ref, dv_ref, dgb_ref, ds):
        n = pl.program_id(0)

        @pl.when(n == 0)
        def _():
            ds[...] = jnp.zeros_like(ds)

        r = lax.broadcasted_iota(jnp.int32, (C, C), 0)
        c = lax.broadcasted_iota(jnp.int32, (C, C), 1)
        causal, strict = r >= c, r > c
        eye = (r == c).astype(F32)
        Lt = causal.astype(F32)
        ones = jnp.ones((C, LANE), F32)
        lane = lax.broadcasted_iota(jnp.int32, (C, LANE), 1)
        rowi = lax.broadcasted_iota(jnp.int32, (C, 1), 0)
        gbv = gb_ref[...]
        gcum = _dot(Lt, gbv, _NN, hi=True)
        gcumT = _dot(gT_ref[0], Lt, _NT, hi=True)
        dgc_all = jnp.zeros((C, LANE), F32)
        dbeta_all = jnp.zeros((C, LANE), F32)
        for h in range(H):
            sl = slice(h * dh, (h + 1) * dh)
            ksl = slice(W + h * dh, W + (h + 1) * dh)
            q, k, vv = qk_ref[:, sl], qk_ref[:, ksl], v_ref[:, sl]
            beta, gcc = _col(gbv, h), _col(gcum, H + h)
            d = _delta_chunk(q, k, vv, beta, gcc, _row(gcumT, h), causal, strict, eye, scale, C)
            T, s0, dO, dS = T_ref[0, h], sp_ref[0, h], do_ref[:, sl], ds[h]
            u = _dot(T, d["vb"], _NN)
            w = _dot(T, d["kbg"], _NN)
            v_new = u - _dot(w, s0, _NN)
            dv_new = _dot(d["attn"], dO, _TN) + _dot(d["kd"], dS, _NN)
            dattn = jnp.where(causal, _dot(dO, v_new, _NT), 0.0)
            dqg = _dot(dO, s0, _NT)
            dkd = _dot(v_new, dS, _NT)
            ddecay = jnp.sum(jnp.sum(s0 * dS, axis=1, keepdims=True), axis=0, keepdims=True)
            ds[h] = _dot(d["qg"], dO, _TN) + d["decay"] * dS - _dot(w, dv_new, _TN)
            dw = -_dot(dv_new, s0, _NT)
            dT = _dot(dv_new, d["vb"], _NT) + _dot(dw, d["kbg"], _NT)
            dvb = _dot(T, dv_new, _TN)
            dkbg = _dot(T, dw, _TN)
            dA = jnp.where(strict, -_dot(_dot(T, dT, _TN, hi=True), T, _NT, hi=True), 0.0)
            dkk = dA * d["gam"]
            dqk_m = dattn * d["gam"]
            m = dA * d["A"] + dattn * d["attn"]
            dkb = _dot(dkk, k, _NN) + dkbg * d["eg"]
            dk = _dot(dkk, d["kb"], _TN) + _dot(dqk_m, d["qs"], _TN) + dkd * d["ek"] + dkb * beta
            dqs = _dot(dqk_m, k, _NN) + dqg * d["eg"]
            r_kd = jnp.sum(dkd * d["kd"], axis=1, keepdims=True)
            dgc = (jnp.sum(m, axis=1, keepdims=True) - jnp.max(_dot(m, ones, _TN, hi=True), axis=1, keepdims=True)
                   + jnp.sum(dqg * d["qg"], axis=1, keepdims=True) - r_kd
                   + jnp.sum(dkbg * d["kbg"], axis=1, keepdims=True))
            dg_last = jnp.sum(r_kd, axis=0, keepdims=True) + ddecay * d["decay"]
            dgc = dgc + jnp.where(rowi == C - 1, dg_last, 0.0)
            dbeta = jnp.sum(dkb * k, axis=1, keepdims=True) + jnp.sum(dvb * vv, axis=1, keepdims=True)
            dqk_ref[:, sl] = dqs * scale
            dqk_ref[:, ksl] = dk
            dv_ref[:, sl] = dvb * beta
            dgc_all = dgc_all + jnp.where(lane == H + h, dgc, 0.0)
            dbeta_all = dbeta_all + jnp.where(lane == h, dbeta, 0.0)
        dgb_ref[...] = _dot(Lt, dgc_all, _TN, hi=True) + dbeta_all

    rev = lambda n: N - 1 - n
    res = _pallas(
        body, (qk, v, gb, gT, sp, Tm, do), name=name, grid=(N,),
        in_specs=[pl.BlockSpec((C, 2 * W), lambda n: (rev(n), 0)), pl.BlockSpec((C, W), lambda n: (rev(n), 0)),
                  pl.BlockSpec((C, LANE), lambda n: (rev(n), 0)), pl.BlockSpec((1, H, C), lambda n: (rev(n), 0, 0)),
                  pl.BlockSpec((1, H, dh, dh), lambda n: (rev(n), 0, 0, 0)),
                  pl.BlockSpec((1, H, C, C), lambda n: (rev(n), 0, 0, 0)),
                  pl.BlockSpec((C, W), lambda n: (rev(n), 0))],
        out_specs=[pl.BlockSpec((C, 2 * W), lambda n: (rev(n), 0)), pl.BlockSpec((C, W), lambda n: (rev(n), 0)),
                   pl.BlockSpec((C, LANE), lambda n: (rev(n), 0))],
        out_shape=[jax.ShapeDtypeStruct((S, 2 * W), F32), jax.ShapeDtypeStruct((S, W), F32),
                   jax.ShapeDtypeStruct((S, LANE), F32)],
        scratch_shapes=[pltpu.VMEM((H, dh, dh), F32)], sem=("arbitrary",), comm=comm)
    return res[:3], res[3:]


_ANY = pl.BlockSpec(memory_space=pl.ANY)


ICI_CHUNKS = 4
D2D_CHUNKS = 4


def _place():
    return lax.axis_index("x"), lax.axis_index("y"), lax.axis_index("c")


def _row_chunks(rows, n):
    n = max(1, min(n, rows // 8))
    while n > 1 and (rows % n or (rows // n) % 8):
        n -= 1
    return [(k * (rows // n), rows // n) for k in range(n)]


class _Plug:
    def __init__(self, ins, outs, sems, start, finish, after):
        self.ins, self.outs, self.sems, self.start, self.finish, self.after = ins, outs, sems, start, finish, after


def _run_plug(name, plug):
    def body(*refs):
        ni, no = len(plug.ins), len(plug.outs)
        plug.start(refs[:ni], refs[ni:ni + no], refs[ni + no:])
        plug.finish(refs[:ni], refs[ni:ni + no], refs[ni + no:])

    return pl.pallas_call(body, name=name, in_specs=[_ANY] * len(plug.ins), out_specs=[_ANY] * len(plug.outs),
                          out_shape=list(plug.outs), scratch_shapes=list(plug.sems))(*plug.ins)


def _gather_plug(flat):
    R, L = flat.shape
    Rh = R // 2
    ici = _row_chunks(Rh, ICI_CHUNKS)
    sub = _row_chunks(ici[0][1], D2D_CHUNKS)
    ni, ns = len(ici), len(sub)

    def parts(ins, outs, sems):
        (x_ref,), (out_ref,), (send_sems, recv_sems) = ins, outs, sems
        x, y, c = _place()
        chips = [(1 - x, y), (x, 1 - y), (1 - x, 1 - y)]

        def rows(px, py, pc, r0, n):
            return out_ref.at[2 * px + py, pl.ds(pc * Rh + r0, n), :]

        def copy(k, src, dst, to):
            return pltpu.make_async_remote_copy(src_ref=src, dst_ref=dst, send_sem=send_sems.at[k],
                                                recv_sem=recv_sems.at[k], device_id=to, device_id_type=MESH)

        first = [copy(k * ni + q, x_ref.at[pl.ds(c * Rh + r0, n), :], rows(x, y, c, r0, n), (*chip, c))
                 for k, chip in enumerate(chips) for q, (r0, n) in enumerate(ici)]
        return x_ref, (x, y, c), chips, rows, copy, first

    def start(ins, outs, sems):
        for cp in parts(ins, outs, sems)[-1]:
            cp.start()

    def finish(ins, outs, sems):
        x_ref, (x, y, c), chips, rows, copy, first = parts(ins, outs, sems)
        sibling = (x, y, 1 - c)
        passed = []
        for k, chip in enumerate(chips):
            for q, (r0, n) in enumerate(ici):
                copy(k * ni + q, x_ref.at[pl.ds(r0, n), :], rows(*chip, c, r0, n), (*chip, c)).wait_recv()
                for t, (s0, m) in enumerate(sub):
                    cp = copy(3 * ni + (k * ni + q) * ns + t, rows(*chip, c, r0 + s0, m), rows(*chip, c, r0 + s0, m), sibling)
                    cp.start()
                    passed.append(cp)
        for k, chip in enumerate(chips):
            for q, (r0, n) in enumerate(ici):
                for t, (s0, m) in enumerate(sub):
                    copy(3 * ni + (k * ni + q) * ns + t, x_ref.at[pl.ds(r0, m), :], rows(*chip, 1 - c, r0 + s0, m),
                         sibling).wait_recv()
        for cp in first + passed:
            cp.wait_send()

    def after(res):
        return lax.dynamic_update_slice(res[0], flat[None], (2 * lax.axis_index("x") + lax.axis_index("y"), 0, 0))

    nsem = 3 * ni * (1 + ns)
    return _Plug([flat], [jax.ShapeDtypeStruct((N_CHIPS, R, L), flat.dtype)],
                 [pltpu.SemaphoreType.DMA((nsem,)), pltpu.SemaphoreType.DMA((nsem,))], start, finish, after)


def _sibling_split(name, g):
    _, R, L = g.shape
    Rh = R // 2

    chunks = [(j, r0, n) for j in range(N_CHIPS) for (r0, n) in _row_chunks(Rh, D2D_CHUNKS)]

    def body(g_ref, got_ref, send_sems, recv_sems):
        x, y, c = _place()
        cps = [pltpu.make_async_remote_copy(src_ref=g_ref.at[j, pl.ds((1 - c) * Rh + r0, n), :],
                                            dst_ref=got_ref.at[j, pl.ds(r0, n), :], send_sem=send_sems.at[k],
                                            recv_sem=recv_sems.at[k], device_id=(x, y, 1 - c), device_id_type=MESH)
               for k, (j, r0, n) in enumerate(chunks)]
        for cp in cps:
            cp.start()
        for cp in cps:
            cp.wait()

    sems = pltpu.SemaphoreType.DMA((len(chunks),))
    got = pl.pallas_call(
        body, name=name, in_specs=[_ANY], out_specs=_ANY, out_shape=jax.ShapeDtypeStruct((N_CHIPS, Rh, L), g.dtype),
        scratch_shapes=[sems, sems],
    )(g)
    own = lax.dynamic_slice(g, (0, lax.axis_index("c") * Rh, 0), (N_CHIPS, Rh, L))
    return own, got


def _exchange_plug(p):
    Rh = p.shape[1]
    ici = _row_chunks(Rh, ICI_CHUNKS)
    ni = len(ici)

    def sends(ins, outs, sems):
        (p_ref,), (q_ref,), (send_sems, recv_sems) = ins, outs, sems
        x, y, c = _place()
        me = 2 * x + y
        chips = [(1 - x, y), (x, 1 - y), (1 - x, 1 - y)]
        return [pltpu.make_async_remote_copy(src_ref=p_ref.at[2 * cx + cy, pl.ds(r0, n), :],
                                             dst_ref=q_ref.at[me, pl.ds(r0, n), :],
                                             send_sem=send_sems.at[k * ni + q], recv_sem=recv_sems.at[k * ni + q],
                                             device_id=(cx, cy, c), device_id_type=MESH)
                for k, (cx, cy) in enumerate(chips) for q, (r0, n) in enumerate(ici)]

    def start(ins, outs, sems):
        for cp in sends(ins, outs, sems):
            cp.start()

    def finish(ins, outs, sems):
        (p_ref,), (q_ref,), (send_sems, recv_sems) = ins, outs, sems
        x, y, c = _place()
        me = 2 * x + y
        chips = [(1 - x, y), (x, 1 - y), (1 - x, 1 - y)]
        for k, (cx, cy) in enumerate(chips):
            for q, (r0, n) in enumerate(ici):
                pltpu.make_async_remote_copy(src_ref=p_ref.at[me, pl.ds(r0, n), :],
                                             dst_ref=q_ref.at[2 * cx + cy, pl.ds(r0, n), :],
                                             send_sem=send_sems.at[k * ni + q], recv_sem=recv_sems.at[k * ni + q],
                                             device_id=(cx, cy, c), device_id_type=MESH).wait_recv()
        for cp in sends(ins, outs, sems):
            cp.wait_send()

    def after(res):
        me = 2 * lax.axis_index("x") + lax.axis_index("y")
        return lax.dynamic_update_slice(res[0], lax.dynamic_slice(p, (me, 0, 0), (1,) + p.shape[1:]), (me, 0, 0))

    return _Plug([p], [jax.ShapeDtypeStruct(p.shape, p.dtype)],
                 [pltpu.SemaphoreType.DMA((3 * ni,)), pltpu.SemaphoreType.DMA((3 * ni,))], start, finish, after)


def _sibling_join(name, half):
    Rh, L = half.shape

    chunks = _row_chunks(Rh, 2 * D2D_CHUNKS)

    def body(h_ref, out_ref, send_sems, recv_sems):
        x, y, c = _place()
        cps = [pltpu.make_async_remote_copy(src_ref=h_ref.at[pl.ds(r0, n), :], dst_ref=out_ref.at[pl.ds(c * Rh + r0, n), :],
                                            send_sem=send_sems.at[k], recv_sem=recv_sems.at[k], device_id=(x, y, 1 - c),
                                            device_id_type=MESH)
               for k, (r0, n) in enumerate(chunks)]
        for cp in cps:
            cp.start()
        for k, (r0, n) in enumerate(chunks):
            pltpu.make_async_remote_copy(src_ref=h_ref.at[pl.ds(r0, n), :],
                                         dst_ref=out_ref.at[pl.ds((1 - c) * Rh + r0, n), :], send_sem=send_sems.at[k],
                                         recv_sem=recv_sems.at[k], device_id=(x, y, 1 - c), device_id_type=MESH).wait_recv()
        for cp in cps:
            cp.wait_send()

    sems = pltpu.SemaphoreType.DMA((len(chunks),))
    out = pl.pallas_call(
        body, name=name, in_specs=[_ANY], out_specs=_ANY, out_shape=jax.ShapeDtypeStruct((2 * Rh, L), half.dtype),
        scratch_shapes=[sems, sems],
    )(half)
    return lax.dynamic_update_slice(out, half, (lax.axis_index("c") * Rh, 0))


def _add_pairs(name, a, b, out_dtype):
    n, Rh, L = a.shape
    tr = _pick(Rh, 512, 8)

    def body(a_ref, b_ref, o_ref):
        o_ref[...] = (a_ref[...].astype(F32) + b_ref[...].astype(F32)).astype(o_ref.dtype)

    spec = pl.BlockSpec((1, tr, L), lambda j, i: (j, i, 0))
    return pl.pallas_call(body, name=name, grid=(n, Rh // tr), in_specs=[spec, spec], out_specs=spec,
                          out_shape=jax.ShapeDtypeStruct(a.shape, out_dtype),
                          compiler_params=_cparams(("parallel", "parallel")))(a, b)


def _sum_chips(name, q):
    n, Rh, L = q.shape
    tr = _pick(Rh, 512, 8)

    def body(q_ref, o_ref):
        acc = q_ref[0].astype(F32)
        for s in range(1, n):
            acc = acc + q_ref[s].astype(F32)
        o_ref[...] = acc

    return pl.pallas_call(body, name=name, grid=(Rh // tr,),
                          in_specs=[pl.BlockSpec((n, tr, L), lambda i: (0, i, 0))],
                          out_specs=pl.BlockSpec((tr, L), lambda i: (i, 0)),
                          out_shape=jax.ShapeDtypeStruct((Rh, L), F32),
                          compiler_params=_cparams(("parallel",)))(q)


def _adamw(name, w, m, v, g):
    shape, size = w.shape, w.size
    rows = -(-size // FLAT_L)
    rows_p = -(-rows // ADAM_ROWS) * ADAM_ROWS
    pad = rows_p * FLAT_L - size

    def flat2d(a):
        a = a.reshape(-1)
        if pad:
            a = jnp.pad(a, (0, pad), constant_values=1.0)
        return a.reshape(rows_p, FLAT_L)

    c1 = 1.0 / (1.0 - ADAM_B1 ** ADAM_STEP)
    c2 = 1.0 / (1.0 - ADAM_B2 ** ADAM_STEP)

    def body(w_ref, m_ref, v_ref, g_ref, go_ref, d_ref, mo_ref, vo_ref):
        g = g_ref[...]
        wv = w_ref[...]
        mn = ADAM_B1 * m_ref[...] + (1.0 - ADAM_B1) * g
        vn = ADAM_B2 * v_ref[...] + (1.0 - ADAM_B2) * (g * g)
        go_ref[...] = g
        mo_ref[...] = mn
        vo_ref[...] = vn
        d_ref[...] = -ADAM_LR * ((mn * c1) / (jnp.sqrt(vn * c2) + ADAM_EPS) + ADAM_WD * wv)

    spec = pl.BlockSpec((ADAM_ROWS, FLAT_L), lambda i: (i, 0))
    shp = jax.ShapeDtypeStruct((rows_p, FLAT_L), F32)
    outs = pl.pallas_call(
        body, name=name, grid=(rows_p // ADAM_ROWS,),
        in_specs=[spec] * 4, out_specs=[spec] * 4, out_shape=[shp] * 4, compiler_params=_cparams(("parallel",)),
    )(flat2d(w), flat2d(m), flat2d(v), flat2d(g))

    def back(a):
        a = a.reshape(-1)
        if pad:
            a = a[:size]
        return a.reshape(shape)

    return tuple(back(a) for a in outs)


SEG_ROWS = 16


def _seg_rows(size):
    rows = -(-size // FLAT_L)
    return -(-rows // SEG_ROWS) * SEG_ROWS


def _pack(pieces, dtype, row_mult):
    segs, offs, r = [], [], 0
    for a in pieces:
        rows = _seg_rows(a.size)
        flat = a.reshape(-1).astype(dtype)
        flat = jnp.pad(flat, (0, rows * FLAT_L - a.size))
        segs.append(flat.reshape(rows, FLAT_L))
        offs.append(r)
        r += rows
    tail = -r % row_mult
    if tail:
        segs.append(jnp.zeros((tail, FLAT_L), dtype))
    return jnp.concatenate(segs, axis=0), offs


def _segment(flat, off, shape):
    size = math.prod(shape)
    return flat[off:off + _seg_rows(size)].reshape(-1)[:size].reshape(shape)


def _gather_unit(shards, axes, dtype):
    flat, offs = _pack(shards, dtype, 64)
    plug = _gather_plug(flat)

    def unpack(full):
        outs = []
        for a, ax, off in zip(shards, axes, offs):
            rows = _seg_rows(a.size)
            seg = full[:, off:off + rows].reshape(N_CHIPS, -1)[:, :a.size].reshape((N_CHIPS,) + a.shape)
            outs.append(jnp.concatenate([seg[j] for j in range(N_CHIPS)], axis=ax))
        return outs

    return plug, unpack


def _reduce_unit(name, grads, axes):
    pieces = [[] for _ in range(N_CHIPS)]
    for g, ax in zip(grads, axes):
        parts = jnp.split(g, N_CHIPS, axis=ax) if ax is not None else [g] * N_CHIPS
        for jc in range(N_CHIPS):
            pieces[jc].append(parts[jc])
    packed = [_pack(pieces[jc], BF16, 512) for jc in range(N_CHIPS)]
    gsend = jnp.stack([pk[0] for pk in packed])
    own, got = _sibling_split(name + "_split", gsend)
    pair = _add_pairs(name + "_add", own, got, BF16)
    return _exchange_plug(pair), packed[0][1], [a.shape for a in pieces[0]]


def _f_rms(row0, j, x, g):
    return _rms(x, g)


def _f_mid(row0, j, x, y, g_a, g_b):
    xn = x + _rms(y, g_a)
    return xn, _rms(xn, g_b)


def _f_resid(row0, j, x, y, g):
    return x + _rms(y, g)


def _relu2(u):
    r = jnp.maximum(u, 0.0)
    return r * r


def _relu2_bwd(d_act, act):
    return d_act * (2.0 * jnp.sqrt(act.astype(F32)))


def _f_l2silu(row0, j, c):
    a = _silu(c)
    return a * lax.rsqrt(jnp.sum(a * a, axis=-1, keepdims=True) + EPS)


def _f_silu(row0, j, c):
    return _silu(c)


def _f_scale(row0, j, y, s):
    return y * s


def _f_glu(row0, j, a, gate):
    return a * _sigmoid(gate)


def _f_lnsilu(row0, j, u, g, b):
    mu = jnp.mean(u, axis=-1, keepdims=True)
    uc = u - mu
    return _silu(uc * lax.rsqrt(jnp.mean(uc * uc, axis=-1, keepdims=True) + EPS) * g + b)


def _f_outgate(row0, j, o, z, g):
    return _rms(o, g) * _silu(z)


def _make_gates(H):
    def f(row0, j, ba, alog, dt):
        lane = lax.broadcasted_iota(jnp.int32, ba.shape, 1)
        beta = _sigmoid(ba)
        g = -jnp.exp(alog) * _softplus(ba + dt)
        return jnp.where(lane < H, beta, jnp.where(lane < 2 * H, g, 0.0))
    return f


def _f_loss(row0, j, y, t):
    e = y - t
    loss = 0.5 * jnp.sum(jnp.mean(e * e, axis=-1, keepdims=True), axis=0, keepdims=True)
    return e * (1.0 / y.shape[-1]), jnp.broadcast_to(loss, (1, LANE))


def _lane_row(vec, start):
    return jnp.pad(vec.astype(F32)[None, :], ((0, 0), (start, LANE - start - vec.shape[0])))


def kernel(x, norm_mix_pre, norm_mix_post, norm_mlp_pre, norm_mlp_post, even_w_in, even_conv, even_a_log, even_dt_bias, even_dn_norm, even_pool_w, even_pool_scale, even_w_out, odd_w_in, odd_dw, odd_dw_b, odd_ln_g, odd_ln_b, odd_w_out, mlp_w_up, mlp_w_down, loss_target, m_norm_mix_pre, m_norm_mix_post, m_norm_mlp_pre, m_norm_mlp_post, m_even_w_in, m_even_conv, m_even_a_log, m_even_dt_bias, m_even_dn_norm, m_even_pool_w, m_even_pool_scale, m_even_w_out, m_odd_w_in, m_odd_dw, m_odd_dw_b, m_odd_ln_g, m_odd_ln_b, m_odd_w_out, m_mlp_w_up, m_mlp_w_down, v_norm_mix_pre, v_norm_mix_post, v_norm_mlp_pre, v_norm_mlp_post, v_even_w_in, v_even_conv, v_even_a_log, v_even_dt_bias, v_even_dn_norm, v_even_pool_w, v_even_pool_scale, v_even_w_out, v_odd_w_in, v_odd_dw, v_odd_dw_b, v_odd_ln_g, v_odd_ln_b, v_odd_w_out, v_mlp_w_up, v_mlp_w_down):
    names = ["norm_mix_pre", "norm_mix_post", "norm_mlp_pre", "norm_mlp_post", "even_w_in", "even_conv", "even_a_log",
             "even_dt_bias", "even_dn_norm", "even_pool_w", "even_pool_scale", "even_w_out", "odd_w_in", "odd_dw",
             "odd_dw_b", "odd_ln_g", "odd_ln_b", "odd_w_out", "mlp_w_up", "mlp_w_down"]
    W = dict(zip(names, (norm_mix_pre, norm_mix_post, norm_mlp_pre, norm_mlp_post, even_w_in, even_conv, even_a_log,
                         even_dt_bias, even_dn_norm, even_pool_w, even_pool_scale, even_w_out, odd_w_in, odd_dw,
                         odd_dw_b, odd_ln_g, odd_ln_b, odd_w_out, mlp_w_up, mlp_w_down)))
    Mo = dict(zip(names, (m_norm_mix_pre, m_norm_mix_post, m_norm_mlp_pre, m_norm_mlp_post, m_even_w_in, m_even_conv,
                          m_even_a_log, m_even_dt_bias, m_even_dn_norm, m_even_pool_w, m_even_pool_scale, m_even_w_out,
                          m_odd_w_in, m_odd_dw, m_odd_dw_b, m_odd_ln_g, m_odd_ln_b, m_odd_w_out, m_mlp_w_up,
                          m_mlp_w_down)))
    Vo = dict(zip(names, (v_norm_mix_pre, v_norm_mix_post, v_norm_mlp_pre, v_norm_mlp_post, v_even_w_in, v_even_conv,
                          v_even_a_log, v_even_dt_bias, v_even_dn_norm, v_even_pool_w, v_even_pool_scale, v_even_w_out,
                          v_odd_w_in, v_odd_dw, v_odd_dw_b, v_odd_ln_g, v_odd_ln_b, v_odd_w_out, v_mlp_w_up,
                          v_mlp_w_down)))
    shard_axis = {"even_w_in": 2, "even_conv": 2, "even_pool_w": 2, "even_w_out": 1, "odd_w_in": 2, "odd_dw": 2,
                  "odd_dw_b": 1, "odd_ln_g": 1, "odd_ln_b": 1, "odd_w_out": 1, "mlp_w_up": 2, "mlp_w_down": 1}

    S, D = x.shape[1], x.shape[2]
    depth = norm_mix_pre.shape[0]
    H = even_a_log.shape[1]
    dh = even_dn_norm.shape[1]
    DNW = H * dh
    PW = even_pool_scale.shape[1]
    G = len(POOL_WINDOWS)
    PG = PW // G
    KC = even_conv.shape[1]
    BAW = 2 * LANE
    P_COLS = 4 * DNW + PW + BAW
    x2 = x.reshape(S, D)
    tgt = loss_target.reshape(S, D)

    small = ["even_conv", "odd_dw", "odd_dw_b", "odd_ln_g", "odd_ln_b"]
    plug_s, unpack_s = _gather_unit([W[n] for n in small], [shard_axis[n] for n in small], F32)
    full = dict(zip(small, unpack_s(plug_s.after(_run_plug("gather_small", plug_s)))))
    CW = odd_w_out.shape[1] * N_CHIPS
    wfull = {}

    def mixer_weights(i):
        return [(n, i // 2) for n in (("even_w_in", "even_pool_w", "even_w_out") if i % 2 == 0 else ("odd_w_in", "odd_w_out"))]

    def gather_unit(keys):
        plug, unpack = _gather_unit([W[n][l] for n, l in keys], [shard_axis[n] - 1 for n, _ in keys], BF16)
        return plug, lambda res: wfull.update(zip(keys, unpack(plug.after(res))))

    def riding_gather(i):
        keys = [("mlp_w_up", i), ("mlp_w_down", i)] + (mixer_weights(i + 1) if i + 1 < depth else [])
        return gather_unit(keys)

    plug0, put0 = gather_unit(mixer_weights(0))
    put0(_run_plug("gather_first", plug0))

    def even_w_in_layout(w):
        o1 = 4 * DNW
        return jnp.concatenate([w[:, :o1], w[:, o1 + 2 * H:], w[:, o1:o1 + 2 * H],
                                jnp.zeros((w.shape[0], BAW - 2 * H), w.dtype)], axis=1)

    def even_w_in_unlayout(g):
        o1 = 4 * DNW
        return jnp.concatenate([g[:, :o1], g[:, o1 + PW:o1 + PW + 2 * H], g[:, o1:o1 + PW]], axis=1)

    def pool_blockdiag(pw):
        return jnp.concatenate([jnp.pad(pw[gi], ((0, 0), (gi * PG, PW - (gi + 1) * PG))) for gi in range(G)], axis=0)

    pool_taps = max(POOL_WINDOWS)
    tap = jnp.arange(pool_taps)[:, None]
    win_c = jnp.repeat(jnp.asarray(POOL_WINDOWS, F32), PG)[None, :]
    pool_mask = (tap >= pool_taps - win_c).astype(F32)

    grads = {}
    tr_full = 128 if D > 1024 else 256

    saved = []
    xc = x2
    for i in range(depth):
        jl = i // 2
        sv = {"x_in": xc}
        g1, g2, g3, g4 = (W[n][i:i + 1] for n in ("norm_mix_pre", "norm_mix_post", "norm_mlp_pre", "norm_mlp_post"))
        (h,) = _rowwise(f"l{i}_rms_in", _f_rms, [(xc, 0, D)], [(g1, None, D)], [(D, BF16)], S=S, tr=tr_full)
        sv["h"] = h
        ride, put = riding_gather(i)
        if i % 2 == 0:
            w_in = even_w_in_layout(wfull["even_w_in", jl])
            p = _matmul(f"l{i}_w_in", h, w_in, "nn", tn=768)
            conv_w = full["even_conv"][jl]
            c = _dwconv_fwd(f"l{i}_conv", p, 0, conv_w, S=S, C=3 * DNW)
            (qk,) = _rowwise(f"l{i}_qk", _f_l2silu, [(c, 0, dh)], [], [(dh, F32)], S=S, ncb=2 * H, tr=1024)
            (vv,) = _rowwise(f"l{i}_v", _f_silu, [(c, 2 * DNW // dh, dh)], [], [(dh, F32)], S=S, ncb=H, tr=1024)
            alog = _lane_row(W["even_a_log"][jl], H)
            dtb = _lane_row(W["even_dt_bias"][jl], H)
            ba_off = (4 * DNW + PW) // LANE
            (gb,) = _rowwise(f"l{i}_gates", _make_gates(H), [(p, ba_off, LANE)], [(alog, None, LANE), (dtb, None, LANE)],
                             [(LANE, F32)], S=S, tr=1024)
            Cn = min(DN_CHUNK, S)
            gT = gb[:, H:2 * H].reshape(S // Cn, Cn, H).transpose(0, 2, 1)
            (o, sp, Tm), landed = _delta_fwd(f"l{i}_delta", qk, vv, gb, gT, S=S, H=H, dh=dh, comm=[ride])
            put(landed)
            dn = W["even_dn_norm"][jl][None, :]
            (on,) = _rowwise(f"l{i}_outgate", _f_outgate, [(o, 0, dh), (p, 3 * DNW // dh, dh)], [(dn, None, dh)],
                             [(dh, BF16)], S=S, ncb=H, tr=1024)
            pcb = _pick(PW, 512)
            pooled = _dwconv_fwd(f"l{i}_pool", p, 4 * DNW // pcb, pool_mask, S=S, C=PW, win=win_c, out_dtype=BF16, cb=pcb)
            wbd = pool_blockdiag(wfull["even_pool_w", jl])
            ypre = _matmul(f"l{i}_pool_w", pooled, wbd, "nn")
            psc = W["even_pool_scale"][jl][None, :]
            (ypool,) = _rowwise(f"l{i}_pool_scale", _f_scale, [(ypre, 0, PW)], [(psc, None, PW)], [(PW, BF16)], S=S)
            mixin = jnp.concatenate([on, ypool], axis=1)
            mix = _matmul(f"l{i}_w_out", mixin, wfull["even_w_out", jl], "nn")
            sv.update(p=p, c=c, qk=qk, v=vv, gb=gb, gT=gT, o=o, sp=sp, Tm=Tm, pooled=pooled, ypre=ypre, mixin=mixin,
                      w_in=w_in, wbd=wbd, alog=alog, dtb=dtb, dn=dn, psc=psc, conv_w=conv_w)
        else:
            p = _matmul(f"l{i}_w_in", h, wfull["odd_w_in", jl], "nn")
            ocb = _pick(CW, 1024)
            (u0,) = _rowwise(f"l{i}_glu", _f_glu, [(p, 0, ocb), (p, CW // ocb, ocb)], [], [(ocb, F32)], S=S,
                             ncb=CW // ocb)
            dw_w, dw_b = full["odd_dw"][jl], full["odd_dw_b"][jl][None, :]
            u1, landed = _dwconv_fwd(f"l{i}_dwconv", u0, 0, dw_w, S=S, C=CW, bias=dw_b, comm=[ride])
            put(landed)
            lg, lb = full["odd_ln_g"][jl][None, :], full["odd_ln_b"][jl][None, :]
            (u2,) = _rowwise(f"l{i}_lnsilu", _f_lnsilu, [(u1, 0, CW)], [(lg, None, CW), (lb, None, CW)], [(CW, BF16)],
                             S=S, tr=tr_full)
            mix = _matmul(f"l{i}_w_out", u2, wfull["odd_w_out", jl], "nn")
            sv.update(p=p, u0=u0, u1=u1, mixin=u2, dw_w=dw_w, lg=lg, lb=lb)
        x_mid, h2 = _rowwise(f"l{i}_mid", _f_mid, [(xc, 0, D), (mix, 0, D)], [(g2, None, D), (g3, None, D)],
                             [(D, F32), (D, BF16)], S=S, tr=tr_full)
        act = _matmul(f"l{i}_w_up", h2, wfull["mlp_w_up", i], "nn", epi=_relu2, out_dtypes=[BF16])
        ff = _matmul(f"l{i}_w_down", act, wfull["mlp_w_down", i], "nn")
        (x_out,) = _rowwise(f"l{i}_out", _f_resid, [(x_mid, 0, D), (ff, 0, D)], [(g4, None, D)], [(D, F32)], S=S,
                            tr=tr_full)
        sv.update(mix=mix, h2=h2, act=act, ff=ff, g=(g1, g2, g3, g4))
        saved.append(sv)
        xc = x_out

    dy, loss_row = _rowwise("loss", _f_loss, [(xc, 0, D), (tgt, 0, D)], [], [(D, F32)], [(1, LANE, False)], S=S,
                            tr=tr_full)
    loss = lax.psum(loss_row[0, 0], ("x", "y", "c"))

    def mixer_params(i):
        ns = (("even_w_in", "even_conv", "even_a_log", "even_dt_bias", "even_dn_norm", "even_pool_w", "even_pool_scale",
               "even_w_out") if i % 2 == 0 else ("odd_w_in", "odd_dw", "odd_dw_b", "odd_ln_g", "odd_ln_b", "odd_w_out"))
        return [(n, i // 2) for n in ns] + [("norm_mix_pre", i)]

    def mlp_params(i):
        return [(n, i) for n in ("mlp_w_up", "mlp_w_down", "norm_mlp_pre", "norm_mlp_post", "norm_mix_post")]

    def reduce_unit(name, keys):
        axes = [shard_axis[n] - 1 if n in shard_axis else None for n, _ in keys]
        plug, offs, shapes = _reduce_unit(name, [grads[k] for k in keys], axes)

        def unit(res):
            return keys, offs, shapes, _sum_chips(name + "_sum", plug.after(res))

        return plug, unit

    units = []
    dx = dy
    for i in reversed(range(depth)):
        jl = i // 2
        sv = saved[i]
        g1, g2, g3, g4 = sv["g"]
        d_ff, dg4 = _rowwise_bwd(f"l{i}_out_b", _f_rms, [(sv["ff"], 0, D)], [(g4, None, D)], [(dx, 0, D)], [BF16],
                                 S=S, tr=tr_full)
        grads["norm_mlp_post", i] = dg4[0]
        du = _matmul(f"l{i}_w_down_bx", d_ff, wfull["mlp_w_down", i], "nt", epi=_relu2_bwd, extras=[sv["act"]],
                     out_dtypes=[BF16])
        grads["mlp_w_down", i] = _matmul(f"l{i}_w_down_bw", sv["act"], d_ff, "tn")
        dh2 = _matmul(f"l{i}_w_up_bx", du, wfull["mlp_w_up", i], "nt")
        grads["mlp_w_up", i] = _matmul(f"l{i}_w_up_bw", sv["h2"], du, "tn")
        dx, d_mix, dg2, dg3 = _rowwise_bwd(
            f"l{i}_mid_b", _f_mid, [(sv["x_in"], 0, D), (sv["mix"], 0, D)], [(g2, None, D), (g3, None, D)],
            [(dx, 0, D), (dh2, 0, D)], [F32, BF16], S=S, tr=tr_full)
        grads["norm_mix_post", i], grads["norm_mlp_pre", i] = dg2[0], dg3[0]
        keys = (mixer_params(i + 1) if i + 1 < depth else []) + mlp_params(i)
        ride, unit = reduce_unit(f"red{i}", keys)
        if i % 2 == 0:
            d_mixin = _matmul(f"l{i}_w_out_bx", d_mix, wfull["even_w_out", jl], "nt")
            grads["even_w_out", jl] = _matmul(f"l{i}_w_out_bw", sv["mixin"], d_mix, "tn")
            p = sv["p"]
            pcb = _pick(PW, 512)
            d_ypre, dpsc = _rowwise_bwd(f"l{i}_pool_scale_b", _f_scale, [(sv["ypre"], 0, PW)], [(sv["psc"], None, PW)],
                                        [(d_mixin, DNW // PW, PW)], [BF16], S=S)
            grads["even_pool_scale", jl] = dpsc[0]
            d_pooled = _matmul(f"l{i}_pool_w_bx", d_ypre, sv["wbd"], "nt")
            dwbd = _matmul(f"l{i}_pool_w_bw", sv["pooled"], d_ypre, "tn")
            grads["even_pool_w", jl] = jnp.stack([dwbd[gi * PG:(gi + 1) * PG, gi * PG:(gi + 1) * PG] for gi in range(G)])
            d_xp = _dwconv_bwd(f"l{i}_pool_b", None, 0, d_pooled, pool_mask, S=S, C=PW, win=win_c, want_dw=False,
                               cb=pcb)[0]
            d_o, d_z, ddn = _rowwise_bwd(f"l{i}_outgate_b", _f_outgate, [(sv["o"], 0, dh), (p, 3 * DNW // dh, dh)],
                                         [(sv["dn"], None, dh)], [(d_mixin, 0, dh)], [F32, F32], S=S, ncb=H, tr=1024)
            grads["even_dn_norm", jl] = ddn[0]
            (dqk, dv, dgb), landed = _delta_bwd(f"l{i}_delta_b", sv["qk"], sv["v"], sv["gb"], sv["gT"], sv["sp"],
                                                sv["Tm"], d_o, S=S, H=H, dh=dh, comm=[ride])
            ba_off = (4 * DNW + PW) // LANE
            d_ba, dalog, ddtb = _rowwise_bwd(f"l{i}_gates_b", _make_gates(H), [(p, ba_off, LANE)],
                                             [(sv["alog"], None, LANE), (sv["dtb"], None, LANE)], [(dgb, 0, LANE)],
                                             [F32], S=S, tr=1024)
            grads["even_a_log", jl], grads["even_dt_bias", jl] = dalog[0, H:2 * H], ddtb[0, H:2 * H]
            (dc_qk,) = _rowwise_bwd(f"l{i}_qk_b", _f_l2silu, [(sv["c"], 0, dh)], [], [(dqk, 0, dh)], [F32], S=S,
                                    ncb=2 * H, tr=1024)
            (dc_v,) = _rowwise_bwd(f"l{i}_v_b", _f_silu, [(sv["c"], 2 * DNW // dh, dh)], [], [(dv, 0, dh)], [F32], S=S,
                                   ncb=H, tr=1024)
            dc = jnp.concatenate([dc_qk, dc_v], axis=1)
            d_qkv, dconv, _ = _dwconv_bwd(f"l{i}_conv_b", p, 0, dc, sv["conv_w"], S=S, C=3 * DNW)
            grads["even_conv", jl] = dconv
            dp = jnp.concatenate([d_qkv.astype(BF16), d_z.astype(BF16), d_xp.astype(BF16), d_ba.astype(BF16),
                                  jnp.zeros((S, BAW - LANE), BF16)], axis=1)
            dh_ = _matmul(f"l{i}_w_in_bx", dp, sv["w_in"], "nt", tk=768)
            grads["even_w_in", jl] = even_w_in_unlayout(_matmul(f"l{i}_w_in_bw", sv["h"], dp, "tn", tn=768))
        else:
            d_u2 = _matmul(f"l{i}_w_out_bx", d_mix, wfull["odd_w_out", jl], "nt")
            grads["odd_w_out", jl] = _matmul(f"l{i}_w_out_bw", sv["mixin"], d_mix, "tn")
            d_u1, dlg, dlb = _rowwise_bwd(f"l{i}_lnsilu_b", _f_lnsilu, [(sv["u1"], 0, CW)],
                                          [(sv["lg"], None, CW), (sv["lb"], None, CW)], [(d_u2, 0, CW)], [F32], S=S,
                                          tr=tr_full)
            grads["odd_ln_g", jl], grads["odd_ln_b", jl] = dlg[0], dlb[0]
            (d_u0, ddw, ddb), landed = _dwconv_bwd(f"l{i}_dwconv_b", sv["u0"], 0, d_u1, sv["dw_w"], S=S, C=CW, comm=[ride])
            grads["odd_dw", jl], grads["odd_dw_b", jl] = ddw, ddb[0]
            p = sv["p"]
            ocb = _pick(CW, 1024)
            da, dgate = _rowwise_bwd(f"l{i}_glu_b", _f_glu, [(p, 0, ocb), (p, CW // ocb, ocb)], [], [(d_u0, 0, ocb)],
                                     [BF16, BF16], S=S, ncb=CW // ocb)
            dp = jnp.concatenate([da, dgate], axis=1)
            dh_ = _matmul(f"l{i}_w_in_bx", dp, wfull["odd_w_in", jl], "nt")
            grads["odd_w_in", jl] = _matmul(f"l{i}_w_in_bw", sv["h"], dp, "tn")
        units.append(unit(landed))
        dx, dg1 = _rowwise_bwd(f"l{i}_rms_in_b", _f_rms, [(sv["x_in"], 0, D)], [(g1, None, D)], [(dh_, 0, D)], [F32],
                               adds=[(dx, 0, D)], S=S, tr=tr_full)
        grads["norm_mix_pre", i] = dg1[0]
    grad_x = dx.reshape(x.shape)
    ride, unit = reduce_unit("red_last", mixer_params(0))
    units.append(unit(_run_plug("red_last_exchange", ride)))

    halves = jnp.concatenate([u[3] for u in units], axis=0)
    rh_all = halves.shape[0]
    joined = _sibling_join("grad_sibling_join", halves)
    gshard, r = {}, 0
    for keys, offs, shapes, half in units:
        rh = half.shape[0]
        whole = jnp.concatenate([joined[r:r + rh], joined[rh_all + r:rh_all + r + rh]], axis=0)
        for k, off, shp in zip(keys, offs, shapes):
            gshard[k] = _segment(whole, off, shp)
        r += rh

    outs_g, outs_d, outs_m, outs_v = [], [], [], []
    for n in names:
        g_n = jnp.stack([gshard[n, l] for l in range(W[n].shape[0])])
        g_o, d_o, m_o, v_o = _adamw(f"adamw_{n}", W[n], Mo[n], Vo[n], g_n)
        outs_g.append(g_o)
        outs_d.append(d_o)
        outs_m.append(m_o)
        outs_v.append(v_o)
    return (loss, grad_x, *outs_g, *outs_d, *outs_m, *outs_v)
```

```python
import functools
import math

import jax
import jax.numpy as jnp
from jax import lax
from jax.experimental import pallas as pl
from jax.experimental.pallas import tpu as pltpu

F32 = jnp.float32
BF16 = jnp.bfloat16
EPS = 1e-6
DN_CHUNK = 64
POOL_WINDOWS = (2, 4, 8, 16)
ADAM_LR, ADAM_B1, ADAM_B2, ADAM_EPS, ADAM_WD, ADAM_STEP = 0.001, 0.9, 0.999, 1e-08, 0.01, 10
LANE = 128
FLAT_L = 2048
ADAM_ROWS = 128
VMEM_LIMIT = 56 * 1024 * 1024
N_CHIPS = 4
HI = lax.Precision.HIGHEST
MESH = pl.DeviceIdType.MESH


def _cparams(sem):
    return pltpu.CompilerParams(dimension_semantics=sem, vmem_limit_bytes=VMEM_LIMIT)


def _pallas(body, args, *, name, grid, in_specs, out_specs, out_shape, scratch_shapes=(), sem=None, comm=()):
    n_in, n_out, n_scr = len(in_specs), len(out_specs), len(scratch_shapes)
    if not comm:
        return pl.pallas_call(body, name=name, grid=grid, in_specs=in_specs, out_specs=out_specs, out_shape=out_shape,
                              scratch_shapes=list(scratch_shapes), compiler_params=_cparams(sem))(*args)
    ci = [len(p.ins) for p in comm]
    co = [len(p.outs) for p in comm]
    cs = [len(p.sems) for p in comm]

    def wrapped(*refs):
        ins, pos = refs[:n_in], n_in
        cins = refs[pos:pos + sum(ci)]
        pos += sum(ci)
        outs = refs[pos:pos + n_out]
        pos += n_out
        couts = refs[pos:pos + sum(co)]
        pos += sum(co)
        scr = refs[pos:pos + n_scr]
        csems = refs[pos + n_scr:]
        ids = [pl.program_id(a) for a in range(len(grid))]
        first, last = ids[0] == 0, ids[0] == grid[0] - 1
        for a in range(1, len(grid)):
            first = jnp.logical_and(first, ids[a] == 0)
            last = jnp.logical_and(last, ids[a] == grid[a] - 1)
        parts, a, b, c = [], 0, 0, 0
        for p, na, nb, nc in zip(comm, ci, co, cs):
            parts.append((p, cins[a:a + na], couts[b:b + nb], csems[c:c + nc]))
            a, b, c = a + na, b + nb, c + nc

        @pl.when(first)
        def _():
            for p, pi, po, ps in parts:
                p.start(pi, po, ps)

        body(*ins, *outs, *scr)

        @pl.when(last)
        def _():
            for p, pi, po, ps in parts:
                p.finish(pi, po, ps)

    any_spec = pl.BlockSpec(memory_space=pl.ANY)
    res = pl.pallas_call(
        wrapped, name=name, grid=grid,
        in_specs=list(in_specs) + [any_spec] * sum(ci), out_specs=list(out_specs) + [any_spec] * sum(co),
        out_shape=list(out_shape) + [s for p in comm for s in p.outs],
        scratch_shapes=list(scratch_shapes) + [s for p in comm for s in p.sems],
        compiler_params=_cparams(("arbitrary",) * len(grid)),
    )(*args, *[a for p in comm for a in p.ins])
    return res


def _pick(dim, target, mult=LANE):
    if dim <= target:
        return dim
    t = (target // mult) * mult
    while t >= mult:
        if dim % t == 0:
            return t
        t -= mult
    return dim


def _sigmoid(x):
    return 1.0 / (1.0 + jnp.exp(-x))


def _silu(x):
    return x * _sigmoid(x)


def _softplus(x):
    return jnp.maximum(x, 0.0) + jnp.log(1.0 + jnp.exp(-jnp.abs(x)))


def _rms(x, g):
    return x * lax.rsqrt(jnp.mean(x * x, axis=-1, keepdims=True) + EPS) * g


def _matmul(name, a, b, mode, out_dtype=F32, tm=1024, tn=1024, tk=2048, epi=None, extras=(), out_dtypes=None):
    if mode == "nn":
        (M, K), (K2, N) = a.shape, b.shape
    elif mode == "nt":
        (M, K), (N, K2) = a.shape, b.shape
    else:
        (K, M), (K2, N) = a.shape, b.shape
    assert K == K2, (name, a.shape, b.shape, mode)
    tm, tn, tk = _pick(M, tm), _pick(N, tn), _pick(K, tk)
    nk = K // tk
    if mode == "nn":
        a_spec = pl.BlockSpec((tm, tk), lambda i, j, k: (i, k))
        b_spec = pl.BlockSpec((tk, tn), lambda i, j, k: (k, j))
        dims = (((1,), (0,)), ((), ()))
    elif mode == "nt":
        a_spec = pl.BlockSpec((tm, tk), lambda i, j, k: (i, k))
        b_spec = pl.BlockSpec((tn, tk), lambda i, j, k: (j, k))
        dims = (((1,), (1,)), ((), ()))
    else:
        a_spec = pl.BlockSpec((tk, tm), lambda i, j, k: (k, i))
        b_spec = pl.BlockSpec((tk, tn), lambda i, j, k: (k, j))
        dims = (((0,), (0,)), ((), ()))
    out_dtypes = list(out_dtypes) if out_dtypes is not None else [out_dtype]
    ne, no = len(extras), len(out_dtypes)
    in_place = epi is None and out_dtypes == [F32]
    use_acc = nk > 1 and not in_place

    def finish(acc, extra_refs, out_refs):
        res = acc if epi is None else epi(acc, *[r[...] for r in extra_refs])
        res = res if isinstance(res, (tuple, list)) else (res,)
        for r, v in zip(out_refs, res):
            r[...] = v.astype(r.dtype)

    def body(a_ref, b_ref, *rest):
        extra_refs, out_refs = rest[:ne], rest[ne:ne + no]
        part = lax.dot_general(a_ref[...].astype(BF16), b_ref[...].astype(BF16), dims, preferred_element_type=F32)
        if nk == 1:
            finish(part, extra_refs, out_refs)
            return
        k = pl.program_id(2)
        acc_ref = rest[-1] if use_acc else out_refs[0]

        @pl.when(k == 0)
        def _():
            acc_ref[...] = part

        @pl.when(k > 0)
        def _():
            acc_ref[...] += part

        if use_acc:
            @pl.when(k == nk - 1)
            def _():
                finish(acc_ref[...], extra_refs, out_refs)

    o_spec = pl.BlockSpec((tm, tn), lambda i, j, k: (i, j))
    res = pl.pallas_call(
        body, name=name, grid=(M // tm, N // tn, nk),
        in_specs=[a_spec, b_spec] + [o_spec] * ne, out_specs=[o_spec] * no,
        out_shape=[jax.ShapeDtypeStruct((M, N), dt) for dt in out_dtypes],
        scratch_shapes=[pltpu.VMEM((tm, tn), F32)] if use_acc else [],
        compiler_params=_cparams(("parallel", "parallel", "arbitrary")),
    )(a, b, *extras)
    return res[0] if no == 1 else res


def _row_spec(tr, C, off):
    return pl.BlockSpec((tr, C), lambda j, i: (i, off + j))


def _par_spec(k, C, off):
    if off is None:
        return pl.BlockSpec((k, C), lambda j, i: (0, 0))
    return pl.BlockSpec((k, C), lambda j, i: (0, off + j))


def _rowwise(name, fn, rows, params, outs, reds=(), *, S, ncb=1, tr=256):
    tr = min(tr, S)
    nr, npar, no = len(rows), len(params), len(outs)

    def body(*refs):
        j, i = pl.program_id(0), pl.program_id(1)
        ins = [r[...].astype(F32) for r in refs[:nr + npar]]
        res = fn(i * tr, j, *ins)
        res = res if isinstance(res, (tuple, list)) else (res,)
        out_refs = refs[nr + npar:]
        for r, v in zip(out_refs[:no], res[:no]):
            r[...] = v.astype(r.dtype)
        for (k, C, per_j), r, v in zip(reds, out_refs[no:], res[no:]):
            first = (i == 0) if per_j else jnp.logical_and(i == 0, j == 0)

            @pl.when(first)
            def _(r=r):
                r[...] = jnp.zeros_like(r)

            r[...] += v

    in_specs = [_row_spec(tr, C, off) for (_, off, C) in rows] + [_par_spec(a.shape[0], C, off) for (a, off, C) in params]
    out_specs = [pl.BlockSpec((tr, C), lambda j, i: (i, j)) for (C, _) in outs]
    out_specs += [pl.BlockSpec((k, C), (lambda j, i: (0, j)) if per_j else (lambda j, i: (0, 0))) for (k, C, per_j) in reds]
    out_shape = [jax.ShapeDtypeStruct((S, ncb * C), dt) for (C, dt) in outs]
    out_shape += [jax.ShapeDtypeStruct((k, C * (ncb if per_j else 1)), F32) for (k, C, per_j) in reds]
    res = pl.pallas_call(
        body, name=name, grid=(ncb, S // tr), in_specs=in_specs, out_specs=out_specs, out_shape=out_shape,
        compiler_params=_cparams(("arbitrary", "arbitrary")),
    )(*[a for (a, _, _) in rows], *[a for (a, _, _) in params])
    return res


def _rowwise_bwd(name, fn, rows, params, cots, drow, adds=None, *, S, ncb=1, tr=128):
    tr = min(tr, S)
    nr, npar, nc = len(rows), len(params), len(cots)
    adds = adds or [None] * nr
    add_list = [a for a in adds if a is not None]
    na = len(add_list)

    def body(*refs):
        j, i = pl.program_id(0), pl.program_id(1)
        ins = [r[...].astype(F32) for r in refs[:nr + npar]]
        cts = [r[...].astype(F32) for r in refs[nr + npar:nr + npar + nc]]
        add_refs = list(refs[nr + npar + nc:nr + npar + nc + na])
        out_refs = list(refs[nr + npar + nc + na:])

        def f(*a):
            res = fn(i * tr, j, *a)
            return tuple(res) if isinstance(res, (tuple, list)) else (res,)

        _, vjp = jax.vjp(f, *ins)
        grads = vjp(tuple(cts))
        for idx in range(nr):
            if drow[idx] is None:
                continue
            g = grads[idx]
            if adds[idx] is not None:
                g = g + add_refs.pop(0)[...].astype(F32)
            r = out_refs.pop(0)
            r[...] = g.astype(r.dtype)
        for idx in range(npar):
            per_j = params[idx][1] is not None
            first = (i == 0) if per_j else jnp.logical_and(i == 0, j == 0)
            r = out_refs.pop(0)

            @pl.when(first)
            def _(r=r):
                r[...] = jnp.zeros_like(r)

            r[...] += grads[nr + idx]

    in_specs = [_row_spec(tr, C, off) for (_, off, C) in rows]
    in_specs += [_par_spec(a.shape[0], C, off) for (a, off, C) in params]
    in_specs += [_row_spec(tr, C, off) for (_, off, C) in cots]
    in_specs += [_row_spec(tr, C, off) for (_, off, C) in add_list]
    out_specs, out_shape = [], []
    for idx in range(nr):
        if drow[idx] is not None:
            C = rows[idx][2]
            out_specs.append(pl.BlockSpec((tr, C), lambda j, i: (i, j)))
            out_shape.append(jax.ShapeDtypeStruct((S, ncb * C), drow[idx]))
    for (a, off, C) in params:
        per_j = off is not None
        out_specs.append(pl.BlockSpec((a.shape[0], C), (lambda j, i: (0, j)) if per_j else (lambda j, i: (0, 0))))
        out_shape.append(jax.ShapeDtypeStruct((a.shape[0], C * (ncb if per_j else 1)), F32))
    return pl.pallas_call(
        body, name=name, grid=(ncb, S // tr), in_specs=in_specs, out_specs=out_specs, out_shape=out_shape,
        compiler_params=_cparams(("arbitrary", "arbitrary")),
    )(*[a for (a, _, _) in rows], *[a for (a, _, _) in params], *[a for (a, _, _) in cots], *[a for (a, _, _) in add_list])


def _halo_rows(K):
    return 8 * ((K - 1 + 7) // 8)


def _inv_count(row0, tr, win):
    t = (row0 + lax.broadcasted_iota(jnp.int32, (tr, 1), 0)).astype(F32)
    return 1.0 / jnp.minimum(t + 1.0, win)


def _dwconv_fwd(name, x, x_off, w, *, S, C, bias=None, win=None, out_dtype=F32, cb=512, tr=256, comm=()):
    K = w.shape[0]
    cb, tr = _pick(C, cb), min(tr, S)
    HB = min(_halo_rows(K), tr)
    assert K - 1 <= HB and tr % HB == 0 and C % cb == 0
    nb = tr // HB
    extra = [a for a in (bias, win) if a is not None]

    def body(xh_ref, x_ref, w_ref, *rest):
        y_ref, xx = rest[-2], rest[-1]
        i = pl.program_id(1)
        xx[0:HB, :] = jnp.where(i > 0, xh_ref[...].astype(F32), 0.0)
        xx[HB:HB + tr, :] = x_ref[...].astype(F32)
        acc = jnp.zeros((tr, cb), F32)
        for jj in range(K):
            o = HB - (K - 1) + jj
            acc = acc + w_ref[jj:jj + 1, :] * xx[o:o + tr, :]
        if bias is not None:
            acc = acc + rest[0][...]
        if win is not None:
            acc = acc * _inv_count(i * tr, tr, rest[0][...]) - x_ref[...].astype(F32)
        y_ref[...] = acc.astype(y_ref.dtype)

    in_specs = [pl.BlockSpec((HB, cb), lambda j, i: (jnp.maximum(i * nb - 1, 0), x_off + j)),
                pl.BlockSpec((tr, cb), lambda j, i: (i, x_off + j)),
                pl.BlockSpec((K, cb), lambda j, i: (0, j))]
    in_specs += [pl.BlockSpec((1, cb), lambda j, i: (0, j)) for _ in extra]
    res = _pallas(body, (x, x, w, *extra), name=name, grid=(C // cb, S // tr), in_specs=in_specs,
                  out_specs=[pl.BlockSpec((tr, cb), lambda j, i: (i, j))],
                  out_shape=[jax.ShapeDtypeStruct((S, C), out_dtype)],
                  scratch_shapes=[pltpu.VMEM((HB + tr, cb), F32)], sem=("parallel", "arbitrary"), comm=comm)
    return (res[0], res[1:]) if comm else res[0]


def _dwconv_bwd(name, x, x_off, dy, w, *, S, C, win=None, want_dw=True, cb=512, tr=256, comm=()):
    K = w.shape[0]
    cb, tr = _pick(C, cb), min(tr, S)
    HB = min(_halo_rows(K), tr)
    nb, nt = tr // HB, S // tr

    def body(*refs):
        if want_dw:
            xh_ref, x_ref, dy_ref, dyn_ref, w_ref = refs[:5]
            rest = refs[5:]
        else:
            dy_ref, dyn_ref, w_ref = refs[:3]
            rest = refs[3:]
        i = pl.program_id(1)
        dyt = dy_ref[...].astype(F32)
        dyn = jnp.where(i < nt - 1, dyn_ref[...].astype(F32), 0.0)
        if win is not None:
            win_v = rest[0][...]
            rest = rest[1:]
            yy_t = dyt * _inv_count(i * tr, tr, win_v)
            dyn = dyn * _inv_count((i + 1) * tr, HB, win_v)
        else:
            yy_t = dyt
        if want_dw:
            dx_ref, dw_ref, db_ref, yy, xx = rest
        else:
            dx_ref, yy = rest
        yy[0:tr, :] = yy_t
        yy[tr:tr + HB, :] = dyn
        acc = jnp.zeros((tr, cb), F32)
        for jj in range(K):
            o = K - 1 - jj
            acc = acc + w_ref[jj:jj + 1, :] * yy[o:o + tr, :]
        if win is not None:
            acc = acc - dyt
        dx_ref[...] = acc.astype(dx_ref.dtype)
        if want_dw:
            xx[0:HB, :] = jnp.where(i > 0, xh_ref[...].astype(F32), 0.0)
            xx[HB:HB + tr, :] = x_ref[...].astype(F32)

            @pl.when(i == 0)
            def _():
                dw_ref[...] = jnp.zeros_like(dw_ref)
                db_ref[...] = jnp.zeros_like(db_ref)

            for jj in range(K):
                o = HB - (K - 1) + jj
                dw_ref[jj:jj + 1, :] += jnp.sum(dyt * xx[o:o + tr, :], axis=0, keepdims=True)
            db_ref[...] += jnp.sum(dyt, axis=0, keepdims=True)

    last = S // HB - 1
    in_specs, args = [], []
    if want_dw:
        in_specs += [pl.BlockSpec((HB, cb), lambda j, i: (jnp.maximum(i * nb - 1, 0), x_off + j)),
                     pl.BlockSpec((tr, cb), lambda j, i: (i, x_off + j))]
        args += [x, x]
    in_specs += [pl.BlockSpec((tr, cb), lambda j, i: (i, j)),
                 pl.BlockSpec((HB, cb), lambda j, i: (jnp.minimum((i + 1) * nb, last), j)),
                 pl.BlockSpec((K, cb), lambda j, i: (0, j))]
    args += [dy, dy, w]
    if win is not None:
        in_specs.append(pl.BlockSpec((1, cb), lambda j, i: (0, j)))
        args.append(win)
    out_specs = [pl.BlockSpec((tr, cb), lambda j, i: (i, j))]
    out_shape = [jax.ShapeDtypeStruct((S, C), F32)]
    scratch = [pltpu.VMEM((tr + HB, cb), F32)]
    if want_dw:
        out_specs += [pl.BlockSpec((K, cb), lambda j, i: (0, j)), pl.BlockSpec((1, cb), lambda j, i: (0, j))]
        out_shape += [jax.ShapeDtypeStruct((K, C), F32), jax.ShapeDtypeStruct((1, C), F32)]
        scratch.append(pltpu.VMEM((HB + tr, cb), F32))
    res = _pallas(body, args, name=name, grid=(C // cb, S // tr), in_specs=in_specs, out_specs=out_specs,
                  out_shape=out_shape, scratch_shapes=scratch, sem=("parallel", "arbitrary"), comm=comm)
    return (res[:len(out_specs)], res[len(out_specs):]) if comm else res


def _dot(a, b, dims, hi=False):
    if hi:
        return lax.dot_general(a, b, (dims, ((), ())), precision=HI, preferred_element_type=F32)
    return lax.dot_general(a.astype(BF16), b.astype(BF16), (dims, ((), ())), preferred_element_type=F32)


_NN, _NT, _TN = ((1,), (0,)), ((1,), (1,)), ((0,), (0,))


def _col(m, idx):
    lane = lax.broadcasted_iota(jnp.int32, m.shape, 1)
    return jnp.sum(jnp.where(lane == idx, m, 0.0), axis=1, keepdims=True)


def _row(m, idx):
    sub = lax.broadcasted_iota(jnp.int32, m.shape, 0)
    return jnp.sum(jnp.where(sub == idx, m, 0.0), axis=0, keepdims=True)


def _delta_chunk(q, k, v, beta, gcc, gcr, causal, strict, eye, scale, C):
    d = {}
    gam = jnp.where(causal, jnp.exp(jnp.where(causal, gcc - gcr, 0.0)), 0.0)
    eg = jnp.exp(gcc)
    g_last = _row(gcc, C - 1)
    d["gam"], d["eg"], d["g_last"] = gam, eg, g_last
    d["ek"] = jnp.exp(g_last - gcc)
    d["decay"] = jnp.exp(g_last)
    qs = q * scale
    kb = k * beta
    d["qs"], d["kb"] = qs, kb
    d["kk"] = _dot(kb, k, _NT)
    d["A"] = jnp.where(strict, d["kk"] * gam, 0.0)
    d["qk"] = _dot(qs, k, _NT)
    d["attn"] = jnp.where(causal, d["qk"] * gam, 0.0)
    d["vb"] = v * beta
    d["kbg"] = kb * eg
    d["qg"] = qs * eg
    d["kd"] = k * d["ek"]
    return d


def _split(m):
    hi = m.astype(BF16)
    return hi, (m - hi.astype(F32)).astype(BF16)


def _dot3(a, b, dims):
    return _dot(a[0], b[0], dims) + (_dot(a[0], b[1], dims) + _dot(a[1], b[0], dims))


def _tri_inverse(As, eye):
    P = [-A for A in As]
    T = [eye + p for p in P]
    n = 1
    while 2 * n < As[0].shape[0]:
        Ps = [_split(p) for p in P]
        P = [_dot3(ps, ps, _NN) for ps in Ps]
        Ts = [_split(t) for t in T]
        Ps = [_split(p) for p in P]
        T = [t + _dot3(ts, ps, _NN) for t, ts, ps in zip(T, Ts, Ps)]
        n *= 2
    return T


def _delta_fwd(name, qk, v, gb, gT, *, S, H, dh, comm=()):
    C = min(DN_CHUNK, S)
    N, W = S // C, H * dh
    scale = dh ** -0.5

    def body(qk_ref, v_ref, gb_ref, gT_ref, o_ref, sp_ref, T_ref, st):
        n = pl.program_id(0)

        @pl.when(n == 0)
        def _():
            st[...] = jnp.zeros_like(st)

        r = lax.broadcasted_iota(jnp.int32, (C, C), 0)
        c = lax.broadcasted_iota(jnp.int32, (C, C), 1)
        causal, strict = r >= c, r > c
        eye = (r == c).astype(F32)
        Lt = causal.astype(F32)
        gbv = gb_ref[...]
        gcum = _dot(Lt, gbv, _NN, hi=True)
        gcumT = _dot(gT_ref[0], Lt, _NT, hi=True)
        hs = range(H)
        sl = [slice(h * dh, (h + 1) * dh) for h in hs]
        d = [_delta_chunk(qk_ref[:, sl[h]], qk_ref[:, W + h * dh:W + (h + 1) * dh], v_ref[:, sl[h]], _col(gbv, h),
                          _col(gcum, H + h), _row(gcumT, h), causal, strict, eye, scale, C) for h in hs]
        T = _tri_inverse([d[h]["A"] for h in hs], eye)
        u = [_dot(T[h], d[h]["vb"], _NN) for h in hs]
        w = [_dot(T[h], d[h]["kbg"], _NN) for h in hs]
        s0 = [st[h] for h in hs]
        ws = [_dot(w[h], s0[h], _NN) for h in hs]
        qs0 = [_dot(d[h]["qg"], s0[h], _NN) for h in hs]
        v_new = [u[h] - ws[h] for h in hs]
        av = [_dot(d[h]["attn"], v_new[h], _NN) for h in hs]
        kv = [_dot(d[h]["kd"], v_new[h], _TN) for h in hs]
        for h in hs:
            sp_ref[0, h] = s0[h]
            T_ref[0, h] = T[h]
            o_ref[:, sl[h]] = qs0[h] + av[h]
            st[h] = s0[h] * d[h]["decay"] + kv[h]

    res = _pallas(
        body, (qk, v, gb, gT), name=name, grid=(N,),
        in_specs=[pl.BlockSpec((C, 2 * W), lambda n: (n, 0)), pl.BlockSpec((C, W), lambda n: (n, 0)),
                  pl.BlockSpec((C, LANE), lambda n: (n, 0)), pl.BlockSpec((1, H, C), lambda n: (n, 0, 0))],
        out_specs=[pl.BlockSpec((C, W), lambda n: (n, 0)), pl.BlockSpec((1, H, dh, dh), lambda n: (n, 0, 0, 0)),
                   pl.BlockSpec((1, H, C, C), lambda n: (n, 0, 0, 0))],
        out_shape=[jax.ShapeDtypeStruct((S, W), F32), jax.ShapeDtypeStruct((N, H, dh, dh), F32),
                   jax.ShapeDtypeStruct((N, H, C, C), F32)],
        scratch_shapes=[pltpu.VMEM((H, dh, dh), F32)], sem=("arbitrary",), comm=comm)
    return res[:3], res[3:]


def _delta_bwd(name, qk, v, gb, gT, sp, Tm, do, *, S, H, dh, comm=()):
    C = min(DN_CHUNK, S)
    N, W = S // C, H * dh
    scale = dh ** -0.5

    def body(qk_ref, v_ref, gb_ref, gT_ref, sp_ref, T_ref, do_ref, dqk_ref, dv_ref, dgb_ref, ds):
        n = pl.program_id(0)

        @pl.when(n == 0)
        def _():
            ds[...] = jnp.zeros_like(ds)

        r = lax.broadcasted_iota(jnp.int32, (C, C), 0)
        c = lax.broadcasted_iota(jnp.int32, (C, C), 1)
        causal, strict = r >= c, r > c
        eye = (r == c).astype(F32)
        Lt = causal.astype(F32)
        ones = jnp.ones((C, LANE), F32)
        lane = lax.broadcasted_iota(jnp.int32, (C, LANE), 1)
        rowi = lax.broadcasted_iota(jnp.int32, (C, 1), 0)
        gbv = gb_ref[...]
        gcum = _dot(Lt, gbv, _NN, hi=True)
        gcumT = _dot(gT_ref[0], Lt, _NT, hi=True)
        dgc_all = jnp.zeros((C, LANE), F32)
        dbeta_all = jnp.zeros((C, LANE), F32)
        hs = range(H)
        sl = [slice(h * dh, (h + 1) * dh) for h in hs]
        ksl = [slice(W + h * dh, W + (h + 1) * dh) for h in hs]
        k = [qk_ref[:, ksl[h]] for h in hs]
        vv = [v_ref[:, sl[h]] for h in hs]
        beta = [_col(gbv, h) for h in hs]
        d = [_delta_chunk(qk_ref[:, sl[h]], k[h], vv[h], beta[h], _col(gcum, H + h), _row(gcumT, h), causal, strict,
                          eye, scale, C) for h in hs]
        T = [T_ref[0, h] for h in hs]
        s0 = [sp_ref[0, h] for h in hs]
        dO = [do_ref[:, sl[h]] for h in hs]
        dS = [ds[h] for h in hs]
        u = [_dot(T[h], d[h]["vb"], _NN) for h in hs]
        w = [_dot(T[h], d[h]["kbg"], _NN) for h in hs]
        ws = [_dot(w[h], s0[h], _NN) for h in hs]
        v_new = [u[h] - ws[h] for h in hs]
        dv_new = [_dot(d[h]["attn"], dO[h], _TN) + _dot(d[h]["kd"], dS[h], _NN) for h in hs]
        dattn = [jnp.where(causal, _dot(dO[h], v_new[h], _NT), 0.0) for h in hs]
        dqg = [_dot(dO[h], s0[h], _NT) for h in hs]
        dkd = [_dot(v_new[h], dS[h], _NT) for h in hs]
        ddecay = [jnp.sum(jnp.sum(s0[h] * dS[h], axis=1, keepdims=True), axis=0, keepdims=True) for h in hs]
        ds_new = [_dot(d[h]["qg"], dO[h], _TN) + d[h]["decay"] * dS[h] - _dot(w[h], dv_new[h], _TN) for h in hs]
        dw = [-_dot(dv_new[h], s0[h], _NT) for h in hs]
        for h in hs:
            ds[h] = ds_new[h]
        dT = [_dot(dv_new[h], d[h]["vb"], _NT) + _dot(dw[h], d[h]["kbg"], _NT) for h in hs]
        dvb = [_dot(T[h], dv_new[h], _TN) for h in hs]
        dkbg = [_dot(T[h], dw[h], _TN) for h in hs]
        Ts = [_split(T[h]) for h in hs]
        x1 = [_dot3(Ts[h], _split(dT[h]), _TN) for h in hs]
        dA = [jnp.where(strict, -_dot3(_split(x1[h]), Ts[h], _NT), 0.0) for h in hs]
        dkk = [dA[h] * d[h]["gam"] for h in hs]
        dqk_m = [dattn[h] * d[h]["gam"] for h in hs]
        m = [_split(dA[h] * d[h]["A"] + dattn[h] * d[h]["attn"]) for h in hs]
        msum = [jnp.sum(dA[h] * d[h]["A"] + dattn[h] * d[h]["attn"], axis=1, keepdims=True) for h in hs]
        mcol = [jnp.max(_dot(m[h][0], ones, _TN) + _dot(m[h][1], ones, _TN), axis=1, keepdims=True) for h in hs]
        dkb = [_dot(dkk[h], k[h], _NN) + dkbg[h] * d[h]["eg"] for h in hs]
        dk = [_dot(dkk[h], d[h]["kb"], _TN) + _dot(dqk_m[h], d[h]["qs"], _TN) + dkd[h] * d[h]["ek"] + dkb[h] * beta[h]
              for h in hs]
        dqs = [_dot(dqk_m[h], k[h], _NN) + dqg[h] * d[h]["eg"] for h in hs]
        for h in hs:
            r_kd = jnp.sum(dkd[h] * d[h]["kd"], axis=1, keepdims=True)
            dgc = (msum[h] - mcol[h] + jnp.sum(dqg[h] * d[h]["qg"], axis=1, keepdims=True) - r_kd
                   + jnp.sum(dkbg[h] * d[h]["kbg"], axis=1, keepdims=True))
            dg_last = jnp.sum(r_kd, axis=0, keepdims=True) + ddecay[h] * d[h]["decay"]
            dgc = dgc + jnp.where(rowi == C - 1, dg_last, 0.0)
            dbeta = jnp.sum(dkb[h] * k[h], axis=1, keepdims=True) + jnp.sum(dvb[h] * vv[h], axis=1, keepdims=True)
            dqk_ref[:, sl[h]] = dqs[h] * scale
            dqk_ref[:, ksl[h]] = dk[h]
            dv_ref[:, sl[h]] = dvb[h] * beta[h]
            dgc_all = dgc_all + jnp.where(lane == H + h, dgc, 0.0)
            dbeta_all = dbeta_all + jnp.where(lane == h, dbeta, 0.0)
        dgb_ref[...] = _dot(Lt, dgc_all, _TN, hi=True) + dbeta_all

    rev = lambda n: N - 1 - n
    res = _pallas(
        body, (qk, v, gb, gT, sp, Tm, do), name=name, grid=(N,),
        in_specs=[pl.BlockSpec((C, 2 * W), lambda n: (rev(n), 0)), pl.BlockSpec((C, W), lambda n: (rev(n), 0)),
                  pl.BlockSpec((C, LANE), lambda n: (rev(n), 0)), pl.BlockSpec((1, H, C), lambda n: (rev(n), 0, 0)),
                  pl.BlockSpec((1, H, dh, dh), lambda n: (rev(n), 0, 0, 0)),
                  pl.BlockSpec((1, H, C, C), lambda n: (rev(n), 0, 0, 0)),
                  pl.BlockSpec((C, W), lambda n: (rev(n), 0))],
        out_specs=[pl.BlockSpec((C, 2 * W), lambda n: (rev(n), 0)), pl.BlockSpec((C, W), lambda n: (rev(n), 0)),
                   pl.BlockSpec((C, LANE), lambda n: (rev(n), 0))],
        out_shape=[jax.ShapeDtypeStruct((S, 2 * W), F32), jax.ShapeDtypeStruct((S, W), F32),
                   jax.ShapeDtypeStruct((S, LANE), F32)],
        scratch_shapes=[pltpu.VMEM((H, dh, dh), F32)], sem=("arbitrary",), comm=comm)
    return res[:3], res[3:]


_ANY = pl.BlockSpec(memory_space=pl.ANY)


ICI_CHUNKS = 4
D2D_CHUNKS = 4


def _place():
    return lax.axis_index("x"), lax.axis_index("y"), lax.axis_index("c")


def _row_chunks(rows, n):
    n = max(1, min(n, rows // 8))
    while n > 1 and (rows % n or (rows // n) % 8):
        n -= 1
    return [(k * (rows // n), rows // n) for k in range(n)]


class _Plug:
    def __init__(self, ins, outs, sems, start, finish, after):
        self.ins, self.outs, self.sems, self.start, self.finish, self.after = ins, outs, sems, start, finish, after


def _run_plug(name, plug):
    def body(*refs):
        ni, no = len(plug.ins), len(plug.outs)
        plug.start(refs[:ni], refs[ni:ni + no], refs[ni + no:])
        plug.finish(refs[:ni], refs[ni:ni + no], refs[ni + no:])

    return pl.pallas_call(body, name=name, in_specs=[_ANY] * len(plug.ins), out_specs=[_ANY] * len(plug.outs),
                          out_shape=list(plug.outs), scratch_shapes=list(plug.sems))(*plug.ins)


def _gather_plug(flat):
    R, L = flat.shape
    Rh = R // 2
    ici = _row_chunks(Rh, ICI_CHUNKS)
    sub = _row_chunks(ici[0][1], D2D_CHUNKS)
    ni, ns = len(ici), len(sub)

    def parts(ins, outs, sems):
        (x_ref,), (out_ref,), (send_sems, recv_sems) = ins, outs, sems
        x, y, c = _place()
        chips = [(1 - x, y), (x, 1 - y), (1 - x, 1 - y)]

        def rows(px, py, pc, r0, n):
            return out_ref.at[2 * px + py, pl.ds(pc * Rh + r0, n), :]

        def copy(k, src, dst, to):
            return pltpu.make_async_remote_copy(src_ref=src, dst_ref=dst, send_sem=send_sems.at[k],
                                                recv_sem=recv_sems.at[k], device_id=to, device_id_type=MESH)

        first = [copy(k * ni + q, x_ref.at[pl.ds(c * Rh + r0, n), :], rows(x, y, c, r0, n), (*chip, c))
                 for k, chip in enumerate(chips) for q, (r0, n) in enumerate(ici)]
        return x_ref, (x, y, c), chips, rows, copy, first

    def start(ins, outs, sems):
        for cp in parts(ins, outs, sems)[-1]:
            cp.start()

    def finish(ins, outs, sems):
        x_ref, (x, y, c), chips, rows, copy, first = parts(ins, outs, sems)
        sibling = (x, y, 1 - c)
        passed = []
        for k, chip in enumerate(chips):
            for q, (r0, n) in enumerate(ici):
                copy(k * ni + q, x_ref.at[pl.ds(r0, n), :], rows(*chip, c, r0, n), (*chip, c)).wait_recv()
                for t, (s0, m) in enumerate(sub):
                    cp = copy(3 * ni + (k * ni + q) * ns + t, rows(*chip, c, r0 + s0, m), rows(*chip, c, r0 + s0, m), sibling)
                    cp.start()
                    passed.append(cp)
        for k, chip in enumerate(chips):
            for q, (r0, n) in enumerate(ici):
                for t, (s0, m) in enumerate(sub):
                    copy(3 * ni + (k * ni + q) * ns + t, x_ref.at[pl.ds(r0, m), :], rows(*chip, 1 - c, r0 + s0, m),
                         sibling).wait_recv()
        for cp in first + passed:
            cp.wait_send()

    def after(res):
        return lax.dynamic_update_slice(res[0], flat[None], (2 * lax.axis_index("x") + lax.axis_index("y"), 0, 0))

    nsem = 3 * ni * (1 + ns)
    return _Plug([flat], [jax.ShapeDtypeStruct((N_CHIPS, R, L), flat.dtype)],
                 [pltpu.SemaphoreType.DMA((nsem,)), pltpu.SemaphoreType.DMA((nsem,))], start, finish, after)


def _sibling_split(name, g):
    _, R, L = g.shape
    Rh = R // 2

    chunks = [(j, r0, n) for j in range(N_CHIPS) for (r0, n) in _row_chunks(Rh, D2D_CHUNKS)]

    def body(g_ref, got_ref, send_sems, recv_sems):
        x, y, c = _place()
        cps = [pltpu.make_async_remote_copy(src_ref=g_ref.at[j, pl.ds((1 - c) * Rh + r0, n), :],
                                            dst_ref=got_ref.at[j, pl.ds(r0, n), :], send_sem=send_sems.at[k],
                                            recv_sem=recv_sems.at[k], device_id=(x, y, 1 - c), device_id_type=MESH)
               for k, (j, r0, n) in enumerate(chunks)]
        for cp in cps:
            cp.start()
        for cp in cps:
            cp.wait()

    sems = pltpu.SemaphoreType.DMA((len(chunks),))
    got = pl.pallas_call(
        body, name=name, in_specs=[_ANY], out_specs=_ANY, out_shape=jax.ShapeDtypeStruct((N_CHIPS, Rh, L), g.dtype),
        scratch_shapes=[sems, sems],
    )(g)
    own = lax.dynamic_slice(g, (0, lax.axis_index("c") * Rh, 0), (N_CHIPS, Rh, L))
    return own, got


def _exchange_plug(p):
    Rh = p.shape[1]
    ici = _row_chunks(Rh, ICI_CHUNKS)
    ni = len(ici)

    def sends(ins, outs, sems):
        (p_ref,), (q_ref,), (send_sems, recv_sems) = ins, outs, sems
        x, y, c = _place()
        me = 2 * x + y
        chips = [(1 - x, y), (x, 1 - y), (1 - x, 1 - y)]
        return [pltpu.make_async_remote_copy(src_ref=p_ref.at[2 * cx + cy, pl.ds(r0, n), :],
                                             dst_ref=q_ref.at[me, pl.ds(r0, n), :],
                                             send_sem=send_sems.at[k * ni + q], recv_sem=recv_sems.at[k * ni + q],
                                             device_id=(cx, cy, c), device_id_type=MESH)
                for k, (cx, cy) in enumerate(chips) for q, (r0, n) in enumerate(ici)]

    def start(ins, outs, sems):
        for cp in sends(ins, outs, sems):
            cp.start()

    def finish(ins, outs, sems):
        (p_ref,), (q_ref,), (send_sems, recv_sems) = ins, outs, sems
        x, y, c = _place()
        me = 2 * x + y
        chips = [(1 - x, y), (x, 1 - y), (1 - x, 1 - y)]
        for k, (cx, cy) in enumerate(chips):
            for q, (r0, n) in enumerate(ici):
                pltpu.make_async_remote_copy(src_ref=p_ref.at[me, pl.ds(r0, n), :],
                                             dst_ref=q_ref.at[2 * cx + cy, pl.ds(r0, n), :],
                                             send_sem=send_sems.at[k * ni + q], recv_sem=recv_sems.at[k * ni + q],
                                             device_id=(cx, cy, c), device_id_type=MESH).wait_recv()
        for cp in sends(ins, outs, sems):
            cp.wait_send()

    def after(res):
        me = 2 * lax.axis_index("x") + lax.axis_index("y")
        return lax.dynamic_update_slice(res[0], lax.dynamic_slice(p, (me, 0, 0), (1,) + p.shape[1:]), (me, 0, 0))

    return _Plug([p], [jax.ShapeDtypeStruct(p.shape, p.dtype)],
                 [pltpu.SemaphoreType.DMA((3 * ni,)), pltpu.SemaphoreType.DMA((3 * ni,))], start, finish, after)


def _sibling_swap(name, half):
    Rh, L = half.shape
    chunks = _row_chunks(Rh, 2 * D2D_CHUNKS)

    def body(h_ref, out_ref, send_sems, recv_sems):
        x, y, c = _place()
        cps = [pltpu.make_async_remote_copy(src_ref=h_ref.at[pl.ds(r0, n), :], dst_ref=out_ref.at[pl.ds(r0, n), :],
                                            send_sem=send_sems.at[k], recv_sem=recv_sems.at[k], device_id=(x, y, 1 - c),
                                            device_id_type=MESH)
               for k, (r0, n) in enumerate(chunks)]
        for cp in cps:
            cp.start()
        for cp in cps:
            cp.wait()

    sems = pltpu.SemaphoreType.DMA((len(chunks),))
    return pl.pallas_call(
        body, name=name, in_specs=[_ANY], out_specs=_ANY, out_shape=jax.ShapeDtypeStruct((Rh, L), half.dtype),
        scratch_shapes=[sems, sems],
    )(half)


def _add_pairs(name, a, b, out_dtype):
    n, Rh, L = a.shape
    tr = _pick(Rh, 512, 8)

    def body(a_ref, b_ref, o_ref):
        o_ref[...] = (a_ref[...].astype(F32) + b_ref[...].astype(F32)).astype(o_ref.dtype)

    spec = pl.BlockSpec((1, tr, L), lambda j, i: (j, i, 0))
    return pl.pallas_call(body, name=name, grid=(n, Rh // tr), in_specs=[spec, spec], out_specs=spec,
                          out_shape=jax.ShapeDtypeStruct(a.shape, out_dtype),
                          compiler_params=_cparams(("parallel", "parallel")))(a, b)


def _sum_chips(name, q):
    n, Rh, L = q.shape
    tr = _pick(Rh, 512, 8)

    def body(q_ref, o_ref):
        acc = q_ref[0].astype(F32)
        for s in range(1, n):
            acc = acc + q_ref[s].astype(F32)
        o_ref[...] = acc

    return pl.pallas_call(body, name=name, grid=(Rh // tr,),
                          in_specs=[pl.BlockSpec((n, tr, L), lambda i: (0, i, 0))],
                          out_specs=pl.BlockSpec((tr, L), lambda i: (i, 0)),
                          out_shape=jax.ShapeDtypeStruct((Rh, L), F32),
                          compiler_params=_cparams(("parallel",)))(q)


def _adamw(name, w, m, v, g):
    shape, size = w.shape, w.size
    rows = -(-size // FLAT_L)
    rows_p = -(-rows // ADAM_ROWS) * ADAM_ROWS
    pad = rows_p * FLAT_L - size

    def flat2d(a):
        a = a.reshape(-1)
        if pad:
            a = jnp.pad(a, (0, pad), constant_values=1.0)
        return a.reshape(rows_p, FLAT_L)

    c1 = 1.0 / (1.0 - ADAM_B1 ** ADAM_STEP)
    c2 = 1.0 / (1.0 - ADAM_B2 ** ADAM_STEP)

    def body(w_ref, m_ref, v_ref, g_ref, go_ref, d_ref, mo_ref, vo_ref):
        g = g_ref[...]
        wv = w_ref[...]
        mn = ADAM_B1 * m_ref[...] + (1.0 - ADAM_B1) * g
        vn = ADAM_B2 * v_ref[...] + (1.0 - ADAM_B2) * (g * g)
        go_ref[...] = g
        mo_ref[...] = mn
        vo_ref[...] = vn
        d_ref[...] = -ADAM_LR * ((mn * c1) / (jnp.sqrt(vn * c2) + ADAM_EPS) + ADAM_WD * wv)

    spec = pl.BlockSpec((ADAM_ROWS, FLAT_L), lambda i: (i, 0))
    shp = jax.ShapeDtypeStruct((rows_p, FLAT_L), F32)
    outs = pl.pallas_call(
        body, name=name, grid=(rows_p // ADAM_ROWS,),
        in_specs=[spec] * 4, out_specs=[spec] * 4, out_shape=[shp] * 4, compiler_params=_cparams(("parallel",)),
    )(flat2d(w), flat2d(m), flat2d(v), flat2d(g))

    def back(a):
        a = a.reshape(-1)
        if pad:
            a = a[:size]
        return a.reshape(shape)

    return tuple(back(a) for a in outs)


SEG_ROWS = 16


def _seg_rows(size):
    rows = -(-size // FLAT_L)
    return -(-rows // SEG_ROWS) * SEG_ROWS


def _pack(pieces, dtype, row_mult):
    segs, offs, r = [], [], 0
    for a in pieces:
        rows = _seg_rows(a.size)
        flat = a.reshape(-1).astype(dtype)
        flat = jnp.pad(flat, (0, rows * FLAT_L - a.size))
        segs.append(flat.reshape(rows, FLAT_L))
        offs.append(r)
        r += rows
    tail = -r % row_mult
    if tail:
        segs.append(jnp.zeros((tail, FLAT_L), dtype))
    return jnp.concatenate(segs, axis=0), offs


def _segment(flat, off, shape):
    size = math.prod(shape)
    return flat[off:off + _seg_rows(size)].reshape(-1)[:size].reshape(shape)


def _gather_unit(shards, axes, dtype):
    flat, offs = _pack(shards, dtype, 64)
    plug = _gather_plug(flat)

    def unpack(full):
        outs = []
        for a, ax, off in zip(shards, axes, offs):
            rows = _seg_rows(a.size)
            seg = full[:, off:off + rows].reshape(N_CHIPS, -1)[:, :a.size].reshape((N_CHIPS,) + a.shape)
            outs.append(jnp.concatenate([seg[j] for j in range(N_CHIPS)], axis=ax))
        return outs

    return plug, unpack


def _reduce_unit(name, grads, axes):
    pieces = [[] for _ in range(N_CHIPS)]
    for g, ax in zip(grads, axes):
        parts = jnp.split(g, N_CHIPS, axis=ax) if ax is not None else [g] * N_CHIPS
        for jc in range(N_CHIPS):
            pieces[jc].append(parts[jc])
    packed = [_pack(pieces[jc], BF16, 512) for jc in range(N_CHIPS)]
    gsend = jnp.stack([pk[0] for pk in packed])
    own, got = _sibling_split(name + "_split", gsend)
    pair = _add_pairs(name + "_add", own, got, BF16)
    return _exchange_plug(pair), packed[0][1], [a.shape for a in pieces[0]]


def _f_rms(row0, j, x, g):
    return _rms(x, g)


def _f_mid(row0, j, x, y, g_a, g_b):
    xn = x + _rms(y, g_a)
    return xn, _rms(xn, g_b)


def _f_resid(row0, j, x, y, g):
    return x + _rms(y, g)


def _relu2(u):
    r = jnp.maximum(u, 0.0)
    return r * r


def _relu2_bwd(d_act, act):
    return d_act * (2.0 * jnp.sqrt(act.astype(F32)))


def _f_l2silu(row0, j, c):
    a = _silu(c)
    return a * lax.rsqrt(jnp.sum(a * a, axis=-1, keepdims=True) + EPS)


def _f_silu(row0, j, c):
    return _silu(c)


def _f_scale(row0, j, y, s):
    return y * s


def _f_glu(row0, j, a, gate):
    return a * _sigmoid(gate)


def _f_lnsilu(row0, j, u, g, b):
    mu = jnp.mean(u, axis=-1, keepdims=True)
    uc = u - mu
    return _silu(uc * lax.rsqrt(jnp.mean(uc * uc, axis=-1, keepdims=True) + EPS) * g + b)


def _f_outgate(row0, j, o, z, g):
    return _rms(o, g) * _silu(z)


def _make_gates(H):
    def f(row0, j, ba, alog, dt):
        lane = lax.broadcasted_iota(jnp.int32, ba.shape, 1)
        beta = _sigmoid(ba)
        g = -jnp.exp(alog) * _softplus(ba + dt)
        return jnp.where(lane < H, beta, jnp.where(lane < 2 * H, g, 0.0))
    return f


def _f_loss(row0, j, y, t):
    e = y - t
    loss = 0.5 * jnp.sum(jnp.mean(e * e, axis=-1, keepdims=True), axis=0, keepdims=True)
    return e * (1.0 / y.shape[-1]), jnp.broadcast_to(loss, (1, LANE))


def _lane_row(vec, start):
    return jnp.pad(vec.astype(F32)[None, :], ((0, 0), (start, LANE - start - vec.shape[0])))


def kernel(x, norm_mix_pre, norm_mix_post, norm_mlp_pre, norm_mlp_post, even_w_in, even_conv, even_a_log, even_dt_bias, even_dn_norm, even_pool_w, even_pool_scale, even_w_out, odd_w_in, odd_dw, odd_dw_b, odd_ln_g, odd_ln_b, odd_w_out, mlp_w_up, mlp_w_down, loss_target, m_norm_mix_pre, m_norm_mix_post, m_norm_mlp_pre, m_norm_mlp_post, m_even_w_in, m_even_conv, m_even_a_log, m_even_dt_bias, m_even_dn_norm, m_even_pool_w, m_even_pool_scale, m_even_w_out, m_odd_w_in, m_odd_dw, m_odd_dw_b, m_odd_ln_g, m_odd_ln_b, m_odd_w_out, m_mlp_w_up, m_mlp_w_down, v_norm_mix_pre, v_norm_mix_post, v_norm_mlp_pre, v_norm_mlp_post, v_even_w_in, v_even_conv, v_even_a_log, v_even_dt_bias, v_even_dn_norm, v_even_pool_w, v_even_pool_scale, v_even_w_out, v_odd_w_in, v_odd_dw, v_odd_dw_b, v_odd_ln_g, v_odd_ln_b, v_odd_w_out, v_mlp_w_up, v_mlp_w_down):
    names = ["norm_mix_pre", "norm_mix_post", "norm_mlp_pre", "norm_mlp_post", "even_w_in", "even_conv", "even_a_log",
             "even_dt_bias", "even_dn_norm", "even_pool_w", "even_pool_scale", "even_w_out", "odd_w_in", "odd_dw",
             "odd_dw_b", "odd_ln_g", "odd_ln_b", "odd_w_out", "mlp_w_up", "mlp_w_down"]
    W = dict(zip(names, (norm_mix_pre, norm_mix_post, norm_mlp_pre, norm_mlp_post, even_w_in, even_conv, even_a_log,
                         even_dt_bias, even_dn_norm, even_pool_w, even_pool_scale, even_w_out, odd_w_in, odd_dw,
                         odd_dw_b, odd_ln_g, odd_ln_b, odd_w_out, mlp_w_up, mlp_w_down)))
    Mo = dict(zip(names, (m_norm_mix_pre, m_norm_mix_post, m_norm_mlp_pre, m_norm_mlp_post, m_even_w_in, m_even_conv,
                          m_even_a_log, m_even_dt_bias, m_even_dn_norm, m_even_pool_w, m_even_pool_scale, m_even_w_out,
                          m_odd_w_in, m_odd_dw, m_odd_dw_b, m_odd_ln_g, m_odd_ln_b, m_odd_w_out, m_mlp_w_up,
                          m_mlp_w_down)))
    Vo = dict(zip(names, (v_norm_mix_pre, v_norm_mix_post, v_norm_mlp_pre, v_norm_mlp_post, v_even_w_in, v_even_conv,
                          v_even_a_log, v_even_dt_bias, v_even_dn_norm, v_even_pool_w, v_even_pool_scale, v_even_w_out,
                          v_odd_w_in, v_odd_dw, v_odd_dw_b, v_odd_ln_g, v_odd_ln_b, v_odd_w_out, v_mlp_w_up,
                          v_mlp_w_down)))
    shard_axis = {"even_w_in": 2, "even_conv": 2, "even_pool_w": 2, "even_w_out": 1, "odd_w_in": 2, "odd_dw": 2,
                  "odd_dw_b": 1, "odd_ln_g": 1, "odd_ln_b": 1, "odd_w_out": 1, "mlp_w_up": 2, "mlp_w_down": 1}

    S, D = x.shape[1], x.shape[2]
    depth = norm_mix_pre.shape[0]
    H = even_a_log.shape[1]
    dh = even_dn_norm.shape[1]
    DNW = H * dh
    PW = even_pool_scale.shape[1]
    G = len(POOL_WINDOWS)
    PG = PW // G
    KC = even_conv.shape[1]
    BAW = 2 * LANE
    P_COLS = 4 * DNW + PW + BAW
    x2 = x.reshape(S, D)
    tgt = loss_target.reshape(S, D)

    small = ["even_conv", "odd_dw", "odd_dw_b", "odd_ln_g", "odd_ln_b"]
    plug_s, unpack_s = _gather_unit([W[n] for n in small], [shard_axis[n] for n in small], F32)
    full = dict(zip(small, unpack_s(plug_s.after(_run_plug("gather_small", plug_s)))))
    CW = odd_w_out.shape[1] * N_CHIPS
    wfull = {}

    def mixer_weights(i):
        return [(n, i // 2) for n in (("even_w_in", "even_pool_w", "even_w_out") if i % 2 == 0 else ("odd_w_in", "odd_w_out"))]

    def gather_unit(keys):
        plug, unpack = _gather_unit([W[n][l] for n, l in keys], [shard_axis[n] - 1 for n, _ in keys], BF16)
        return plug, lambda res: wfull.update(zip(keys, unpack(plug.after(res))))

    def riding_gather(i):
        keys = [("mlp_w_up", i), ("mlp_w_down", i)] + (mixer_weights(i + 1) if i + 1 < depth else [])
        return gather_unit(keys)

    plug0, put0 = gather_unit(mixer_weights(0))
    put0(_run_plug("gather_first", plug0))

    def even_w_in_layout(w):
        o1 = 4 * DNW
        return jnp.concatenate([w[:, :o1], w[:, o1 + 2 * H:], w[:, o1:o1 + 2 * H],
                                jnp.zeros((w.shape[0], BAW - 2 * H), w.dtype)], axis=1)

    def even_w_in_unlayout(g):
        o1 = 4 * DNW
        return jnp.concatenate([g[:, :o1], g[:, o1 + PW:o1 + PW + 2 * H], g[:, o1:o1 + PW]], axis=1)

    def pool_blockdiag(pw):
        return jnp.concatenate([jnp.pad(pw[gi], ((0, 0), (gi * PG, PW - (gi + 1) * PG))) for gi in range(G)], axis=0)

    pool_taps = max(POOL_WINDOWS)
    tap = jnp.arange(pool_taps)[:, None]
    win_c = jnp.repeat(jnp.asarray(POOL_WINDOWS, F32), PG)[None, :]
    pool_mask = (tap >= pool_taps - win_c).astype(F32)

    grads = {}
    tr_full = 128 if D > 1024 else 256

    saved = []
    xc = x2
    for i in range(depth):
        jl = i // 2
        sv = {"x_in": xc}
        g1, g2, g3, g4 = (W[n][i:i + 1] for n in ("norm_mix_pre", "norm_mix_post", "norm_mlp_pre", "norm_mlp_post"))
        (h,) = _rowwise(f"l{i}_rms_in", _f_rms, [(xc, 0, D)], [(g1, None, D)], [(D, BF16)], S=S, tr=tr_full)
        sv["h"] = h
        ride, put = riding_gather(i)
        if i % 2 == 0:
            w_in = even_w_in_layout(wfull["even_w_in", jl])
            p = _matmul(f"l{i}_w_in", h, w_in, "nn", tn=768)
            conv_w = full["even_conv"][jl]
            c = _dwconv_fwd(f"l{i}_conv", p, 0, conv_w, S=S, C=3 * DNW)
            (qk,) = _rowwise(f"l{i}_qk", _f_l2silu, [(c, 0, dh)], [], [(dh, F32)], S=S, ncb=2 * H, tr=1024)
            (vv,) = _rowwise(f"l{i}_v", _f_silu, [(c, 2 * DNW // dh, dh)], [], [(dh, F32)], S=S, ncb=H, tr=1024)
            alog = _lane_row(W["even_a_log"][jl], H)
            dtb = _lane_row(W["even_dt_bias"][jl], H)
            ba_off = (4 * DNW + PW) // LANE
            (gb,) = _rowwise(f"l{i}_gates", _make_gates(H), [(p, ba_off, LANE)], [(alog, None, LANE), (dtb, None, LANE)],
                             [(LANE, F32)], S=S, tr=1024)
            Cn = min(DN_CHUNK, S)
            gT = gb[:, H:2 * H].reshape(S // Cn, Cn, H).transpose(0, 2, 1)
            (o, sp, Tm), landed = _delta_fwd(f"l{i}_delta", qk, vv, gb, gT, S=S, H=H, dh=dh, comm=[ride])
            put(landed)
            dn = W["even_dn_norm"][jl][None, :]
            (on,) = _rowwise(f"l{i}_outgate", _f_outgate, [(o, 0, dh), (p, 3 * DNW // dh, dh)], [(dn, None, dh)],
                             [(dh, BF16)], S=S, ncb=H, tr=1024)
            pcb = _pick(PW, 512)
            pooled = _dwconv_fwd(f"l{i}_pool", p, 4 * DNW // pcb, pool_mask, S=S, C=PW, win=win_c, out_dtype=BF16, cb=pcb)
            wbd = pool_blockdiag(wfull["even_pool_w", jl])
            ypre = _matmul(f"l{i}_pool_w", pooled, wbd, "nn")
            psc = W["even_pool_scale"][jl][None, :]
            (ypool,) = _rowwise(f"l{i}_pool_scale", _f_scale, [(ypre, 0, PW)], [(psc, None, PW)], [(PW, BF16)], S=S)
            mixin = jnp.concatenate([on, ypool], axis=1)
            mix = _matmul(f"l{i}_w_out", mixin, wfull["even_w_out", jl], "nn")
            sv.update(p=p, c=c, qk=qk, v=vv, gb=gb, gT=gT, o=o, sp=sp, Tm=Tm, pooled=pooled, ypre=ypre, mixin=mixin,
                      w_in=w_in, wbd=wbd, alog=alog, dtb=dtb, dn=dn, psc=psc, conv_w=conv_w)
        else:
            p = _matmul(f"l{i}_w_in", h, wfull["odd_w_in", jl], "nn")
            ocb = _pick(CW, 1024)
            (u0,) = _rowwise(f"l{i}_glu", _f_glu, [(p, 0, ocb), (p, CW // ocb, ocb)], [], [(ocb, F32)], S=S,
                             ncb=CW // ocb)
            dw_w, dw_b = full["odd_dw"][jl], full["odd_dw_b"][jl][None, :]
            u1, landed = _dwconv_fwd(f"l{i}_dwconv", u0, 0, dw_w, S=S, C=CW, bias=dw_b, comm=[ride])
            put(landed)
            lg, lb = full["odd_ln_g"][jl][None, :], full["odd_ln_b"][jl][None, :]
            (u2,) = _rowwise(f"l{i}_lnsilu", _f_lnsilu, [(u1, 0, CW)], [(lg, None, CW), (lb, None, CW)], [(CW, BF16)],
                             S=S, tr=tr_full)
            mix = _matmul(f"l{i}_w_out", u2, wfull["odd_w_out", jl], "nn")
            sv.update(p=p, u0=u0, u1=u1, mixin=u2, dw_w=dw_w, lg=lg, lb=lb)
        x_mid, h2 = _rowwise(f"l{i}_mid", _f_mid, [(xc, 0, D), (mix, 0, D)], [(g2, None, D), (g3, None, D)],
                             [(D, F32), (D, BF16)], S=S, tr=tr_full)
        act = _matmul(f"l{i}_w_up", h2, wfull["mlp_w_up", i], "nn", epi=_relu2, out_dtypes=[BF16])
        ff = _matmul(f"l{i}_w_down", act, wfull["mlp_w_down", i], "nn")
        (x_out,) = _rowwise(f"l{i}_out", _f_resid, [(x_mid, 0, D), (ff, 0, D)], [(g4, None, D)], [(D, F32)], S=S,
                            tr=tr_full)
        sv.update(mix=mix, h2=h2, act=act, ff=ff, g=(g1, g2, g3, g4))
        saved.append(sv)
        xc = x_out

    dy, loss_row = _rowwise("loss", _f_loss, [(xc, 0, D), (tgt, 0, D)], [], [(D, F32)], [(1, LANE, False)], S=S,
                            tr=tr_full)
    loss = lax.psum(loss_row[0, 0], ("x", "y", "c"))

    def mixer_params(i):
        ns = (("even_w_in", "even_conv", "even_a_log", "even_dt_bias", "even_dn_norm", "even_pool_w", "even_pool_scale",
               "even_w_out") if i % 2 == 0 else ("odd_w_in", "odd_dw", "odd_dw_b", "odd_ln_g", "odd_ln_b", "odd_w_out"))
        return [(n, i // 2) for n in ns] + [("norm_mix_pre", i)]

    def mlp_params(i):
        return [(n, i) for n in ("mlp_w_up", "mlp_w_down", "norm_mlp_pre", "norm_mlp_post", "norm_mix_post")]

    def reduce_unit(name, keys):
        axes = [shard_axis[n] - 1 if n in shard_axis else None for n, _ in keys]
        plug, offs, shapes = _reduce_unit(name, [grads[k] for k in keys], axes)

        def unit(res):
            return keys, offs, shapes, _sum_chips(name + "_sum", plug.after(res))

        return plug, unit

    units = []
    dx = dy
    for i in reversed(range(depth)):
        jl = i // 2
        sv = saved[i]
        g1, g2, g3, g4 = sv["g"]
        d_ff, dg4 = _rowwise_bwd(f"l{i}_out_b", _f_rms, [(sv["ff"], 0, D)], [(g4, None, D)], [(dx, 0, D)], [BF16],
                                 S=S, tr=tr_full)
        grads["norm_mlp_post", i] = dg4[0]
        du = _matmul(f"l{i}_w_down_bx", d_ff, wfull["mlp_w_down", i], "nt", epi=_relu2_bwd, extras=[sv["act"]],
                     out_dtypes=[BF16])
        grads["mlp_w_down", i] = _matmul(f"l{i}_w_down_bw", sv["act"], d_ff, "tn")
        dh2 = _matmul(f"l{i}_w_up_bx", du, wfull["mlp_w_up", i], "nt")
        grads["mlp_w_up", i] = _matmul(f"l{i}_w_up_bw", sv["h2"], du, "tn")
        dx, d_mix, dg2, dg3 = _rowwise_bwd(
            f"l{i}_mid_b", _f_mid, [(sv["x_in"], 0, D), (sv["mix"], 0, D)], [(g2, None, D), (g3, None, D)],
            [(dx, 0, D), (dh2, 0, D)], [F32, BF16], S=S, tr=tr_full)
        grads["norm_mix_post", i], grads["norm_mlp_pre", i] = dg2[0], dg3[0]
        keys = (mixer_params(i + 1) if i + 1 < depth else []) + mlp_params(i)
        ride, unit = reduce_unit(f"red{i}", keys)
        if i % 2 == 0:
            d_mixin = _matmul(f"l{i}_w_out_bx", d_mix, wfull["even_w_out", jl], "nt")
            grads["even_w_out", jl] = _matmul(f"l{i}_w_out_bw", sv["mixin"], d_mix, "tn")
            p = sv["p"]
            pcb = _pick(PW, 512)
            d_ypre, dpsc = _rowwise_bwd(f"l{i}_pool_scale_b", _f_scale, [(sv["ypre"], 0, PW)], [(sv["psc"], None, PW)],
                                        [(d_mixin, DNW // PW, PW)], [BF16], S=S)
            grads["even_pool_scale", jl] = dpsc[0]
            d_pooled = _matmul(f"l{i}_pool_w_bx", d_ypre, sv["wbd"], "nt")
            dwbd = _matmul(f"l{i}_pool_w_bw", sv["pooled"], d_ypre, "tn")
            grads["even_pool_w", jl] = jnp.stack([dwbd[gi * PG:(gi + 1) * PG, gi * PG:(gi + 1) * PG] for gi in range(G)])
            d_xp = _dwconv_bwd(f"l{i}_pool_b", None, 0, d_pooled, pool_mask, S=S, C=PW, win=win_c, want_dw=False,
                               cb=pcb)[0]
            d_o, d_z, ddn = _rowwise_bwd(f"l{i}_outgate_b", _f_outgate, [(sv["o"], 0, dh), (p, 3 * DNW // dh, dh)],
                                         [(sv["dn"], None, dh)], [(d_mixin, 0, dh)], [F32, F32], S=S, ncb=H, tr=1024)
            grads["even_dn_norm", jl] = ddn[0]
            (dqk, dv, dgb), landed = _delta_bwd(f"l{i}_delta_b", sv["qk"], sv["v"], sv["gb"], sv["gT"], sv["sp"],
                                                sv["Tm"], d_o, S=S, H=H, dh=dh, comm=[ride])
            ba_off = (4 * DNW + PW) // LANE
            d_ba, dalog, ddtb = _rowwise_bwd(f"l{i}_gates_b", _make_gates(H), [(p, ba_off, LANE)],
                                             [(sv["alog"], None, LANE), (sv["dtb"], None, LANE)], [(dgb, 0, LANE)],
                                             [F32], S=S, tr=1024)
            grads["even_a_log", jl], grads["even_dt_bias", jl] = dalog[0, H:2 * H], ddtb[0, H:2 * H]
            (dc_qk,) = _rowwise_bwd(f"l{i}_qk_b", _f_l2silu, [(sv["c"], 0, dh)], [], [(dqk, 0, dh)], [F32], S=S,
                                    ncb=2 * H, tr=1024)
            (dc_v,) = _rowwise_bwd(f"l{i}_v_b", _f_silu, [(sv["c"], 2 * DNW // dh, dh)], [], [(dv, 0, dh)], [F32], S=S,
                                   ncb=H, tr=1024)
            dc = jnp.concatenate([dc_qk, dc_v], axis=1)
            d_qkv, dconv, _ = _dwconv_bwd(f"l{i}_conv_b", p, 0, dc, sv["conv_w"], S=S, C=3 * DNW)
            grads["even_conv", jl] = dconv
            dp = jnp.concatenate([d_qkv.astype(BF16), d_z.astype(BF16), d_xp.astype(BF16), d_ba.astype(BF16),
                                  jnp.zeros((S, BAW - LANE), BF16)], axis=1)
            dh_ = _matmul(f"l{i}_w_in_bx", dp, sv["w_in"], "nt", tk=768)
            grads["even_w_in", jl] = even_w_in_unlayout(_matmul(f"l{i}_w_in_bw", sv["h"], dp, "tn", tn=768))
        else:
            d_u2 = _matmul(f"l{i}_w_out_bx", d_mix, wfull["odd_w_out", jl], "nt")
            grads["odd_w_out", jl] = _matmul(f"l{i}_w_out_bw", sv["mixin"], d_mix, "tn")
            d_u1, dlg, dlb = _rowwise_bwd(f"l{i}_lnsilu_b", _f_lnsilu, [(sv["u1"], 0, CW)],
                                          [(sv["lg"], None, CW), (sv["lb"], None, CW)], [(d_u2, 0, CW)], [F32], S=S,
                                          tr=tr_full)
            grads["odd_ln_g", jl], grads["odd_ln_b", jl] = dlg[0], dlb[0]
            (d_u0, ddw, ddb), landed = _dwconv_bwd(f"l{i}_dwconv_b", sv["u0"], 0, d_u1, sv["dw_w"], S=S, C=CW, comm=[ride])
            grads["odd_dw", jl], grads["odd_dw_b", jl] = ddw, ddb[0]
            p = sv["p"]
            ocb = _pick(CW, 1024)
            da, dgate = _rowwise_bwd(f"l{i}_glu_b", _f_glu, [(p, 0, ocb), (p, CW // ocb, ocb)], [], [(d_u0, 0, ocb)],
                                     [BF16, BF16], S=S, ncb=CW // ocb)
            dp = jnp.concatenate([da, dgate], axis=1)
            dh_ = _matmul(f"l{i}_w_in_bx", dp, wfull["odd_w_in", jl], "nt")
            grads["odd_w_in", jl] = _matmul(f"l{i}_w_in_bw", sv["h"], dp, "tn")
        units.append(unit(landed))
        dx, dg1 = _rowwise_bwd(f"l{i}_rms_in_b", _f_rms, [(sv["x_in"], 0, D)], [(g1, None, D)], [(dh_, 0, D)], [F32],
                               adds=[(dx, 0, D)], S=S, tr=tr_full)
        grads["norm_mix_pre", i] = dg1[0]
    grad_x = dx.reshape(x.shape)
    ride, unit = reduce_unit("red_last", mixer_params(0))
    units.append(unit(_run_plug("red_last_exchange", ride)))

    theirs = _sibling_swap("grad_sibling_swap", jnp.concatenate([u[3] for u in units], axis=0))
    south = lax.axis_index("c") == 0
    gshard, r = {}, 0
    for keys, offs, shapes, half in units:
        rh = half.shape[0]
        other = theirs[r:r + rh]
        whole = jnp.concatenate([jnp.where(south, half, other), jnp.where(south, other, half)], axis=0)
        for k, off, shp in zip(keys, offs, shapes):
            gshard[k] = _segment(whole, off, shp)
        r += rh

    outs_g, outs_d, outs_m, outs_v = [], [], [], []
    for n in names:
        g_n = jnp.stack([gshard[n, l] for l in range(W[n].shape[0])])
        g_o, d_o, m_o, v_o = _adamw(f"adamw_{n}", W[n], Mo[n], Vo[n], g_n)
        outs_g.append(g_o)
        outs_d.append(d_o)
        outs_m.append(m_o)
        outs_v.append(v_o)
    return (loss, grad_x, *outs_g, *outs_d, *outs_m, *outs_v)
```

```python
import functools
import math

import jax
import jax.numpy as jnp
from jax import lax
from jax.experimental import pallas as pl
from jax.experimental.pallas import tpu as pltpu

F32 = jnp.float32
BF16 = jnp.bfloat16
EPS = 1e-6
DN_CHUNK = 64
POOL_WINDOWS = (2, 4, 8, 16)
ADAM_LR, ADAM_B1, ADAM_B2, ADAM_EPS, ADAM_WD, ADAM_STEP = 0.001, 0.9, 0.999, 1e-08, 0.01, 10
LANE = 128
FLAT_L = 2048
ADAM_ROWS = 128
VMEM_LIMIT = 56 * 1024 * 1024
N_CHIPS = 4
HI = lax.Precision.HIGHEST
MESH = pl.DeviceIdType.MESH


def _cparams(sem):
    return pltpu.CompilerParams(dimension_semantics=sem, vmem_limit_bytes=VMEM_LIMIT)


def _pallas(body, args, *, name, grid, in_specs, out_specs, out_shape, scratch_shapes=(), sem=None, comm=()):
    n_in, n_out, n_scr = len(in_specs), len(out_specs), len(scratch_shapes)
    if not comm:
        return pl.pallas_call(body, name=name, grid=grid, in_specs=in_specs, out_specs=out_specs, out_shape=out_shape,
                              scratch_shapes=list(scratch_shapes), compiler_params=_cparams(sem))(*args)
    ci = [len(p.ins) for p in comm]
    co = [len(p.outs) for p in comm]
    cs = [len(p.sems) for p in comm]

    def wrapped(*refs):
        ins, pos = refs[:n_in], n_in
        cins = refs[pos:pos + sum(ci)]
        pos += sum(ci)
        outs = refs[pos:pos + n_out]
        pos += n_out
        couts = refs[pos:pos + sum(co)]
        pos += sum(co)
        scr = refs[pos:pos + n_scr]
        csems = refs[pos + n_scr:]
        ids = [pl.program_id(a) for a in range(len(grid))]
        first, last = ids[0] == 0, ids[0] == grid[0] - 1
        for a in range(1, len(grid)):
            first = jnp.logical_and(first, ids[a] == 0)
            last = jnp.logical_and(last, ids[a] == grid[a] - 1)
        parts, a, b, c = [], 0, 0, 0
        for p, na, nb, nc in zip(comm, ci, co, cs):
            parts.append((p, cins[a:a + na], couts[b:b + nb], csems[c:c + nc]))
            a, b, c = a + na, b + nb, c + nc

        @pl.when(first)
        def _():
            for p, pi, po, ps in parts:
                p.start(pi, po, ps)

        body(*ins, *outs, *scr)

        @pl.when(last)
        def _():
            for p, pi, po, ps in parts:
                p.finish(pi, po, ps)

    any_spec = pl.BlockSpec(memory_space=pl.ANY)
    res = pl.pallas_call(
        wrapped, name=name, grid=grid,
        in_specs=list(in_specs) + [any_spec] * sum(ci), out_specs=list(out_specs) + [any_spec] * sum(co),
        out_shape=list(out_shape) + [s for p in comm for s in p.outs],
        scratch_shapes=list(scratch_shapes) + [s for p in comm for s in p.sems],
        compiler_params=_cparams(("arbitrary",) * len(grid)),
    )(*args, *[a for p in comm for a in p.ins])
    return res


def _pick(dim, target, mult=LANE):
    if dim <= target:
        return dim
    t = (target // mult) * mult
    while t >= mult:
        if dim % t == 0:
            return t
        t -= mult
    return dim


def _sigmoid(x):
    return 1.0 / (1.0 + jnp.exp(-x))


def _silu(x):
    return x * _sigmoid(x)


def _softplus(x):
    return jnp.maximum(x, 0.0) + jnp.log(1.0 + jnp.exp(-jnp.abs(x)))


def _rms(x, g):
    return x * lax.rsqrt(jnp.mean(x * x, axis=-1, keepdims=True) + EPS) * g


def _matmul(name, a, b, mode, out_dtype=F32, tm=1024, tn=1024, tk=2048, epi=None, extras=(), out_dtypes=None, comm=()):
    if mode == "nn":
        (M, K), (K2, N) = a.shape, b.shape
    elif mode == "nt":
        (M, K), (N, K2) = a.shape, b.shape
    else:
        (K, M), (K2, N) = a.shape, b.shape
    assert K == K2, (name, a.shape, b.shape, mode)
    tm, tn, tk = _pick(M, tm), _pick(N, tn), _pick(K, tk)
    nk = K // tk
    if mode == "nn":
        a_spec = pl.BlockSpec((tm, tk), lambda i, j, k: (i, k))
        b_spec = pl.BlockSpec((tk, tn), lambda i, j, k: (k, j))
        dims = (((1,), (0,)), ((), ()))
    elif mode == "nt":
        a_spec = pl.BlockSpec((tm, tk), lambda i, j, k: (i, k))
        b_spec = pl.BlockSpec((tn, tk), lambda i, j, k: (j, k))
        dims = (((1,), (1,)), ((), ()))
    else:
        a_spec = pl.BlockSpec((tk, tm), lambda i, j, k: (k, i))
        b_spec = pl.BlockSpec((tk, tn), lambda i, j, k: (k, j))
        dims = (((0,), (0,)), ((), ()))
    out_dtypes = list(out_dtypes) if out_dtypes is not None else [out_dtype]
    ne, no = len(extras), len(out_dtypes)
    in_place = epi is None and out_dtypes == [F32]
    use_acc = nk > 1 and not in_place

    def finish(acc, extra_refs, out_refs):
        res = acc if epi is None else epi(acc, *[r[...] for r in extra_refs])
        res = res if isinstance(res, (tuple, list)) else (res,)
        for r, v in zip(out_refs, res):
            r[...] = v.astype(r.dtype)

    def body(a_ref, b_ref, *rest):
        extra_refs, out_refs = rest[:ne], rest[ne:ne + no]
        part = lax.dot_general(a_ref[...].astype(BF16), b_ref[...].astype(BF16), dims, preferred_element_type=F32)
        if nk == 1:
            finish(part, extra_refs, out_refs)
            return
        k = pl.program_id(2)
        acc_ref = rest[-1] if use_acc else out_refs[0]

        @pl.when(k == 0)
        def _():
            acc_ref[...] = part

        @pl.when(k > 0)
        def _():
            acc_ref[...] += part

        if use_acc:
            @pl.when(k == nk - 1)
            def _():
                finish(acc_ref[...], extra_refs, out_refs)

    o_spec = pl.BlockSpec((tm, tn), lambda i, j, k: (i, j))
    res = _pallas(body, (a, b, *extras), name=name, grid=(M // tm, N // tn, nk),
                  in_specs=[a_spec, b_spec] + [o_spec] * ne, out_specs=[o_spec] * no,
                  out_shape=[jax.ShapeDtypeStruct((M, N), dt) for dt in out_dtypes],
                  scratch_shapes=[pltpu.VMEM((tm, tn), F32)] if use_acc else [],
                  sem=("parallel", "parallel", "arbitrary"), comm=comm)
    outs = res[0] if no == 1 else res[:no]
    return (outs, res[no:]) if comm else outs


def _row_spec(tr, C, off):
    return pl.BlockSpec((tr, C), lambda j, i: (i, off + j))


def _par_spec(k, C, off):
    if off is None:
        return pl.BlockSpec((k, C), lambda j, i: (0, 0))
    return pl.BlockSpec((k, C), lambda j, i: (0, off + j))


def _rowwise(name, fn, rows, params, outs, reds=(), *, S, ncb=1, tr=256):
    tr = min(tr, S)
    nr, npar, no = len(rows), len(params), len(outs)

    def body(*refs):
        j, i = pl.program_id(0), pl.program_id(1)
        ins = [r[...].astype(F32) for r in refs[:nr + npar]]
        res = fn(i * tr, j, *ins)
        res = res if isinstance(res, (tuple, list)) else (res,)
        out_refs = refs[nr + npar:]
        for r, v in zip(out_refs[:no], res[:no]):
            r[...] = v.astype(r.dtype)
        for (k, C, per_j), r, v in zip(reds, out_refs[no:], res[no:]):
            first = (i == 0) if per_j else jnp.logical_and(i == 0, j == 0)

            @pl.when(first)
            def _(r=r):
                r[...] = jnp.zeros_like(r)

            r[...] += v

    in_specs = [_row_spec(tr, C, off) for (_, off, C) in rows] + [_par_spec(a.shape[0], C, off) for (a, off, C) in params]
    out_specs = [pl.BlockSpec((tr, C), lambda j, i: (i, j)) for (C, _) in outs]
    out_specs += [pl.BlockSpec((k, C), (lambda j, i: (0, j)) if per_j else (lambda j, i: (0, 0))) for (k, C, per_j) in reds]
    out_shape = [jax.ShapeDtypeStruct((S, ncb * C), dt) for (C, dt) in outs]
    out_shape += [jax.ShapeDtypeStruct((k, C * (ncb if per_j else 1)), F32) for (k, C, per_j) in reds]
    res = pl.pallas_call(
        body, name=name, grid=(ncb, S // tr), in_specs=in_specs, out_specs=out_specs, out_shape=out_shape,
        compiler_params=_cparams(("arbitrary", "arbitrary")),
    )(*[a for (a, _, _) in rows], *[a for (a, _, _) in params])
    return res


def _rowwise_bwd(name, fn, rows, params, cots, drow, adds=None, *, S, ncb=1, tr=128):
    tr = min(tr, S)
    nr, npar, nc = len(rows), len(params), len(cots)
    adds = adds or [None] * nr
    add_list = [a for a in adds if a is not None]
    na = len(add_list)

    def body(*refs):
        j, i = pl.program_id(0), pl.program_id(1)
        ins = [r[...].astype(F32) for r in refs[:nr + npar]]
        cts = [r[...].astype(F32) for r in refs[nr + npar:nr + npar + nc]]
        add_refs = list(refs[nr + npar + nc:nr + npar + nc + na])
        out_refs = list(refs[nr + npar + nc + na:])

        def f(*a):
            res = fn(i * tr, j, *a)
            return tuple(res) if isinstance(res, (tuple, list)) else (res,)

        _, vjp = jax.vjp(f, *ins)
        grads = vjp(tuple(cts))
        for idx in range(nr):
            if drow[idx] is None:
                continue
            g = grads[idx]
            if adds[idx] is not None:
                g = g + add_refs.pop(0)[...].astype(F32)
            r = out_refs.pop(0)
            r[...] = g.astype(r.dtype)
        for idx in range(npar):
            per_j = params[idx][1] is not None
            first = (i == 0) if per_j else jnp.logical_and(i == 0, j == 0)
            r = out_refs.pop(0)

            @pl.when(first)
            def _(r=r):
                r[...] = jnp.zeros_like(r)

            r[...] += grads[nr + idx]

    in_specs = [_row_spec(tr, C, off) for (_, off, C) in rows]
    in_specs += [_par_spec(a.shape[0], C, off) for (a, off, C) in params]
    in_specs += [_row_spec(tr, C, off) for (_, off, C) in cots]
    in_specs += [_row_spec(tr, C, off) for (_, off, C) in add_list]
    out_specs, out_shape = [], []
    for idx in range(nr):
        if drow[idx] is not None:
            C = rows[idx][2]
            out_specs.append(pl.BlockSpec((tr, C), lambda j, i: (i, j)))
            out_shape.append(jax.ShapeDtypeStruct((S, ncb * C), drow[idx]))
    for (a, off, C) in params:
        per_j = off is not None
        out_specs.append(pl.BlockSpec((a.shape[0], C), (lambda j, i: (0, j)) if per_j else (lambda j, i: (0, 0))))
        out_shape.append(jax.ShapeDtypeStruct((a.shape[0], C * (ncb if per_j else 1)), F32))
    return pl.pallas_call(
        body, name=name, grid=(ncb, S // tr), in_specs=in_specs, out_specs=out_specs, out_shape=out_shape,
        compiler_params=_cparams(("arbitrary", "arbitrary")),
    )(*[a for (a, _, _) in rows], *[a for (a, _, _) in params], *[a for (a, _, _) in cots], *[a for (a, _, _) in add_list])


def _halo_rows(K):
    return 8 * ((K - 1 + 7) // 8)


def _inv_count(row0, tr, win):
    t = (row0 + lax.broadcasted_iota(jnp.int32, (tr, 1), 0)).astype(F32)
    return 1.0 / jnp.minimum(t + 1.0, win)


CONV_ROWS = 32


def _shifted_down(xp, rows, K):
    for p in sorted({s % 8 for s in range(K)} - {0}):
        xp[p, 8:rows, :] = xp[0, 8 - p:rows - p, :]


def _shifted_up(yp, rows, K):
    for p in sorted({s % 8 for s in range(K)} - {0}):
        yp[p, 0:rows - 8, :] = yp[0, p:rows - 8 + p, :]


def _dwconv_fwd(name, x, x_off, w, *, S, C, bias=None, win=None, out_dtype=F32, cb=512, tr=256, comm=()):
    K = w.shape[0]
    cb, tr = _pick(C, cb), min(tr, S)
    HB = min(_halo_rows(K), tr)
    assert K - 1 <= HB and tr % HB == 0 and C % cb == 0
    nb = tr // HB
    RB = min(CONV_ROWS, tr)
    extra = [a for a in (bias, win) if a is not None]

    def body(xh_ref, x_ref, w_ref, *rest):
        y_ref, xp = rest[-2], rest[-1]
        i = pl.program_id(1)
        xp[0, 0:HB, :] = jnp.where(i > 0, xh_ref[...].astype(F32), 0.0)
        xp[0, HB:HB + tr, :] = x_ref[...].astype(F32)
        _shifted_down(xp, HB + tr, K)

        def sub(rb, carry):
            r0 = rb * RB
            acc = jnp.zeros((RB, cb), F32)
            for jj in range(K):
                s = K - 1 - jj
                start = pl.multiple_of(HB - 8 * (s // 8) + r0, 8)
                acc = acc + w_ref[jj:jj + 1, :] * xp[s % 8, pl.ds(start, RB), :]
            if bias is not None:
                acc = acc + rest[0][...]
            if win is not None:
                rows = pl.ds(pl.multiple_of(r0, 8), RB)
                acc = acc * _inv_count(i * tr + r0, RB, rest[0][...]) - x_ref[rows, :].astype(F32)
            y_ref[pl.ds(pl.multiple_of(r0, 8), RB), :] = acc.astype(y_ref.dtype)
            return carry

        lax.fori_loop(0, tr // RB, sub, 0)

    in_specs = [pl.BlockSpec((HB, cb), lambda j, i: (jnp.maximum(i * nb - 1, 0), x_off + j)),
                pl.BlockSpec((tr, cb), lambda j, i: (i, x_off + j)),
                pl.BlockSpec((K, cb), lambda j, i: (0, j))]
    in_specs += [pl.BlockSpec((1, cb), lambda j, i: (0, j)) for _ in extra]
    res = _pallas(body, (x, x, w, *extra), name=name, grid=(C // cb, S // tr), in_specs=in_specs,
                  out_specs=[pl.BlockSpec((tr, cb), lambda j, i: (i, j))],
                  out_shape=[jax.ShapeDtypeStruct((S, C), out_dtype)],
                  scratch_shapes=[pltpu.VMEM((8, HB + tr, cb), F32)], sem=("parallel", "arbitrary"), comm=comm)
    return (res[0], res[1:]) if comm else res[0]


def _dwconv_bwd(name, x, x_off, dy, w, *, S, C, win=None, want_dw=True, cb=512, tr=256, comm=()):
    K = w.shape[0]
    cb, tr = _pick(C, cb), min(tr, S)
    HB = min(_halo_rows(K), tr)
    nb, nt = tr // HB, S // tr
    RB = min(CONV_ROWS, tr)

    def body(*refs):
        if want_dw:
            xh_ref, x_ref, dy_ref, dyn_ref, w_ref = refs[:5]
            rest = refs[5:]
        else:
            dy_ref, dyn_ref, w_ref = refs[:3]
            rest = refs[3:]
        i = pl.program_id(1)
        dyt = dy_ref[...].astype(F32)
        dyn = jnp.where(i < nt - 1, dyn_ref[...].astype(F32), 0.0)
        if win is not None:
            win_v = rest[0][...]
            rest = rest[1:]
            yy_t = dyt * _inv_count(i * tr, tr, win_v)
            dyn = dyn * _inv_count((i + 1) * tr, HB, win_v)
        else:
            yy_t = dyt
        if want_dw:
            dx_ref, dw_ref, db_ref, yp, xp, dw8, db8 = rest
        else:
            dx_ref, yp = rest
        yp[0, 0:tr, :] = yy_t
        yp[0, tr:tr + HB, :] = dyn
        _shifted_up(yp, tr + HB, K)

        def sub(rb, carry):
            r0 = rb * RB
            acc = jnp.zeros((RB, cb), F32)
            for jj in range(K):
                s = K - 1 - jj
                start = pl.multiple_of(8 * (s // 8) + r0, 8)
                acc = acc + w_ref[jj:jj + 1, :] * yp[s % 8, pl.ds(start, RB), :]
            rows = pl.ds(pl.multiple_of(r0, 8), RB)
            if win is not None:
                acc = acc - dy_ref[rows, :].astype(F32)
            dx_ref[rows, :] = acc.astype(dx_ref.dtype)
            return carry

        lax.fori_loop(0, tr // RB, sub, 0)
        if want_dw:
            xp[0, 0:HB, :] = jnp.where(i > 0, xh_ref[...].astype(F32), 0.0)
            xp[0, HB:HB + tr, :] = x_ref[...].astype(F32)
            _shifted_down(xp, HB + tr, K)

            @pl.when(i == 0)
            def _():
                dw8[...] = jnp.zeros_like(dw8)
                db8[...] = jnp.zeros_like(db8)

            def sub_w(rb, carry):
                r0 = rb * RB
                dyb = dy_ref[pl.ds(pl.multiple_of(r0, 8), RB), :].astype(F32)
                for jj in range(K):
                    s = K - 1 - jj
                    start = pl.multiple_of(HB - 8 * (s // 8) + r0, 8)
                    prod = dyb * xp[s % 8, pl.ds(start, RB), :]
                    dw8[jj] += jnp.sum(prod.reshape(RB // 8, 8, cb), axis=0)
                db8[...] += jnp.sum(dyb.reshape(RB // 8, 8, cb), axis=0)
                return carry

            lax.fori_loop(0, tr // RB, sub_w, 0)

            @pl.when(i == nt - 1)
            def _():
                for jj in range(K):
                    dw_ref[jj:jj + 1, :] = jnp.sum(dw8[jj], axis=0, keepdims=True)
                db_ref[...] = jnp.sum(db8[...], axis=0, keepdims=True)

    last = S // HB - 1
    in_specs, args = [], []
    if want_dw:
        in_specs += [pl.BlockSpec((HB, cb), lambda j, i: (jnp.maximum(i * nb - 1, 0), x_off + j)),
                     pl.BlockSpec((tr, cb), lambda j, i: (i, x_off + j))]
        args += [x, x]
    in_specs += [pl.BlockSpec((tr, cb), lambda j, i: (i, j)),
                 pl.BlockSpec((HB, cb), lambda j, i: (jnp.minimum((i + 1) * nb, last), j)),
                 pl.BlockSpec((K, cb), lambda j, i: (0, j))]
    args += [dy, dy, w]
    if win is not None:
        in_specs.append(pl.BlockSpec((1, cb), lambda j, i: (0, j)))
        args.append(win)
    out_specs = [pl.BlockSpec((tr, cb), lambda j, i: (i, j))]
    out_shape = [jax.ShapeDtypeStruct((S, C), F32)]
    scratch = [pltpu.VMEM((8, tr + HB, cb), F32)]
    if want_dw:
        out_specs += [pl.BlockSpec((K, cb), lambda j, i: (0, j)), pl.BlockSpec((1, cb), lambda j, i: (0, j))]
        out_shape += [jax.ShapeDtypeStruct((K, C), F32), jax.ShapeDtypeStruct((1, C), F32)]
        scratch += [pltpu.VMEM((8, HB + tr, cb), F32), pltpu.VMEM((K, 8, cb), F32), pltpu.VMEM((8, cb), F32)]
    res = _pallas(body, args, name=name, grid=(C // cb, S // tr), in_specs=in_specs, out_specs=out_specs,
                  out_shape=out_shape, scratch_shapes=scratch, sem=("parallel", "arbitrary"), comm=comm)
    return (res[:len(out_specs)], res[len(out_specs):]) if comm else res


def _dot(a, b, dims, hi=False):
    if hi:
        return lax.dot_general(a, b, (dims, ((), ())), precision=HI, preferred_element_type=F32)
    return lax.dot_general(a.astype(BF16), b.astype(BF16), (dims, ((), ())), preferred_element_type=F32)


_NN, _NT, _TN = ((1,), (0,)), ((1,), (1,)), ((0,), (0,))


def _col(m, idx):
    lane = lax.broadcasted_iota(jnp.int32, m.shape, 1)
    return jnp.sum(jnp.where(lane == idx, m, 0.0), axis=1, keepdims=True)


def _row(m, idx):
    sub = lax.broadcasted_iota(jnp.int32, m.shape, 0)
    return jnp.sum(jnp.where(sub == idx, m, 0.0), axis=0, keepdims=True)


def _delta_chunk(q, k, v, beta, gcc, gcr, causal, strict, eye, scale, C):
    d = {}
    gam = jnp.where(causal, jnp.exp(jnp.where(causal, gcc - gcr, 0.0)), 0.0)
    eg = jnp.exp(gcc)
    g_last = _row(gcc, C - 1)
    d["gam"], d["eg"], d["g_last"] = gam, eg, g_last
    d["ek"] = jnp.exp(g_last - gcc)
    d["decay"] = jnp.exp(g_last)
    qs = q * scale
    kb = k * beta
    d["qs"], d["kb"] = qs, kb
    d["kk"] = _dot(kb, k, _NT)
    d["A"] = jnp.where(strict, d["kk"] * gam, 0.0)
    d["qk"] = _dot(qs, k, _NT)
    d["attn"] = jnp.where(causal, d["qk"] * gam, 0.0)
    d["vb"] = v * beta
    d["kbg"] = kb * eg
    d["qg"] = qs * eg
    d["kd"] = k * d["ek"]
    return d


def _split(m):
    hi = m.astype(BF16)
    return hi, (m - hi.astype(F32)).astype(BF16)


def _dot3(a, b, dims):
    return _dot(a[0], b[0], dims) + (_dot(a[0], b[1], dims) + _dot(a[1], b[0], dims))


def _tri_inverse(As, eye):
    P = [-A for A in As]
    T = [eye + p for p in P]
    n = 1
    while 2 * n < As[0].shape[0]:
        Ps = [_split(p) for p in P]
        P = [_dot3(ps, ps, _NN) for ps in Ps]
        Ts = [_split(t) for t in T]
        Ps = [_split(p) for p in P]
        T = [t + _dot3(ts, ps, _NN) for t, ts, ps in zip(T, Ts, Ps)]
        n *= 2
    return T


def _delta_fwd(name, qk, v, gb, gT, *, S, H, dh, comm=()):
    C = min(DN_CHUNK, S)
    N, W = S // C, H * dh
    scale = dh ** -0.5

    def body(qk_ref, v_ref, gb_ref, gT_ref, o_ref, sp_ref, T_ref, st):
        n = pl.program_id(0)

        @pl.when(n == 0)
        def _():
            st[...] = jnp.zeros_like(st)

        r = lax.broadcasted_iota(jnp.int32, (C, C), 0)
        c = lax.broadcasted_iota(jnp.int32, (C, C), 1)
        causal, strict = r >= c, r > c
        eye = (r == c).astype(F32)
        Lt = causal.astype(F32)
        gbv = gb_ref[...]
        gcum = _dot(Lt, gbv, _NN, hi=True)
        gcumT = _dot(gT_ref[0], Lt, _NT, hi=True)
        hs = range(H)
        sl = [slice(h * dh, (h + 1) * dh) for h in hs]
        d = [_delta_chunk(qk_ref[:, sl[h]], qk_ref[:, W + h * dh:W + (h + 1) * dh], v_ref[:, sl[h]], _col(gbv, h),
                          _col(gcum, H + h), _row(gcumT, h), causal, strict, eye, scale, C) for h in hs]
        T = _tri_inverse([d[h]["A"] for h in hs], eye)
        u = [_dot(T[h], d[h]["vb"], _NN) for h in hs]
        w = [_dot(T[h], d[h]["kbg"], _NN) for h in hs]
        s0 = [st[h] for h in hs]
        ws = [_dot(w[h], s0[h], _NN) for h in hs]
        qs0 = [_dot(d[h]["qg"], s0[h], _NN) for h in hs]
        v_new = [u[h] - ws[h] for h in hs]
        av = [_dot(d[h]["attn"], v_new[h], _NN) for h in hs]
        kv = [_dot(d[h]["kd"], v_new[h], _TN) for h in hs]
        for h in hs:
            sp_ref[0, h] = s0[h]
            T_ref[0, h] = T[h]
            o_ref[:, sl[h]] = qs0[h] + av[h]
            st[h] = s0[h] * d[h]["decay"] + kv[h]

    res = _pallas(
        body, (qk, v, gb, gT), name=name, grid=(N,),
        in_specs=[pl.BlockSpec((C, 2 * W), lambda n: (n, 0)), pl.BlockSpec((C, W), lambda n: (n, 0)),
                  pl.BlockSpec((C, LANE), lambda n: (n, 0)), pl.BlockSpec((1, H, C), lambda n: (n, 0, 0))],
        out_specs=[pl.BlockSpec((C, W), lambda n: (n, 0)), pl.BlockSpec((1, H, dh, dh), lambda n: (n, 0, 0, 0)),
                   pl.BlockSpec((1, H, C, C), lambda n: (n, 0, 0, 0))],
        out_shape=[jax.ShapeDtypeStruct((S, W), F32), jax.ShapeDtypeStruct((N, H, dh, dh), F32),
                   jax.ShapeDtypeStruct((N, H, C, C), F32)],
        scratch_shapes=[pltpu.VMEM((H, dh, dh), F32)], sem=("arbitrary",), comm=comm)
    return (res[:3], res[3:]) if comm else res[:3]


def _delta_bwd(name, qk, v, gb, gT, sp, Tm, do, *, S, H, dh, comm=()):
    C = min(DN_CHUNK, S)
    N, W = S // C, H * dh
    scale = dh ** -0.5

    def body(qk_ref, v_ref, gb_ref, gT_ref, sp_ref, T_ref, do_ref, dqk_ref, dv_ref, dgb_ref, ds):
        n = pl.program_id(0)

        @pl.when(n == 0)
        def _():
            ds[...] = jnp.zeros_like(ds)

        r = lax.broadcasted_iota(jnp.int32, (C, C), 0)
        c = lax.broadcasted_iota(jnp.int32, (C, C), 1)
        causal, strict = r >= c, r > c
        eye = (r == c).astype(F32)
        Lt = causal.astype(F32)
        ones = jnp.ones((C, LANE), F32)
        lane = lax.broadcasted_iota(jnp.int32, (C, LANE), 1)
        rowi = lax.broadcasted_iota(jnp.int32, (C, 1), 0)
        gbv = gb_ref[...]
        gcum = _dot(Lt, gbv, _NN, hi=True)
        gcumT = _dot(gT_ref[0], Lt, _NT, hi=True)
        dgc_all = jnp.zeros((C, LANE), F32)
        dbeta_all = jnp.zeros((C, LANE), F32)
        hs = range(H)
        sl = [slice(h * dh, (h + 1) * dh) for h in hs]
        ksl = [slice(W + h * dh, W + (h + 1) * dh) for h in hs]
        k = [qk_ref[:, ksl[h]] for h in hs]
        vv = [v_ref[:, sl[h]] for h in hs]
        beta = [_col(gbv, h) for h in hs]
        d = [_delta_chunk(qk_ref[:, sl[h]], k[h], vv[h], beta[h], _col(gcum, H + h), _row(gcumT, h), causal, strict,
                          eye, scale, C) for h in hs]
        T = [T_ref[0, h] for h in hs]
        s0 = [sp_ref[0, h] for h in hs]
        dO = [do_ref[:, sl[h]] for h in hs]
        dS = [ds[h] for h in hs]
        u = [_dot(T[h], d[h]["vb"], _NN) for h in hs]
        w = [_dot(T[h], d[h]["kbg"], _NN) for h in hs]
        ws = [_dot(w[h], s0[h], _NN) for h in hs]
        v_new = [u[h] - ws[h] for h in hs]
        dv_new = [_dot(d[h]["attn"], dO[h], _TN) + _dot(d[h]["kd"], dS[h], _NN) for h in hs]
        dattn = [jnp.where(causal, _dot(dO[h], v_new[h], _NT), 0.0) for h in hs]
        dqg = [_dot(dO[h], s0[h], _NT) for h in hs]
        dkd = [_dot(v_new[h], dS[h], _NT) for h in hs]
        ddecay = [jnp.sum(jnp.sum(s0[h] * dS[h], axis=1, keepdims=True), axis=0, keepdims=True) for h in hs]
        ds_new = [_dot(d[h]["qg"], dO[h], _TN) + d[h]["decay"] * dS[h] - _dot(w[h], dv_new[h], _TN) for h in hs]
        dw = [-_dot(dv_new[h], s0[h], _NT) for h in hs]
        for h in hs:
            ds[h] = ds_new[h]
        dT = [_dot(dv_new[h], d[h]["vb"], _NT) + _dot(dw[h], d[h]["kbg"], _NT) for h in hs]
        dvb = [_dot(T[h], dv_new[h], _TN) for h in hs]
        dkbg = [_dot(T[h], dw[h], _TN) for h in hs]
        Ts = [_split(T[h]) for h in hs]
        x1 = [_dot3(Ts[h], _split(dT[h]), _TN) for h in hs]
        dA = [jnp.where(strict, -_dot3(_split(x1[h]), Ts[h], _NT), 0.0) for h in hs]
        dkk = [dA[h] * d[h]["gam"] for h in hs]
        dqk_m = [dattn[h] * d[h]["gam"] for h in hs]
        m = [_split(dA[h] * d[h]["A"] + dattn[h] * d[h]["attn"]) for h in hs]
        msum = [jnp.sum(dA[h] * d[h]["A"] + dattn[h] * d[h]["attn"], axis=1, keepdims=True) for h in hs]
        mcol = [jnp.max(_dot(m[h][0], ones, _TN) + _dot(m[h][1], ones, _TN), axis=1, keepdims=True) for h in hs]
        dkb = [_dot(dkk[h], k[h], _NN) + dkbg[h] * d[h]["eg"] for h in hs]
        dk = [_dot(dkk[h], d[h]["kb"], _TN) + _dot(dqk_m[h], d[h]["qs"], _TN) + dkd[h] * d[h]["ek"] + dkb[h] * beta[h]
              for h in hs]
        dqs = [_dot(dqk_m[h], k[h], _NN) + dqg[h] * d[h]["eg"] for h in hs]
        for h in hs:
            r_kd = jnp.sum(dkd[h] * d[h]["kd"], axis=1, keepdims=True)
            dgc = (msum[h] - mcol[h] + jnp.sum(dqg[h] * d[h]["qg"], axis=1, keepdims=True) - r_kd
                   + jnp.sum(dkbg[h] * d[h]["kbg"], axis=1, keepdims=True))
            dg_last = jnp.sum(r_kd, axis=0, keepdims=True) + ddecay[h] * d[h]["decay"]
            dgc = dgc + jnp.where(rowi == C - 1, dg_last, 0.0)
            dbeta = jnp.sum(dkb[h] * k[h], axis=1, keepdims=True) + jnp.sum(dvb[h] * vv[h], axis=1, keepdims=True)
            dqk_ref[:, sl[h]] = dqs[h] * scale
            dqk_ref[:, ksl[h]] = dk[h]
            dv_ref[:, sl[h]] = dvb[h] * beta[h]
            dgc_all = dgc_all + jnp.where(lane == H + h, dgc, 0.0)
            dbeta_all = dbeta_all + jnp.where(lane == h, dbeta, 0.0)
        dgb_ref[...] = _dot(Lt, dgc_all, _TN, hi=True) + dbeta_all

    rev = lambda n: N - 1 - n
    res = _pallas(
        body, (qk, v, gb, gT, sp, Tm, do), name=name, grid=(N,),
        in_specs=[pl.BlockSpec((C, 2 * W), lambda n: (rev(n), 0)), pl.BlockSpec((C, W), lambda n: (rev(n), 0)),
                  pl.BlockSpec((C, LANE), lambda n: (rev(n), 0)), pl.BlockSpec((1, H, C), lambda n: (rev(n), 0, 0)),
                  pl.BlockSpec((1, H, dh, dh), lambda n: (rev(n), 0, 0, 0)),
                  pl.BlockSpec((1, H, C, C), lambda n: (rev(n), 0, 0, 0)),
                  pl.BlockSpec((C, W), lambda n: (rev(n), 0))],
        out_specs=[pl.BlockSpec((C, 2 * W), lambda n: (rev(n), 0)), pl.BlockSpec((C, W), lambda n: (rev(n), 0)),
                   pl.BlockSpec((C, LANE), lambda n: (rev(n), 0))],
        out_shape=[jax.ShapeDtypeStruct((S, 2 * W), F32), jax.ShapeDtypeStruct((S, W), F32),
                   jax.ShapeDtypeStruct((S, LANE), F32)],
        scratch_shapes=[pltpu.VMEM((H, dh, dh), F32)], sem=("arbitrary",), comm=comm)
    return (res[:3], res[3:]) if comm else res[:3]


_ANY = pl.BlockSpec(memory_space=pl.ANY)


ICI_CHUNKS = 4
D2D_CHUNKS = 4
EXCHANGE_PARTS = 3


def _place():
    return lax.axis_index("x"), lax.axis_index("y"), lax.axis_index("c")


def _row_chunks(rows, n):
    n = max(1, min(n, rows // 8))
    while n > 1 and (rows % n or (rows // n) % 8):
        n -= 1
    return [(k * (rows // n), rows // n) for k in range(n)]


class _Plug:
    def __init__(self, ins, outs, sems, start, finish, after):
        self.ins, self.outs, self.sems, self.start, self.finish, self.after = ins, outs, sems, start, finish, after


def _run_plug(name, plug):
    def body(*refs):
        ni, no = len(plug.ins), len(plug.outs)
        plug.start(refs[:ni], refs[ni:ni + no], refs[ni + no:])
        plug.finish(refs[:ni], refs[ni:ni + no], refs[ni + no:])

    return pl.pallas_call(body, name=name, in_specs=[_ANY] * len(plug.ins), out_specs=[_ANY] * len(plug.outs),
                          out_shape=list(plug.outs), scratch_shapes=list(plug.sems))(*plug.ins)


def _gather_plug(flat):
    R, L = flat.shape
    Rh = R // 2
    ici = _row_chunks(Rh, ICI_CHUNKS)
    sub = _row_chunks(ici[0][1], D2D_CHUNKS)
    ni, ns = len(ici), len(sub)

    def parts(ins, outs, sems):
        (x_ref,), (out_ref,), (send_sems, recv_sems) = ins, outs, sems
        x, y, c = _place()
        chips = [(1 - x, y), (x, 1 - y), (1 - x, 1 - y)]

        def rows(px, py, pc, r0, n):
            return out_ref.at[2 * px + py, pl.ds(pc * Rh + r0, n), :]

        def copy(k, src, dst, to):
            return pltpu.make_async_remote_copy(src_ref=src, dst_ref=dst, send_sem=send_sems.at[k],
                                                recv_sem=recv_sems.at[k], device_id=to, device_id_type=MESH)

        first = [copy(k * ni + q, x_ref.at[pl.ds(c * Rh + r0, n), :], rows(x, y, c, r0, n), (*chip, c))
                 for k, chip in enumerate(chips) for q, (r0, n) in enumerate(ici)]
        return x_ref, (x, y, c), chips, rows, copy, first

    def start(ins, outs, sems):
        for cp in parts(ins, outs, sems)[-1]:
            cp.start()

    def finish(ins, outs, sems):
        x_ref, (x, y, c), chips, rows, copy, first = parts(ins, outs, sems)
        sibling = (x, y, 1 - c)
        passed = []
        for k, chip in enumerate(chips):
            for q, (r0, n) in enumerate(ici):
                copy(k * ni + q, x_ref.at[pl.ds(r0, n), :], rows(*chip, c, r0, n), (*chip, c)).wait_recv()
                for t, (s0, m) in enumerate(sub):
                    cp = copy(3 * ni + (k * ni + q) * ns + t, rows(*chip, c, r0 + s0, m), rows(*chip, c, r0 + s0, m), sibling)
                    cp.start()
                    passed.append(cp)
        for k, chip in enumerate(chips):
            for q, (r0, n) in enumerate(ici):
                for t, (s0, m) in enumerate(sub):
                    copy(3 * ni + (k * ni + q) * ns + t, x_ref.at[pl.ds(r0, m), :], rows(*chip, 1 - c, r0 + s0, m),
                         sibling).wait_recv()
        for cp in first + passed:
            cp.wait_send()

    def after(res):
        return lax.dynamic_update_slice(res[0], flat[None], (2 * lax.axis_index("x") + lax.axis_index("y"), 0, 0))

    nsem = 3 * ni * (1 + ns)
    return _Plug([flat], [jax.ShapeDtypeStruct((N_CHIPS, R, L), flat.dtype)],
                 [pltpu.SemaphoreType.DMA((nsem,)), pltpu.SemaphoreType.DMA((nsem,))], start, finish, after)


def _sibling_split(name, g):
    _, R, L = g.shape
    Rh = R // 2

    chunks = [(j, r0, n) for j in range(N_CHIPS) for (r0, n) in _row_chunks(Rh, D2D_CHUNKS)]

    def body(g_ref, got_ref, send_sems, recv_sems):
        x, y, c = _place()
        cps = [pltpu.make_async_remote_copy(src_ref=g_ref.at[j, pl.ds((1 - c) * Rh + r0, n), :],
                                            dst_ref=got_ref.at[j, pl.ds(r0, n), :], send_sem=send_sems.at[k],
                                            recv_sem=recv_sems.at[k], device_id=(x, y, 1 - c), device_id_type=MESH)
               for k, (j, r0, n) in enumerate(chunks)]
        for cp in cps:
            cp.start()
        for cp in cps:
            cp.wait()

    sems = pltpu.SemaphoreType.DMA((len(chunks),))
    got = pl.pallas_call(
        body, name=name, in_specs=[_ANY], out_specs=_ANY, out_shape=jax.ShapeDtypeStruct((N_CHIPS, Rh, L), g.dtype),
        scratch_shapes=[sems, sems],
    )(g)
    own = lax.dynamic_slice(g, (0, lax.axis_index("c") * Rh, 0), (N_CHIPS, Rh, L))
    return own, got


def _exchange_plug(p, row0, nrows):
    ici = _row_chunks(nrows, ICI_CHUNKS)
    ni = len(ici)

    def sends(ins, outs, sems):
        (p_ref,), (q_ref,), (send_sems, recv_sems) = ins, outs, sems
        x, y, c = _place()
        me = 2 * x + y
        chips = [(1 - x, y), (x, 1 - y), (1 - x, 1 - y)]
        return [pltpu.make_async_remote_copy(src_ref=p_ref.at[2 * cx + cy, pl.ds(row0 + r0, n), :],
                                             dst_ref=q_ref.at[me, pl.ds(r0, n), :],
                                             send_sem=send_sems.at[k * ni + q], recv_sem=recv_sems.at[k * ni + q],
                                             device_id=(cx, cy, c), device_id_type=MESH)
                for k, (cx, cy) in enumerate(chips) for q, (r0, n) in enumerate(ici)]

    def start(ins, outs, sems):
        for cp in sends(ins, outs, sems):
            cp.start()

    def finish(ins, outs, sems):
        (p_ref,), (q_ref,), (send_sems, recv_sems) = ins, outs, sems
        x, y, c = _place()
        me = 2 * x + y
        chips = [(1 - x, y), (x, 1 - y), (1 - x, 1 - y)]
        for k, (cx, cy) in enumerate(chips):
            for q, (r0, n) in enumerate(ici):
                pltpu.make_async_remote_copy(src_ref=p_ref.at[me, pl.ds(r0, n), :],
                                             dst_ref=q_ref.at[2 * cx + cy, pl.ds(r0, n), :],
                                             send_sem=send_sems.at[k * ni + q], recv_sem=recv_sems.at[k * ni + q],
                                             device_id=(cx, cy, c), device_id_type=MESH).wait_recv()
        for cp in sends(ins, outs, sems):
            cp.wait_send()

    def after(res):
        me = 2 * lax.axis_index("x") + lax.axis_index("y")
        mine = lax.dynamic_slice(p, (me, row0, 0), (1, nrows, p.shape[2]))
        return lax.dynamic_update_slice(res[0], mine, (me, 0, 0))

    return _Plug([p], [jax.ShapeDtypeStruct((N_CHIPS, nrows, p.shape[2]), p.dtype)],
                 [pltpu.SemaphoreType.DMA((3 * ni,)), pltpu.SemaphoreType.DMA((3 * ni,))], start, finish, after)


def _sibling_swap(name, half):
    Rh, L = half.shape
    chunks = _row_chunks(Rh, 2 * D2D_CHUNKS)

    def body(h_ref, out_ref, send_sems, recv_sems):
        x, y, c = _place()
        cps = [pltpu.make_async_remote_copy(src_ref=h_ref.at[pl.ds(r0, n), :], dst_ref=out_ref.at[pl.ds(r0, n), :],
                                            send_sem=send_sems.at[k], recv_sem=recv_sems.at[k], device_id=(x, y, 1 - c),
                                            device_id_type=MESH)
               for k, (r0, n) in enumerate(chunks)]
        for cp in cps:
            cp.start()
        for cp in cps:
            cp.wait()

    sems = pltpu.SemaphoreType.DMA((len(chunks),))
    return pl.pallas_call(
        body, name=name, in_specs=[_ANY], out_specs=_ANY, out_shape=jax.ShapeDtypeStruct((Rh, L), half.dtype),
        scratch_shapes=[sems, sems],
    )(half)


def _add_pairs(name, a, b, out_dtype):
    n, Rh, L = a.shape
    tr = _pick(Rh, 512, 8)

    def body(a_ref, b_ref, o_ref):
        o_ref[...] = (a_ref[...].astype(F32) + b_ref[...].astype(F32)).astype(o_ref.dtype)

    spec = pl.BlockSpec((1, tr, L), lambda j, i: (j, i, 0))
    return pl.pallas_call(body, name=name, grid=(n, Rh // tr), in_specs=[spec, spec], out_specs=spec,
                          out_shape=jax.ShapeDtypeStruct(a.shape, out_dtype),
                          compiler_params=_cparams(("parallel", "parallel")))(a, b)


def _sum_chips(name, q):
    n, Rh, L = q.shape
    tr = _pick(Rh, 512, 8)

    def body(q_ref, o_ref):
        acc = q_ref[0].astype(F32)
        for s in range(1, n):
            acc = acc + q_ref[s].astype(F32)
        o_ref[...] = acc

    return pl.pallas_call(body, name=name, grid=(Rh // tr,),
                          in_specs=[pl.BlockSpec((n, tr, L), lambda i: (0, i, 0))],
                          out_specs=pl.BlockSpec((tr, L), lambda i: (i, 0)),
                          out_shape=jax.ShapeDtypeStruct((Rh, L), F32),
                          compiler_params=_cparams(("parallel",)))(q)


def _adamw(name, w, m, v, g):
    shape, size = w.shape, w.size
    rows = -(-size // FLAT_L)
    rows_p = -(-rows // ADAM_ROWS) * ADAM_ROWS
    pad = rows_p * FLAT_L - size

    def flat2d(a):
        a = a.reshape(-1)
        if pad:
            a = jnp.pad(a, (0, pad), constant_values=1.0)
        return a.reshape(rows_p, FLAT_L)

    c1 = 1.0 / (1.0 - ADAM_B1 ** ADAM_STEP)
    c2 = 1.0 / (1.0 - ADAM_B2 ** ADAM_STEP)

    def body(w_ref, m_ref, v_ref, g_ref, go_ref, d_ref, mo_ref, vo_ref):
        g = g_ref[...]
        wv = w_ref[...]
        mn = ADAM_B1 * m_ref[...] + (1.0 - ADAM_B1) * g
        vn = ADAM_B2 * v_ref[...] + (1.0 - ADAM_B2) * (g * g)
        go_ref[...] = g
        mo_ref[...] = mn
        vo_ref[...] = vn
        d_ref[...] = -ADAM_LR * ((mn * c1) / (jnp.sqrt(vn * c2) + ADAM_EPS) + ADAM_WD * wv)

    spec = pl.BlockSpec((ADAM_ROWS, FLAT_L), lambda i: (i, 0))
    shp = jax.ShapeDtypeStruct((rows_p, FLAT_L), F32)
    outs = pl.pallas_call(
        body, name=name, grid=(rows_p // ADAM_ROWS,),
        in_specs=[spec] * 4, out_specs=[spec] * 4, out_shape=[shp] * 4, compiler_params=_cparams(("parallel",)),
    )(flat2d(w), flat2d(m), flat2d(v), flat2d(g))

    def back(a):
        a = a.reshape(-1)
        if pad:
            a = a[:size]
        return a.reshape(shape)

    return tuple(back(a) for a in outs)


SEG_ROWS = 16


def _seg_rows(size):
    rows = -(-size // FLAT_L)
    return -(-rows // SEG_ROWS) * SEG_ROWS


def _pack(pieces, dtype, row_mult):
    segs, offs, r = [], [], 0
    for a in pieces:
        rows = _seg_rows(a.size)
        flat = a.reshape(-1).astype(dtype)
        flat = jnp.pad(flat, (0, rows * FLAT_L - a.size))
        segs.append(flat.reshape(rows, FLAT_L))
        offs.append(r)
        r += rows
    tail = -r % row_mult
    if tail:
        segs.append(jnp.zeros((tail, FLAT_L), dtype))
    return jnp.concatenate(segs, axis=0), offs


def _segment(flat, off, shape):
    size = math.prod(shape)
    return flat[off:off + _seg_rows(size)].reshape(-1)[:size].reshape(shape)


def _gather_unit(shards, axes, dtype):
    flat, offs = _pack(shards, dtype, 64)
    plug = _gather_plug(flat)

    def unpack(full):
        outs = []
        for a, ax, off in zip(shards, axes, offs):
            rows = _seg_rows(a.size)
            seg = full[:, off:off + rows].reshape(N_CHIPS, -1)[:, :a.size].reshape((N_CHIPS,) + a.shape)
            outs.append(jnp.concatenate([seg[j] for j in range(N_CHIPS)], axis=ax))
        return outs

    return plug, unpack


def _reduce_unit(name, grads, axes):
    pieces = [[] for _ in range(N_CHIPS)]
    for g, ax in zip(grads, axes):
        parts = jnp.split(g, N_CHIPS, axis=ax) if ax is not None else [g] * N_CHIPS
        for jc in range(N_CHIPS):
            pieces[jc].append(parts[jc])
    packed = [_pack(pieces[jc], BF16, 512) for jc in range(N_CHIPS)]
    gsend = jnp.stack([pk[0] for pk in packed])
    own, got = _sibling_split(name + "_split", gsend)
    pair = _add_pairs(name + "_add", own, got, BF16)
    return pair, packed[0][1], [a.shape for a in pieces[0]]


def _f_rms(row0, j, x, g):
    return _rms(x, g)


def _f_mid(row0, j, x, y, g_a, g_b):
    xn = x + _rms(y, g_a)
    return xn, _rms(xn, g_b)


def _f_resid(row0, j, x, y, g):
    return x + _rms(y, g)


def _relu2(u):
    r = jnp.maximum(u, 0.0)
    return r * r


def _relu2_bwd(d_act, act):
    return d_act * (2.0 * jnp.sqrt(act.astype(F32)))


def _f_l2silu(row0, j, c):
    a = _silu(c)
    return a * lax.rsqrt(jnp.sum(a * a, axis=-1, keepdims=True) + EPS)


def _f_silu(row0, j, c):
    return _silu(c)


def _f_scale(row0, j, y, s):
    return y * s


def _f_glu(row0, j, a, gate):
    return a * _sigmoid(gate)


def _f_lnsilu(row0, j, u, g, b):
    mu = jnp.mean(u, axis=-1, keepdims=True)
    uc = u - mu
    return _silu(uc * lax.rsqrt(jnp.mean(uc * uc, axis=-1, keepdims=True) + EPS) * g + b)


def _f_outgate(row0, j, o, z, g):
    return _rms(o, g) * _silu(z)


def _make_gates(H):
    def f(row0, j, ba, alog, dt):
        lane = lax.broadcasted_iota(jnp.int32, ba.shape, 1)
        beta = _sigmoid(ba)
        g = -jnp.exp(alog) * _softplus(ba + dt)
        return jnp.where(lane < H, beta, jnp.where(lane < 2 * H, g, 0.0))
    return f


def _f_loss(row0, j, y, t):
    e = y - t
    loss = 0.5 * jnp.sum(jnp.mean(e * e, axis=-1, keepdims=True), axis=0, keepdims=True)
    return e * (1.0 / y.shape[-1]), jnp.broadcast_to(loss, (1, LANE))


def _lane_row(vec, start):
    return jnp.pad(vec.astype(F32)[None, :], ((0, 0), (start, LANE - start - vec.shape[0])))


def kernel(x, norm_mix_pre, norm_mix_post, norm_mlp_pre, norm_mlp_post, even_w_in, even_conv, even_a_log, even_dt_bias, even_dn_norm, even_pool_w, even_pool_scale, even_w_out, odd_w_in, odd_dw, odd_dw_b, odd_ln_g, odd_ln_b, odd_w_out, mlp_w_up, mlp_w_down, loss_target, m_norm_mix_pre, m_norm_mix_post, m_norm_mlp_pre, m_norm_mlp_post, m_even_w_in, m_even_conv, m_even_a_log, m_even_dt_bias, m_even_dn_norm, m_even_pool_w, m_even_pool_scale, m_even_w_out, m_odd_w_in, m_odd_dw, m_odd_dw_b, m_odd_ln_g, m_odd_ln_b, m_odd_w_out, m_mlp_w_up, m_mlp_w_down, v_norm_mix_pre, v_norm_mix_post, v_norm_mlp_pre, v_norm_mlp_post, v_even_w_in, v_even_conv, v_even_a_log, v_even_dt_bias, v_even_dn_norm, v_even_pool_w, v_even_pool_scale, v_even_w_out, v_odd_w_in, v_odd_dw, v_odd_dw_b, v_odd_ln_g, v_odd_ln_b, v_odd_w_out, v_mlp_w_up, v_mlp_w_down):
    names = ["norm_mix_pre", "norm_mix_post", "norm_mlp_pre", "norm_mlp_post", "even_w_in", "even_conv", "even_a_log",
             "even_dt_bias", "even_dn_norm", "even_pool_w", "even_pool_scale", "even_w_out", "odd_w_in", "odd_dw",
             "odd_dw_b", "odd_ln_g", "odd_ln_b", "odd_w_out", "mlp_w_up", "mlp_w_down"]
    W = dict(zip(names, (norm_mix_pre, norm_mix_post, norm_mlp_pre, norm_mlp_post, even_w_in, even_conv, even_a_log,
                         even_dt_bias, even_dn_norm, even_pool_w, even_pool_scale, even_w_out, odd_w_in, odd_dw,
                         odd_dw_b, odd_ln_g, odd_ln_b, odd_w_out, mlp_w_up, mlp_w_down)))
    Mo = dict(zip(names, (m_norm_mix_pre, m_norm_mix_post, m_norm_mlp_pre, m_norm_mlp_post, m_even_w_in, m_even_conv,
                          m_even_a_log, m_even_dt_bias, m_even_dn_norm, m_even_pool_w, m_even_pool_scale, m_even_w_out,
                          m_odd_w_in, m_odd_dw, m_odd_dw_b, m_odd_ln_g, m_odd_ln_b, m_odd_w_out, m_mlp_w_up,
                          m_mlp_w_down)))
    Vo = dict(zip(names, (v_norm_mix_pre, v_norm_mix_post, v_norm_mlp_pre, v_norm_mlp_post, v_even_w_in, v_even_conv,
                          v_even_a_log, v_even_dt_bias, v_even_dn_norm, v_even_pool_w, v_even_pool_scale, v_even_w_out,
                          v_odd_w_in, v_odd_dw, v_odd_dw_b, v_odd_ln_g, v_odd_ln_b, v_odd_w_out, v_mlp_w_up,
                          v_mlp_w_down)))
    shard_axis = {"even_w_in": 2, "even_conv": 2, "even_pool_w": 2, "even_w_out": 1, "odd_w_in": 2, "odd_dw": 2,
                  "odd_dw_b": 1, "odd_ln_g": 1, "odd_ln_b": 1, "odd_w_out": 1, "mlp_w_up": 2, "mlp_w_down": 1}

    S, D = x.shape[1], x.shape[2]
    depth = norm_mix_pre.shape[0]
    H = even_a_log.shape[1]
    dh = even_dn_norm.shape[1]
    DNW = H * dh
    PW = even_pool_scale.shape[1]
    G = len(POOL_WINDOWS)
    PG = PW // G
    KC = even_conv.shape[1]
    BAW = 2 * LANE
    P_COLS = 4 * DNW + PW + BAW
    x2 = x.reshape(S, D)
    tgt = loss_target.reshape(S, D)

    small = ["even_conv", "odd_dw", "odd_dw_b", "odd_ln_g", "odd_ln_b"]
    plug_s, unpack_s = _gather_unit([W[n] for n in small], [shard_axis[n] for n in small], F32)
    full = dict(zip(small, unpack_s(plug_s.after(_run_plug("gather_small", plug_s)))))
    CW = odd_w_out.shape[1] * N_CHIPS
    wfull = {}

    def mixer_weights(i):
        return [(n, i // 2) for n in (("even_w_in", "even_pool_w", "even_w_out") if i % 2 == 0 else ("odd_w_in", "odd_w_out"))]

    riders = []

    def hosted(fn, *a, **k):
        if not riders:
            return fn(*a, **k)
        plug, done = riders.pop(0)
        outs, landed = fn(*a, comm=[plug], **k)
        done(plug.after(landed))
        return outs

    def queue_gather(keys):
        plug, unpack = _gather_unit([W[n][l] for n, l in keys], [shard_axis[n] - 1 for n, _ in keys], BF16)
        riders.append((plug, lambda full_: wfull.update(zip(keys, unpack(full_)))))

    def weight(key):
        while key not in wfull:
            plug, done = riders.pop(0)
            done(plug.after(_run_plug(f"alone_{len(wfull)}_{len(riders)}", plug)))
        return wfull[key]

    queue_gather(mixer_weights(0))
    for i in range(depth):
        queue_gather([("mlp_w_up", i)])
        queue_gather([("mlp_w_down", i)])
        if i + 1 < depth:
            queue_gather(mixer_weights(i + 1))

    def even_w_in_layout(w):
        o1 = 4 * DNW
        return jnp.concatenate([w[:, :o1], w[:, o1 + 2 * H:], w[:, o1:o1 + 2 * H],
                                jnp.zeros((w.shape[0], BAW - 2 * H), w.dtype)], axis=1)

    def even_w_in_unlayout(g):
        o1 = 4 * DNW
        return jnp.concatenate([g[:, :o1], g[:, o1 + PW:o1 + PW + 2 * H], g[:, o1:o1 + PW]], axis=1)

    def pool_blockdiag(pw):
        return jnp.concatenate([jnp.pad(pw[gi], ((0, 0), (gi * PG, PW - (gi + 1) * PG))) for gi in range(G)], axis=0)

    pool_taps = max(POOL_WINDOWS)
    tap = jnp.arange(pool_taps)[:, None]
    win_c = jnp.repeat(jnp.asarray(POOL_WINDOWS, F32), PG)[None, :]
    pool_mask = (tap >= pool_taps - win_c).astype(F32)

    grads = {}
    tr_full = 128 if D > 1024 else 256

    saved = []
    xc = x2
    for i in range(depth):
        jl = i // 2
        sv = {"x_in": xc}
        g1, g2, g3, g4 = (W[n][i:i + 1] for n in ("norm_mix_pre", "norm_mix_post", "norm_mlp_pre", "norm_mlp_post"))
        (h,) = _rowwise(f"l{i}_rms_in", _f_rms, [(xc, 0, D)], [(g1, None, D)], [(D, BF16)], S=S, tr=tr_full)
        sv["h"] = h
        if i % 2 == 0:
            w_in = even_w_in_layout(weight(("even_w_in", jl)))
            p = hosted(_matmul, f"l{i}_w_in", h, w_in, "nn", tn=768)
            conv_w = full["even_conv"][jl]
            c = _dwconv_fwd(f"l{i}_conv", p, 0, conv_w, S=S, C=3 * DNW)
            (qk,) = _rowwise(f"l{i}_qk", _f_l2silu, [(c, 0, dh)], [], [(dh, F32)], S=S, ncb=2 * H, tr=1024)
            (vv,) = _rowwise(f"l{i}_v", _f_silu, [(c, 2 * DNW // dh, dh)], [], [(dh, F32)], S=S, ncb=H, tr=1024)
            alog = _lane_row(W["even_a_log"][jl], H)
            dtb = _lane_row(W["even_dt_bias"][jl], H)
            ba_off = (4 * DNW + PW) // LANE
            (gb,) = _rowwise(f"l{i}_gates", _make_gates(H), [(p, ba_off, LANE)], [(alog, None, LANE), (dtb, None, LANE)],
                             [(LANE, F32)], S=S, tr=1024)
            Cn = min(DN_CHUNK, S)
            gT = gb[:, H:2 * H].reshape(S // Cn, Cn, H).transpose(0, 2, 1)
            o, sp, Tm = hosted(_delta_fwd, f"l{i}_delta", qk, vv, gb, gT, S=S, H=H, dh=dh)
            dn = W["even_dn_norm"][jl][None, :]
            (on,) = _rowwise(f"l{i}_outgate", _f_outgate, [(o, 0, dh), (p, 3 * DNW // dh, dh)], [(dn, None, dh)],
                             [(dh, BF16)], S=S, ncb=H, tr=1024)
            pcb = _pick(PW, 512)
            pooled = _dwconv_fwd(f"l{i}_pool", p, 4 * DNW // pcb, pool_mask, S=S, C=PW, win=win_c, out_dtype=BF16, cb=pcb)
            wbd = pool_blockdiag(weight(("even_pool_w", jl)))
            ypre = _matmul(f"l{i}_pool_w", pooled, wbd, "nn")
            psc = W["even_pool_scale"][jl][None, :]
            (ypool,) = _rowwise(f"l{i}_pool_scale", _f_scale, [(ypre, 0, PW)], [(psc, None, PW)], [(PW, BF16)], S=S)
            mixin = jnp.concatenate([on, ypool], axis=1)
            mix = _matmul(f"l{i}_w_out", mixin, weight(("even_w_out", jl)), "nn")
            sv.update(p=p, c=c, qk=qk, v=vv, gb=gb, gT=gT, o=o, sp=sp, Tm=Tm, pooled=pooled, ypre=ypre, mixin=mixin,
                      w_in=w_in, wbd=wbd, alog=alog, dtb=dtb, dn=dn, psc=psc, conv_w=conv_w)
        else:
            p = hosted(_matmul, f"l{i}_w_in", h, weight(("odd_w_in", jl)), "nn")
            ocb = _pick(CW, 1024)
            (u0,) = _rowwise(f"l{i}_glu", _f_glu, [(p, 0, ocb), (p, CW // ocb, ocb)], [], [(ocb, F32)], S=S,
                             ncb=CW // ocb)
            dw_w, dw_b = full["odd_dw"][jl], full["odd_dw_b"][jl][None, :]
            u1 = hosted(_dwconv_fwd, f"l{i}_dwconv", u0, 0, dw_w, S=S, C=CW, bias=dw_b)
            lg, lb = full["odd_ln_g"][jl][None, :], full["odd_ln_b"][jl][None, :]
            (u2,) = _rowwise(f"l{i}_lnsilu", _f_lnsilu, [(u1, 0, CW)], [(lg, None, CW), (lb, None, CW)], [(CW, BF16)],
                             S=S, tr=tr_full)
            mix = _matmul(f"l{i}_w_out", u2, weight(("odd_w_out", jl)), "nn")
            sv.update(p=p, u0=u0, u1=u1, mixin=u2, dw_w=dw_w, lg=lg, lb=lb)
        x_mid, h2 = _rowwise(f"l{i}_mid", _f_mid, [(xc, 0, D), (mix, 0, D)], [(g2, None, D), (g3, None, D)],
                             [(D, F32), (D, BF16)], S=S, tr=tr_full)
        act = hosted(_matmul, f"l{i}_w_up", h2, weight(("mlp_w_up", i)), "nn", epi=_relu2, out_dtypes=[BF16])
        ff = hosted(_matmul, f"l{i}_w_down", act, weight(("mlp_w_down", i)), "nn")
        (x_out,) = _rowwise(f"l{i}_out", _f_resid, [(x_mid, 0, D), (ff, 0, D)], [(g4, None, D)], [(D, F32)], S=S,
                            tr=tr_full)
        sv.update(mix=mix, h2=h2, act=act, ff=ff, g=(g1, g2, g3, g4))
        saved.append(sv)
        xc = x_out

    dy, loss_row = _rowwise("loss", _f_loss, [(xc, 0, D), (tgt, 0, D)], [], [(D, F32)], [(1, LANE, False)], S=S,
                            tr=tr_full)
    loss = lax.psum(loss_row[0, 0], ("x", "y", "c"))

    def mixer_params(i):
        ns = (("even_w_in", "even_conv", "even_a_log", "even_dt_bias", "even_dn_norm", "even_pool_w", "even_pool_scale",
               "even_w_out") if i % 2 == 0 else ("odd_w_in", "odd_dw", "odd_dw_b", "odd_ln_g", "odd_ln_b", "odd_w_out"))
        return [(n, i // 2) for n in ns] + [("norm_mix_pre", i)]

    def mlp_params(i):
        return [(n, i) for n in ("mlp_w_up", "mlp_w_down", "norm_mlp_pre", "norm_mlp_post", "norm_mix_post")]

    units = []

    def queue_reduce(name, keys):
        axes = [shard_axis[n] - 1 if n in shard_axis else None for n, _ in keys]
        pair, offs, shapes = _reduce_unit(name, [grads[k] for k in keys], axes)
        parts = []
        units.append((keys, offs, shapes, parts))
        for t, (r0, n) in enumerate(_row_chunks(pair.shape[1], EXCHANGE_PARTS)):
            plug = _exchange_plug(pair, r0, n)
            riders.append((plug, lambda q, t=t: parts.append((t, _sum_chips(f"{name}_sum{t}", q)))))

    dx = dy
    for i in reversed(range(depth)):
        jl = i // 2
        sv = saved[i]
        g1, g2, g3, g4 = sv["g"]
        d_ff, dg4 = _rowwise_bwd(f"l{i}_out_b", _f_rms, [(sv["ff"], 0, D)], [(g4, None, D)], [(dx, 0, D)], [BF16],
                                 S=S, tr=tr_full)
        grads["norm_mlp_post", i] = dg4[0]
        du = hosted(_matmul, f"l{i}_w_down_bx", d_ff, wfull["mlp_w_down", i], "nt", epi=_relu2_bwd, extras=[sv["act"]],
                    out_dtypes=[BF16])
        grads["mlp_w_down", i] = hosted(_matmul, f"l{i}_w_down_bw", sv["act"], d_ff, "tn")
        dh2 = hosted(_matmul, f"l{i}_w_up_bx", du, wfull["mlp_w_up", i], "nt")
        grads["mlp_w_up", i] = hosted(_matmul, f"l{i}_w_up_bw", sv["h2"], du, "tn")
        dx, d_mix, dg2, dg3 = _rowwise_bwd(
            f"l{i}_mid_b", _f_mid, [(sv["x_in"], 0, D), (sv["mix"], 0, D)], [(g2, None, D), (g3, None, D)],
            [(dx, 0, D), (dh2, 0, D)], [F32, BF16], S=S, tr=tr_full)
        grads["norm_mix_post", i], grads["norm_mlp_pre", i] = dg2[0], dg3[0]
        queue_reduce(f"red{i}", (mixer_params(i + 1) if i + 1 < depth else []) + mlp_params(i))
        if i % 2 == 0:
            d_mixin = hosted(_matmul, f"l{i}_w_out_bx", d_mix, wfull["even_w_out", jl], "nt")
            grads["even_w_out", jl] = hosted(_matmul, f"l{i}_w_out_bw", sv["mixin"], d_mix, "tn")
            p = sv["p"]
            pcb = _pick(PW, 512)
            d_ypre, dpsc = _rowwise_bwd(f"l{i}_pool_scale_b", _f_scale, [(sv["ypre"], 0, PW)], [(sv["psc"], None, PW)],
                                        [(d_mixin, DNW // PW, PW)], [BF16], S=S)
            grads["even_pool_scale", jl] = dpsc[0]
            d_pooled = _matmul(f"l{i}_pool_w_bx", d_ypre, sv["wbd"], "nt")
            dwbd = _matmul(f"l{i}_pool_w_bw", sv["pooled"], d_ypre, "tn")
            grads["even_pool_w", jl] = jnp.stack([dwbd[gi * PG:(gi + 1) * PG, gi * PG:(gi + 1) * PG] for gi in range(G)])
            d_xp = _dwconv_bwd(f"l{i}_pool_b", None, 0, d_pooled, pool_mask, S=S, C=PW, win=win_c, want_dw=False,
                               cb=pcb)[0]
            d_o, d_z, ddn = _rowwise_bwd(f"l{i}_outgate_b", _f_outgate, [(sv["o"], 0, dh), (p, 3 * DNW // dh, dh)],
                                         [(sv["dn"], None, dh)], [(d_mixin, 0, dh)], [F32, F32], S=S, ncb=H, tr=1024)
            grads["even_dn_norm", jl] = ddn[0]
            dqk, dv, dgb = hosted(_delta_bwd, f"l{i}_delta_b", sv["qk"], sv["v"], sv["gb"], sv["gT"], sv["sp"],
                                  sv["Tm"], d_o, S=S, H=H, dh=dh)
            ba_off = (4 * DNW + PW) // LANE
            d_ba, dalog, ddtb = _rowwise_bwd(f"l{i}_gates_b", _make_gates(H), [(p, ba_off, LANE)],
                                             [(sv["alog"], None, LANE), (sv["dtb"], None, LANE)], [(dgb, 0, LANE)],
                                             [F32], S=S, tr=1024)
            grads["even_a_log", jl], grads["even_dt_bias", jl] = dalog[0, H:2 * H], ddtb[0, H:2 * H]
            (dc_qk,) = _rowwise_bwd(f"l{i}_qk_b", _f_l2silu, [(sv["c"], 0, dh)], [], [(dqk, 0, dh)], [F32], S=S,
                                    ncb=2 * H, tr=1024)
            (dc_v,) = _rowwise_bwd(f"l{i}_v_b", _f_silu, [(sv["c"], 2 * DNW // dh, dh)], [], [(dv, 0, dh)], [F32], S=S,
                                   ncb=H, tr=1024)
            dc = jnp.concatenate([dc_qk, dc_v], axis=1)
            d_qkv, dconv, _ = _dwconv_bwd(f"l{i}_conv_b", p, 0, dc, sv["conv_w"], S=S, C=3 * DNW)
            grads["even_conv", jl] = dconv
            dp = jnp.concatenate([d_qkv.astype(BF16), d_z.astype(BF16), d_xp.astype(BF16), d_ba.astype(BF16),
                                  jnp.zeros((S, BAW - LANE), BF16)], axis=1)
            dh_ = hosted(_matmul, f"l{i}_w_in_bx", dp, sv["w_in"], "nt", tk=768)
            grads["even_w_in", jl] = even_w_in_unlayout(hosted(_matmul, f"l{i}_w_in_bw", sv["h"], dp, "tn", tn=768))
        else:
            d_u2 = hosted(_matmul, f"l{i}_w_out_bx", d_mix, wfull["odd_w_out", jl], "nt")
            grads["odd_w_out", jl] = hosted(_matmul, f"l{i}_w_out_bw", sv["mixin"], d_mix, "tn")
            d_u1, dlg, dlb = _rowwise_bwd(f"l{i}_lnsilu_b", _f_lnsilu, [(sv["u1"], 0, CW)],
                                          [(sv["lg"], None, CW), (sv["lb"], None, CW)], [(d_u2, 0, CW)], [F32], S=S,
                                          tr=tr_full)
            grads["odd_ln_g", jl], grads["odd_ln_b", jl] = dlg[0], dlb[0]
            d_u0, ddw, ddb = hosted(_dwconv_bwd, f"l{i}_dwconv_b", sv["u0"], 0, d_u1, sv["dw_w"], S=S, C=CW)
            grads["odd_dw", jl], grads["odd_dw_b", jl] = ddw, ddb[0]
            p = sv["p"]
            ocb = _pick(CW, 1024)
            da, dgate = _rowwise_bwd(f"l{i}_glu_b", _f_glu, [(p, 0, ocb), (p, CW // ocb, ocb)], [], [(d_u0, 0, ocb)],
                                     [BF16, BF16], S=S, ncb=CW // ocb)
            dp = jnp.concatenate([da, dgate], axis=1)
            dh_ = hosted(_matmul, f"l{i}_w_in_bx", dp, wfull["odd_w_in", jl], "nt")
            grads["odd_w_in", jl] = hosted(_matmul, f"l{i}_w_in_bw", sv["h"], dp, "tn")
        dx, dg1 = _rowwise_bwd(f"l{i}_rms_in_b", _f_rms, [(sv["x_in"], 0, D)], [(g1, None, D)], [(dh_, 0, D)], [F32],
                               adds=[(dx, 0, D)], S=S, tr=tr_full)
        grads["norm_mix_pre", i] = dg1[0]
    grad_x = dx.reshape(x.shape)
    queue_reduce("red_last", mixer_params(0))
    while riders:
        plug, done = riders.pop(0)
        done(plug.after(_run_plug(f"alone_last_{len(riders)}", plug)))

    halves = [jnp.concatenate([q for _, q in sorted(parts, key=lambda tq: tq[0])], axis=0) for *_, parts in units]
    theirs = _sibling_swap("grad_sibling_swap", jnp.concatenate(halves, axis=0))
    south = lax.axis_index("c") == 0
    gshard, r = {}, 0
    for (keys, offs, shapes, _), half in zip(units, halves):
        rh = half.shape[0]
        other = theirs[r:r + rh]
        whole = jnp.concatenate([jnp.where(south, half, other), jnp.where(south, other, half)], axis=0)
        for k, off, shp in zip(keys, offs, shapes):
            gshard[k] = _segment(whole, off, shp)
        r += rh

    outs_g, outs_d, outs_m, outs_v = [], [], [], []
    for n in names:
        g_n = jnp.stack([gshard[n, l] for l in range(W[n].shape[0])])
        g_o, d_o, m_o, v_o = _adamw(f"adamw_{n}", W[n], Mo[n], Vo[n], g_n)
        outs_g.append(g_o)
        outs_d.append(d_o)
        outs_m.append(m_o)
        outs_v.append(v_o)
    return (loss, grad_x, *outs_g, *outs_d, *outs_m, *outs_v)
```

```python
import functools
import math

import jax
import jax.numpy as jnp
from jax import lax
from jax.experimental import pallas as pl
from jax.experimental.pallas import tpu as pltpu

F32 = jnp.float32
BF16 = jnp.bfloat16
EPS = 1e-6
DN_CHUNK = 64
POOL_WINDOWS = (2, 4, 8, 16)
ADAM_LR, ADAM_B1, ADAM_B2, ADAM_EPS, ADAM_WD, ADAM_STEP = 0.001, 0.9, 0.999, 1e-08, 0.01, 10
LANE = 128
FLAT_L = 2048
ADAM_ROWS = 128
VMEM_LIMIT = 56 * 1024 * 1024
N_CHIPS = 4
HI = lax.Precision.HIGHEST
MESH = pl.DeviceIdType.MESH


def _cparams(sem):
    return pltpu.CompilerParams(dimension_semantics=sem, vmem_limit_bytes=VMEM_LIMIT)


def _pallas(body, args, *, name, grid, in_specs, out_specs, out_shape, scratch_shapes=(), sem=None, comm=()):
    n_in, n_out, n_scr = len(in_specs), len(out_specs), len(scratch_shapes)
    if not comm:
        return pl.pallas_call(body, name=name, grid=grid, in_specs=in_specs, out_specs=out_specs, out_shape=out_shape,
                              scratch_shapes=list(scratch_shapes), compiler_params=_cparams(sem))(*args)
    ci = [len(p.ins) for p in comm]
    co = [len(p.outs) for p in comm]
    cs = [len(p.sems) for p in comm]

    def wrapped(*refs):
        ins, pos = refs[:n_in], n_in
        cins = refs[pos:pos + sum(ci)]
        pos += sum(ci)
        outs = refs[pos:pos + n_out]
        pos += n_out
        couts = refs[pos:pos + sum(co)]
        pos += sum(co)
        scr = refs[pos:pos + n_scr]
        csems = refs[pos + n_scr:]
        ids = [pl.program_id(a) for a in range(len(grid))]
        first, last = ids[0] == 0, ids[0] == grid[0] - 1
        for a in range(1, len(grid)):
            first = jnp.logical_and(first, ids[a] == 0)
            last = jnp.logical_and(last, ids[a] == grid[a] - 1)
        parts, a, b, c = [], 0, 0, 0
        for p, na, nb, nc in zip(comm, ci, co, cs):
            parts.append((p, cins[a:a + na], couts[b:b + nb], csems[c:c + nc]))
            a, b, c = a + na, b + nb, c + nc

        @pl.when(first)
        def _():
            for p, pi, po, ps in parts:
                p.start(pi, po, ps)

        body(*ins, *outs, *scr)

        @pl.when(last)
        def _():
            for p, pi, po, ps in parts:
                p.finish(pi, po, ps)

    any_spec = pl.BlockSpec(memory_space=pl.ANY)
    res = pl.pallas_call(
        wrapped, name=name, grid=grid,
        in_specs=list(in_specs) + [any_spec] * sum(ci), out_specs=list(out_specs) + [any_spec] * sum(co),
        out_shape=list(out_shape) + [s for p in comm for s in p.outs],
        scratch_shapes=list(scratch_shapes) + [s for p in comm for s in p.sems],
        compiler_params=_cparams(("arbitrary",) * len(grid)),
    )(*args, *[a for p in comm for a in p.ins])
    return res


def _pick(dim, target, mult=LANE):
    if dim <= target:
        return dim
    t = (target // mult) * mult
    while t >= mult:
        if dim % t == 0:
            return t
        t -= mult
    return dim


def _sigmoid(x):
    return 1.0 / (1.0 + jnp.exp(-x))


def _silu(x):
    return x * _sigmoid(x)


def _softplus(x):
    return jnp.maximum(x, 0.0) + jnp.log(1.0 + jnp.exp(-jnp.abs(x)))


def _rms(x, g):
    return x * lax.rsqrt(jnp.mean(x * x, axis=-1, keepdims=True) + EPS) * g


def _matmul(name, a, b, mode, out_dtype=F32, tm=1024, tn=1024, tk=2048, epi=None, extras=(), out_dtypes=None, comm=()):
    if mode == "nn":
        (M, K), (K2, N) = a.shape, b.shape
    elif mode == "nt":
        (M, K), (N, K2) = a.shape, b.shape
    else:
        (K, M), (K2, N) = a.shape, b.shape
    assert K == K2, (name, a.shape, b.shape, mode)
    tm, tn, tk = _pick(M, tm), _pick(N, tn), _pick(K, tk)
    nk = K // tk
    if mode == "nn":
        a_spec = pl.BlockSpec((tm, tk), lambda i, j, k: (i, k))
        b_spec = pl.BlockSpec((tk, tn), lambda i, j, k: (k, j))
        dims = (((1,), (0,)), ((), ()))
    elif mode == "nt":
        a_spec = pl.BlockSpec((tm, tk), lambda i, j, k: (i, k))
        b_spec = pl.BlockSpec((tn, tk), lambda i, j, k: (j, k))
        dims = (((1,), (1,)), ((), ()))
    else:
        a_spec = pl.BlockSpec((tk, tm), lambda i, j, k: (k, i))
        b_spec = pl.BlockSpec((tk, tn), lambda i, j, k: (k, j))
        dims = (((0,), (0,)), ((), ()))
    out_dtypes = list(out_dtypes) if out_dtypes is not None else [out_dtype]
    ne, no = len(extras), len(out_dtypes)
    in_place = epi is None and out_dtypes == [F32]
    use_acc = nk > 1 and not in_place

    def finish(acc, extra_refs, out_refs):
        res = acc if epi is None else epi(acc, *[r[...] for r in extra_refs])
        res = res if isinstance(res, (tuple, list)) else (res,)
        for r, v in zip(out_refs, res):
            r[...] = v.astype(r.dtype)

    def body(a_ref, b_ref, *rest):
        extra_refs, out_refs = rest[:ne], rest[ne:ne + no]
        part = lax.dot_general(a_ref[...].astype(BF16), b_ref[...].astype(BF16), dims, preferred_element_type=F32)
        if nk == 1:
            finish(part, extra_refs, out_refs)
            return
        k = pl.program_id(2)
        acc_ref = rest[-1] if use_acc else out_refs[0]

        @pl.when(k == 0)
        def _():
            acc_ref[...] = part

        @pl.when(k > 0)
        def _():
            acc_ref[...] += part

        if use_acc:
            @pl.when(k == nk - 1)
            def _():
                finish(acc_ref[...], extra_refs, out_refs)

    o_spec = pl.BlockSpec((tm, tn), lambda i, j, k: (i, j))
    res = _pallas(body, (a, b, *extras), name=name, grid=(M // tm, N // tn, nk),
                  in_specs=[a_spec, b_spec] + [o_spec] * ne, out_specs=[o_spec] * no,
                  out_shape=[jax.ShapeDtypeStruct((M, N), dt) for dt in out_dtypes],
                  scratch_shapes=[pltpu.VMEM((tm, tn), F32)] if use_acc else [],
                  sem=("parallel", "parallel", "arbitrary"), comm=comm)
    outs = res[0] if no == 1 else res[:no]
    return (outs, res[no:]) if comm else outs


def _row_spec(tr, C, off):
    return pl.BlockSpec((tr, C), lambda j, i: (i, off + j))


def _par_spec(k, C, off):
    if off is None:
        return pl.BlockSpec((k, C), lambda j, i: (0, 0))
    return pl.BlockSpec((k, C), lambda j, i: (0, off + j))


def _rowwise(name, fn, rows, params, outs, reds=(), *, S, ncb=1, tr=256):
    tr = min(tr, S)
    nr, npar, no = len(rows), len(params), len(outs)

    def body(*refs):
        j, i = pl.program_id(0), pl.program_id(1)
        ins = [r[...].astype(F32) for r in refs[:nr + npar]]
        res = fn(i * tr, j, *ins)
        res = res if isinstance(res, (tuple, list)) else (res,)
        out_refs = refs[nr + npar:]
        for r, v in zip(out_refs[:no], res[:no]):
            r[...] = v.astype(r.dtype)
        for (k, C, per_j), r, v in zip(reds, out_refs[no:], res[no:]):
            first = (i == 0) if per_j else jnp.logical_and(i == 0, j == 0)

            @pl.when(first)
            def _(r=r):
                r[...] = jnp.zeros_like(r)

            r[...] += v

    in_specs = [_row_spec(tr, C, off) for (_, off, C) in rows] + [_par_spec(a.shape[0], C, off) for (a, off, C) in params]
    out_specs = [pl.BlockSpec((tr, C), lambda j, i: (i, j)) for (C, _) in outs]
    out_specs += [pl.BlockSpec((k, C), (lambda j, i: (0, j)) if per_j else (lambda j, i: (0, 0))) for (k, C, per_j) in reds]
    out_shape = [jax.ShapeDtypeStruct((S, ncb * C), dt) for (C, dt) in outs]
    out_shape += [jax.ShapeDtypeStruct((k, C * (ncb if per_j else 1)), F32) for (k, C, per_j) in reds]
    res = pl.pallas_call(
        body, name=name, grid=(ncb, S // tr), in_specs=in_specs, out_specs=out_specs, out_shape=out_shape,
        compiler_params=_cparams(("arbitrary", "arbitrary")),
    )(*[a for (a, _, _) in rows], *[a for (a, _, _) in params])
    return res


def _rowwise_bwd(name, fn, rows, params, cots, drow, adds=None, *, S, ncb=1, tr=128):
    tr = min(tr, S)
    nr, npar, nc = len(rows), len(params), len(cots)
    adds = adds or [None] * nr
    add_list = [a for a in adds if a is not None]
    na = len(add_list)

    def body(*refs):
        j, i = pl.program_id(0), pl.program_id(1)
        ins = [r[...].astype(F32) for r in refs[:nr + npar]]
        cts = [r[...].astype(F32) for r in refs[nr + npar:nr + npar + nc]]
        add_refs = list(refs[nr + npar + nc:nr + npar + nc + na])
        out_refs = list(refs[nr + npar + nc + na:])

        def f(*a):
            res = fn(i * tr, j, *a)
            return tuple(res) if isinstance(res, (tuple, list)) else (res,)

        _, vjp = jax.vjp(f, *ins)
        grads = vjp(tuple(cts))
        for idx in range(nr):
            if drow[idx] is None:
                continue
            g = grads[idx]
            if adds[idx] is not None:
                g = g + add_refs.pop(0)[...].astype(F32)
            r = out_refs.pop(0)
            r[...] = g.astype(r.dtype)
        for idx in range(npar):
            per_j = params[idx][1] is not None
            first = (i == 0) if per_j else jnp.logical_and(i == 0, j == 0)
            r = out_refs.pop(0)

            @pl.when(first)
            def _(r=r):
                r[...] = jnp.zeros_like(r)

            r[...] += grads[nr + idx]

    in_specs = [_row_spec(tr, C, off) for (_, off, C) in rows]
    in_specs += [_par_spec(a.shape[0], C, off) for (a, off, C) in params]
    in_specs += [_row_spec(tr, C, off) for (_, off, C) in cots]
    in_specs += [_row_spec(tr, C, off) for (_, off, C) in add_list]
    out_specs, out_shape = [], []
    for idx in range(nr):
        if drow[idx] is not None:
            C = rows[idx][2]
            out_specs.append(pl.BlockSpec((tr, C), lambda j, i: (i, j)))
            out_shape.append(jax.ShapeDtypeStruct((S, ncb * C), drow[idx]))
    for (a, off, C) in params:
        per_j = off is not None
        out_specs.append(pl.BlockSpec((a.shape[0], C), (lambda j, i: (0, j)) if per_j else (lambda j, i: (0, 0))))
        out_shape.append(jax.ShapeDtypeStruct((a.shape[0], C * (ncb if per_j else 1)), F32))
    return pl.pallas_call(
        body, name=name, grid=(ncb, S // tr), in_specs=in_specs, out_specs=out_specs, out_shape=out_shape,
        compiler_params=_cparams(("arbitrary", "arbitrary")),
    )(*[a for (a, _, _) in rows], *[a for (a, _, _) in params], *[a for (a, _, _) in cots], *[a for (a, _, _) in add_list])


def _halo_rows(K):
    return 8 * ((K - 1 + 7) // 8)


def _inv_count(row0, tr, win):
    t = (row0 + lax.broadcasted_iota(jnp.int32, (tr, 1), 0)).astype(F32)
    return 1.0 / jnp.minimum(t + 1.0, win)


CONV_ROWS = 32


def _shifted_down(xp, rows, K):
    for p in sorted({s % 8 for s in range(K)} - {0}):
        xp[p, 8:rows, :] = xp[0, 8 - p:rows - p, :]


def _shifted_up(yp, rows, K):
    for p in sorted({s % 8 for s in range(K)} - {0}):
        yp[p, 0:rows - 8, :] = yp[0, p:rows - 8 + p, :]


def _dwconv_fwd(name, x, x_off, w, *, S, C, bias=None, win=None, out_dtype=F32, cb=512, tr=256, comm=()):
    K = w.shape[0]
    cb, tr = _pick(C, cb), min(tr, S)
    HB = min(_halo_rows(K), tr)
    assert K - 1 <= HB and tr % HB == 0 and C % cb == 0
    nb = tr // HB
    RB = min(CONV_ROWS, tr)
    extra = [a for a in (bias, win) if a is not None]

    def body(xh_ref, x_ref, w_ref, *rest):
        y_ref, xp = rest[-2], rest[-1]
        i = pl.program_id(1)
        xp[0, 0:HB, :] = jnp.where(i > 0, xh_ref[...].astype(F32), 0.0)
        xp[0, HB:HB + tr, :] = x_ref[...].astype(F32)
        _shifted_down(xp, HB + tr, K)

        def sub(rb, carry):
            r0 = rb * RB
            acc = jnp.zeros((RB, cb), F32)
            for jj in range(K):
                s = K - 1 - jj
                start = pl.multiple_of(HB - 8 * (s // 8) + r0, 8)
                acc = acc + w_ref[jj:jj + 1, :] * xp[s % 8, pl.ds(start, RB), :]
            if bias is not None:
                acc = acc + rest[0][...]
            if win is not None:
                rows = pl.ds(pl.multiple_of(r0, 8), RB)
                acc = acc * _inv_count(i * tr + r0, RB, rest[0][...]) - x_ref[rows, :].astype(F32)
            y_ref[pl.ds(pl.multiple_of(r0, 8), RB), :] = acc.astype(y_ref.dtype)
            return carry

        lax.fori_loop(0, tr // RB, sub, 0)

    in_specs = [pl.BlockSpec((HB, cb), lambda j, i: (jnp.maximum(i * nb - 1, 0), x_off + j)),
                pl.BlockSpec((tr, cb), lambda j, i: (i, x_off + j)),
                pl.BlockSpec((K, cb), lambda j, i: (0, j))]
    in_specs += [pl.BlockSpec((1, cb), lambda j, i: (0, j)) for _ in extra]
    res = _pallas(body, (x, x, w, *extra), name=name, grid=(C // cb, S // tr), in_specs=in_specs,
                  out_specs=[pl.BlockSpec((tr, cb), lambda j, i: (i, j))],
                  out_shape=[jax.ShapeDtypeStruct((S, C), out_dtype)],
                  scratch_shapes=[pltpu.VMEM((8, HB + tr, cb), F32)], sem=("parallel", "arbitrary"), comm=comm)
    return (res[0], res[1:]) if comm else res[0]


def _dwconv_bwd(name, x, x_off, dy, w, *, S, C, win=None, want_dw=True, cb=512, tr=256, comm=()):
    K = w.shape[0]
    cb, tr = _pick(C, cb), min(tr, S)
    HB = min(_halo_rows(K), tr)
    nb, nt = tr // HB, S // tr
    RB = min(CONV_ROWS, tr)

    def body(*refs):
        if want_dw:
            xh_ref, x_ref, dy_ref, dyn_ref, w_ref = refs[:5]
            rest = refs[5:]
        else:
            dy_ref, dyn_ref, w_ref = refs[:3]
            rest = refs[3:]
        i = pl.program_id(1)
        dyt = dy_ref[...].astype(F32)
        dyn = jnp.where(i < nt - 1, dyn_ref[...].astype(F32), 0.0)
        if win is not None:
            win_v = rest[0][...]
            rest = rest[1:]
            yy_t = dyt * _inv_count(i * tr, tr, win_v)
            dyn = dyn * _inv_count((i + 1) * tr, HB, win_v)
        else:
            yy_t = dyt
        if want_dw:
            dx_ref, dw_ref, db_ref, yp, xp, dw8, db8 = rest
        else:
            dx_ref, yp = rest
        yp[0, 0:tr, :] = yy_t
        yp[0, tr:tr + HB, :] = dyn
        _shifted_up(yp, tr + HB, K)

        def sub(rb, carry):
            r0 = rb * RB
            acc = jnp.zeros((RB, cb), F32)
            for jj in range(K):
                s = K - 1 - jj
                start = pl.multiple_of(8 * (s // 8) + r0, 8)
                acc = acc + w_ref[jj:jj + 1, :] * yp[s % 8, pl.ds(start, RB), :]
            rows = pl.ds(pl.multiple_of(r0, 8), RB)
            if win is not None:
                acc = acc - dy_ref[rows, :].astype(F32)
            dx_ref[rows, :] = acc.astype(dx_ref.dtype)
            return carry

        lax.fori_loop(0, tr // RB, sub, 0)
        if want_dw:
            xp[0, 0:HB, :] = jnp.where(i > 0, xh_ref[...].astype(F32), 0.0)
            xp[0, HB:HB + tr, :] = x_ref[...].astype(F32)
            _shifted_down(xp, HB + tr, K)

            @pl.when(i == 0)
            def _():
                dw8[...] = jnp.zeros_like(dw8)
                db8[...] = jnp.zeros_like(db8)

            def sub_w(rb, carry):
                r0 = rb * RB
                dyb = dy_ref[pl.ds(pl.multiple_of(r0, 8), RB), :].astype(F32)
                for jj in range(K):
                    s = K - 1 - jj
                    start = pl.multiple_of(HB - 8 * (s // 8) + r0, 8)
                    prod = dyb * xp[s % 8, pl.ds(start, RB), :]
                    dw8[jj] += jnp.sum(prod.reshape(RB // 8, 8, cb), axis=0)
                db8[...] += jnp.sum(dyb.reshape(RB // 8, 8, cb), axis=0)
                return carry

            lax.fori_loop(0, tr // RB, sub_w, 0)

            @pl.when(i == nt - 1)
            def _():
                for jj in range(K):
                    dw_ref[jj:jj + 1, :] = jnp.sum(dw8[jj], axis=0, keepdims=True)
                db_ref[...] = jnp.sum(db8[...], axis=0, keepdims=True)

    last = S // HB - 1
    in_specs, args = [], []
    if want_dw:
        in_specs += [pl.BlockSpec((HB, cb), lambda j, i: (jnp.maximum(i * nb - 1, 0), x_off + j)),
                     pl.BlockSpec((tr, cb), lambda j, i: (i, x_off + j))]
        args += [x, x]
    in_specs += [pl.BlockSpec((tr, cb), lambda j, i: (i, j)),
                 pl.BlockSpec((HB, cb), lambda j, i: (jnp.minimum((i + 1) * nb, last), j)),
                 pl.BlockSpec((K, cb), lambda j, i: (0, j))]
    args += [dy, dy, w]
    if win is not None:
        in_specs.append(pl.BlockSpec((1, cb), lambda j, i: (0, j)))
        args.append(win)
    out_specs = [pl.BlockSpec((tr, cb), lambda j, i: (i, j))]
    out_shape = [jax.ShapeDtypeStruct((S, C), F32)]
    scratch = [pltpu.VMEM((8, tr + HB, cb), F32)]
    if want_dw:
        out_specs += [pl.BlockSpec((K, cb), lambda j, i: (0, j)), pl.BlockSpec((1, cb), lambda j, i: (0, j))]
        out_shape += [jax.ShapeDtypeStruct((K, C), F32), jax.ShapeDtypeStruct((1, C), F32)]
        scratch += [pltpu.VMEM((8, HB + tr, cb), F32), pltpu.VMEM((K, 8, cb), F32), pltpu.VMEM((8, cb), F32)]
    res = _pallas(body, args, name=name, grid=(C // cb, S // tr), in_specs=in_specs, out_specs=out_specs,
                  out_shape=out_shape, scratch_shapes=scratch, sem=("parallel", "arbitrary"), comm=comm)
    return (res[:len(out_specs)], res[len(out_specs):]) if comm else res


def _dot(a, b, dims, hi=False):
    if hi:
        return lax.dot_general(a, b, (dims, ((), ())), precision=HI, preferred_element_type=F32)
    return lax.dot_general(a.astype(BF16), b.astype(BF16), (dims, ((), ())), preferred_element_type=F32)


_NN, _NT, _TN = ((1,), (0,)), ((1,), (1,)), ((0,), (0,))


def _col(m, idx):
    lane = lax.broadcasted_iota(jnp.int32, m.shape, 1)
    return jnp.sum(jnp.where(lane == idx, m, 0.0), axis=1, keepdims=True)


def _row(m, idx):
    sub = lax.broadcasted_iota(jnp.int32, m.shape, 0)
    return jnp.sum(jnp.where(sub == idx, m, 0.0), axis=0, keepdims=True)


def _delta_chunk(q, k, v, beta, gcc, gcr, causal, strict, eye, scale, C):
    d = {}
    gam = jnp.where(causal, jnp.exp(jnp.where(causal, gcc - gcr, 0.0)), 0.0)
    eg = jnp.exp(gcc)
    g_last = _row(gcc, C - 1)
    d["gam"], d["eg"], d["g_last"] = gam, eg, g_last
    d["ek"] = jnp.exp(g_last - gcc)
    d["decay"] = jnp.exp(g_last)
    qs = q * scale
    kb = k * beta
    d["qs"], d["kb"] = qs, kb
    d["kk"] = _dot(kb, k, _NT)
    d["A"] = jnp.where(strict, d["kk"] * gam, 0.0)
    d["qk"] = _dot(qs, k, _NT)
    d["attn"] = jnp.where(causal, d["qk"] * gam, 0.0)
    d["vb"] = v * beta
    d["kbg"] = kb * eg
    d["qg"] = qs * eg
    d["kd"] = k * d["ek"]
    return d


def _split(m):
    hi = m.astype(BF16)
    return hi, (m - hi.astype(F32)).astype(BF16)


def _dot3(a, b, dims):
    return _dot(a[0], b[0], dims) + (_dot(a[0], b[1], dims) + _dot(a[1], b[0], dims))


def _tri_inverse(As, eye):
    P = [-A for A in As]
    T = [eye + p for p in P]
    n = 1
    while 2 * n < As[0].shape[0]:
        Ps = [_split(p) for p in P]
        P = [_dot3(ps, ps, _NN) for ps in Ps]
        Ts = [_split(t) for t in T]
        Ps = [_split(p) for p in P]
        T = [t + _dot3(ts, ps, _NN) for t, ts, ps in zip(T, Ts, Ps)]
        n *= 2
    return T


def _delta_fwd(name, qk, v, gb, gT, *, S, H, dh, comm=()):
    C = min(DN_CHUNK, S)
    N, W = S // C, H * dh
    scale = dh ** -0.5

    def body(qk_ref, v_ref, gb_ref, gT_ref, o_ref, sp_ref, T_ref, st):
        n = pl.program_id(0)

        @pl.when(n == 0)
        def _():
            st[...] = jnp.zeros_like(st)

        r = lax.broadcasted_iota(jnp.int32, (C, C), 0)
        c = lax.broadcasted_iota(jnp.int32, (C, C), 1)
        causal, strict = r >= c, r > c
        eye = (r == c).astype(F32)
        Lt = causal.astype(F32)
        gbv = gb_ref[...]
        gcum = _dot(Lt, gbv, _NN, hi=True)
        gcumT = _dot(gT_ref[0], Lt, _NT, hi=True)
        hs = range(H)
        sl = [slice(h * dh, (h + 1) * dh) for h in hs]
        d = [_delta_chunk(qk_ref[:, sl[h]], qk_ref[:, W + h * dh:W + (h + 1) * dh], v_ref[:, sl[h]], _col(gbv, h),
                          _col(gcum, H + h), _row(gcumT, h), causal, strict, eye, scale, C) for h in hs]
        T = _tri_inverse([d[h]["A"] for h in hs], eye)
        u = [_dot(T[h], d[h]["vb"], _NN) for h in hs]
        w = [_dot(T[h], d[h]["kbg"], _NN) for h in hs]
        s0 = [st[h] for h in hs]
        ws = [_dot(w[h], s0[h], _NN) for h in hs]
        qs0 = [_dot(d[h]["qg"], s0[h], _NN) for h in hs]
        v_new = [u[h] - ws[h] for h in hs]
        av = [_dot(d[h]["attn"], v_new[h], _NN) for h in hs]
        kv = [_dot(d[h]["kd"], v_new[h], _TN) for h in hs]
        for h in hs:
            sp_ref[0, h] = s0[h]
            T_ref[0, h] = T[h]
            o_ref[:, sl[h]] = qs0[h] + av[h]
            st[h] = s0[h] * d[h]["decay"] + kv[h]

    res = _pallas(
        body, (qk, v, gb, gT), name=name, grid=(N,),
        in_specs=[pl.BlockSpec((C, 2 * W), lambda n: (n, 0)), pl.BlockSpec((C, W), lambda n: (n, 0)),
                  pl.BlockSpec((C, LANE), lambda n: (n, 0)), pl.BlockSpec((1, H, C), lambda n: (n, 0, 0))],
        out_specs=[pl.BlockSpec((C, W), lambda n: (n, 0)), pl.BlockSpec((1, H, dh, dh), lambda n: (n, 0, 0, 0)),
                   pl.BlockSpec((1, H, C, C), lambda n: (n, 0, 0, 0))],
        out_shape=[jax.ShapeDtypeStruct((S, W), F32), jax.ShapeDtypeStruct((N, H, dh, dh), F32),
                   jax.ShapeDtypeStruct((N, H, C, C), F32)],
        scratch_shapes=[pltpu.VMEM((H, dh, dh), F32)], sem=("arbitrary",), comm=comm)
    return (res[:3], res[3:]) if comm else res[:3]


def _delta_bwd(name, qk, v, gb, gT, sp, Tm, do, *, S, H, dh, comm=()):
    C = min(DN_CHUNK, S)
    N, W = S // C, H * dh
    scale = dh ** -0.5

    def body(qk_ref, v_ref, gb_ref, gT_ref, sp_ref, T_ref, do_ref, dqk_ref, dv_ref, dgb_ref, ds):
        n = pl.program_id(0)

        @pl.when(n == 0)
        def _():
            ds[...] = jnp.zeros_like(ds)

        r = lax.broadcasted_iota(jnp.int32, (C, C), 0)
        c = lax.broadcasted_iota(jnp.int32, (C, C), 1)
        causal, strict = r >= c, r > c
        eye = (r == c).astype(F32)
        Lt = causal.astype(F32)
        ones = jnp.ones((C, LANE), F32)
        lane = lax.broadcasted_iota(jnp.int32, (C, LANE), 1)
        rowi = lax.broadcasted_iota(jnp.int32, (C, 1), 0)
        gbv = gb_ref[...]
        gcum = _dot(Lt, gbv, _NN, hi=True)
        gcumT = _dot(gT_ref[0], Lt, _NT, hi=True)
        dgc_all = jnp.zeros((C, LANE), F32)
        dbeta_all = jnp.zeros((C, LANE), F32)
        hs = range(H)
        sl = [slice(h * dh, (h + 1) * dh) for h in hs]
        ksl = [slice(W + h * dh, W + (h + 1) * dh) for h in hs]
        k = [qk_ref[:, ksl[h]] for h in hs]
        vv = [v_ref[:, sl[h]] for h in hs]
        beta = [_col(gbv, h) for h in hs]
        d = [_delta_chunk(qk_ref[:, sl[h]], k[h], vv[h], beta[h], _col(gcum, H + h), _row(gcumT, h), causal, strict,
                          eye, scale, C) for h in hs]
        T = [T_ref[0, h] for h in hs]
        s0 = [sp_ref[0, h] for h in hs]
        dO = [do_ref[:, sl[h]] for h in hs]
        dS = [ds[h] for h in hs]
        u = [_dot(T[h], d[h]["vb"], _NN) for h in hs]
        w = [_dot(T[h], d[h]["kbg"], _NN) for h in hs]
        ws = [_dot(w[h], s0[h], _NN) for h in hs]
        v_new = [u[h] - ws[h] for h in hs]
        dv_new = [_dot(d[h]["attn"], dO[h], _TN) + _dot(d[h]["kd"], dS[h], _NN) for h in hs]
        dattn = [jnp.where(causal, _dot(dO[h], v_new[h], _NT), 0.0) for h in hs]
        dqg = [_dot(dO[h], s0[h], _NT) for h in hs]
        dkd = [_dot(v_new[h], dS[h], _NT) for h in hs]
        ddecay = [jnp.sum(jnp.sum(s0[h] * dS[h], axis=1, keepdims=True), axis=0, keepdims=True) for h in hs]
        ds_new = [_dot(d[h]["qg"], dO[h], _TN) + d[h]["decay"] * dS[h] - _dot(w[h], dv_new[h], _TN) for h in hs]
        dw = [-_dot(dv_new[h], s0[h], _NT) for h in hs]
        for h in hs:
            ds[h] = ds_new[h]
        dT = [_dot(dv_new[h], d[h]["vb"], _NT) + _dot(dw[h], d[h]["kbg"], _NT) for h in hs]
        dvb = [_dot(T[h], dv_new[h], _TN) for h in hs]
        dkbg = [_dot(T[h], dw[h], _TN) for h in hs]
        Ts = [_split(T[h]) for h in hs]
        x1 = [_dot3(Ts[h], _split(dT[h]), _TN) for h in hs]
        dA = [jnp.where(strict, -_dot3(_split(x1[h]), Ts[h], _NT), 0.0) for h in hs]
        dkk = [dA[h] * d[h]["gam"] for h in hs]
        dqk_m = [dattn[h] * d[h]["gam"] for h in hs]
        m = [_split(dA[h] * d[h]["A"] + dattn[h] * d[h]["attn"]) for h in hs]
        msum = [jnp.sum(dA[h] * d[h]["A"] + dattn[h] * d[h]["attn"], axis=1, keepdims=True) for h in hs]
        mcol = [jnp.max(_dot(m[h][0], ones, _TN) + _dot(m[h][1], ones, _TN), axis=1, keepdims=True) for h in hs]
        dkb = [_dot(dkk[h], k[h], _NN) + dkbg[h] * d[h]["eg"] for h in hs]
        dk = [_dot(dkk[h], d[h]["kb"], _TN) + _dot(dqk_m[h], d[h]["qs"], _TN) + dkd[h] * d[h]["ek"] + dkb[h] * beta[h]
              for h in hs]
        dqs = [_dot(dqk_m[h], k[h], _NN) + dqg[h] * d[h]["eg"] for h in hs]
        for h in hs:
            r_kd = jnp.sum(dkd[h] * d[h]["kd"], axis=1, keepdims=True)
            dgc = (msum[h] - mcol[h] + jnp.sum(dqg[h] * d[h]["qg"], axis=1, keepdims=True) - r_kd
                   + jnp.sum(dkbg[h] * d[h]["kbg"], axis=1, keepdims=True))
            dg_last = jnp.sum(r_kd, axis=0, keepdims=True) + ddecay[h] * d[h]["decay"]
            dgc = dgc + jnp.where(rowi == C - 1, dg_last, 0.0)
            dbeta = jnp.sum(dkb[h] * k[h], axis=1, keepdims=True) + jnp.sum(dvb[h] * vv[h], axis=1, keepdims=True)
            dqk_ref[:, sl[h]] = dqs[h] * scale
            dqk_ref[:, ksl[h]] = dk[h]
            dv_ref[:, sl[h]] = dvb[h] * beta[h]
            dgc_all = dgc_all + jnp.where(lane == H + h, dgc, 0.0)
            dbeta_all = dbeta_all + jnp.where(lane == h, dbeta, 0.0)
        dgb_ref[...] = _dot(Lt, dgc_all, _TN, hi=True) + dbeta_all

    rev = lambda n: N - 1 - n
    res = _pallas(
        body, (qk, v, gb, gT, sp, Tm, do), name=name, grid=(N,),
        in_specs=[pl.BlockSpec((C, 2 * W), lambda n: (rev(n), 0)), pl.BlockSpec((C, W), lambda n: (rev(n), 0)),
                  pl.BlockSpec((C, LANE), lambda n: (rev(n), 0)), pl.BlockSpec((1, H, C), lambda n: (rev(n), 0, 0)),
                  pl.BlockSpec((1, H, dh, dh), lambda n: (rev(n), 0, 0, 0)),
                  pl.BlockSpec((1, H, C, C), lambda n: (rev(n), 0, 0, 0)),
                  pl.BlockSpec((C, W), lambda n: (rev(n), 0))],
        out_specs=[pl.BlockSpec((C, 2 * W), lambda n: (rev(n), 0)), pl.BlockSpec((C, W), lambda n: (rev(n), 0)),
                   pl.BlockSpec((C, LANE), lambda n: (rev(n), 0))],
        out_shape=[jax.ShapeDtypeStruct((S, 2 * W), F32), jax.ShapeDtypeStruct((S, W), F32),
                   jax.ShapeDtypeStruct((S, LANE), F32)],
        scratch_shapes=[pltpu.VMEM((H, dh, dh), F32)], sem=("arbitrary",), comm=comm)
    return (res[:3], res[3:]) if comm else res[:3]


_ANY = pl.BlockSpec(memory_space=pl.ANY)


ICI_CHUNKS = 4
D2D_CHUNKS = 4
EXCHANGE_PARTS = 3


def _place():
    return lax.axis_index("x"), lax.axis_index("y"), lax.axis_index("c")


def _row_chunks(rows, n):
    n = max(1, min(n, rows // 8))
    while n > 1 and (rows % n or (rows // n) % 8):
        n -= 1
    return [(k * (rows // n), rows // n) for k in range(n)]


class _Plug:
    def __init__(self, ins, outs, sems, start, finish, after):
        self.ins, self.outs, self.sems, self.start, self.finish, self.after = ins, outs, sems, start, finish, after


def _run_plug(name, plug):
    def body(*refs):
        ni, no = len(plug.ins), len(plug.outs)
        plug.start(refs[:ni], refs[ni:ni + no], refs[ni + no:])
        plug.finish(refs[:ni], refs[ni:ni + no], refs[ni + no:])

    return pl.pallas_call(body, name=name, in_specs=[_ANY] * len(plug.ins), out_specs=[_ANY] * len(plug.outs),
                          out_shape=list(plug.outs), scratch_shapes=list(plug.sems))(*plug.ins)


def _gather_plug(flat):
    R, L = flat.shape
    Rh = R // 2
    ici = _row_chunks(Rh, ICI_CHUNKS)
    sub = _row_chunks(ici[0][1], D2D_CHUNKS)
    ni, ns = len(ici), len(sub)

    def parts(ins, outs, sems):
        (x_ref,), (out_ref,), (send_sems, recv_sems) = ins, outs, sems
        x, y, c = _place()
        chips = [(1 - x, y), (x, 1 - y), (1 - x, 1 - y)]

        def rows(px, py, pc, r0, n):
            return out_ref.at[2 * px + py, pl.ds(pc * Rh + r0, n), :]

        def copy(k, src, dst, to):
            return pltpu.make_async_remote_copy(src_ref=src, dst_ref=dst, send_sem=send_sems.at[k],
                                                recv_sem=recv_sems.at[k], device_id=to, device_id_type=MESH)

        first = [copy(k * ni + q, x_ref.at[pl.ds(c * Rh + r0, n), :], rows(x, y, c, r0, n), (*chip, c))
                 for k, chip in enumerate(chips) for q, (r0, n) in enumerate(ici)]
        return x_ref, (x, y, c), chips, rows, copy, first

    def start(ins, outs, sems):
        for cp in parts(ins, outs, sems)[-1]:
            cp.start()

    def finish(ins, outs, sems):
        x_ref, (x, y, c), chips, rows, copy, first = parts(ins, outs, sems)
        sibling = (x, y, 1 - c)
        passed = []
        for k, chip in enumerate(chips):
            for q, (r0, n) in enumerate(ici):
                copy(k * ni + q, x_ref.at[pl.ds(r0, n), :], rows(*chip, c, r0, n), (*chip, c)).wait_recv()
                for t, (s0, m) in enumerate(sub):
                    cp = copy(3 * ni + (k * ni + q) * ns + t, rows(*chip, c, r0 + s0, m), rows(*chip, c, r0 + s0, m), sibling)
                    cp.start()
                    passed.append(cp)
        for k, chip in enumerate(chips):
            for q, (r0, n) in enumerate(ici):
                for t, (s0, m) in enumerate(sub):
                    copy(3 * ni + (k * ni + q) * ns + t, x_ref.at[pl.ds(r0, m), :], rows(*chip, 1 - c, r0 + s0, m),
                         sibling).wait_recv()
        for cp in first + passed:
            cp.wait_send()

    def after(res):
        return lax.dynamic_update_slice(res[0], flat[None], (2 * lax.axis_index("x") + lax.axis_index("y"), 0, 0))

    nsem = 3 * ni * (1 + ns)
    return _Plug([flat], [jax.ShapeDtypeStruct((N_CHIPS, R, L), flat.dtype)],
                 [pltpu.SemaphoreType.DMA((nsem,)), pltpu.SemaphoreType.DMA((nsem,))], start, finish, after)


def _sibling_split(name, g):
    _, R, L = g.shape
    Rh = R // 2

    chunks = [(j, r0, n) for j in range(N_CHIPS) for (r0, n) in _row_chunks(Rh, D2D_CHUNKS)]

    def body(g_ref, got_ref, send_sems, recv_sems):
        x, y, c = _place()
        cps = [pltpu.make_async_remote_copy(src_ref=g_ref.at[j, pl.ds((1 - c) * Rh + r0, n), :],
                                            dst_ref=got_ref.at[j, pl.ds(r0, n), :], send_sem=send_sems.at[k],
                                            recv_sem=recv_sems.at[k], device_id=(x, y, 1 - c), device_id_type=MESH)
               for k, (j, r0, n) in enumerate(chunks)]
        for cp in cps:
            cp.start()
        for cp in cps:
            cp.wait()

    sems = pltpu.SemaphoreType.DMA((len(chunks),))
    got = pl.pallas_call(
        body, name=name, in_specs=[_ANY], out_specs=_ANY, out_shape=jax.ShapeDtypeStruct((N_CHIPS, Rh, L), g.dtype),
        scratch_shapes=[sems, sems],
    )(g)
    own = lax.dynamic_slice(g, (0, lax.axis_index("c") * Rh, 0), (N_CHIPS, Rh, L))
    return own, got


def _exchange_plug(p, row0, nrows):
    ici = _row_chunks(nrows, ICI_CHUNKS)
    ni = len(ici)

    def sends(ins, outs, sems):
        (p_ref,), (q_ref,), (send_sems, recv_sems) = ins, outs, sems
        x, y, c = _place()
        me = 2 * x + y
        chips = [(1 - x, y), (x, 1 - y), (1 - x, 1 - y)]
        return [pltpu.make_async_remote_copy(src_ref=p_ref.at[2 * cx + cy, pl.ds(row0 + r0, n), :],
                                             dst_ref=q_ref.at[me, pl.ds(r0, n), :],
                                             send_sem=send_sems.at[k * ni + q], recv_sem=recv_sems.at[k * ni + q],
                                             device_id=(cx, cy, c), device_id_type=MESH)
                for k, (cx, cy) in enumerate(chips) for q, (r0, n) in enumerate(ici)]

    def start(ins, outs, sems):
        for cp in sends(ins, outs, sems):
            cp.start()

    def finish(ins, outs, sems):
        (p_ref,), (q_ref,), (send_sems, recv_sems) = ins, outs, sems
        x, y, c = _place()
        me = 2 * x + y
        chips = [(1 - x, y), (x, 1 - y), (1 - x, 1 - y)]
        for k, (cx, cy) in enumerate(chips):
            for q, (r0, n) in enumerate(ici):
                pltpu.make_async_remote_copy(src_ref=p_ref.at[me, pl.ds(r0, n), :],
                                             dst_ref=q_ref.at[2 * cx + cy, pl.ds(r0, n), :],
                                             send_sem=send_sems.at[k * ni + q], recv_sem=recv_sems.at[k * ni + q],
                                             device_id=(cx, cy, c), device_id_type=MESH).wait_recv()
        for cp in sends(ins, outs, sems):
            cp.wait_send()

    def after(res):
        me = 2 * lax.axis_index("x") + lax.axis_index("y")
        mine = lax.dynamic_slice(p, (me, row0, 0), (1, nrows, p.shape[2]))
        return lax.dynamic_update_slice(res[0], mine, (me, 0, 0))

    return _Plug([p], [jax.ShapeDtypeStruct((N_CHIPS, nrows, p.shape[2]), p.dtype)],
                 [pltpu.SemaphoreType.DMA((3 * ni,)), pltpu.SemaphoreType.DMA((3 * ni,))], start, finish, after)


def _sibling_swap(name, half):
    Rh, L = half.shape
    chunks = _row_chunks(Rh, 2 * D2D_CHUNKS)

    def body(h_ref, out_ref, send_sems, recv_sems):
        x, y, c = _place()
        cps = [pltpu.make_async_remote_copy(src_ref=h_ref.at[pl.ds(r0, n), :], dst_ref=out_ref.at[pl.ds(r0, n), :],
                                            send_sem=send_sems.at[k], recv_sem=recv_sems.at[k], device_id=(x, y, 1 - c),
                                            device_id_type=MESH)
               for k, (r0, n) in enumerate(chunks)]
        for cp in cps:
            cp.start()
        for cp in cps:
            cp.wait()

    sems = pltpu.SemaphoreType.DMA((len(chunks),))
    return pl.pallas_call(
        body, name=name, in_specs=[_ANY], out_specs=_ANY, out_shape=jax.ShapeDtypeStruct((Rh, L), half.dtype),
        scratch_shapes=[sems, sems],
    )(half)


def _add_pairs(name, a, b, out_dtype):
    n, Rh, L = a.shape
    tr = _pick(Rh, 512, 8)

    def body(a_ref, b_ref, o_ref):
        o_ref[...] = (a_ref[...].astype(F32) + b_ref[...].astype(F32)).astype(o_ref.dtype)

    spec = pl.BlockSpec((1, tr, L), lambda j, i: (j, i, 0))
    return pl.pallas_call(body, name=name, grid=(n, Rh // tr), in_specs=[spec, spec], out_specs=spec,
                          out_shape=jax.ShapeDtypeStruct(a.shape, out_dtype),
                          compiler_params=_cparams(("parallel", "parallel")))(a, b)


def _sum_chips(name, q):
    n, Rh, L = q.shape
    tr = _pick(Rh, 512, 8)

    def body(q_ref, o_ref):
        acc = q_ref[0].astype(F32)
        for s in range(1, n):
            acc = acc + q_ref[s].astype(F32)
        o_ref[...] = acc

    return pl.pallas_call(body, name=name, grid=(Rh // tr,),
                          in_specs=[pl.BlockSpec((n, tr, L), lambda i: (0, i, 0))],
                          out_specs=pl.BlockSpec((tr, L), lambda i: (i, 0)),
                          out_shape=jax.ShapeDtypeStruct((Rh, L), F32),
                          compiler_params=_cparams(("parallel",)))(q)


def _adamw(name, w, m, v, g):
    shape, size = w.shape, w.size
    rows = -(-size // FLAT_L)
    rows_p = -(-rows // ADAM_ROWS) * ADAM_ROWS
    pad = rows_p * FLAT_L - size

    def flat2d(a):
        a = a.reshape(-1)
        if pad:
            a = jnp.pad(a, (0, pad), constant_values=1.0)
        return a.reshape(rows_p, FLAT_L)

    c1 = 1.0 / (1.0 - ADAM_B1 ** ADAM_STEP)
    c2 = 1.0 / (1.0 - ADAM_B2 ** ADAM_STEP)

    def body(w_ref, m_ref, v_ref, g_ref, go_ref, d_ref, mo_ref, vo_ref):
        g = g_ref[...]
        wv = w_ref[...]
        mn = ADAM_B1 * m_ref[...] + (1.0 - ADAM_B1) * g
        vn = ADAM_B2 * v_ref[...] + (1.0 - ADAM_B2) * (g * g)
        go_ref[...] = g
        mo_ref[...] = mn
        vo_ref[...] = vn
        d_ref[...] = -ADAM_LR * ((mn * c1) / (jnp.sqrt(vn * c2) + ADAM_EPS) + ADAM_WD * wv)

    spec = pl.BlockSpec((ADAM_ROWS, FLAT_L), lambda i: (i, 0))
    shp = jax.ShapeDtypeStruct((rows_p, FLAT_L), F32)
    outs = pl.pallas_call(
        body, name=name, grid=(rows_p // ADAM_ROWS,),
        in_specs=[spec] * 4, out_specs=[spec] * 4, out_shape=[shp] * 4, compiler_params=_cparams(("parallel",)),
    )(flat2d(w), flat2d(m), flat2d(v), flat2d(g))

    def back(a):
        a = a.reshape(-1)
        if pad:
            a = a[:size]
        return a.reshape(shape)

    return tuple(back(a) for a in outs)


SEG_ROWS = 16


def _keeps_rows(shape):
    return len(shape) >= 2 and shape[-1] < FLAT_L and FLAT_L % shape[-1] != 0


def _seg_rows(shape):
    rows = math.prod(shape[:-1]) if _keeps_rows(shape) else -(-math.prod(shape) // FLAT_L)
    return -(-rows // SEG_ROWS) * SEG_ROWS


def _to_rows(a, dtype):
    rows = _seg_rows(a.shape)
    if _keeps_rows(a.shape):
        r = a.reshape(-1, a.shape[-1]).astype(dtype)
        return jnp.pad(r, ((0, rows - r.shape[0]), (0, FLAT_L - a.shape[-1])))
    flat = jnp.pad(a.reshape(-1).astype(dtype), (0, rows * FLAT_L - a.size))
    return flat.reshape(rows, FLAT_L)


def _from_rows(seg, shape):
    if _keeps_rows(shape):
        return seg[:math.prod(shape[:-1]), :shape[-1]].reshape(shape)
    return seg.reshape(-1)[:math.prod(shape)].reshape(shape)


def _pack(pieces, dtype, row_mult):
    segs, offs, r = [], [], 0
    for a in pieces:
        segs.append(_to_rows(a, dtype))
        offs.append(r)
        r += segs[-1].shape[0]
    tail = -r % row_mult
    if tail:
        segs.append(jnp.zeros((tail, FLAT_L), dtype))
    return jnp.concatenate(segs, axis=0), offs


def _segment(flat, off, shape):
    return _from_rows(flat[off:off + _seg_rows(shape)], shape)


def _gather_unit(shards, axes, dtype):
    flat, offs = _pack(shards, dtype, 64)
    plug = _gather_plug(flat)

    def unpack(full):
        return [jnp.concatenate([_segment(full[j], off, a.shape) for j in range(N_CHIPS)], axis=ax)
                for a, ax, off in zip(shards, axes, offs)]

    return plug, unpack


def _reduce_unit(name, grads, axes):
    pieces = [[] for _ in range(N_CHIPS)]
    for g, ax in zip(grads, axes):
        parts = jnp.split(g, N_CHIPS, axis=ax) if ax is not None else [g] * N_CHIPS
        for jc in range(N_CHIPS):
            pieces[jc].append(parts[jc])
    packed = [_pack(pieces[jc], BF16, 512) for jc in range(N_CHIPS)]
    gsend = jnp.stack([pk[0] for pk in packed])
    own, got = _sibling_split(name + "_split", gsend)
    pair = _add_pairs(name + "_add", own, got, BF16)
    return pair, packed[0][1], [a.shape for a in pieces[0]]


def _f_rms(row0, j, x, g):
    return _rms(x, g)


def _f_mid(row0, j, x, y, g_a, g_b):
    xn = x + _rms(y, g_a)
    return xn, _rms(xn, g_b)


def _f_resid(row0, j, x, y, g):
    return x + _rms(y, g)


def _relu2(u):
    r = jnp.maximum(u, 0.0)
    return r * r


def _relu2_bwd(d_act, act):
    return d_act * (2.0 * jnp.sqrt(act.astype(F32)))


def _f_l2silu(row0, j, c):
    a = _silu(c)
    return a * lax.rsqrt(jnp.sum(a * a, axis=-1, keepdims=True) + EPS)


def _f_silu(row0, j, c):
    return _silu(c)


def _f_scale(row0, j, y, s):
    return y * s


def _f_glu(row0, j, a, gate):
    return a * _sigmoid(gate)


def _f_lnsilu(row0, j, u, g, b):
    mu = jnp.mean(u, axis=-1, keepdims=True)
    uc = u - mu
    return _silu(uc * lax.rsqrt(jnp.mean(uc * uc, axis=-1, keepdims=True) + EPS) * g + b)


def _f_outgate(row0, j, o, z, g):
    return _rms(o, g) * _silu(z)


def _make_gates(H):
    def f(row0, j, ba, alog, dt):
        lane = lax.broadcasted_iota(jnp.int32, ba.shape, 1)
        beta = _sigmoid(ba)
        g = -jnp.exp(alog) * _softplus(ba + dt)
        return jnp.where(lane < H, beta, jnp.where(lane < 2 * H, g, 0.0))
    return f


def _f_loss(row0, j, y, t):
    e = y - t
    loss = 0.5 * jnp.sum(jnp.mean(e * e, axis=-1, keepdims=True), axis=0, keepdims=True)
    return e * (1.0 / y.shape[-1]), jnp.broadcast_to(loss, (1, LANE))


def _lane_row(vec, start):
    return jnp.pad(vec.astype(F32)[None, :], ((0, 0), (start, LANE - start - vec.shape[0])))


def kernel(x, norm_mix_pre, norm_mix_post, norm_mlp_pre, norm_mlp_post, even_w_in, even_conv, even_a_log, even_dt_bias, even_dn_norm, even_pool_w, even_pool_scale, even_w_out, odd_w_in, odd_dw, odd_dw_b, odd_ln_g, odd_ln_b, odd_w_out, mlp_w_up, mlp_w_down, loss_target, m_norm_mix_pre, m_norm_mix_post, m_norm_mlp_pre, m_norm_mlp_post, m_even_w_in, m_even_conv, m_even_a_log, m_even_dt_bias, m_even_dn_norm, m_even_pool_w, m_even_pool_scale, m_even_w_out, m_odd_w_in, m_odd_dw, m_odd_dw_b, m_odd_ln_g, m_odd_ln_b, m_odd_w_out, m_mlp_w_up, m_mlp_w_down, v_norm_mix_pre, v_norm_mix_post, v_norm_mlp_pre, v_norm_mlp_post, v_even_w_in, v_even_conv, v_even_a_log, v_even_dt_bias, v_even_dn_norm, v_even_pool_w, v_even_pool_scale, v_even_w_out, v_odd_w_in, v_odd_dw, v_odd_dw_b, v_odd_ln_g, v_odd_ln_b, v_odd_w_out, v_mlp_w_up, v_mlp_w_down):
    names = ["norm_mix_pre", "norm_mix_post", "norm_mlp_pre", "norm_mlp_post", "even_w_in", "even_conv", "even_a_log",
             "even_dt_bias", "even_dn_norm", "even_pool_w", "even_pool_scale", "even_w_out", "odd_w_in", "odd_dw",
             "odd_dw_b", "odd_ln_g", "odd_ln_b", "odd_w_out", "mlp_w_up", "mlp_w_down"]
    W = dict(zip(names, (norm_mix_pre, norm_mix_post, norm_mlp_pre, norm_mlp_post, even_w_in, even_conv, even_a_log,
                         even_dt_bias, even_dn_norm, even_pool_w, even_pool_scale, even_w_out, odd_w_in, odd_dw,
                         odd_dw_b, odd_ln_g, odd_ln_b, odd_w_out, mlp_w_up, mlp_w_down)))
    Mo = dict(zip(names, (m_norm_mix_pre, m_norm_mix_post, m_norm_mlp_pre, m_norm_mlp_post, m_even_w_in, m_even_conv,
                          m_even_a_log, m_even_dt_bias, m_even_dn_norm, m_even_pool_w, m_even_pool_scale, m_even_w_out,
                          m_odd_w_in, m_odd_dw, m_odd_dw_b, m_odd_ln_g, m_odd_ln_b, m_odd_w_out, m_mlp_w_up,
                          m_mlp_w_down)))
    Vo = dict(zip(names, (v_norm_mix_pre, v_norm_mix_post, v_norm_mlp_pre, v_norm_mlp_post, v_even_w_in, v_even_conv,
                          v_even_a_log, v_even_dt_bias, v_even_dn_norm, v_even_pool_w, v_even_pool_scale, v_even_w_out,
                          v_odd_w_in, v_odd_dw, v_odd_dw_b, v_odd_ln_g, v_odd_ln_b, v_odd_w_out, v_mlp_w_up,
                          v_mlp_w_down)))
    shard_axis = {"even_w_in": 2, "even_conv": 2, "even_pool_w": 2, "even_w_out": 1, "odd_w_in": 2, "odd_dw": 2,
                  "odd_dw_b": 1, "odd_ln_g": 1, "odd_ln_b": 1, "odd_w_out": 1, "mlp_w_up": 2, "mlp_w_down": 1}

    S, D = x.shape[1], x.shape[2]
    depth = norm_mix_pre.shape[0]
    H = even_a_log.shape[1]
    dh = even_dn_norm.shape[1]
    DNW = H * dh
    PW = even_pool_scale.shape[1]
    G = len(POOL_WINDOWS)
    PG = PW // G
    KC = even_conv.shape[1]
    BAW = 2 * LANE
    P_COLS = 4 * DNW + PW + BAW
    x2 = x.reshape(S, D)
    tgt = loss_target.reshape(S, D)

    small = ["even_conv", "odd_dw", "odd_dw_b", "odd_ln_g", "odd_ln_b"]
    plug_s, unpack_s = _gather_unit([W[n] for n in small], [shard_axis[n] for n in small], F32)
    full = dict(zip(small, unpack_s(plug_s.after(_run_plug("gather_small", plug_s)))))
    CW = odd_w_out.shape[1] * N_CHIPS
    wfull = {}

    def mixer_weights(i):
        return [(n, i // 2) for n in (("even_w_in", "even_pool_w", "even_w_out") if i % 2 == 0 else ("odd_w_in", "odd_w_out"))]

    riders = []

    def hosted(fn, *a, **k):
        if not riders:
            return fn(*a, **k)
        plug, done = riders.pop(0)
        outs, landed = fn(*a, comm=[plug], **k)
        done(plug.after(landed))
        return outs

    def queue_gather(keys):
        plug, unpack = _gather_unit([W[n][l] for n, l in keys], [shard_axis[n] - 1 for n, _ in keys], BF16)
        riders.append((plug, lambda full_: wfull.update(zip(keys, unpack(full_)))))

    def weight(key):
        while key not in wfull:
            plug, done = riders.pop(0)
            done(plug.after(_run_plug(f"alone_{len(wfull)}_{len(riders)}", plug)))
        return wfull[key]

    queue_gather(mixer_weights(0))
    for i in range(depth):
        queue_gather([("mlp_w_up", i)])
        queue_gather([("mlp_w_down", i)])
        if i + 1 < depth:
            queue_gather(mixer_weights(i + 1))

    def even_w_in_layout(w):
        o1 = 4 * DNW
        return jnp.concatenate([w[:, :o1], w[:, o1 + 2 * H:], w[:, o1:o1 + 2 * H],
                                jnp.zeros((w.shape[0], BAW - 2 * H), w.dtype)], axis=1)

    def even_w_in_unlayout(g):
        o1 = 4 * DNW
        return jnp.concatenate([g[:, :o1], g[:, o1 + PW:o1 + PW + 2 * H], g[:, o1:o1 + PW]], axis=1)

    def pool_blockdiag(pw):
        return jnp.concatenate([jnp.pad(pw[gi], ((0, 0), (gi * PG, PW - (gi + 1) * PG))) for gi in range(G)], axis=0)

    pool_taps = max(POOL_WINDOWS)
    tap = jnp.arange(pool_taps)[:, None]
    win_c = jnp.repeat(jnp.asarray(POOL_WINDOWS, F32), PG)[None, :]
    pool_mask = (tap >= pool_taps - win_c).astype(F32)

    grads = {}
    tr_full = 128 if D > 1024 else 256

    saved = []
    xc = x2
    for i in range(depth):
        jl = i // 2
        sv = {"x_in": xc}
        g1, g2, g3, g4 = (W[n][i:i + 1] for n in ("norm_mix_pre", "norm_mix_post", "norm_mlp_pre", "norm_mlp_post"))
        (h,) = _rowwise(f"l{i}_rms_in", _f_rms, [(xc, 0, D)], [(g1, None, D)], [(D, BF16)], S=S, tr=tr_full)
        sv["h"] = h
        if i % 2 == 0:
            w_in = even_w_in_layout(weight(("even_w_in", jl)))
            p = hosted(_matmul, f"l{i}_w_in", h, w_in, "nn", tn=768)
            conv_w = full["even_conv"][jl]
            c = _dwconv_fwd(f"l{i}_conv", p, 0, conv_w, S=S, C=3 * DNW)
            (qk,) = _rowwise(f"l{i}_qk", _f_l2silu, [(c, 0, dh)], [], [(dh, F32)], S=S, ncb=2 * H, tr=1024)
            (vv,) = _rowwise(f"l{i}_v", _f_silu, [(c, 2 * DNW // dh, dh)], [], [(dh, F32)], S=S, ncb=H, tr=1024)
            alog = _lane_row(W["even_a_log"][jl], H)
            dtb = _lane_row(W["even_dt_bias"][jl], H)
            ba_off = (4 * DNW + PW) // LANE
            (gb,) = _rowwise(f"l{i}_gates", _make_gates(H), [(p, ba_off, LANE)], [(alog, None, LANE), (dtb, None, LANE)],
                             [(LANE, F32)], S=S, tr=1024)
            Cn = min(DN_CHUNK, S)
            gT = gb[:, H:2 * H].reshape(S // Cn, Cn, H).transpose(0, 2, 1)
            o, sp, Tm = hosted(_delta_fwd, f"l{i}_delta", qk, vv, gb, gT, S=S, H=H, dh=dh)
            dn = W["even_dn_norm"][jl][None, :]
            (on,) = _rowwise(f"l{i}_outgate", _f_outgate, [(o, 0, dh), (p, 3 * DNW // dh, dh)], [(dn, None, dh)],
                             [(dh, BF16)], S=S, ncb=H, tr=1024)
            pcb = _pick(PW, 512)
            pooled = _dwconv_fwd(f"l{i}_pool", p, 4 * DNW // pcb, pool_mask, S=S, C=PW, win=win_c, out_dtype=BF16, cb=pcb)
            wbd = pool_blockdiag(weight(("even_pool_w", jl)))
            ypre = _matmul(f"l{i}_pool_w", pooled, wbd, "nn")
            psc = W["even_pool_scale"][jl][None, :]
            (ypool,) = _rowwise(f"l{i}_pool_scale", _f_scale, [(ypre, 0, PW)], [(psc, None, PW)], [(PW, BF16)], S=S)
            mixin = jnp.concatenate([on, ypool], axis=1)
            mix = _matmul(f"l{i}_w_out", mixin, weight(("even_w_out", jl)), "nn")
            sv.update(p=p, c=c, qk=qk, v=vv, gb=gb, gT=gT, o=o, sp=sp, Tm=Tm, pooled=pooled, ypre=ypre, mixin=mixin,
                      w_in=w_in, wbd=wbd, alog=alog, dtb=dtb, dn=dn, psc=psc, conv_w=conv_w)
        else:
            p = hosted(_matmul, f"l{i}_w_in", h, weight(("odd_w_in", jl)), "nn")
            ocb = _pick(CW, 1024)
            (u0,) = _rowwise(f"l{i}_glu", _f_glu, [(p, 0, ocb), (p, CW // ocb, ocb)], [], [(ocb, F32)], S=S,
                             ncb=CW // ocb)
            dw_w, dw_b = full["odd_dw"][jl], full["odd_dw_b"][jl][None, :]
            u1 = hosted(_dwconv_fwd, f"l{i}_dwconv", u0, 0, dw_w, S=S, C=CW, bias=dw_b)
            lg, lb = full["odd_ln_g"][jl][None, :], full["odd_ln_b"][jl][None, :]
            (u2,) = _rowwise(f"l{i}_lnsilu", _f_lnsilu, [(u1, 0, CW)], [(lg, None, CW), (lb, None, CW)], [(CW, BF16)],
                             S=S, tr=tr_full)
            mix = _matmul(f"l{i}_w_out", u2, weight(("odd_w_out", jl)), "nn")
            sv.update(p=p, u0=u0, u1=u1, mixin=u2, dw_w=dw_w, lg=lg, lb=lb)
        x_mid, h2 = _rowwise(f"l{i}_mid", _f_mid, [(xc, 0, D), (mix, 0, D)], [(g2, None, D), (g3, None, D)],
                             [(D, F32), (D, BF16)], S=S, tr=tr_full)
        act = hosted(_matmul, f"l{i}_w_up", h2, weight(("mlp_w_up", i)), "nn", epi=_relu2, out_dtypes=[BF16])
        ff = hosted(_matmul, f"l{i}_w_down", act, weight(("mlp_w_down", i)), "nn")
        (x_out,) = _rowwise(f"l{i}_out", _f_resid, [(x_mid, 0, D), (ff, 0, D)], [(g4, None, D)], [(D, F32)], S=S,
                            tr=tr_full)
        sv.update(mix=mix, h2=h2, act=act, ff=ff, g=(g1, g2, g3, g4))
        saved.append(sv)
        xc = x_out

    dy, loss_row = _rowwise("loss", _f_loss, [(xc, 0, D), (tgt, 0, D)], [], [(D, F32)], [(1, LANE, False)], S=S,
                            tr=tr_full)
    loss = lax.psum(loss_row[0, 0], ("x", "y", "c"))

    def mixer_params(i):
        ns = (("even_w_in", "even_conv", "even_a_log", "even_dt_bias", "even_dn_norm", "even_pool_w", "even_pool_scale",
               "even_w_out") if i % 2 == 0 else ("odd_w_in", "odd_dw", "odd_dw_b", "odd_ln_g", "odd_ln_b", "odd_w_out"))
        return [(n, i // 2) for n in ns] + [("norm_mix_pre", i)]

    def mlp_params(i):
        return [(n, i) for n in ("mlp_w_up", "mlp_w_down", "norm_mlp_pre", "norm_mlp_post", "norm_mix_post")]

    units = []

    def queue_reduce(name, keys):
        axes = [shard_axis[n] - 1 if n in shard_axis else None for n, _ in keys]
        pair, offs, shapes = _reduce_unit(name, [grads[k] for k in keys], axes)
        parts = []
        units.append((keys, offs, shapes, parts))
        for t, (r0, n) in enumerate(_row_chunks(pair.shape[1], EXCHANGE_PARTS)):
            plug = _exchange_plug(pair, r0, n)
            riders.append((plug, lambda q, t=t: parts.append((t, _sum_chips(f"{name}_sum{t}", q)))))

    dx = dy
    for i in reversed(range(depth)):
        jl = i // 2
        sv = saved[i]
        g1, g2, g3, g4 = sv["g"]
        d_ff, dg4 = _rowwise_bwd(f"l{i}_out_b", _f_rms, [(sv["ff"], 0, D)], [(g4, None, D)], [(dx, 0, D)], [BF16],
                                 S=S, tr=tr_full)
        grads["norm_mlp_post", i] = dg4[0]
        du = hosted(_matmul, f"l{i}_w_down_bx", d_ff, wfull["mlp_w_down", i], "nt", epi=_relu2_bwd, extras=[sv["act"]],
                    out_dtypes=[BF16])
        grads["mlp_w_down", i] = hosted(_matmul, f"l{i}_w_down_bw", sv["act"], d_ff, "tn", out_dtype=BF16)
        dh2 = hosted(_matmul, f"l{i}_w_up_bx", du, wfull["mlp_w_up", i], "nt")
        grads["mlp_w_up", i] = hosted(_matmul, f"l{i}_w_up_bw", sv["h2"], du, "tn", out_dtype=BF16)
        dx, d_mix, dg2, dg3 = _rowwise_bwd(
            f"l{i}_mid_b", _f_mid, [(sv["x_in"], 0, D), (sv["mix"], 0, D)], [(g2, None, D), (g3, None, D)],
            [(dx, 0, D), (dh2, 0, D)], [F32, BF16], S=S, tr=tr_full)
        grads["norm_mix_post", i], grads["norm_mlp_pre", i] = dg2[0], dg3[0]
        queue_reduce(f"red{i}", (mixer_params(i + 1) if i + 1 < depth else []) + mlp_params(i))
        if i % 2 == 0:
            d_mixin = _matmul(f"l{i}_w_out_bx", d_mix, wfull["even_w_out", jl], "nt")
            grads["even_w_out", jl] = _matmul(f"l{i}_w_out_bw", sv["mixin"], d_mix, "tn", out_dtype=BF16)
            p = sv["p"]
            pcb = _pick(PW, 512)
            d_ypre, dpsc = _rowwise_bwd(f"l{i}_pool_scale_b", _f_scale, [(sv["ypre"], 0, PW)], [(sv["psc"], None, PW)],
                                        [(d_mixin, DNW // PW, PW)], [BF16], S=S)
            grads["even_pool_scale", jl] = dpsc[0]
            d_pooled = _matmul(f"l{i}_pool_w_bx", d_ypre, sv["wbd"], "nt")
            dwbd = _matmul(f"l{i}_pool_w_bw", sv["pooled"], d_ypre, "tn")
            grads["even_pool_w", jl] = jnp.stack([dwbd[gi * PG:(gi + 1) * PG, gi * PG:(gi + 1) * PG] for gi in range(G)])
            d_xp = _dwconv_bwd(f"l{i}_pool_b", None, 0, d_pooled, pool_mask, S=S, C=PW, win=win_c, want_dw=False,
                               cb=pcb)[0]
            d_o, d_z, ddn = _rowwise_bwd(f"l{i}_outgate_b", _f_outgate, [(sv["o"], 0, dh), (p, 3 * DNW // dh, dh)],
                                         [(sv["dn"], None, dh)], [(d_mixin, 0, dh)], [F32, F32], S=S, ncb=H, tr=1024)
            grads["even_dn_norm", jl] = ddn[0]
            dqk, dv, dgb = hosted(_delta_bwd, f"l{i}_delta_b", sv["qk"], sv["v"], sv["gb"], sv["gT"], sv["sp"],
                                  sv["Tm"], d_o, S=S, H=H, dh=dh)
            ba_off = (4 * DNW + PW) // LANE
            d_ba, dalog, ddtb = _rowwise_bwd(f"l{i}_gates_b", _make_gates(H), [(p, ba_off, LANE)],
                                             [(sv["alog"], None, LANE), (sv["dtb"], None, LANE)], [(dgb, 0, LANE)],
                                             [F32], S=S, tr=1024)
            grads["even_a_log", jl], grads["even_dt_bias", jl] = dalog[0, H:2 * H], ddtb[0, H:2 * H]
            (dc_qk,) = _rowwise_bwd(f"l{i}_qk_b", _f_l2silu, [(sv["c"], 0, dh)], [], [(dqk, 0, dh)], [F32], S=S,
                                    ncb=2 * H, tr=1024)
            (dc_v,) = _rowwise_bwd(f"l{i}_v_b", _f_silu, [(sv["c"], 2 * DNW // dh, dh)], [], [(dv, 0, dh)], [F32], S=S,
                                   ncb=H, tr=1024)
            dc = jnp.concatenate([dc_qk, dc_v], axis=1)
            d_qkv, dconv, _ = _dwconv_bwd(f"l{i}_conv_b", p, 0, dc, sv["conv_w"], S=S, C=3 * DNW)
            grads["even_conv", jl] = dconv
            dp = jnp.concatenate([d_qkv.astype(BF16), d_z.astype(BF16), d_xp.astype(BF16), d_ba.astype(BF16),
                                  jnp.zeros((S, BAW - LANE), BF16)], axis=1)
            dh_ = hosted(_matmul, f"l{i}_w_in_bx", dp, sv["w_in"], "nt", tk=768)
            grads["even_w_in", jl] = even_w_in_unlayout(hosted(_matmul, f"l{i}_w_in_bw", sv["h"], dp, "tn", tn=768,
                                                               out_dtype=BF16))
        else:
            d_u2 = _matmul(f"l{i}_w_out_bx", d_mix, wfull["odd_w_out", jl], "nt")
            grads["odd_w_out", jl] = _matmul(f"l{i}_w_out_bw", sv["mixin"], d_mix, "tn", out_dtype=BF16)
            d_u1, dlg, dlb = _rowwise_bwd(f"l{i}_lnsilu_b", _f_lnsilu, [(sv["u1"], 0, CW)],
                                          [(sv["lg"], None, CW), (sv["lb"], None, CW)], [(d_u2, 0, CW)], [F32], S=S,
                                          tr=tr_full)
            grads["odd_ln_g", jl], grads["odd_ln_b", jl] = dlg[0], dlb[0]
            d_u0, ddw, ddb = hosted(_dwconv_bwd, f"l{i}_dwconv_b", sv["u0"], 0, d_u1, sv["dw_w"], S=S, C=CW)
            grads["odd_dw", jl], grads["odd_dw_b", jl] = ddw, ddb[0]
            p = sv["p"]
            ocb = _pick(CW, 1024)
            da, dgate = _rowwise_bwd(f"l{i}_glu_b", _f_glu, [(p, 0, ocb), (p, CW // ocb, ocb)], [], [(d_u0, 0, ocb)],
                                     [BF16, BF16], S=S, ncb=CW // ocb)
            dp = jnp.concatenate([da, dgate], axis=1)
            dh_ = hosted(_matmul, f"l{i}_w_in_bx", dp, wfull["odd_w_in", jl], "nt")
            grads["odd_w_in", jl] = hosted(_matmul, f"l{i}_w_in_bw", sv["h"], dp, "tn", out_dtype=BF16)
        dx, dg1 = _rowwise_bwd(f"l{i}_rms_in_b", _f_rms, [(sv["x_in"], 0, D)], [(g1, None, D)], [(dh_, 0, D)], [F32],
                               adds=[(dx, 0, D)], S=S, tr=tr_full)
        grads["norm_mix_pre", i] = dg1[0]
    grad_x = dx.reshape(x.shape)
    queue_reduce("red_last", mixer_params(0))
    while riders:
        plug, done = riders.pop(0)
        done(plug.after(_run_plug(f"alone_last_{len(riders)}", plug)))

    halves = [jnp.concatenate([q for _, q in sorted(parts, key=lambda tq: tq[0])], axis=0) for *_, parts in units]
    theirs = _sibling_swap("grad_sibling_swap", jnp.concatenate(halves, axis=0))
    south = lax.axis_index("c") == 0
    gshard, r = {}, 0
    for (keys, offs, shapes, _), half in zip(units, halves):
        rh = half.shape[0]
        other = theirs[r:r + rh]
        whole = jnp.concatenate([jnp.where(south, half, other), jnp.where(south, other, half)], axis=0)
        for k, off, shp in zip(keys, offs, shapes):
            gshard[k] = _segment(whole, off, shp)
        r += rh

    outs_g, outs_d, outs_m, outs_v = [], [], [], []
    for n in names:
        g_n = jnp.stack([gshard[n, l] for l in range(W[n].shape[0])])
        g_o, d_o, m_o, v_o = _adamw(f"adamw_{n}", W[n], Mo[n], Vo[n], g_n)
        outs_g.append(g_o)
        outs_d.append(d_o)
        outs_m.append(m_o)
        outs_v.append(v_o)
    return (loss, grad_x, *outs_g, *outs_d, *outs_m, *outs_v)
```

```python
import functools
import math

import jax
import jax.numpy as jnp
from jax import lax
from jax.experimental import pallas as pl
from jax.experimental.pallas import tpu as pltpu

F32 = jnp.float32
BF16 = jnp.bfloat16
EPS = 1e-6
DN_CHUNK = 64
POOL_WINDOWS = (2, 4, 8, 16)
ADAM_LR, ADAM_B1, ADAM_B2, ADAM_EPS, ADAM_WD, ADAM_STEP = 0.001, 0.9, 0.999, 1e-08, 0.01, 10
LANE = 128
FLAT_L = 2048
ADAM_ROWS = 128
VMEM_LIMIT = 56 * 1024 * 1024
N_CHIPS = 4
HI = lax.Precision.HIGHEST
MESH = pl.DeviceIdType.MESH


def _cparams(sem):
    return pltpu.CompilerParams(dimension_semantics=sem, vmem_limit_bytes=VMEM_LIMIT)


def _pallas(body, args, *, name, grid, in_specs, out_specs, out_shape, scratch_shapes=(), sem=None, comm=()):
    n_in, n_out, n_scr = len(in_specs), len(out_specs), len(scratch_shapes)
    if not comm:
        return pl.pallas_call(body, name=name, grid=grid, in_specs=in_specs, out_specs=out_specs, out_shape=out_shape,
                              scratch_shapes=list(scratch_shapes), compiler_params=_cparams(sem))(*args)
    ci = [len(p.ins) for p in comm]
    co = [len(p.outs) for p in comm]
    cs = [len(p.sems) for p in comm]

    def wrapped(*refs):
        ins, pos = refs[:n_in], n_in
        cins = refs[pos:pos + sum(ci)]
        pos += sum(ci)
        outs = refs[pos:pos + n_out]
        pos += n_out
        couts = refs[pos:pos + sum(co)]
        pos += sum(co)
        scr = refs[pos:pos + n_scr]
        csems = refs[pos + n_scr:]
        ids = [pl.program_id(a) for a in range(len(grid))]
        first, last = ids[0] == 0, ids[0] == grid[0] - 1
        for a in range(1, len(grid)):
            first = jnp.logical_and(first, ids[a] == 0)
            last = jnp.logical_and(last, ids[a] == grid[a] - 1)
        parts, a, b, c = [], 0, 0, 0
        for p, na, nb, nc in zip(comm, ci, co, cs):
            parts.append((p, cins[a:a + na], couts[b:b + nb], csems[c:c + nc]))
            a, b, c = a + na, b + nb, c + nc

        @pl.when(first)
        def _():
            for p, pi, po, ps in parts:
                p.start(pi, po, ps)

        body(*ins, *outs, *scr)

        @pl.when(last)
        def _():
            for p, pi, po, ps in parts:
                p.finish(pi, po, ps)

    any_spec = pl.BlockSpec(memory_space=pl.ANY)
    res = pl.pallas_call(
        wrapped, name=name, grid=grid,
        in_specs=list(in_specs) + [any_spec] * sum(ci), out_specs=list(out_specs) + [any_spec] * sum(co),
        out_shape=list(out_shape) + [s for p in comm for s in p.outs],
        scratch_shapes=list(scratch_shapes) + [s for p in comm for s in p.sems],
        compiler_params=_cparams(("arbitrary",) * len(grid)),
    )(*args, *[a for p in comm for a in p.ins])
    return res


def _pick(dim, target, mult=LANE):
    if dim <= target:
        return dim
    t = (target // mult) * mult
    while t >= mult:
        if dim % t == 0:
            return t
        t -= mult
    return dim


def _sigmoid(x):
    return 1.0 / (1.0 + jnp.exp(-x))


def _silu(x):
    return x * _sigmoid(x)


def _softplus(x):
    return jnp.maximum(x, 0.0) + jnp.log(1.0 + jnp.exp(-jnp.abs(x)))


def _rms(x, g):
    return x * lax.rsqrt(jnp.mean(x * x, axis=-1, keepdims=True) + EPS) * g


def _matmul(name, a, b, mode, out_dtype=F32, tm=1024, tn=1024, tk=2048, epi=None, extras=(), out_dtypes=None, comm=()):
    if mode == "nn":
        (M, K), (K2, N) = a.shape, b.shape
    elif mode == "nt":
        (M, K), (N, K2) = a.shape, b.shape
    else:
        (K, M), (K2, N) = a.shape, b.shape
    assert K == K2, (name, a.shape, b.shape, mode)
    tm, tn, tk = _pick(M, tm), _pick(N, tn), _pick(K, tk)
    nk = K // tk
    if mode == "nn":
        a_spec = pl.BlockSpec((tm, tk), lambda i, j, k: (i, k))
        b_spec = pl.BlockSpec((tk, tn), lambda i, j, k: (k, j))
        dims = (((1,), (0,)), ((), ()))
    elif mode == "nt":
        a_spec = pl.BlockSpec((tm, tk), lambda i, j, k: (i, k))
        b_spec = pl.BlockSpec((tn, tk), lambda i, j, k: (j, k))
        dims = (((1,), (1,)), ((), ()))
    else:
        a_spec = pl.BlockSpec((tk, tm), lambda i, j, k: (k, i))
        b_spec = pl.BlockSpec((tk, tn), lambda i, j, k: (k, j))
        dims = (((0,), (0,)), ((), ()))
    out_dtypes = list(out_dtypes) if out_dtypes is not None else [out_dtype]
    ne, no = len(extras), len(out_dtypes)
    in_place = epi is None and out_dtypes == [F32]
    use_acc = nk > 1 and not in_place

    def finish(acc, extra_refs, out_refs):
        res = acc if epi is None else epi(acc, *[r[...] for r in extra_refs])
        res = res if isinstance(res, (tuple, list)) else (res,)
        for r, v in zip(out_refs, res):
            r[...] = v.astype(r.dtype)

    def body(a_ref, b_ref, *rest):
        extra_refs, out_refs = rest[:ne], rest[ne:ne + no]
        part = lax.dot_general(a_ref[...].astype(BF16), b_ref[...].astype(BF16), dims, preferred_element_type=F32)
        if nk == 1:
            finish(part, extra_refs, out_refs)
            return
        k = pl.program_id(2)
        acc_ref = rest[-1] if use_acc else out_refs[0]

        @pl.when(k == 0)
        def _():
            acc_ref[...] = part

        @pl.when(k > 0)
        def _():
            acc_ref[...] += part

        if use_acc:
            @pl.when(k == nk - 1)
            def _():
                finish(acc_ref[...], extra_refs, out_refs)

    o_spec = pl.BlockSpec((tm, tn), lambda i, j, k: (i, j))
    res = _pallas(body, (a, b, *extras), name=name, grid=(M // tm, N // tn, nk),
                  in_specs=[a_spec, b_spec] + [o_spec] * ne, out_specs=[o_spec] * no,
                  out_shape=[jax.ShapeDtypeStruct((M, N), dt) for dt in out_dtypes],
                  scratch_shapes=[pltpu.VMEM((tm, tn), F32)] if use_acc else [],
                  sem=("parallel", "parallel", "arbitrary"), comm=comm)
    outs = res[0] if no == 1 else res[:no]
    return (outs, res[no:]) if comm else outs


def _row_spec(tr, C, off):
    return pl.BlockSpec((tr, C), lambda j, i: (i, off + j))


def _par_spec(k, C, off):
    if off is None:
        return pl.BlockSpec((k, C), lambda j, i: (0, 0))
    return pl.BlockSpec((k, C), lambda j, i: (0, off + j))


def _rowwise(name, fn, rows, params, outs, reds=(), *, S, ncb=1, tr=256):
    tr = min(tr, S)
    nr, npar, no = len(rows), len(params), len(outs)

    def body(*refs):
        j, i = pl.program_id(0), pl.program_id(1)
        ins = [r[...].astype(F32) for r in refs[:nr + npar]]
        res = fn(i * tr, j, *ins)
        res = res if isinstance(res, (tuple, list)) else (res,)
        out_refs = refs[nr + npar:]
        for r, v in zip(out_refs[:no], res[:no]):
            r[...] = v.astype(r.dtype)
        for (k, C, per_j), r, v in zip(reds, out_refs[no:], res[no:]):
            first = (i == 0) if per_j else jnp.logical_and(i == 0, j == 0)

            @pl.when(first)
            def _(r=r):
                r[...] = jnp.zeros_like(r)

            r[...] += v

    in_specs = [_row_spec(tr, C, off) for (_, off, C) in rows] + [_par_spec(a.shape[0], C, off) for (a, off, C) in params]
    out_specs = [pl.BlockSpec((tr, C), lambda j, i: (i, j)) for (C, _) in outs]
    out_specs += [pl.BlockSpec((k, C), (lambda j, i: (0, j)) if per_j else (lambda j, i: (0, 0))) for (k, C, per_j) in reds]
    out_shape = [jax.ShapeDtypeStruct((S, ncb * C), dt) for (C, dt) in outs]
    out_shape += [jax.ShapeDtypeStruct((k, C * (ncb if per_j else 1)), F32) for (k, C, per_j) in reds]
    res = pl.pallas_call(
        body, name=name, grid=(ncb, S // tr), in_specs=in_specs, out_specs=out_specs, out_shape=out_shape,
        compiler_params=_cparams(("arbitrary", "arbitrary")),
    )(*[a for (a, _, _) in rows], *[a for (a, _, _) in params])
    return res


def _rowwise_bwd(name, fn, rows, params, cots, drow, adds=None, *, S, ncb=1, tr=128):
    tr = min(tr, S)
    nr, npar, nc = len(rows), len(params), len(cots)
    adds = adds or [None] * nr
    add_list = [a for a in adds if a is not None]
    na = len(add_list)

    def body(*refs):
        j, i = pl.program_id(0), pl.program_id(1)
        ins = [r[...].astype(F32) for r in refs[:nr + npar]]
        cts = [r[...].astype(F32) for r in refs[nr + npar:nr + npar + nc]]
        add_refs = list(refs[nr + npar + nc:nr + npar + nc + na])
        out_refs = list(refs[nr + npar + nc + na:])

        def f(*a):
            res = fn(i * tr, j, *a)
            return tuple(res) if isinstance(res, (tuple, list)) else (res,)

        _, vjp = jax.vjp(f, *ins)
        grads = vjp(tuple(cts))
        for idx in range(nr):
            if drow[idx] is None:
                continue
            g = grads[idx]
            if adds[idx] is not None:
                g = g + add_refs.pop(0)[...].astype(F32)
            r = out_refs.pop(0)
            r[...] = g.astype(r.dtype)
        for idx in range(npar):
            per_j = params[idx][1] is not None
            first = (i == 0) if per_j else jnp.logical_and(i == 0, j == 0)
            r = out_refs.pop(0)

            @pl.when(first)
            def _(r=r):
                r[...] = jnp.zeros_like(r)

            r[...] += grads[nr + idx]

    in_specs = [_row_spec(tr, C, off) for (_, off, C) in rows]
    in_specs += [_par_spec(a.shape[0], C, off) for (a, off, C) in params]
    in_specs += [_row_spec(tr, C, off) for (_, off, C) in cots]
    in_specs += [_row_spec(tr, C, off) for (_, off, C) in add_list]
    out_specs, out_shape = [], []
    for idx in range(nr):
        if drow[idx] is not None:
            C = rows[idx][2]
            out_specs.append(pl.BlockSpec((tr, C), lambda j, i: (i, j)))
            out_shape.append(jax.ShapeDtypeStruct((S, ncb * C), drow[idx]))
    for (a, off, C) in params:
        per_j = off is not None
        out_specs.append(pl.BlockSpec((a.shape[0], C), (lambda j, i: (0, j)) if per_j else (lambda j, i: (0, 0))))
        out_shape.append(jax.ShapeDtypeStruct((a.shape[0], C * (ncb if per_j else 1)), F32))
    return pl.pallas_call(
        body, name=name, grid=(ncb, S // tr), in_specs=in_specs, out_specs=out_specs, out_shape=out_shape,
        compiler_params=_cparams(("arbitrary", "arbitrary")),
    )(*[a for (a, _, _) in rows], *[a for (a, _, _) in params], *[a for (a, _, _) in cots], *[a for (a, _, _) in add_list])


def _halo_rows(K):
    return 8 * ((K - 1 + 7) // 8)


def _inv_count(row0, tr, win):
    t = (row0 + lax.broadcasted_iota(jnp.int32, (tr, 1), 0)).astype(F32)
    return 1.0 / jnp.minimum(t + 1.0, win)


CONV_ROWS = 32


def _shifted_down(xp, rows, K):
    for p in sorted({s % 8 for s in range(K)} - {0}):
        xp[p, 8:rows, :] = xp[0, 8 - p:rows - p, :]


def _shifted_up(yp, rows, K):
    for p in sorted({s % 8 for s in range(K)} - {0}):
        yp[p, 0:rows - 8, :] = yp[0, p:rows - 8 + p, :]


def _dwconv_fwd(name, x, x_off, w, *, S, C, bias=None, win=None, out_dtype=F32, cb=512, tr=256, comm=()):
    K = w.shape[0]
    cb, tr = _pick(C, cb), min(tr, S)
    HB = min(_halo_rows(K), tr)
    assert K - 1 <= HB and tr % HB == 0 and C % cb == 0
    nb = tr // HB
    RB = min(CONV_ROWS, tr)
    extra = [a for a in (bias, win) if a is not None]

    def body(xh_ref, x_ref, w_ref, *rest):
        y_ref, xp = rest[-2], rest[-1]
        i = pl.program_id(1)
        xp[0, 0:HB, :] = jnp.where(i > 0, xh_ref[...].astype(F32), 0.0)
        xp[0, HB:HB + tr, :] = x_ref[...].astype(F32)
        _shifted_down(xp, HB + tr, K)

        def sub(rb, carry):
            r0 = rb * RB
            acc = jnp.zeros((RB, cb), F32)
            for jj in range(K):
                s = K - 1 - jj
                start = pl.multiple_of(HB - 8 * (s // 8) + r0, 8)
                acc = acc + w_ref[jj:jj + 1, :] * xp[s % 8, pl.ds(start, RB), :]
            if bias is not None:
                acc = acc + rest[0][...]
            if win is not None:
                rows = pl.ds(pl.multiple_of(r0, 8), RB)
                acc = acc * _inv_count(i * tr + r0, RB, rest[0][...]) - x_ref[rows, :].astype(F32)
            y_ref[pl.ds(pl.multiple_of(r0, 8), RB), :] = acc.astype(y_ref.dtype)
            return carry

        lax.fori_loop(0, tr // RB, sub, 0)

    in_specs = [pl.BlockSpec((HB, cb), lambda j, i: (jnp.maximum(i * nb - 1, 0), x_off + j)),
                pl.BlockSpec((tr, cb), lambda j, i: (i, x_off + j)),
                pl.BlockSpec((K, cb), lambda j, i: (0, j))]
    in_specs += [pl.BlockSpec((1, cb), lambda j, i: (0, j)) for _ in extra]
    res = _pallas(body, (x, x, w, *extra), name=name, grid=(C // cb, S // tr), in_specs=in_specs,
                  out_specs=[pl.BlockSpec((tr, cb), lambda j, i: (i, j))],
                  out_shape=[jax.ShapeDtypeStruct((S, C), out_dtype)],
                  scratch_shapes=[pltpu.VMEM((8, HB + tr, cb), F32)], sem=("parallel", "arbitrary"), comm=comm)
    return (res[0], res[1:]) if comm else res[0]


def _dwconv_bwd(name, x, x_off, dy, w, *, S, C, win=None, want_dw=True, cb=512, tr=256, comm=()):
    K = w.shape[0]
    cb, tr = _pick(C, cb), min(tr, S)
    HB = min(_halo_rows(K), tr)
    nb, nt = tr // HB, S // tr
    RB = min(CONV_ROWS, tr)

    def body(*refs):
        if want_dw:
            xh_ref, x_ref, dy_ref, dyn_ref, w_ref = refs[:5]
            rest = refs[5:]
        else:
            dy_ref, dyn_ref, w_ref = refs[:3]
            rest = refs[3:]
        i = pl.program_id(1)
        dyt = dy_ref[...].astype(F32)
        dyn = jnp.where(i < nt - 1, dyn_ref[...].astype(F32), 0.0)
        if win is not None:
            win_v = rest[0][...]
            rest = rest[1:]
            yy_t = dyt * _inv_count(i * tr, tr, win_v)
            dyn = dyn * _inv_count((i + 1) * tr, HB, win_v)
        else:
            yy_t = dyt
        if want_dw:
            dx_ref, dw_ref, db_ref, yp, xp, dw8, db8 = rest
        else:
            dx_ref, yp = rest
        yp[0, 0:tr, :] = yy_t
        yp[0, tr:tr + HB, :] = dyn
        _shifted_up(yp, tr + HB, K)

        def sub(rb, carry):
            r0 = rb * RB
            acc = jnp.zeros((RB, cb), F32)
            for jj in range(K):
                s = K - 1 - jj
                start = pl.multiple_of(8 * (s // 8) + r0, 8)
                acc = acc + w_ref[jj:jj + 1, :] * yp[s % 8, pl.ds(start, RB), :]
            rows = pl.ds(pl.multiple_of(r0, 8), RB)
            if win is not None:
                acc = acc - dy_ref[rows, :].astype(F32)
            dx_ref[rows, :] = acc.astype(dx_ref.dtype)
            return carry

        lax.fori_loop(0, tr // RB, sub, 0)
        if want_dw:
            xp[0, 0:HB, :] = jnp.where(i > 0, xh_ref[...].astype(F32), 0.0)
            xp[0, HB:HB + tr, :] = x_ref[...].astype(F32)
            _shifted_down(xp, HB + tr, K)

            @pl.when(i == 0)
            def _():
                dw8[...] = jnp.zeros_like(dw8)
                db8[...] = jnp.zeros_like(db8)

            def sub_w(rb, carry):
                r0 = rb * RB
                dyb = dy_ref[pl.ds(pl.multiple_of(r0, 8), RB), :].astype(F32)
                for jj in range(K):
                    s = K - 1 - jj
                    start = pl.multiple_of(HB - 8 * (s // 8) + r0, 8)
                    prod = dyb * xp[s % 8, pl.ds(start, RB), :]
                    dw8[jj] += jnp.sum(prod.reshape(RB // 8, 8, cb), axis=0)
                db8[...] += jnp.sum(dyb.reshape(RB // 8, 8, cb), axis=0)
                return carry

            lax.fori_loop(0, tr // RB, sub_w, 0)

            @pl.when(i == nt - 1)
            def _():
                for jj in range(K):
                    dw_ref[jj:jj + 1, :] = jnp.sum(dw8[jj], axis=0, keepdims=True)
                db_ref[...] = jnp.sum(db8[...], axis=0, keepdims=True)

    last = S // HB - 1
    in_specs, args = [], []
    if want_dw:
        in_specs += [pl.BlockSpec((HB, cb), lambda j, i: (jnp.maximum(i * nb - 1, 0), x_off + j)),
                     pl.BlockSpec((tr, cb), lambda j, i: (i, x_off + j))]
        args += [x, x]
    in_specs += [pl.BlockSpec((tr, cb), lambda j, i: (i, j)),
                 pl.BlockSpec((HB, cb), lambda j, i: (jnp.minimum((i + 1) * nb, last), j)),
                 pl.BlockSpec((K, cb), lambda j, i: (0, j))]
    args += [dy, dy, w]
    if win is not None:
        in_specs.append(pl.BlockSpec((1, cb), lambda j, i: (0, j)))
        args.append(win)
    out_specs = [pl.BlockSpec((tr, cb), lambda j, i: (i, j))]
    out_shape = [jax.ShapeDtypeStruct((S, C), F32)]
    scratch = [pltpu.VMEM((8, tr + HB, cb), F32)]
    if want_dw:
        out_specs += [pl.BlockSpec((K, cb), lambda j, i: (0, j)), pl.BlockSpec((1, cb), lambda j, i: (0, j))]
        out_shape += [jax.ShapeDtypeStruct((K, C), F32), jax.ShapeDtypeStruct((1, C), F32)]
        scratch += [pltpu.VMEM((8, HB + tr, cb), F32), pltpu.VMEM((K, 8, cb), F32), pltpu.VMEM((8, cb), F32)]
    res = _pallas(body, args, name=name, grid=(C // cb, S // tr), in_specs=in_specs, out_specs=out_specs,
                  out_shape=out_shape, scratch_shapes=scratch, sem=("parallel", "arbitrary"), comm=comm)
    return (res[:len(out_specs)], res[len(out_specs):]) if comm else res


def _dot(a, b, dims, hi=False):
    if hi:
        return lax.dot_general(a, b, (dims, ((), ())), precision=HI, preferred_element_type=F32)
    return lax.dot_general(a.astype(BF16), b.astype(BF16), (dims, ((), ())), preferred_element_type=F32)


_NN, _NT, _TN = ((1,), (0,)), ((1,), (1,)), ((0,), (0,))


def _col(m, idx):
    lane = lax.broadcasted_iota(jnp.int32, m.shape, 1)
    return jnp.sum(jnp.where(lane == idx, m, 0.0), axis=1, keepdims=True)


def _row(m, idx):
    sub = lax.broadcasted_iota(jnp.int32, m.shape, 0)
    return jnp.sum(jnp.where(sub == idx, m, 0.0), axis=0, keepdims=True)


def _delta_chunk(q, k, v, beta, gcc, gcr, causal, strict, eye, scale, C):
    d = {}
    gam = jnp.where(causal, jnp.exp(jnp.where(causal, gcc - gcr, 0.0)), 0.0)
    eg = jnp.exp(gcc)
    g_last = _row(gcc, C - 1)
    d["gam"], d["eg"], d["g_last"] = gam, eg, g_last
    d["ek"] = jnp.exp(g_last - gcc)
    d["decay"] = jnp.exp(g_last)
    qs = q * scale
    kb = k * beta
    d["qs"], d["kb"] = qs, kb
    d["kk"] = _dot(kb, k, _NT)
    d["A"] = jnp.where(strict, d["kk"] * gam, 0.0)
    d["qk"] = _dot(qs, k, _NT)
    d["attn"] = jnp.where(causal, d["qk"] * gam, 0.0)
    d["vb"] = v * beta
    d["kbg"] = kb * eg
    d["qg"] = qs * eg
    d["kd"] = k * d["ek"]
    return d


def _split(m):
    hi = m.astype(BF16)
    return hi, (m - hi.astype(F32)).astype(BF16)


def _dot3(a, b, dims):
    return _dot(a[0], b[0], dims) + (_dot(a[0], b[1], dims) + _dot(a[1], b[0], dims))


def _tri_inverse(As, eye):
    P = [-A for A in As]
    T = [eye + p for p in P]
    n = 1
    while 2 * n < As[0].shape[0]:
        Ps = [_split(p) for p in P]
        P = [_dot3(ps, ps, _NN) for ps in Ps]
        Ts = [_split(t) for t in T]
        Ps = [_split(p) for p in P]
        T = [t + _dot3(ts, ps, _NN) for t, ts, ps in zip(T, Ts, Ps)]
        n *= 2
    return T


def _delta_fwd(name, qk, v, gb, gT, *, S, H, dh, comm=()):
    C = min(DN_CHUNK, S)
    N, W = S // C, H * dh
    scale = dh ** -0.5

    def body(qk_ref, v_ref, gb_ref, gT_ref, o_ref, sp_ref, T_ref, st):
        n = pl.program_id(0)

        @pl.when(n == 0)
        def _():
            st[...] = jnp.zeros_like(st)

        r = lax.broadcasted_iota(jnp.int32, (C, C), 0)
        c = lax.broadcasted_iota(jnp.int32, (C, C), 1)
        causal, strict = r >= c, r > c
        eye = (r == c).astype(F32)
        Lt = causal.astype(F32)
        gbv = gb_ref[...]
        gcum = _dot(Lt, gbv, _NN, hi=True)
        gcumT = _dot(gT_ref[0], Lt, _NT, hi=True)
        hs = range(H)
        sl = [slice(h * dh, (h + 1) * dh) for h in hs]
        d = [_delta_chunk(qk_ref[:, sl[h]], qk_ref[:, W + h * dh:W + (h + 1) * dh], v_ref[:, sl[h]], _col(gbv, h),
                          _col(gcum, H + h), _row(gcumT, h), causal, strict, eye, scale, C) for h in hs]
        T = _tri_inverse([d[h]["A"] for h in hs], eye)
        u = [_dot(T[h], d[h]["vb"], _NN) for h in hs]
        w = [_dot(T[h], d[h]["kbg"], _NN) for h in hs]
        s0 = [st[h] for h in hs]
        ws = [_dot(w[h], s0[h], _NN) for h in hs]
        qs0 = [_dot(d[h]["qg"], s0[h], _NN) for h in hs]
        v_new = [u[h] - ws[h] for h in hs]
        av = [_dot(d[h]["attn"], v_new[h], _NN) for h in hs]
        kv = [_dot(d[h]["kd"], v_new[h], _TN) for h in hs]
        for h in hs:
            sp_ref[0, h] = s0[h]
            T_ref[0, h] = T[h]
            o_ref[:, sl[h]] = qs0[h] + av[h]
            st[h] = s0[h] * d[h]["decay"] + kv[h]

    res = _pallas(
        body, (qk, v, gb, gT), name=name, grid=(N,),
        in_specs=[pl.BlockSpec((C, 2 * W), lambda n: (n, 0)), pl.BlockSpec((C, W), lambda n: (n, 0)),
                  pl.BlockSpec((C, LANE), lambda n: (n, 0)), pl.BlockSpec((1, H, C), lambda n: (n, 0, 0))],
        out_specs=[pl.BlockSpec((C, W), lambda n: (n, 0)), pl.BlockSpec((1, H, dh, dh), lambda n: (n, 0, 0, 0)),
                   pl.BlockSpec((1, H, C, C), lambda n: (n, 0, 0, 0))],
        out_shape=[jax.ShapeDtypeStruct((S, W), F32), jax.ShapeDtypeStruct((N, H, dh, dh), F32),
                   jax.ShapeDtypeStruct((N, H, C, C), F32)],
        scratch_shapes=[pltpu.VMEM((H, dh, dh), F32)], sem=("arbitrary",), comm=comm)
    return (res[:3], res[3:]) if comm else res[:3]


def _delta_bwd(name, qk, v, gb, gT, sp, Tm, do, *, S, H, dh, comm=()):
    C = min(DN_CHUNK, S)
    N, W = S // C, H * dh
    scale = dh ** -0.5

    def body(qk_ref, v_ref, gb_ref, gT_ref, sp_ref, T_ref, do_ref, dqk_ref, dv_ref, dgb_ref, ds):
        n = pl.program_id(0)

        @pl.when(n == 0)
        def _():
            ds[...] = jnp.zeros_like(ds)

        r = lax.broadcasted_iota(jnp.int32, (C, C), 0)
        c = lax.broadcasted_iota(jnp.int32, (C, C), 1)
        causal, strict = r >= c, r > c
        eye = (r == c).astype(F32)
        Lt = causal.astype(F32)
        ones = jnp.ones((C, LANE), F32)
        lane = lax.broadcasted_iota(jnp.int32, (C, LANE), 1)
        rowi = lax.broadcasted_iota(jnp.int32, (C, 1), 0)
        gbv = gb_ref[...]
        gcum = _dot(Lt, gbv, _NN, hi=True)
        gcumT = _dot(gT_ref[0], Lt, _NT, hi=True)
        dgc_all = jnp.zeros((C, LANE), F32)
        dbeta_all = jnp.zeros((C, LANE), F32)
        hs = range(H)
        sl = [slice(h * dh, (h + 1) * dh) for h in hs]
        ksl = [slice(W + h * dh, W + (h + 1) * dh) for h in hs]
        k = [qk_ref[:, ksl[h]] for h in hs]
        vv = [v_ref[:, sl[h]] for h in hs]
        beta = [_col(gbv, h) for h in hs]
        d = [_delta_chunk(qk_ref[:, sl[h]], k[h], vv[h], beta[h], _col(gcum, H + h), _row(gcumT, h), causal, strict,
                          eye, scale, C) for h in hs]
        T = [T_ref[0, h] for h in hs]
        s0 = [sp_ref[0, h] for h in hs]
        dO = [do_ref[:, sl[h]] for h in hs]
        dS = [ds[h] for h in hs]
        u = [_dot(T[h], d[h]["vb"], _NN) for h in hs]
        w = [_dot(T[h], d[h]["kbg"], _NN) for h in hs]
        ws = [_dot(w[h], s0[h], _NN) for h in hs]
        v_new = [u[h] - ws[h] for h in hs]
        dv_new = [_dot(d[h]["attn"], dO[h], _TN) + _dot(d[h]["kd"], dS[h], _NN) for h in hs]
        dattn = [jnp.where(causal, _dot(dO[h], v_new[h], _NT), 0.0) for h in hs]
        dqg = [_dot(dO[h], s0[h], _NT) for h in hs]
        dkd = [_dot(v_new[h], dS[h], _NT) for h in hs]
        ddecay = [jnp.sum(jnp.sum(s0[h] * dS[h], axis=1, keepdims=True), axis=0, keepdims=True) for h in hs]
        ds_new = [_dot(d[h]["qg"], dO[h], _TN) + d[h]["decay"] * dS[h] - _dot(w[h], dv_new[h], _TN) for h in hs]
        dw = [-_dot(dv_new[h], s0[h], _NT) for h in hs]
        for h in hs:
            ds[h] = ds_new[h]
        dT = [_dot(dv_new[h], d[h]["vb"], _NT) + _dot(dw[h], d[h]["kbg"], _NT) for h in hs]
        dvb = [_dot(T[h], dv_new[h], _TN) for h in hs]
        dkbg = [_dot(T[h], dw[h], _TN) for h in hs]
        Ts = [_split(T[h]) for h in hs]
        x1 = [_dot3(Ts[h], _split(dT[h]), _TN) for h in hs]
        dA = [jnp.where(strict, -_dot3(_split(x1[h]), Ts[h], _NT), 0.0) for h in hs]
        dkk = [dA[h] * d[h]["gam"] for h in hs]
        dqk_m = [dattn[h] * d[h]["gam"] for h in hs]
        m = [_split(dA[h] * d[h]["A"] + dattn[h] * d[h]["attn"]) for h in hs]
        msum = [jnp.sum(dA[h] * d[h]["A"] + dattn[h] * d[h]["attn"], axis=1, keepdims=True) for h in hs]
        mcol = [jnp.max(_dot(m[h][0], ones, _TN) + _dot(m[h][1], ones, _TN), axis=1, keepdims=True) for h in hs]
        dkb = [_dot(dkk[h], k[h], _NN) + dkbg[h] * d[h]["eg"] for h in hs]
        dk = [_dot(dkk[h], d[h]["kb"], _TN) + _dot(dqk_m[h], d[h]["qs"], _TN) + dkd[h] * d[h]["ek"] + dkb[h] * beta[h]
              for h in hs]
        dqs = [_dot(dqk_m[h], k[h], _NN) + dqg[h] * d[h]["eg"] for h in hs]
        for h in hs:
            r_kd = jnp.sum(dkd[h] * d[h]["kd"], axis=1, keepdims=True)
            dgc = (msum[h] - mcol[h] + jnp.sum(dqg[h] * d[h]["qg"], axis=1, keepdims=True) - r_kd
                   + jnp.sum(dkbg[h] * d[h]["kbg"], axis=1, keepdims=True))
            dg_last = jnp.sum(r_kd, axis=0, keepdims=True) + ddecay[h] * d[h]["decay"]
            dgc = dgc + jnp.where(rowi == C - 1, dg_last, 0.0)
            dbeta = jnp.sum(dkb[h] * k[h], axis=1, keepdims=True) + jnp.sum(dvb[h] * vv[h], axis=1, keepdims=True)
            dqk_ref[:, sl[h]] = dqs[h] * scale
            dqk_ref[:, ksl[h]] = dk[h]
            dv_ref[:, sl[h]] = dvb[h] * beta[h]
            dgc_all = dgc_all + jnp.where(lane == H + h, dgc, 0.0)
            dbeta_all = dbeta_all + jnp.where(lane == h, dbeta, 0.0)
        dgb_ref[...] = _dot(Lt, dgc_all, _TN, hi=True) + dbeta_all

    rev = lambda n: N - 1 - n
    res = _pallas(
        body, (qk, v, gb, gT, sp, Tm, do), name=name, grid=(N,),
        in_specs=[pl.BlockSpec((C, 2 * W), lambda n: (rev(n), 0)), pl.BlockSpec((C, W), lambda n: (rev(n), 0)),
                  pl.BlockSpec((C, LANE), lambda n: (rev(n), 0)), pl.BlockSpec((1, H, C), lambda n: (rev(n), 0, 0)),
                  pl.BlockSpec((1, H, dh, dh), lambda n: (rev(n), 0, 0, 0)),
                  pl.BlockSpec((1, H, C, C), lambda n: (rev(n), 0, 0, 0)),
                  pl.BlockSpec((C, W), lambda n: (rev(n), 0))],
        out_specs=[pl.BlockSpec((C, 2 * W), lambda n: (rev(n), 0)), pl.BlockSpec((C, W), lambda n: (rev(n), 0)),
                   pl.BlockSpec((C, LANE), lambda n: (rev(n), 0))],
        out_shape=[jax.ShapeDtypeStruct((S, 2 * W), F32), jax.ShapeDtypeStruct((S, W), F32),
                   jax.ShapeDtypeStruct((S, LANE), F32)],
        scratch_shapes=[pltpu.VMEM((H, dh, dh), F32)], sem=("arbitrary",), comm=comm)
    return (res[:3], res[3:]) if comm else res[:3]


_ANY = pl.BlockSpec(memory_space=pl.ANY)


ICI_CHUNKS = 4
D2D_CHUNKS = 4
EXCHANGE_PARTS = 3


def _place():
    return lax.axis_index("x"), lax.axis_index("y"), lax.axis_index("c")


def _row_chunks(rows, n):
    n = max(1, min(n, rows // 8))
    while n > 1 and (rows % n or (rows // n) % 8):
        n -= 1
    return [(k * (rows // n), rows // n) for k in range(n)]


class _Plug:
    def __init__(self, ins, outs, sems, start, finish, after):
        self.ins, self.outs, self.sems, self.start, self.finish, self.after = ins, outs, sems, start, finish, after


def _run_plug(name, plug):
    def body(*refs):
        ni, no = len(plug.ins), len(plug.outs)
        plug.start(refs[:ni], refs[ni:ni + no], refs[ni + no:])
        plug.finish(refs[:ni], refs[ni:ni + no], refs[ni + no:])

    return pl.pallas_call(body, name=name, in_specs=[_ANY] * len(plug.ins), out_specs=[_ANY] * len(plug.outs),
                          out_shape=list(plug.outs), scratch_shapes=list(plug.sems))(*plug.ins)


def _gather_plug(flat):
    R, L = flat.shape
    Rh = R // 2
    ici = _row_chunks(Rh, ICI_CHUNKS)
    sub = _row_chunks(ici[0][1], D2D_CHUNKS)
    ni, ns = len(ici), len(sub)

    def parts(ins, outs, sems):
        (x_ref,), (out_ref,), (send_sems, recv_sems) = ins, outs, sems
        x, y, c = _place()
        chips = [(1 - x, y), (x, 1 - y), (1 - x, 1 - y)]

        def rows(px, py, pc, r0, n):
            return out_ref.at[2 * px + py, pl.ds(pc * Rh + r0, n), :]

        def copy(k, src, dst, to):
            return pltpu.make_async_remote_copy(src_ref=src, dst_ref=dst, send_sem=send_sems.at[k],
                                                recv_sem=recv_sems.at[k], device_id=to, device_id_type=MESH)

        first = [copy(k * ni + q, x_ref.at[pl.ds(c * Rh + r0, n), :], rows(x, y, c, r0, n), (*chip, c))
                 for k, chip in enumerate(chips) for q, (r0, n) in enumerate(ici)]
        return x_ref, (x, y, c), chips, rows, copy, first

    def start(ins, outs, sems):
        for cp in parts(ins, outs, sems)[-1]:
            cp.start()

    def finish(ins, outs, sems):
        x_ref, (x, y, c), chips, rows, copy, first = parts(ins, outs, sems)
        sibling = (x, y, 1 - c)
        passed = []
        for k, chip in enumerate(chips):
            for q, (r0, n) in enumerate(ici):
                copy(k * ni + q, x_ref.at[pl.ds(r0, n), :], rows(*chip, c, r0, n), (*chip, c)).wait_recv()
                for t, (s0, m) in enumerate(sub):
                    cp = copy(3 * ni + (k * ni + q) * ns + t, rows(*chip, c, r0 + s0, m), rows(*chip, c, r0 + s0, m), sibling)
                    cp.start()
                    passed.append(cp)
        for k, chip in enumerate(chips):
            for q, (r0, n) in enumerate(ici):
                for t, (s0, m) in enumerate(sub):
                    copy(3 * ni + (k * ni + q) * ns + t, x_ref.at[pl.ds(r0, m), :], rows(*chip, 1 - c, r0 + s0, m),
                         sibling).wait_recv()
        for cp in first + passed:
            cp.wait_send()

    def after(res):
        return lax.dynamic_update_slice(res[0], flat[None], (2 * lax.axis_index("x") + lax.axis_index("y"), 0, 0))

    nsem = 3 * ni * (1 + ns)
    return _Plug([flat], [jax.ShapeDtypeStruct((N_CHIPS, R, L), flat.dtype)],
                 [pltpu.SemaphoreType.DMA((nsem,)), pltpu.SemaphoreType.DMA((nsem,))], start, finish, after)


def _sibling_split(name, g):
    _, R, L = g.shape
    Rh = R // 2

    chunks = [(j, r0, n) for j in range(N_CHIPS) for (r0, n) in _row_chunks(Rh, D2D_CHUNKS)]

    def body(g_ref, got_ref, send_sems, recv_sems):
        x, y, c = _place()
        cps = [pltpu.make_async_remote_copy(src_ref=g_ref.at[j, pl.ds((1 - c) * Rh + r0, n), :],
                                            dst_ref=got_ref.at[j, pl.ds(r0, n), :], send_sem=send_sems.at[k],
                                            recv_sem=recv_sems.at[k], device_id=(x, y, 1 - c), device_id_type=MESH)
               for k, (j, r0, n) in enumerate(chunks)]
        for cp in cps:
            cp.start()
        for cp in cps:
            cp.wait()

    sems = pltpu.SemaphoreType.DMA((len(chunks),))
    got = pl.pallas_call(
        body, name=name, in_specs=[_ANY], out_specs=_ANY, out_shape=jax.ShapeDtypeStruct((N_CHIPS, Rh, L), g.dtype),
        scratch_shapes=[sems, sems],
    )(g)
    own = lax.dynamic_slice(g, (0, lax.axis_index("c") * Rh, 0), (N_CHIPS, Rh, L))
    return own, got


def _exchange_plug(p, row0, nrows):
    ici = _row_chunks(nrows, ICI_CHUNKS)
    ni = len(ici)

    def sends(ins, outs, sems):
        (p_ref,), (q_ref,), (send_sems, recv_sems) = ins, outs, sems
        x, y, c = _place()
        me = 2 * x + y
        chips = [(1 - x, y), (x, 1 - y), (1 - x, 1 - y)]
        return [pltpu.make_async_remote_copy(src_ref=p_ref.at[2 * cx + cy, pl.ds(row0 + r0, n), :],
                                             dst_ref=q_ref.at[me, pl.ds(r0, n), :],
                                             send_sem=send_sems.at[k * ni + q], recv_sem=recv_sems.at[k * ni + q],
                                             device_id=(cx, cy, c), device_id_type=MESH)
                for k, (cx, cy) in enumerate(chips) for q, (r0, n) in enumerate(ici)]

    def start(ins, outs, sems):
        for cp in sends(ins, outs, sems):
            cp.start()

    def finish(ins, outs, sems):
        (p_ref,), (q_ref,), (send_sems, recv_sems) = ins, outs, sems
        x, y, c = _place()
        me = 2 * x + y
        chips = [(1 - x, y), (x, 1 - y), (1 - x, 1 - y)]
        for k, (cx, cy) in enumerate(chips):
            for q, (r0, n) in enumerate(ici):
                pltpu.make_async_remote_copy(src_ref=p_ref.at[me, pl.ds(r0, n), :],
                                             dst_ref=q_ref.at[2 * cx + cy, pl.ds(r0, n), :],
                                             send_sem=send_sems.at[k * ni + q], recv_sem=recv_sems.at[k * ni + q],
                                             device_id=(cx, cy, c), device_id_type=MESH).wait_recv()
        for cp in sends(ins, outs, sems):
            cp.wait_send()

    def after(res):
        me = 2 * lax.axis_index("x") + lax.axis_index("y")
        mine = lax.dynamic_slice(p, (me, row0, 0), (1, nrows, p.shape[2]))
        return lax.dynamic_update_slice(res[0], mine, (me, 0, 0))

    return _Plug([p], [jax.ShapeDtypeStruct((N_CHIPS, nrows, p.shape[2]), p.dtype)],
                 [pltpu.SemaphoreType.DMA((3 * ni,)), pltpu.SemaphoreType.DMA((3 * ni,))], start, finish, after)


def _sibling_swap(name, half):
    Rh, L = half.shape
    chunks = _row_chunks(Rh, 2 * D2D_CHUNKS)

    def body(h_ref, out_ref, send_sems, recv_sems):
        x, y, c = _place()
        cps = [pltpu.make_async_remote_copy(src_ref=h_ref.at[pl.ds(r0, n), :], dst_ref=out_ref.at[pl.ds(r0, n), :],
                                            send_sem=send_sems.at[k], recv_sem=recv_sems.at[k], device_id=(x, y, 1 - c),
                                            device_id_type=MESH)
               for k, (r0, n) in enumerate(chunks)]
        for cp in cps:
            cp.start()
        for cp in cps:
            cp.wait()

    sems = pltpu.SemaphoreType.DMA((len(chunks),))
    return pl.pallas_call(
        body, name=name, in_specs=[_ANY], out_specs=_ANY, out_shape=jax.ShapeDtypeStruct((Rh, L), half.dtype),
        scratch_shapes=[sems, sems],
    )(half)


def _add_pairs(name, a, b, out_dtype):
    n, Rh, L = a.shape
    tr = _pick(Rh, 512, 8)

    def body(a_ref, b_ref, o_ref):
        o_ref[...] = (a_ref[...].astype(F32) + b_ref[...].astype(F32)).astype(o_ref.dtype)

    spec = pl.BlockSpec((1, tr, L), lambda j, i: (j, i, 0))
    return pl.pallas_call(body, name=name, grid=(n, Rh // tr), in_specs=[spec, spec], out_specs=spec,
                          out_shape=jax.ShapeDtypeStruct(a.shape, out_dtype),
                          compiler_params=_cparams(("parallel", "parallel")))(a, b)


def _sum_chips(name, q):
    n, Rh, L = q.shape
    tr = _pick(Rh, 512, 8)

    def body(q_ref, o_ref):
        acc = q_ref[0].astype(F32)
        for s in range(1, n):
            acc = acc + q_ref[s].astype(F32)
        o_ref[...] = acc

    return pl.pallas_call(body, name=name, grid=(Rh // tr,),
                          in_specs=[pl.BlockSpec((n, tr, L), lambda i: (0, i, 0))],
                          out_specs=pl.BlockSpec((tr, L), lambda i: (i, 0)),
                          out_shape=jax.ShapeDtypeStruct((Rh, L), F32),
                          compiler_params=_cparams(("parallel",)))(q)


def _adamw(name, w, m, v, g):
    shape, size = w.shape, w.size
    rows = -(-size // FLAT_L)
    rows_p = -(-rows // ADAM_ROWS) * ADAM_ROWS
    pad = rows_p * FLAT_L - size

    def flat2d(a):
        a = a.reshape(-1)
        if pad:
            a = jnp.pad(a, (0, pad), constant_values=1.0)
        return a.reshape(rows_p, FLAT_L)

    c1 = 1.0 / (1.0 - ADAM_B1 ** ADAM_STEP)
    c2 = 1.0 / (1.0 - ADAM_B2 ** ADAM_STEP)

    def body(w_ref, m_ref, v_ref, g_ref, go_ref, d_ref, mo_ref, vo_ref):
        g = g_ref[...]
        wv = w_ref[...]
        mn = ADAM_B1 * m_ref[...] + (1.0 - ADAM_B1) * g
        vn = ADAM_B2 * v_ref[...] + (1.0 - ADAM_B2) * (g * g)
        go_ref[...] = g
        mo_ref[...] = mn
        vo_ref[...] = vn
        d_ref[...] = -ADAM_LR * ((mn * c1) / (jnp.sqrt(vn * c2) + ADAM_EPS) + ADAM_WD * wv)

    spec = pl.BlockSpec((ADAM_ROWS, FLAT_L), lambda i: (i, 0))
    shp = jax.ShapeDtypeStruct((rows_p, FLAT_L), F32)
    outs = pl.pallas_call(
        body, name=name, grid=(rows_p // ADAM_ROWS,),
        in_specs=[spec] * 4, out_specs=[spec] * 4, out_shape=[shp] * 4, compiler_params=_cparams(("parallel",)),
    )(flat2d(w), flat2d(m), flat2d(v), flat2d(g))

    def back(a):
        a = a.reshape(-1)
        if pad:
            a = a[:size]
        return a.reshape(shape)

    return tuple(back(a) for a in outs)


SEG_ROWS = 16


def _keeps_rows(shape):
    return len(shape) >= 2 and shape[-1] < FLAT_L and FLAT_L % shape[-1] != 0


def _seg_rows(shape):
    rows = math.prod(shape[:-1]) if _keeps_rows(shape) else -(-math.prod(shape) // FLAT_L)
    return -(-rows // SEG_ROWS) * SEG_ROWS


def _to_rows(a, dtype):
    rows = _seg_rows(a.shape)
    if _keeps_rows(a.shape):
        r = a.reshape(-1, a.shape[-1]).astype(dtype)
        return jnp.pad(r, ((0, rows - r.shape[0]), (0, FLAT_L - a.shape[-1])))
    flat = jnp.pad(a.reshape(-1).astype(dtype), (0, rows * FLAT_L - a.size))
    return flat.reshape(rows, FLAT_L)


def _from_rows(seg, shape):
    if _keeps_rows(shape):
        return seg[:math.prod(shape[:-1]), :shape[-1]].reshape(shape)
    return seg.reshape(-1)[:math.prod(shape)].reshape(shape)


def _pack(pieces, dtype, row_mult):
    segs, offs, r = [], [], 0
    for a in pieces:
        segs.append(_to_rows(a, dtype))
        offs.append(r)
        r += segs[-1].shape[0]
    tail = -r % row_mult
    if tail:
        segs.append(jnp.zeros((tail, FLAT_L), dtype))
    return jnp.concatenate(segs, axis=0), offs


def _segment(flat, off, shape):
    return _from_rows(flat[off:off + _seg_rows(shape)], shape)


def _gather_unit(shards, axes, dtype):
    flat, offs = _pack(shards, dtype, 64)
    plug = _gather_plug(flat)

    def unpack(full):
        return [jnp.concatenate([_segment(full[j], off, a.shape) for j in range(N_CHIPS)], axis=ax)
                for a, ax, off in zip(shards, axes, offs)]

    return plug, unpack


def _reduce_unit(name, grads, axes):
    pieces = [[] for _ in range(N_CHIPS)]
    for g, ax in zip(grads, axes):
        parts = jnp.split(g, N_CHIPS, axis=ax) if ax is not None else [g] * N_CHIPS
        for jc in range(N_CHIPS):
            pieces[jc].append(parts[jc])
    packed = [_pack(pieces[jc], BF16, 512) for jc in range(N_CHIPS)]
    gsend = jnp.stack([pk[0] for pk in packed])
    own, got = _sibling_split(name + "_split", gsend)
    pair = _add_pairs(name + "_add", own, got, BF16)
    return pair, packed[0][1], [a.shape for a in pieces[0]]


def _f_rms(row0, j, x, g):
    return _rms(x, g)


def _f_mid(row0, j, x, y, g_a, g_b):
    xn = x + _rms(y, g_a)
    return xn, _rms(xn, g_b)


def _f_resid(row0, j, x, y, g):
    return x + _rms(y, g)


def _relu2(u):
    r = jnp.maximum(u, 0.0)
    return r * r


def _relu2_bwd(d_act, act):
    return d_act * (2.0 * jnp.sqrt(act.astype(F32)))


def _f_l2silu(row0, j, c):
    a = _silu(c)
    return a * lax.rsqrt(jnp.sum(a * a, axis=-1, keepdims=True) + EPS)


def _f_silu(row0, j, c):
    return _silu(c)


def _f_scale(row0, j, y, s):
    return y * s


def _f_glu(row0, j, a, gate):
    return a * _sigmoid(gate)


def _f_lnsilu(row0, j, u, g, b):
    mu = jnp.mean(u, axis=-1, keepdims=True)
    uc = u - mu
    return _silu(uc * lax.rsqrt(jnp.mean(uc * uc, axis=-1, keepdims=True) + EPS) * g + b)


def _f_outgate(row0, j, o, z, g):
    return _rms(o, g) * _silu(z)


def _make_gates(H):
    def f(row0, j, ba, alog, dt):
        lane = lax.broadcasted_iota(jnp.int32, ba.shape, 1)
        beta = _sigmoid(ba)
        g = -jnp.exp(alog) * _softplus(ba + dt)
        return jnp.where(lane < H, beta, jnp.where(lane < 2 * H, g, 0.0))
    return f


def _f_loss(row0, j, y, t):
    e = y - t
    loss = 0.5 * jnp.sum(jnp.mean(e * e, axis=-1, keepdims=True), axis=0, keepdims=True)
    return e * (1.0 / y.shape[-1]), jnp.broadcast_to(loss, (1, LANE))


def _lane_row(vec, start):
    return jnp.pad(vec.astype(F32)[None, :], ((0, 0), (start, LANE - start - vec.shape[0])))


def kernel(x, norm_mix_pre, norm_mix_post, norm_mlp_pre, norm_mlp_post, even_w_in, even_conv, even_a_log, even_dt_bias, even_dn_norm, even_pool_w, even_pool_scale, even_w_out, odd_w_in, odd_dw, odd_dw_b, odd_ln_g, odd_ln_b, odd_w_out, mlp_w_up, mlp_w_down, loss_target, m_norm_mix_pre, m_norm_mix_post, m_norm_mlp_pre, m_norm_mlp_post, m_even_w_in, m_even_conv, m_even_a_log, m_even_dt_bias, m_even_dn_norm, m_even_pool_w, m_even_pool_scale, m_even_w_out, m_odd_w_in, m_odd_dw, m_odd_dw_b, m_odd_ln_g, m_odd_ln_b, m_odd_w_out, m_mlp_w_up, m_mlp_w_down, v_norm_mix_pre, v_norm_mix_post, v_norm_mlp_pre, v_norm_mlp_post, v_even_w_in, v_even_conv, v_even_a_log, v_even_dt_bias, v_even_dn_norm, v_even_pool_w, v_even_pool_scale, v_even_w_out, v_odd_w_in, v_odd_dw, v_odd_dw_b, v_odd_ln_g, v_odd_ln_b, v_odd_w_out, v_mlp_w_up, v_mlp_w_down):
    names = ["norm_mix_pre", "norm_mix_post", "norm_mlp_pre", "norm_mlp_post", "even_w_in", "even_conv", "even_a_log",
             "even_dt_bias", "even_dn_norm", "even_pool_w", "even_pool_scale", "even_w_out", "odd_w_in", "odd_dw",
             "odd_dw_b", "odd_ln_g", "odd_ln_b", "odd_w_out", "mlp_w_up", "mlp_w_down"]
    W = dict(zip(names, (norm_mix_pre, norm_mix_post, norm_mlp_pre, norm_mlp_post, even_w_in, even_conv, even_a_log,
                         even_dt_bias, even_dn_norm, even_pool_w, even_pool_scale, even_w_out, odd_w_in, odd_dw,
                         odd_dw_b, odd_ln_g, odd_ln_b, odd_w_out, mlp_w_up, mlp_w_down)))
    Mo = dict(zip(names, (m_norm_mix_pre, m_norm_mix_post, m_norm_mlp_pre, m_norm_mlp_post, m_even_w_in, m_even_conv,
                          m_even_a_log, m_even_dt_bias, m_even_dn_norm, m_even_pool_w, m_even_pool_scale, m_even_w_out,
                          m_odd_w_in, m_odd_dw, m_odd_dw_b, m_odd_ln_g, m_odd_ln_b, m_odd_w_out, m_mlp_w_up,
                          m_mlp_w_down)))
    Vo = dict(zip(names, (v_norm_mix_pre, v_norm_mix_post, v_norm_mlp_pre, v_norm_mlp_post, v_even_w_in, v_even_conv,
                          v_even_a_log, v_even_dt_bias, v_even_dn_norm, v_even_pool_w, v_even_pool_scale, v_even_w_out,
                          v_odd_w_in, v_odd_dw, v_odd_dw_b, v_odd_ln_g, v_odd_ln_b, v_odd_w_out, v_mlp_w_up,
                          v_mlp_w_down)))
    shard_axis = {"even_w_in": 2, "even_conv": 2, "even_pool_w": 2, "even_w_out": 1, "odd_w_in": 2, "odd_dw": 2,
                  "odd_dw_b": 1, "odd_ln_g": 1, "odd_ln_b": 1, "odd_w_out": 1, "mlp_w_up": 2, "mlp_w_down": 1}

    S, D = x.shape[1], x.shape[2]
    depth = norm_mix_pre.shape[0]
    H = even_a_log.shape[1]
    dh = even_dn_norm.shape[1]
    DNW = H * dh
    PW = even_pool_scale.shape[1]
    G = len(POOL_WINDOWS)
    PG = PW // G
    KC = even_conv.shape[1]
    BAW = 2 * LANE
    P_COLS = 4 * DNW + PW + BAW
    x2 = x.reshape(S, D)
    tgt = loss_target.reshape(S, D)

    small = ["even_conv", "odd_dw", "odd_dw_b", "odd_ln_g", "odd_ln_b"]
    plug_s, unpack_s = _gather_unit([W[n] for n in small], [shard_axis[n] for n in small], F32)
    full = dict(zip(small, unpack_s(plug_s.after(_run_plug("gather_small", plug_s)))))
    CW = odd_w_out.shape[1] * N_CHIPS
    wfull = {}

    def mixer_weights(i):
        return [(n, i // 2) for n in (("even_w_in", "even_pool_w", "even_w_out") if i % 2 == 0 else ("odd_w_in", "odd_w_out"))]

    riders = []

    def hosted(fn, *a, **k):
        if not riders:
            return fn(*a, **k)
        plug, done = riders.pop(0)
        outs, landed = fn(*a, comm=[plug], **k)
        done(plug.after(landed))
        return outs

    def queue_gather(keys):
        plug, unpack = _gather_unit([W[n][l] for n, l in keys], [shard_axis[n] - 1 for n, _ in keys], BF16)
        riders.append((plug, lambda full_: wfull.update(zip(keys, unpack(full_)))))

    def weight(key):
        while key not in wfull:
            plug, done = riders.pop(0)
            done(plug.after(_run_plug(f"alone_{len(wfull)}_{len(riders)}", plug)))
        return wfull[key]

    queue_gather(mixer_weights(0)[:1])
    queue_gather(mixer_weights(0)[1:])
    for i in range(depth):
        queue_gather([("mlp_w_up", i)])
        queue_gather([("mlp_w_down", i)])
        if i + 1 < depth:
            queue_gather(mixer_weights(i + 1))

    def even_w_in_layout(w):
        o1 = 4 * DNW
        return jnp.concatenate([w[:, :o1], w[:, o1 + 2 * H:], w[:, o1:o1 + 2 * H],
                                jnp.zeros((w.shape[0], BAW - 2 * H), w.dtype)], axis=1)

    def even_w_in_unlayout(g):
        o1 = 4 * DNW
        return jnp.concatenate([g[:, :o1], g[:, o1 + PW:o1 + PW + 2 * H], g[:, o1:o1 + PW]], axis=1)

    def pool_blockdiag(pw):
        return jnp.concatenate([jnp.pad(pw[gi], ((0, 0), (gi * PG, PW - (gi + 1) * PG))) for gi in range(G)], axis=0)

    pool_taps = max(POOL_WINDOWS)
    tap = jnp.arange(pool_taps)[:, None]
    win_c = jnp.repeat(jnp.asarray(POOL_WINDOWS, F32), PG)[None, :]
    pool_mask = (tap >= pool_taps - win_c).astype(F32)

    grads = {}
    tr_full = 128 if D > 1024 else 256

    saved = []
    xc = x2
    tr_fwd = 256
    (h,) = _rowwise("l0_rms_in", _f_rms, [(xc, 0, D)], [(W["norm_mix_pre"][0:1], None, D)], [(D, BF16)], S=S, tr=tr_fwd)
    for i in range(depth):
        jl = i // 2
        sv = {"x_in": xc}
        g1, g2, g3, g4 = (W[n][i:i + 1] for n in ("norm_mix_pre", "norm_mix_post", "norm_mlp_pre", "norm_mlp_post"))
        sv["h"] = h
        if i % 2 == 0:
            w_in = even_w_in_layout(weight(("even_w_in", jl)))
            p = hosted(_matmul, f"l{i}_w_in", h, w_in, "nn", tn=768)
            conv_w = full["even_conv"][jl]
            c = _dwconv_fwd(f"l{i}_conv", p, 0, conv_w, S=S, C=3 * DNW)
            (qk,) = _rowwise(f"l{i}_qk", _f_l2silu, [(c, 0, dh)], [], [(dh, F32)], S=S, ncb=2 * H, tr=1024)
            (vv,) = _rowwise(f"l{i}_v", _f_silu, [(c, 2 * DNW // dh, dh)], [], [(dh, F32)], S=S, ncb=H, tr=1024)
            alog = _lane_row(W["even_a_log"][jl], H)
            dtb = _lane_row(W["even_dt_bias"][jl], H)
            ba_off = (4 * DNW + PW) // LANE
            (gb,) = _rowwise(f"l{i}_gates", _make_gates(H), [(p, ba_off, LANE)], [(alog, None, LANE), (dtb, None, LANE)],
                             [(LANE, F32)], S=S, tr=1024)
            Cn = min(DN_CHUNK, S)
            gT = gb[:, H:2 * H].reshape(S // Cn, Cn, H).transpose(0, 2, 1)
            o, sp, Tm = hosted(_delta_fwd, f"l{i}_delta", qk, vv, gb, gT, S=S, H=H, dh=dh)
            dn = W["even_dn_norm"][jl][None, :]
            (on,) = _rowwise(f"l{i}_outgate", _f_outgate, [(o, 0, dh), (p, 3 * DNW // dh, dh)], [(dn, None, dh)],
                             [(dh, BF16)], S=S, ncb=H, tr=1024)
            pcb = _pick(PW, 512)
            pooled = _dwconv_fwd(f"l{i}_pool", p, 4 * DNW // pcb, pool_mask, S=S, C=PW, win=win_c, out_dtype=BF16, cb=pcb)
            wbd = pool_blockdiag(weight(("even_pool_w", jl)))
            ypre = _matmul(f"l{i}_pool_w", pooled, wbd, "nn")
            psc = W["even_pool_scale"][jl][None, :]
            (ypool,) = _rowwise(f"l{i}_pool_scale", _f_scale, [(ypre, 0, PW)], [(psc, None, PW)], [(PW, BF16)], S=S)
            mixin = jnp.concatenate([on, ypool], axis=1)
            mix = _matmul(f"l{i}_w_out", mixin, weight(("even_w_out", jl)), "nn")
            sv.update(p=p, c=c, qk=qk, v=vv, gb=gb, gT=gT, o=o, sp=sp, Tm=Tm, pooled=pooled, ypre=ypre, mixin=mixin,
                      w_in=w_in, wbd=wbd, alog=alog, dtb=dtb, dn=dn, psc=psc, conv_w=conv_w)
        else:
            p = hosted(_matmul, f"l{i}_w_in", h, weight(("odd_w_in", jl)), "nn")
            ocb = _pick(CW, 1024)
            (u0,) = _rowwise(f"l{i}_glu", _f_glu, [(p, 0, ocb), (p, CW // ocb, ocb)], [], [(ocb, F32)], S=S,
                             ncb=CW // ocb)
            dw_w, dw_b = full["odd_dw"][jl], full["odd_dw_b"][jl][None, :]
            u1 = hosted(_dwconv_fwd, f"l{i}_dwconv", u0, 0, dw_w, S=S, C=CW, bias=dw_b)
            lg, lb = full["odd_ln_g"][jl][None, :], full["odd_ln_b"][jl][None, :]
            (u2,) = _rowwise(f"l{i}_lnsilu", _f_lnsilu, [(u1, 0, CW)], [(lg, None, CW), (lb, None, CW)], [(CW, BF16)],
                             S=S, tr=tr_full)
            mix = _matmul(f"l{i}_w_out", u2, weight(("odd_w_out", jl)), "nn")
            sv.update(p=p, u0=u0, u1=u1, mixin=u2, dw_w=dw_w, lg=lg, lb=lb)
        x_mid, h2 = _rowwise(f"l{i}_mid", _f_mid, [(xc, 0, D), (mix, 0, D)], [(g2, None, D), (g3, None, D)],
                             [(D, F32), (D, BF16)], S=S, tr=tr_fwd)
        act = hosted(_matmul, f"l{i}_w_up", h2, weight(("mlp_w_up", i)), "nn", epi=_relu2, out_dtypes=[BF16])
        ff = hosted(_matmul, f"l{i}_w_down", act, weight(("mlp_w_down", i)), "nn")
        if i + 1 < depth:
            x_out, h = _rowwise(f"l{i}_out", _f_mid, [(x_mid, 0, D), (ff, 0, D)],
                                [(g4, None, D), (W["norm_mix_pre"][i + 1:i + 2], None, D)], [(D, F32), (D, BF16)], S=S,
                                tr=tr_fwd)
        else:
            (x_out,) = _rowwise(f"l{i}_out", _f_resid, [(x_mid, 0, D), (ff, 0, D)], [(g4, None, D)], [(D, F32)], S=S,
                                tr=tr_fwd)
        sv.update(mix=mix, x_mid=x_mid, h2=h2, act=act, ff=ff, g=(g1, g2, g3, g4))
        saved.append(sv)
        xc = x_out

    dy, loss_row = _rowwise("loss", _f_loss, [(xc, 0, D), (tgt, 0, D)], [], [(D, F32)], [(1, LANE, False)], S=S,
                            tr=tr_full)
    loss = lax.psum(loss_row[0, 0], ("x", "y", "c"))

    def mixer_params(i):
        ns = (("even_w_in", "even_conv", "even_a_log", "even_dt_bias", "even_dn_norm", "even_pool_w", "even_pool_scale",
               "even_w_out") if i % 2 == 0 else ("odd_w_in", "odd_dw", "odd_dw_b", "odd_ln_g", "odd_ln_b", "odd_w_out"))
        return [(n, i // 2) for n in ns] + [("norm_mix_pre", i)]

    def mlp_params(i):
        return [(n, i) for n in ("mlp_w_up", "mlp_w_down", "norm_mlp_pre", "norm_mlp_post", "norm_mix_post")]

    units = []

    def queue_reduce(name, keys):
        axes = [shard_axis[n] - 1 if n in shard_axis else None for n, _ in keys]
        pair, offs, shapes = _reduce_unit(name, [grads[k] for k in keys], axes)
        parts = []
        units.append((keys, offs, shapes, parts))
        for t, (r0, n) in enumerate(_row_chunks(pair.shape[1], EXCHANGE_PARTS)):
            plug = _exchange_plug(pair, r0, n)
            riders.append((plug, lambda q, t=t: parts.append((t, _sum_chips(f"{name}_sum{t}", q)))))

    dx = dy
    for i in reversed(range(depth)):
        jl = i // 2
        sv = saved[i]
        g1, g2, g3, g4 = sv["g"]
        if i + 1 < depth:
            dx, d_ff, dg4, dg1 = _rowwise_bwd(
                f"l{i}_out_b", _f_mid, [(sv["x_mid"], 0, D), (sv["ff"], 0, D)],
                [(g4, None, D), (saved[i + 1]["g"][0], None, D)], [(dx, 0, D), (dh_, 0, D)], [F32, BF16], S=S, tr=tr_full)
            grads["norm_mix_pre", i + 1] = dg1[0]
        else:
            d_ff, dg4 = _rowwise_bwd(f"l{i}_out_b", _f_rms, [(sv["ff"], 0, D)], [(g4, None, D)], [(dx, 0, D)], [BF16],
                                     S=S, tr=tr_full)
        grads["norm_mlp_post", i] = dg4[0]
        du = hosted(_matmul, f"l{i}_w_down_bx", d_ff, wfull["mlp_w_down", i], "nt", epi=_relu2_bwd, extras=[sv["act"]],
                    out_dtypes=[BF16])
        grads["mlp_w_down", i] = hosted(_matmul, f"l{i}_w_down_bw", sv["act"], d_ff, "tn", out_dtype=BF16)
        dh2 = hosted(_matmul, f"l{i}_w_up_bx", du, wfull["mlp_w_up", i], "nt")
        grads["mlp_w_up", i] = hosted(_matmul, f"l{i}_w_up_bw", sv["h2"], du, "tn", out_dtype=BF16)
        dx, d_mix, dg2, dg3 = _rowwise_bwd(
            f"l{i}_mid_b", _f_mid, [(sv["x_in"], 0, D), (sv["mix"], 0, D)], [(g2, None, D), (g3, None, D)],
            [(dx, 0, D), (dh2, 0, D)], [F32, BF16], S=S, tr=tr_full)
        grads["norm_mix_post", i], grads["norm_mlp_pre", i] = dg2[0], dg3[0]
        queue_reduce(f"red{i}", (mixer_params(i + 1) if i + 1 < depth else []) + mlp_params(i))
        if i % 2 == 0:
            d_mixin = _matmul(f"l{i}_w_out_bx", d_mix, wfull["even_w_out", jl], "nt")
            grads["even_w_out", jl] = _matmul(f"l{i}_w_out_bw", sv["mixin"], d_mix, "tn", out_dtype=BF16)
            p = sv["p"]
            pcb = _pick(PW, 512)
            d_ypre, dpsc = _rowwise_bwd(f"l{i}_pool_scale_b", _f_scale, [(sv["ypre"], 0, PW)], [(sv["psc"], None, PW)],
                                        [(d_mixin, DNW // PW, PW)], [BF16], S=S)
            grads["even_pool_scale", jl] = dpsc[0]
            d_pooled = _matmul(f"l{i}_pool_w_bx", d_ypre, sv["wbd"], "nt")
            dwbd = _matmul(f"l{i}_pool_w_bw", sv["pooled"], d_ypre, "tn")
            grads["even_pool_w", jl] = jnp.stack([dwbd[gi * PG:(gi + 1) * PG, gi * PG:(gi + 1) * PG] for gi in range(G)])
            d_xp = _dwconv_bwd(f"l{i}_pool_b", None, 0, d_pooled, pool_mask, S=S, C=PW, win=win_c, want_dw=False,
                               cb=pcb)[0]
            d_o, d_z, ddn = _rowwise_bwd(f"l{i}_outgate_b", _f_outgate, [(sv["o"], 0, dh), (p, 3 * DNW // dh, dh)],
                                         [(sv["dn"], None, dh)], [(d_mixin, 0, dh)], [F32, F32], S=S, ncb=H, tr=1024)
            grads["even_dn_norm", jl] = ddn[0]
            dqk, dv, dgb = hosted(_delta_bwd, f"l{i}_delta_b", sv["qk"], sv["v"], sv["gb"], sv["gT"], sv["sp"],
                                  sv["Tm"], d_o, S=S, H=H, dh=dh)
            ba_off = (4 * DNW + PW) // LANE
            d_ba, dalog, ddtb = _rowwise_bwd(f"l{i}_gates_b", _make_gates(H), [(p, ba_off, LANE)],
                                             [(sv["alog"], None, LANE), (sv["dtb"], None, LANE)], [(dgb, 0, LANE)],
                                             [F32], S=S, tr=1024)
            grads["even_a_log", jl], grads["even_dt_bias", jl] = dalog[0, H:2 * H], ddtb[0, H:2 * H]
            (dc_qk,) = _rowwise_bwd(f"l{i}_qk_b", _f_l2silu, [(sv["c"], 0, dh)], [], [(dqk, 0, dh)], [F32], S=S,
                                    ncb=2 * H, tr=1024)
            (dc_v,) = _rowwise_bwd(f"l{i}_v_b", _f_silu, [(sv["c"], 2 * DNW // dh, dh)], [], [(dv, 0, dh)], [F32], S=S,
                                   ncb=H, tr=1024)
            dc = jnp.concatenate([dc_qk, dc_v], axis=1)
            d_qkv, dconv, _ = _dwconv_bwd(f"l{i}_conv_b", p, 0, dc, sv["conv_w"], S=S, C=3 * DNW)
            grads["even_conv", jl] = dconv
            dp = jnp.concatenate([d_qkv.astype(BF16), d_z.astype(BF16), d_xp.astype(BF16), d_ba.astype(BF16),
                                  jnp.zeros((S, BAW - LANE), BF16)], axis=1)
            dh_ = hosted(_matmul, f"l{i}_w_in_bx", dp, sv["w_in"], "nt", tk=768)
            grads["even_w_in", jl] = even_w_in_unlayout(hosted(_matmul, f"l{i}_w_in_bw", sv["h"], dp, "tn", tn=768,
                                                               out_dtype=BF16))
        else:
            d_u2 = _matmul(f"l{i}_w_out_bx", d_mix, wfull["odd_w_out", jl], "nt")
            grads["odd_w_out", jl] = _matmul(f"l{i}_w_out_bw", sv["mixin"], d_mix, "tn", out_dtype=BF16)
            d_u1, dlg, dlb = _rowwise_bwd(f"l{i}_lnsilu_b", _f_lnsilu, [(sv["u1"], 0, CW)],
                                          [(sv["lg"], None, CW), (sv["lb"], None, CW)], [(d_u2, 0, CW)], [F32], S=S,
                                          tr=tr_full)
            grads["odd_ln_g", jl], grads["odd_ln_b", jl] = dlg[0], dlb[0]
            d_u0, ddw, ddb = hosted(_dwconv_bwd, f"l{i}_dwconv_b", sv["u0"], 0, d_u1, sv["dw_w"], S=S, C=CW)
            grads["odd_dw", jl], grads["odd_dw_b", jl] = ddw, ddb[0]
            p = sv["p"]
            ocb = _pick(CW, 1024)
            da, dgate = _rowwise_bwd(f"l{i}_glu_b", _f_glu, [(p, 0, ocb), (p, CW // ocb, ocb)], [], [(d_u0, 0, ocb)],
                                     [BF16, BF16], S=S, ncb=CW // ocb)
            dp = jnp.concatenate([da, dgate], axis=1)
            dh_ = hosted(_matmul, f"l{i}_w_in_bx", dp, wfull["odd_w_in", jl], "nt")
            grads["odd_w_in", jl] = hosted(_matmul, f"l{i}_w_in_bw", sv["h"], dp, "tn", out_dtype=BF16)
    dx, dg1 = _rowwise_bwd("l0_rms_in_b", _f_rms, [(saved[0]["x_in"], 0, D)], [(saved[0]["g"][0], None, D)],
                           [(dh_, 0, D)], [F32], adds=[(dx, 0, D)], S=S, tr=tr_full)
    grads["norm_mix_pre", 0] = dg1[0]
    grad_x = dx.reshape(x.shape)
    queue_reduce("red_last", mixer_params(0))
    while riders:
        plug, done = riders.pop(0)
        done(plug.after(_run_plug(f"alone_last_{len(riders)}", plug)))

    halves = [jnp.concatenate([q for _, q in sorted(parts, key=lambda tq: tq[0])], axis=0) for *_, parts in units]
    theirs = _sibling_swap("grad_sibling_swap", jnp.concatenate(halves, axis=0))
    south = lax.axis_index("c") == 0
    gshard, r = {}, 0
    for (keys, offs, shapes, _), half in zip(units, halves):
        rh = half.shape[0]
        other = theirs[r:r + rh]
        whole = jnp.concatenate([jnp.where(south, half, other), jnp.where(south, other, half)], axis=0)
        for k, off, shp in zip(keys, offs, shapes):
            gshard[k] = _segment(whole, off, shp)
        r += rh

    outs_g, outs_d, outs_m, outs_v = [], [], [], []
    for n in names:
        g_n = jnp.stack([gshard[n, l] for l in range(W[n].shape[0])])
        g_o, d_o, m_o, v_o = _adamw(f"adamw_{n}", W[n], Mo[n], Vo[n], g_n)
        outs_g.append(g_o)
        outs_d.append(d_o)
        outs_m.append(m_o)
        outs_v.append(v_o)
    return (loss, grad_x, *outs_g, *outs_d, *outs_m, *outs_v)
```

```python
import functools
import math

import jax
import jax.numpy as jnp
from jax import lax
from jax.experimental import pallas as pl
from jax.experimental.pallas import tpu as pltpu

F32 = jnp.float32
BF16 = jnp.bfloat16
EPS = 1e-6
DN_CHUNK = 64
POOL_WINDOWS = (2, 4, 8, 16)
ADAM_LR, ADAM_B1, ADAM_B2, ADAM_EPS, ADAM_WD, ADAM_STEP = 0.001, 0.9, 0.999, 1e-08, 0.01, 10
LANE = 128
FLAT_L = 2048
ADAM_ROWS = 128
VMEM_LIMIT = 56 * 1024 * 1024
TK_LONG = 4096
N_CHIPS = 4
HI = lax.Precision.HIGHEST
MESH = pl.DeviceIdType.MESH


def _cparams(sem):
    return pltpu.CompilerParams(dimension_semantics=sem, vmem_limit_bytes=VMEM_LIMIT)


def _pallas(body, args, *, name, grid, in_specs, out_specs, out_shape, scratch_shapes=(), sem=None, comm=()):
    n_in, n_out, n_scr = len(in_specs), len(out_specs), len(scratch_shapes)
    if not comm:
        return pl.pallas_call(body, name=name, grid=grid, in_specs=in_specs, out_specs=out_specs, out_shape=out_shape,
                              scratch_shapes=list(scratch_shapes), compiler_params=_cparams(sem))(*args)
    ci = [len(p.ins) for p in comm]
    co = [len(p.outs) for p in comm]
    cs = [len(p.sems) for p in comm]

    def wrapped(*refs):
        ins, pos = refs[:n_in], n_in
        cins = refs[pos:pos + sum(ci)]
        pos += sum(ci)
        outs = refs[pos:pos + n_out]
        pos += n_out
        couts = refs[pos:pos + sum(co)]
        pos += sum(co)
        scr = refs[pos:pos + n_scr]
        csems = refs[pos + n_scr:]
        ids = [pl.program_id(a) for a in range(len(grid))]
        first, last = ids[0] == 0, ids[0] == grid[0] - 1
        for a in range(1, len(grid)):
            first = jnp.logical_and(first, ids[a] == 0)
            last = jnp.logical_and(last, ids[a] == grid[a] - 1)
        parts, a, b, c = [], 0, 0, 0
        for p, na, nb, nc in zip(comm, ci, co, cs):
            parts.append((p, cins[a:a + na], couts[b:b + nb], csems[c:c + nc]))
            a, b, c = a + na, b + nb, c + nc

        @pl.when(first)
        def _():
            for p, pi, po, ps in parts:
                p.start(pi, po, ps)

        body(*ins, *outs, *scr)

        @pl.when(last)
        def _():
            for p, pi, po, ps in parts:
                p.finish(pi, po, ps)

    any_spec = pl.BlockSpec(memory_space=pl.ANY)
    res = pl.pallas_call(
        wrapped, name=name, grid=grid,
        in_specs=list(in_specs) + [any_spec] * sum(ci), out_specs=list(out_specs) + [any_spec] * sum(co),
        out_shape=list(out_shape) + [s for p in comm for s in p.outs],
        scratch_shapes=list(scratch_shapes) + [s for p in comm for s in p.sems],
        compiler_params=_cparams(("arbitrary",) * len(grid)),
    )(*args, *[a for p in comm for a in p.ins])
    return res


def _pick(dim, target, mult=LANE):
    if dim <= target:
        return dim
    t = (target // mult) * mult
    while t >= mult:
        if dim % t == 0:
            return t
        t -= mult
    return dim


def _sigmoid(x):
    return 1.0 / (1.0 + jnp.exp(-x))


def _silu(x):
    return x * _sigmoid(x)


def _softplus(x):
    return jnp.maximum(x, 0.0) + jnp.log(1.0 + jnp.exp(-jnp.abs(x)))


def _rms(x, g):
    return x * lax.rsqrt(jnp.mean(x * x, axis=-1, keepdims=True) + EPS) * g


def _matmul(name, a, b, mode, out_dtype=F32, tm=1024, tn=1024, tk=2048, epi=None, extras=(), out_dtypes=None, comm=()):
    if mode == "nn":
        (M, K), (K2, N) = a.shape, b.shape
    elif mode == "nt":
        (M, K), (N, K2) = a.shape, b.shape
    else:
        (K, M), (K2, N) = a.shape, b.shape
    assert K == K2, (name, a.shape, b.shape, mode)
    tm, tn, tk = _pick(M, tm), _pick(N, tn), _pick(K, tk)
    nk = K // tk
    if mode == "nn":
        a_spec = pl.BlockSpec((tm, tk), lambda i, j, k: (i, k))
        b_spec = pl.BlockSpec((tk, tn), lambda i, j, k: (k, j))
        dims = (((1,), (0,)), ((), ()))
    elif mode == "nt":
        a_spec = pl.BlockSpec((tm, tk), lambda i, j, k: (i, k))
        b_spec = pl.BlockSpec((tn, tk), lambda i, j, k: (j, k))
        dims = (((1,), (1,)), ((), ()))
    else:
        a_spec = pl.BlockSpec((tk, tm), lambda i, j, k: (k, i))
        b_spec = pl.BlockSpec((tk, tn), lambda i, j, k: (k, j))
        dims = (((0,), (0,)), ((), ()))
    out_dtypes = list(out_dtypes) if out_dtypes is not None else [out_dtype]
    ne, no = len(extras), len(out_dtypes)
    in_place = epi is None and out_dtypes == [F32]
    use_acc = nk > 1 and not in_place

    def finish(acc, extra_refs, out_refs):
        res = acc if epi is None else epi(acc, *[r[...] for r in extra_refs])
        res = res if isinstance(res, (tuple, list)) else (res,)
        for r, v in zip(out_refs, res):
            r[...] = v.astype(r.dtype)

    def body(a_ref, b_ref, *rest):
        extra_refs, out_refs = rest[:ne], rest[ne:ne + no]
        part = lax.dot_general(a_ref[...].astype(BF16), b_ref[...].astype(BF16), dims, preferred_element_type=F32)
        if nk == 1:
            finish(part, extra_refs, out_refs)
            return
        k = pl.program_id(2)
        acc_ref = rest[-1] if use_acc else out_refs[0]

        @pl.when(k == 0)
        def _():
            acc_ref[...] = part

        @pl.when(k > 0)
        def _():
            acc_ref[...] += part

        if use_acc:
            @pl.when(k == nk - 1)
            def _():
                finish(acc_ref[...], extra_refs, out_refs)

    o_spec = pl.BlockSpec((tm, tn), lambda i, j, k: (i, j))
    res = _pallas(body, (a, b, *extras), name=name, grid=(M // tm, N // tn, nk),
                  in_specs=[a_spec, b_spec] + [o_spec] * ne, out_specs=[o_spec] * no,
                  out_shape=[jax.ShapeDtypeStruct((M, N), dt) for dt in out_dtypes],
                  scratch_shapes=[pltpu.VMEM((tm, tn), F32)] if use_acc else [],
                  sem=("parallel", "parallel", "arbitrary"), comm=comm)
    outs = res[0] if no == 1 else res[:no]
    return (outs, res[no:]) if comm else outs


def _row_spec(tr, C, off):
    return pl.BlockSpec((tr, C), lambda j, i: (i, off + j))


def _par_spec(k, C, off):
    if off is None:
        return pl.BlockSpec((k, C), lambda j, i: (0, 0))
    return pl.BlockSpec((k, C), lambda j, i: (0, off + j))


def _rowwise(name, fn, rows, params, outs, reds=(), *, S, ncb=1, tr=256):
    tr = min(tr, S)
    nr, npar, no = len(rows), len(params), len(outs)

    def body(*refs):
        j, i = pl.program_id(0), pl.program_id(1)
        ins = [r[...].astype(F32) for r in refs[:nr + npar]]
        res = fn(i * tr, j, *ins)
        res = res if isinstance(res, (tuple, list)) else (res,)
        out_refs = refs[nr + npar:]
        for r, v in zip(out_refs[:no], res[:no]):
            r[...] = v.astype(r.dtype)
        for (k, C, per_j), r, v in zip(reds, out_refs[no:], res[no:]):
            first = (i == 0) if per_j else jnp.logical_and(i == 0, j == 0)

            @pl.when(first)
            def _(r=r):
                r[...] = jnp.zeros_like(r)

            r[...] += v

    in_specs = [_row_spec(tr, C, off) for (_, off, C) in rows] + [_par_spec(a.shape[0], C, off) for (a, off, C) in params]
    out_specs = [pl.BlockSpec((tr, C), lambda j, i: (i, j)) for (C, _) in outs]
    out_specs += [pl.BlockSpec((k, C), (lambda j, i: (0, j)) if per_j else (lambda j, i: (0, 0))) for (k, C, per_j) in reds]
    out_shape = [jax.ShapeDtypeStruct((S, ncb * C), dt) for (C, dt) in outs]
    out_shape += [jax.ShapeDtypeStruct((k, C * (ncb if per_j else 1)), F32) for (k, C, per_j) in reds]
    res = pl.pallas_call(
        body, name=name, grid=(ncb, S // tr), in_specs=in_specs, out_specs=out_specs, out_shape=out_shape,
        compiler_params=_cparams(("arbitrary", "arbitrary")),
    )(*[a for (a, _, _) in rows], *[a for (a, _, _) in params])
    return res


def _rowwise_bwd(name, fn, rows, params, cots, drow, adds=None, *, S, ncb=1, tr=128):
    tr = min(tr, S)
    nr, npar, nc = len(rows), len(params), len(cots)
    adds = adds or [None] * nr
    add_list = [a for a in adds if a is not None]
    na = len(add_list)

    def body(*refs):
        j, i = pl.program_id(0), pl.program_id(1)
        ins = [r[...].astype(F32) for r in refs[:nr + npar]]
        cts = [r[...].astype(F32) for r in refs[nr + npar:nr + npar + nc]]
        add_refs = list(refs[nr + npar + nc:nr + npar + nc + na])
        out_refs = list(refs[nr + npar + nc + na:])

        def f(*a):
            res = fn(i * tr, j, *a)
            return tuple(res) if isinstance(res, (tuple, list)) else (res,)

        _, vjp = jax.vjp(f, *ins)
        grads = vjp(tuple(cts))
        for idx in range(nr):
            if drow[idx] is None:
                continue
            g = grads[idx]
            if adds[idx] is not None:
                g = g + add_refs.pop(0)[...].astype(F32)
            r = out_refs.pop(0)
            r[...] = g.astype(r.dtype)
        for idx in range(npar):
            per_j = params[idx][1] is not None
            first = (i == 0) if per_j else jnp.logical_and(i == 0, j == 0)
            r = out_refs.pop(0)

            @pl.when(first)
            def _(r=r):
                r[...] = jnp.zeros_like(r)

            r[...] += grads[nr + idx]

    in_specs = [_row_spec(tr, C, off) for (_, off, C) in rows]
    in_specs += [_par_spec(a.shape[0], C, off) for (a, off, C) in params]
    in_specs += [_row_spec(tr, C, off) for (_, off, C) in cots]
    in_specs += [_row_spec(tr, C, off) for (_, off, C) in add_list]
    out_specs, out_shape = [], []
    for idx in range(nr):
        if drow[idx] is not None:
            C = rows[idx][2]
            out_specs.append(pl.BlockSpec((tr, C), lambda j, i: (i, j)))
            out_shape.append(jax.ShapeDtypeStruct((S, ncb * C), drow[idx]))
    for (a, off, C) in params:
        per_j = off is not None
        out_specs.append(pl.BlockSpec((a.shape[0], C), (lambda j, i: (0, j)) if per_j else (lambda j, i: (0, 0))))
        out_shape.append(jax.ShapeDtypeStruct((a.shape[0], C * (ncb if per_j else 1)), F32))
    return pl.pallas_call(
        body, name=name, grid=(ncb, S // tr), in_specs=in_specs, out_specs=out_specs, out_shape=out_shape,
        compiler_params=_cparams(("arbitrary", "arbitrary")),
    )(*[a for (a, _, _) in rows], *[a for (a, _, _) in params], *[a for (a, _, _) in cots], *[a for (a, _, _) in add_list])


def _halo_rows(K):
    return 8 * ((K - 1 + 7) // 8)


def _inv_count(row0, tr, win):
    t = (row0 + lax.broadcasted_iota(jnp.int32, (tr, 1), 0)).astype(F32)
    return 1.0 / jnp.minimum(t + 1.0, win)


CONV_ROWS = 32


def _shifted_down(xp, rows, K):
    for p in sorted({s % 8 for s in range(K)} - {0}):
        xp[p, 8:rows, :] = xp[0, 8 - p:rows - p, :]


def _shifted_up(yp, rows, K):
    for p in sorted({s % 8 for s in range(K)} - {0}):
        yp[p, 0:rows - 8, :] = yp[0, p:rows - 8 + p, :]


def _dwconv_fwd(name, x, x_off, w, *, S, C, bias=None, win=None, out_dtype=F32, cb=512, tr=256, comm=()):
    K = w.shape[0]
    cb, tr = _pick(C, cb), min(tr, S)
    HB = min(_halo_rows(K), tr)
    assert K - 1 <= HB and tr % HB == 0 and C % cb == 0
    nb = tr // HB
    RB = min(CONV_ROWS, tr)
    extra = [a for a in (bias, win) if a is not None]

    def body(xh_ref, x_ref, w_ref, *rest):
        y_ref, xp = rest[-2], rest[-1]
        i = pl.program_id(1)
        xp[0, 0:HB, :] = jnp.where(i > 0, xh_ref[...].astype(F32), 0.0)
        xp[0, HB:HB + tr, :] = x_ref[...].astype(F32)
        _shifted_down(xp, HB + tr, K)

        def sub(rb, carry):
            r0 = rb * RB
            acc = jnp.zeros((RB, cb), F32)
            for jj in range(K):
                s = K - 1 - jj
                start = pl.multiple_of(HB - 8 * (s // 8) + r0, 8)
                acc = acc + w_ref[jj:jj + 1, :] * xp[s % 8, pl.ds(start, RB), :]
            if bias is not None:
                acc = acc + rest[0][...]
            if win is not None:
                rows = pl.ds(pl.multiple_of(r0, 8), RB)
                acc = acc * _inv_count(i * tr + r0, RB, rest[0][...]) - x_ref[rows, :].astype(F32)
            y_ref[pl.ds(pl.multiple_of(r0, 8), RB), :] = acc.astype(y_ref.dtype)
            return carry

        lax.fori_loop(0, tr // RB, sub, 0)

    in_specs = [pl.BlockSpec((HB, cb), lambda j, i: (jnp.maximum(i * nb - 1, 0), x_off + j)),
                pl.BlockSpec((tr, cb), lambda j, i: (i, x_off + j)),
                pl.BlockSpec((K, cb), lambda j, i: (0, j))]
    in_specs += [pl.BlockSpec((1, cb), lambda j, i: (0, j)) for _ in extra]
    res = _pallas(body, (x, x, w, *extra), name=name, grid=(C // cb, S // tr), in_specs=in_specs,
                  out_specs=[pl.BlockSpec((tr, cb), lambda j, i: (i, j))],
                  out_shape=[jax.ShapeDtypeStruct((S, C), out_dtype)],
                  scratch_shapes=[pltpu.VMEM((8, HB + tr, cb), F32)], sem=("parallel", "arbitrary"), comm=comm)
    return (res[0], res[1:]) if comm else res[0]


def _dwconv_bwd(name, x, x_off, dy, w, *, S, C, win=None, want_dw=True, cb=512, tr=256, comm=()):
    K = w.shape[0]
    cb, tr = _pick(C, cb), min(tr, S)
    HB = min(_halo_rows(K), tr)
    nb, nt = tr // HB, S // tr
    RB = min(CONV_ROWS, tr)

    def body(*refs):
        if want_dw:
            xh_ref, x_ref, dy_ref, dyn_ref, w_ref = refs[:5]
            rest = refs[5:]
        else:
            dy_ref, dyn_ref, w_ref = refs[:3]
            rest = refs[3:]
        i = pl.program_id(1)
        dyt = dy_ref[...].astype(F32)
        dyn = jnp.where(i < nt - 1, dyn_ref[...].astype(F32), 0.0)
        if win is not None:
            win_v = rest[0][...]
            rest = rest[1:]
            yy_t = dyt * _inv_count(i * tr, tr, win_v)
            dyn = dyn * _inv_count((i + 1) * tr, HB, win_v)
        else:
            yy_t = dyt
        if want_dw:
            dx_ref, dw_ref, db_ref, yp, xp, dw8, db8 = rest
        else:
            dx_ref, yp = rest
        yp[0, 0:tr, :] = yy_t
        yp[0, tr:tr + HB, :] = dyn
        _shifted_up(yp, tr + HB, K)

        def sub(rb, carry):
            r0 = rb * RB
            acc = jnp.zeros((RB, cb), F32)
            for jj in range(K):
                s = K - 1 - jj
                start = pl.multiple_of(8 * (s // 8) + r0, 8)
                acc = acc + w_ref[jj:jj + 1, :] * yp[s % 8, pl.ds(start, RB), :]
            rows = pl.ds(pl.multiple_of(r0, 8), RB)
            if win is not None:
                acc = acc - dy_ref[rows, :].astype(F32)
            dx_ref[rows, :] = acc.astype(dx_ref.dtype)
            return carry

        lax.fori_loop(0, tr // RB, sub, 0)
        if want_dw:
            xp[0, 0:HB, :] = jnp.where(i > 0, xh_ref[...].astype(F32), 0.0)
            xp[0, HB:HB + tr, :] = x_ref[...].astype(F32)
            _shifted_down(xp, HB + tr, K)

            @pl.when(i == 0)
            def _():
                dw8[...] = jnp.zeros_like(dw8)
                db8[...] = jnp.zeros_like(db8)

            def sub_w(rb, carry):
                r0 = rb * RB
                dyb = dy_ref[pl.ds(pl.multiple_of(r0, 8), RB), :].astype(F32)
                for jj in range(K):
                    s = K - 1 - jj
                    start = pl.multiple_of(HB - 8 * (s // 8) + r0, 8)
                    prod = dyb * xp[s % 8, pl.ds(start, RB), :]
                    dw8[jj] += jnp.sum(prod.reshape(RB // 8, 8, cb), axis=0)
                db8[...] += jnp.sum(dyb.reshape(RB // 8, 8, cb), axis=0)
                return carry

            lax.fori_loop(0, tr // RB, sub_w, 0)

            @pl.when(i == nt - 1)
            def _():
                for jj in range(K):
                    dw_ref[jj:jj + 1, :] = jnp.sum(dw8[jj], axis=0, keepdims=True)
                db_ref[...] = jnp.sum(db8[...], axis=0, keepdims=True)

    last = S // HB - 1
    in_specs, args = [], []
    if want_dw:
        in_specs += [pl.BlockSpec((HB, cb), lambda j, i: (jnp.maximum(i * nb - 1, 0), x_off + j)),
                     pl.BlockSpec((tr, cb), lambda j, i: (i, x_off + j))]
        args += [x, x]
    in_specs += [pl.BlockSpec((tr, cb), lambda j, i: (i, j)),
                 pl.BlockSpec((HB, cb), lambda j, i: (jnp.minimum((i + 1) * nb, last), j)),
                 pl.BlockSpec((K, cb), lambda j, i: (0, j))]
    args += [dy, dy, w]
    if win is not None:
        in_specs.append(pl.BlockSpec((1, cb), lambda j, i: (0, j)))
        args.append(win)
    out_specs = [pl.BlockSpec((tr, cb), lambda j, i: (i, j))]
    out_shape = [jax.ShapeDtypeStruct((S, C), F32)]
    scratch = [pltpu.VMEM((8, tr + HB, cb), F32)]
    if want_dw:
        out_specs += [pl.BlockSpec((K, cb), lambda j, i: (0, j)), pl.BlockSpec((1, cb), lambda j, i: (0, j))]
        out_shape += [jax.ShapeDtypeStruct((K, C), F32), jax.ShapeDtypeStruct((1, C), F32)]
        scratch += [pltpu.VMEM((8, HB + tr, cb), F32), pltpu.VMEM((K, 8, cb), F32), pltpu.VMEM((8, cb), F32)]
    res = _pallas(body, args, name=name, grid=(C // cb, S // tr), in_specs=in_specs, out_specs=out_specs,
                  out_shape=out_shape, scratch_shapes=scratch, sem=("parallel", "arbitrary"), comm=comm)
    return (res[:len(out_specs)], res[len(out_specs):]) if comm else res


def _dot(a, b, dims, hi=False):
    if hi:
        return lax.dot_general(a, b, (dims, ((), ())), precision=HI, preferred_element_type=F32)
    return lax.dot_general(a.astype(BF16), b.astype(BF16), (dims, ((), ())), preferred_element_type=F32)


_NN, _NT, _TN = ((1,), (0,)), ((1,), (1,)), ((0,), (0,))


def _col(m, idx):
    lane = lax.broadcasted_iota(jnp.int32, m.shape, 1)
    return jnp.sum(jnp.where(lane == idx, m, 0.0), axis=1, keepdims=True)


def _row(m, idx):
    sub = lax.broadcasted_iota(jnp.int32, m.shape, 0)
    return jnp.sum(jnp.where(sub == idx, m, 0.0), axis=0, keepdims=True)


def _delta_chunk(q, k, v, beta, gcc, gcr, causal, strict, eye, scale, C):
    d = {}
    gam = jnp.where(causal, jnp.exp(jnp.where(causal, gcc - gcr, 0.0)), 0.0)
    eg = jnp.exp(gcc)
    g_last = _row(gcc, C - 1)
    d["gam"], d["eg"], d["g_last"] = gam, eg, g_last
    d["ek"] = jnp.exp(g_last - gcc)
    d["decay"] = jnp.exp(g_last)
    qs = q * scale
    kb = k * beta
    d["qs"], d["kb"] = qs, kb
    d["kk"] = _dot(kb, k, _NT)
    d["A"] = jnp.where(strict, d["kk"] * gam, 0.0)
    d["qk"] = _dot(qs, k, _NT)
    d["attn"] = jnp.where(causal, d["qk"] * gam, 0.0)
    d["vb"] = v * beta
    d["kbg"] = kb * eg
    d["qg"] = qs * eg
    d["kd"] = k * d["ek"]
    return d


def _split(m):
    hi = m.astype(BF16)
    return hi, (m - hi.astype(F32)).astype(BF16)


def _dot3(a, b, dims):
    return _dot(a[0], b[0], dims) + (_dot(a[0], b[1], dims) + _dot(a[1], b[0], dims))


def _tri_inverse(As, eye):
    P = [-A for A in As]
    T = [eye + p for p in P]
    n = 1
    while 2 * n < As[0].shape[0]:
        Ps = [_split(p) for p in P]
        P = [_dot3(ps, ps, _NN) for ps in Ps]
        Ts = [_split(t) for t in T]
        Ps = [_split(p) for p in P]
        T = [t + _dot3(ts, ps, _NN) for t, ts, ps in zip(T, Ts, Ps)]
        n *= 2
    return T


def _delta_fwd(name, qk, v, gb, gT, *, S, H, dh, comm=()):
    C = min(DN_CHUNK, S)
    N, W = S // C, H * dh
    scale = dh ** -0.5

    def body(qk_ref, v_ref, gb_ref, gT_ref, o_ref, sp_ref, T_ref, st):
        n = pl.program_id(0)

        @pl.when(n == 0)
        def _():
            st[...] = jnp.zeros_like(st)

        r = lax.broadcasted_iota(jnp.int32, (C, C), 0)
        c = lax.broadcasted_iota(jnp.int32, (C, C), 1)
        causal, strict = r >= c, r > c
        eye = (r == c).astype(F32)
        Lt = causal.astype(F32)
        gbv = gb_ref[...]
        gcum = _dot(Lt, gbv, _NN, hi=True)
        gcumT = _dot(gT_ref[0], Lt, _NT, hi=True)
        hs = range(H)
        sl = [slice(h * dh, (h + 1) * dh) for h in hs]
        d = [_delta_chunk(qk_ref[:, sl[h]], qk_ref[:, W + h * dh:W + (h + 1) * dh], v_ref[:, sl[h]], _col(gbv, h),
                          _col(gcum, H + h), _row(gcumT, h), causal, strict, eye, scale, C) for h in hs]
        T = _tri_inverse([d[h]["A"] for h in hs], eye)
        u = [_dot(T[h], d[h]["vb"], _NN) for h in hs]
        w = [_dot(T[h], d[h]["kbg"], _NN) for h in hs]
        s0 = [st[h] for h in hs]
        ws = [_dot(w[h], s0[h], _NN) for h in hs]
        qs0 = [_dot(d[h]["qg"], s0[h], _NN) for h in hs]
        v_new = [u[h] - ws[h] for h in hs]
        av = [_dot(d[h]["attn"], v_new[h], _NN) for h in hs]
        kv = [_dot(d[h]["kd"], v_new[h], _TN) for h in hs]
        for h in hs:
            sp_ref[0, h] = s0[h]
            T_ref[0, h] = T[h]
            o_ref[:, sl[h]] = qs0[h] + av[h]
            st[h] = s0[h] * d[h]["decay"] + kv[h]

    res = _pallas(
        body, (qk, v, gb, gT), name=name, grid=(N,),
        in_specs=[pl.BlockSpec((C, 2 * W), lambda n: (n, 0)), pl.BlockSpec((C, W), lambda n: (n, 0)),
                  pl.BlockSpec((C, LANE), lambda n: (n, 0)), pl.BlockSpec((1, H, C), lambda n: (n, 0, 0))],
        out_specs=[pl.BlockSpec((C, W), lambda n: (n, 0)), pl.BlockSpec((1, H, dh, dh), lambda n: (n, 0, 0, 0)),
                   pl.BlockSpec((1, H, C, C), lambda n: (n, 0, 0, 0))],
        out_shape=[jax.ShapeDtypeStruct((S, W), F32), jax.ShapeDtypeStruct((N, H, dh, dh), F32),
                   jax.ShapeDtypeStruct((N, H, C, C), F32)],
        scratch_shapes=[pltpu.VMEM((H, dh, dh), F32)], sem=("arbitrary",), comm=comm)
    return (res[:3], res[3:]) if comm else res[:3]


def _delta_bwd(name, qk, v, gb, gT, sp, Tm, do, *, S, H, dh, comm=()):
    C = min(DN_CHUNK, S)
    N, W = S // C, H * dh
    scale = dh ** -0.5

    def body(qk_ref, v_ref, gb_ref, gT_ref, sp_ref, T_ref, do_ref, dqk_ref, dv_ref, dgb_ref, ds):
        n = pl.program_id(0)

        @pl.when(n == 0)
        def _():
            ds[...] = jnp.zeros_like(ds)

        r = lax.broadcasted_iota(jnp.int32, (C, C), 0)
        c = lax.broadcasted_iota(jnp.int32, (C, C), 1)
        causal, strict = r >= c, r > c
        eye = (r == c).astype(F32)
        Lt = causal.astype(F32)
        ones = jnp.ones((C, LANE), F32)
        lane = lax.broadcasted_iota(jnp.int32, (C, LANE), 1)
        rowi = lax.broadcasted_iota(jnp.int32, (C, 1), 0)
        gbv = gb_ref[...]
        gcum = _dot(Lt, gbv, _NN, hi=True)
        gcumT = _dot(gT_ref[0], Lt, _NT, hi=True)
        dgc_all = jnp.zeros((C, LANE), F32)
        dbeta_all = jnp.zeros((C, LANE), F32)
        hs = range(H)
        sl = [slice(h * dh, (h + 1) * dh) for h in hs]
        ksl = [slice(W + h * dh, W + (h + 1) * dh) for h in hs]
        k = [qk_ref[:, ksl[h]] for h in hs]
        vv = [v_ref[:, sl[h]] for h in hs]
        beta = [_col(gbv, h) for h in hs]
        d = [_delta_chunk(qk_ref[:, sl[h]], k[h], vv[h], beta[h], _col(gcum, H + h), _row(gcumT, h), causal, strict,
                          eye, scale, C) for h in hs]
        T = [T_ref[0, h] for h in hs]
        s0 = [sp_ref[0, h] for h in hs]
        dO = [do_ref[:, sl[h]] for h in hs]
        dS = [ds[h] for h in hs]
        u = [_dot(T[h], d[h]["vb"], _NN) for h in hs]
        w = [_dot(T[h], d[h]["kbg"], _NN) for h in hs]
        ws = [_dot(w[h], s0[h], _NN) for h in hs]
        v_new = [u[h] - ws[h] for h in hs]
        dv_new = [_dot(d[h]["attn"], dO[h], _TN) + _dot(d[h]["kd"], dS[h], _NN) for h in hs]
        dattn = [jnp.where(causal, _dot(dO[h], v_new[h], _NT), 0.0) for h in hs]
        dqg = [_dot(dO[h], s0[h], _NT) for h in hs]
        dkd = [_dot(v_new[h], dS[h], _NT) for h in hs]
        ddecay = [jnp.sum(jnp.sum(s0[h] * dS[h], axis=1, keepdims=True), axis=0, keepdims=True) for h in hs]
        ds_new = [_dot(d[h]["qg"], dO[h], _TN) + d[h]["decay"] * dS[h] - _dot(w[h], dv_new[h], _TN) for h in hs]
        dw = [-_dot(dv_new[h], s0[h], _NT) for h in hs]
        for h in hs:
            ds[h] = ds_new[h]
        dT = [_dot(dv_new[h], d[h]["vb"], _NT) + _dot(dw[h], d[h]["kbg"], _NT) for h in hs]
        dvb = [_dot(T[h], dv_new[h], _TN) for h in hs]
        dkbg = [_dot(T[h], dw[h], _TN) for h in hs]
        Ts = [_split(T[h]) for h in hs]
        x1 = [_dot3(Ts[h], _split(dT[h]), _TN) for h in hs]
        dA = [jnp.where(strict, -_dot3(_split(x1[h]), Ts[h], _NT), 0.0) for h in hs]
        dkk = [dA[h] * d[h]["gam"] for h in hs]
        dqk_m = [dattn[h] * d[h]["gam"] for h in hs]
        m = [_split(dA[h] * d[h]["A"] + dattn[h] * d[h]["attn"]) for h in hs]
        msum = [jnp.sum(dA[h] * d[h]["A"] + dattn[h] * d[h]["attn"], axis=1, keepdims=True) for h in hs]
        mcol = [jnp.max(_dot(m[h][0], ones, _TN) + _dot(m[h][1], ones, _TN), axis=1, keepdims=True) for h in hs]
        dkb = [_dot(dkk[h], k[h], _NN) + dkbg[h] * d[h]["eg"] for h in hs]
        dk = [_dot(dkk[h], d[h]["kb"], _TN) + _dot(dqk_m[h], d[h]["qs"], _TN) + dkd[h] * d[h]["ek"] + dkb[h] * beta[h]
              for h in hs]
        dqs = [_dot(dqk_m[h], k[h], _NN) + dqg[h] * d[h]["eg"] for h in hs]
        for h in hs:
            r_kd = jnp.sum(dkd[h] * d[h]["kd"], axis=1, keepdims=True)
            dgc = (msum[h] - mcol[h] + jnp.sum(dqg[h] * d[h]["qg"], axis=1, keepdims=True) - r_kd
                   + jnp.sum(dkbg[h] * d[h]["kbg"], axis=1, keepdims=True))
            dg_last = jnp.sum(r_kd, axis=0, keepdims=True) + ddecay[h] * d[h]["decay"]
            dgc = dgc + jnp.where(rowi == C - 1, dg_last, 0.0)
            dbeta = jnp.sum(dkb[h] * k[h], axis=1, keepdims=True) + jnp.sum(dvb[h] * vv[h], axis=1, keepdims=True)
            dqk_ref[:, sl[h]] = dqs[h] * scale
            dqk_ref[:, ksl[h]] = dk[h]
            dv_ref[:, sl[h]] = dvb[h] * beta[h]
            dgc_all = dgc_all + jnp.where(lane == H + h, dgc, 0.0)
            dbeta_all = dbeta_all + jnp.where(lane == h, dbeta, 0.0)
        dgb_ref[...] = _dot(Lt, dgc_all, _TN, hi=True) + dbeta_all

    rev = lambda n: N - 1 - n
    res = _pallas(
        body, (qk, v, gb, gT, sp, Tm, do), name=name, grid=(N,),
        in_specs=[pl.BlockSpec((C, 2 * W), lambda n: (rev(n), 0)), pl.BlockSpec((C, W), lambda n: (rev(n), 0)),
                  pl.BlockSpec((C, LANE), lambda n: (rev(n), 0)), pl.BlockSpec((1, H, C), lambda n: (rev(n), 0, 0)),
                  pl.BlockSpec((1, H, dh, dh), lambda n: (rev(n), 0, 0, 0)),
                  pl.BlockSpec((1, H, C, C), lambda n: (rev(n), 0, 0, 0)),
                  pl.BlockSpec((C, W), lambda n: (rev(n), 0))],
        out_specs=[pl.BlockSpec((C, 2 * W), lambda n: (rev(n), 0)), pl.BlockSpec((C, W), lambda n: (rev(n), 0)),
                   pl.BlockSpec((C, LANE), lambda n: (rev(n), 0))],
        out_shape=[jax.ShapeDtypeStruct((S, 2 * W), F32), jax.ShapeDtypeStruct((S, W), F32),
                   jax.ShapeDtypeStruct((S, LANE), F32)],
        scratch_shapes=[pltpu.VMEM((H, dh, dh), F32)], sem=("arbitrary",), comm=comm)
    return (res[:3], res[3:]) if comm else res[:3]


_ANY = pl.BlockSpec(memory_space=pl.ANY)


ICI_CHUNKS = 4
D2D_CHUNKS = 4
EXCHANGE_PARTS = 3


def _place():
    return lax.axis_index("x"), lax.axis_index("y"), lax.axis_index("c")


def _row_chunks(rows, n):
    n = max(1, min(n, rows // 8))
    while n > 1 and (rows % n or (rows // n) % 8):
        n -= 1
    return [(k * (rows // n), rows // n) for k in range(n)]


class _Plug:
    def __init__(self, ins, outs, sems, start, finish, after):
        self.ins, self.outs, self.sems, self.start, self.finish, self.after = ins, outs, sems, start, finish, after


def _run_plug(name, plug):
    def body(*refs):
        ni, no = len(plug.ins), len(plug.outs)
        plug.start(refs[:ni], refs[ni:ni + no], refs[ni + no:])
        plug.finish(refs[:ni], refs[ni:ni + no], refs[ni + no:])

    return pl.pallas_call(body, name=name, in_specs=[_ANY] * len(plug.ins), out_specs=[_ANY] * len(plug.outs),
                          out_shape=list(plug.outs), scratch_shapes=list(plug.sems))(*plug.ins)


def _gather_plug(flat):
    R, L = flat.shape
    Rh = R // 2
    ici = _row_chunks(Rh, ICI_CHUNKS)
    sub = _row_chunks(ici[0][1], D2D_CHUNKS)
    ni, ns = len(ici), len(sub)

    def parts(ins, outs, sems):
        (x_ref,), (out_ref,), (send_sems, recv_sems) = ins, outs, sems
        x, y, c = _place()
        chips = [(1 - x, y), (x, 1 - y), (1 - x, 1 - y)]

        def rows(px, py, pc, r0, n):
            return out_ref.at[2 * px + py, pl.ds(pc * Rh + r0, n), :]

        def copy(k, src, dst, to):
            return pltpu.make_async_remote_copy(src_ref=src, dst_ref=dst, send_sem=send_sems.at[k],
                                                recv_sem=recv_sems.at[k], device_id=to, device_id_type=MESH)

        first = [copy(k * ni + q, x_ref.at[pl.ds(c * Rh + r0, n), :], rows(x, y, c, r0, n), (*chip, c))
                 for k, chip in enumerate(chips) for q, (r0, n) in enumerate(ici)]
        return x_ref, (x, y, c), chips, rows, copy, first

    def start(ins, outs, sems):
        for cp in parts(ins, outs, sems)[-1]:
            cp.start()

    def finish(ins, outs, sems):
        x_ref, (x, y, c), chips, rows, copy, first = parts(ins, outs, sems)
        sibling = (x, y, 1 - c)
        passed = []
        for k, chip in enumerate(chips):
            for q, (r0, n) in enumerate(ici):
                copy(k * ni + q, x_ref.at[pl.ds(r0, n), :], rows(*chip, c, r0, n), (*chip, c)).wait_recv()
                for t, (s0, m) in enumerate(sub):
                    cp = copy(3 * ni + (k * ni + q) * ns + t, rows(*chip, c, r0 + s0, m), rows(*chip, c, r0 + s0, m), sibling)
                    cp.start()
                    passed.append(cp)
        for k, chip in enumerate(chips):
            for q, (r0, n) in enumerate(ici):
                for t, (s0, m) in enumerate(sub):
                    copy(3 * ni + (k * ni + q) * ns + t, x_ref.at[pl.ds(r0, m), :], rows(*chip, 1 - c, r0 + s0, m),
                         sibling).wait_recv()
        for cp in first + passed:
            cp.wait_send()

    def after(res):
        return lax.dynamic_update_slice(res[0], flat[None], (2 * lax.axis_index("x") + lax.axis_index("y"), 0, 0))

    nsem = 3 * ni * (1 + ns)
    return _Plug([flat], [jax.ShapeDtypeStruct((N_CHIPS, R, L), flat.dtype)],
                 [pltpu.SemaphoreType.DMA((nsem,)), pltpu.SemaphoreType.DMA((nsem,))], start, finish, after)


def _sibling_split(name, g):
    _, R, L = g.shape
    Rh = R // 2

    chunks = [(j, r0, n) for j in range(N_CHIPS) for (r0, n) in _row_chunks(Rh, D2D_CHUNKS)]

    def body(g_ref, got_ref, send_sems, recv_sems):
        x, y, c = _place()
        cps = [pltpu.make_async_remote_copy(src_ref=g_ref.at[j, pl.ds((1 - c) * Rh + r0, n), :],
                                            dst_ref=got_ref.at[j, pl.ds(r0, n), :], send_sem=send_sems.at[k],
                                            recv_sem=recv_sems.at[k], device_id=(x, y, 1 - c), device_id_type=MESH)
               for k, (j, r0, n) in enumerate(chunks)]
        for cp in cps:
            cp.start()
        for cp in cps:
            cp.wait()

    sems = pltpu.SemaphoreType.DMA((len(chunks),))
    got = pl.pallas_call(
        body, name=name, in_specs=[_ANY], out_specs=_ANY, out_shape=jax.ShapeDtypeStruct((N_CHIPS, Rh, L), g.dtype),
        scratch_shapes=[sems, sems],
    )(g)
    own = lax.dynamic_slice(g, (0, lax.axis_index("c") * Rh, 0), (N_CHIPS, Rh, L))
    return own, got


def _exchange_plug(p, row0, nrows):
    ici = _row_chunks(nrows, ICI_CHUNKS)
    ni = len(ici)

    def sends(ins, outs, sems):
        (p_ref,), (q_ref,), (send_sems, recv_sems) = ins, outs, sems
        x, y, c = _place()
        me = 2 * x + y
        chips = [(1 - x, y), (x, 1 - y), (1 - x, 1 - y)]
        return [pltpu.make_async_remote_copy(src_ref=p_ref.at[2 * cx + cy, pl.ds(row0 + r0, n), :],
                                             dst_ref=q_ref.at[me, pl.ds(r0, n), :],
                                             send_sem=send_sems.at[k * ni + q], recv_sem=recv_sems.at[k * ni + q],
                                             device_id=(cx, cy, c), device_id_type=MESH)
                for k, (cx, cy) in enumerate(chips) for q, (r0, n) in enumerate(ici)]

    def start(ins, outs, sems):
        for cp in sends(ins, outs, sems):
            cp.start()

    def finish(ins, outs, sems):
        (p_ref,), (q_ref,), (send_sems, recv_sems) = ins, outs, sems
        x, y, c = _place()
        me = 2 * x + y
        chips = [(1 - x, y), (x, 1 - y), (1 - x, 1 - y)]
        for k, (cx, cy) in enumerate(chips):
            for q, (r0, n) in enumerate(ici):
                pltpu.make_async_remote_copy(src_ref=p_ref.at[me, pl.ds(r0, n), :],
                                             dst_ref=q_ref.at[2 * cx + cy, pl.ds(r0, n), :],
                                             send_sem=send_sems.at[k * ni + q], recv_sem=recv_sems.at[k * ni + q],
                                             device_id=(cx, cy, c), device_id_type=MESH).wait_recv()
        for cp in sends(ins, outs, sems):
            cp.wait_send()

    def after(res):
        me = 2 * lax.axis_index("x") + lax.axis_index("y")
        mine = lax.dynamic_slice(p, (me, row0, 0), (1, nrows, p.shape[2]))
        return lax.dynamic_update_slice(res[0], mine, (me, 0, 0))

    return _Plug([p], [jax.ShapeDtypeStruct((N_CHIPS, nrows, p.shape[2]), p.dtype)],
                 [pltpu.SemaphoreType.DMA((3 * ni,)), pltpu.SemaphoreType.DMA((3 * ni,))], start, finish, after)


def _sibling_swap(name, half):
    Rh, L = half.shape
    chunks = _row_chunks(Rh, 2 * D2D_CHUNKS)

    def body(h_ref, out_ref, send_sems, recv_sems):
        x, y, c = _place()
        cps = [pltpu.make_async_remote_copy(src_ref=h_ref.at[pl.ds(r0, n), :], dst_ref=out_ref.at[pl.ds(r0, n), :],
                                            send_sem=send_sems.at[k], recv_sem=recv_sems.at[k], device_id=(x, y, 1 - c),
                                            device_id_type=MESH)
               for k, (r0, n) in enumerate(chunks)]
        for cp in cps:
            cp.start()
        for cp in cps:
            cp.wait()

    sems = pltpu.SemaphoreType.DMA((len(chunks),))
    return pl.pallas_call(
        body, name=name, in_specs=[_ANY], out_specs=_ANY, out_shape=jax.ShapeDtypeStruct((Rh, L), half.dtype),
        scratch_shapes=[sems, sems],
    )(half)


def _add_pairs(name, a, b, out_dtype):
    n, Rh, L = a.shape
    tr = _pick(Rh, 512, 8)

    def body(a_ref, b_ref, o_ref):
        o_ref[...] = (a_ref[...].astype(F32) + b_ref[...].astype(F32)).astype(o_ref.dtype)

    spec = pl.BlockSpec((1, tr, L), lambda j, i: (j, i, 0))
    return pl.pallas_call(body, name=name, grid=(n, Rh // tr), in_specs=[spec, spec], out_specs=spec,
                          out_shape=jax.ShapeDtypeStruct(a.shape, out_dtype),
                          compiler_params=_cparams(("parallel", "parallel")))(a, b)


def _sum_chips(name, q):
    n, Rh, L = q.shape
    tr = _pick(Rh, 512, 8)

    def body(q_ref, o_ref):
        acc = q_ref[0].astype(F32)
        for s in range(1, n):
            acc = acc + q_ref[s].astype(F32)
        o_ref[...] = acc

    return pl.pallas_call(body, name=name, grid=(Rh // tr,),
                          in_specs=[pl.BlockSpec((n, tr, L), lambda i: (0, i, 0))],
                          out_specs=pl.BlockSpec((tr, L), lambda i: (i, 0)),
                          out_shape=jax.ShapeDtypeStruct((Rh, L), F32),
                          compiler_params=_cparams(("parallel",)))(q)


def _adamw(name, w, m, v, g):
    shape, size = w.shape, w.size
    rows = -(-size // FLAT_L)
    rows_p = -(-rows // ADAM_ROWS) * ADAM_ROWS
    pad = rows_p * FLAT_L - size

    def flat2d(a):
        a = a.reshape(-1)
        if pad:
            a = jnp.pad(a, (0, pad), constant_values=1.0)
        return a.reshape(rows_p, FLAT_L)

    c1 = 1.0 / (1.0 - ADAM_B1 ** ADAM_STEP)
    c2 = 1.0 / (1.0 - ADAM_B2 ** ADAM_STEP)

    def body(w_ref, m_ref, v_ref, g_ref, go_ref, d_ref, mo_ref, vo_ref):
        g = g_ref[...]
        wv = w_ref[...]
        mn = ADAM_B1 * m_ref[...] + (1.0 - ADAM_B1) * g
        vn = ADAM_B2 * v_ref[...] + (1.0 - ADAM_B2) * (g * g)
        go_ref[...] = g
        mo_ref[...] = mn
        vo_ref[...] = vn
        d_ref[...] = -ADAM_LR * ((mn * c1) / (jnp.sqrt(vn * c2) + ADAM_EPS) + ADAM_WD * wv)

    spec = pl.BlockSpec((ADAM_ROWS, FLAT_L), lambda i: (i, 0))
    shp = jax.ShapeDtypeStruct((rows_p, FLAT_L), F32)
    outs = pl.pallas_call(
        body, name=name, grid=(rows_p // ADAM_ROWS,),
        in_specs=[spec] * 4, out_specs=[spec] * 4, out_shape=[shp] * 4, compiler_params=_cparams(("parallel",)),
    )(flat2d(w), flat2d(m), flat2d(v), flat2d(g))

    def back(a):
        a = a.reshape(-1)
        if pad:
            a = a[:size]
        return a.reshape(shape)

    return tuple(back(a) for a in outs)


SEG_ROWS = 16


def _keeps_rows(shape):
    return len(shape) >= 2 and shape[-1] < FLAT_L and FLAT_L % shape[-1] != 0


def _seg_rows(shape):
    rows = math.prod(shape[:-1]) if _keeps_rows(shape) else -(-math.prod(shape) // FLAT_L)
    return -(-rows // SEG_ROWS) * SEG_ROWS


def _to_rows(a, dtype):
    rows = _seg_rows(a.shape)
    if _keeps_rows(a.shape):
        r = a.reshape(-1, a.shape[-1]).astype(dtype)
        return jnp.pad(r, ((0, rows - r.shape[0]), (0, FLAT_L - a.shape[-1])))
    flat = jnp.pad(a.reshape(-1).astype(dtype), (0, rows * FLAT_L - a.size))
    return flat.reshape(rows, FLAT_L)


def _from_rows(seg, shape):
    if _keeps_rows(shape):
        return seg[:math.prod(shape[:-1]), :shape[-1]].reshape(shape)
    return seg.reshape(-1)[:math.prod(shape)].reshape(shape)


def _pack(pieces, dtype, row_mult):
    segs, offs, r = [], [], 0
    for a in pieces:
        segs.append(_to_rows(a, dtype))
        offs.append(r)
        r += segs[-1].shape[0]
    tail = -r % row_mult
    if tail:
        segs.append(jnp.zeros((tail, FLAT_L), dtype))
    return jnp.concatenate(segs, axis=0), offs


def _segment(flat, off, shape):
    return _from_rows(flat[off:off + _seg_rows(shape)], shape)


def _gather_unit(shards, axes, dtype):
    flat, offs = _pack(shards, dtype, 64)
    plug = _gather_plug(flat)

    def unpack(full):
        return [jnp.concatenate([_segment(full[j], off, a.shape) for j in range(N_CHIPS)], axis=ax)
                for a, ax, off in zip(shards, axes, offs)]

    return plug, unpack


def _reduce_unit(name, grads, axes):
    pieces = [[] for _ in range(N_CHIPS)]
    for g, ax in zip(grads, axes):
        parts = jnp.split(g, N_CHIPS, axis=ax) if ax is not None else [g] * N_CHIPS
        for jc in range(N_CHIPS):
            pieces[jc].append(parts[jc])
    packed = [_pack(pieces[jc], BF16, 512) for jc in range(N_CHIPS)]
    gsend = jnp.stack([pk[0] for pk in packed])
    own, got = _sibling_split(name + "_split", gsend)
    pair = _add_pairs(name + "_add", own, got, BF16)
    return pair, packed[0][1], [a.shape for a in pieces[0]]


def _f_rms(row0, j, x, g):
    return _rms(x, g)


def _f_mid(row0, j, x, y, g_a, g_b):
    xn = x + _rms(y, g_a)
    return xn, _rms(xn, g_b)


def _f_resid(row0, j, x, y, g):
    return x + _rms(y, g)


def _relu2(u):
    r = jnp.maximum(u, 0.0)
    return r * r


def _relu2_bwd(d_act, act):
    return d_act * (2.0 * jnp.sqrt(act.astype(F32)))


def _f_l2silu(row0, j, c):
    a = _silu(c)
    return a * lax.rsqrt(jnp.sum(a * a, axis=-1, keepdims=True) + EPS)


def _f_silu(row0, j, c):
    return _silu(c)


def _f_scale(row0, j, y, s):
    return y * s


def _f_glu(row0, j, a, gate):
    return a * _sigmoid(gate)


def _f_lnsilu(row0, j, u, g, b):
    mu = jnp.mean(u, axis=-1, keepdims=True)
    uc = u - mu
    return _silu(uc * lax.rsqrt(jnp.mean(uc * uc, axis=-1, keepdims=True) + EPS) * g + b)


def _f_outgate(row0, j, o, z, g):
    return _rms(o, g) * _silu(z)


def _make_gates(H):
    def f(row0, j, ba, alog, dt):
        lane = lax.broadcasted_iota(jnp.int32, ba.shape, 1)
        beta = _sigmoid(ba)
        g = -jnp.exp(alog) * _softplus(ba + dt)
        return jnp.where(lane < H, beta, jnp.where(lane < 2 * H, g, 0.0))
    return f


def _f_loss(row0, j, y, t):
    e = y - t
    loss = 0.5 * jnp.sum(jnp.mean(e * e, axis=-1, keepdims=True), axis=0, keepdims=True)
    return e * (1.0 / y.shape[-1]), jnp.broadcast_to(loss, (1, LANE))


def _lane_row(vec, start):
    return jnp.pad(vec.astype(F32)[None, :], ((0, 0), (start, LANE - start - vec.shape[0])))


def kernel(x, norm_mix_pre, norm_mix_post, norm_mlp_pre, norm_mlp_post, even_w_in, even_conv, even_a_log, even_dt_bias, even_dn_norm, even_pool_w, even_pool_scale, even_w_out, odd_w_in, odd_dw, odd_dw_b, odd_ln_g, odd_ln_b, odd_w_out, mlp_w_up, mlp_w_down, loss_target, m_norm_mix_pre, m_norm_mix_post, m_norm_mlp_pre, m_norm_mlp_post, m_even_w_in, m_even_conv, m_even_a_log, m_even_dt_bias, m_even_dn_norm, m_even_pool_w, m_even_pool_scale, m_even_w_out, m_odd_w_in, m_odd_dw, m_odd_dw_b, m_odd_ln_g, m_odd_ln_b, m_odd_w_out, m_mlp_w_up, m_mlp_w_down, v_norm_mix_pre, v_norm_mix_post, v_norm_mlp_pre, v_norm_mlp_post, v_even_w_in, v_even_conv, v_even_a_log, v_even_dt_bias, v_even_dn_norm, v_even_pool_w, v_even_pool_scale, v_even_w_out, v_odd_w_in, v_odd_dw, v_odd_dw_b, v_odd_ln_g, v_odd_ln_b, v_odd_w_out, v_mlp_w_up, v_mlp_w_down):
    names = ["norm_mix_pre", "norm_mix_post", "norm_mlp_pre", "norm_mlp_post", "even_w_in", "even_conv", "even_a_log",
             "even_dt_bias", "even_dn_norm", "even_pool_w", "even_pool_scale", "even_w_out", "odd_w_in", "odd_dw",
             "odd_dw_b", "odd_ln_g", "odd_ln_b", "odd_w_out", "mlp_w_up", "mlp_w_down"]
    W = dict(zip(names, (norm_mix_pre, norm_mix_post, norm_mlp_pre, norm_mlp_post, even_w_in, even_conv, even_a_log,
                         even_dt_bias, even_dn_norm, even_pool_w, even_pool_scale, even_w_out, odd_w_in, odd_dw,
                         odd_dw_b, odd_ln_g, odd_ln_b, odd_w_out, mlp_w_up, mlp_w_down)))
    Mo = dict(zip(names, (m_norm_mix_pre, m_norm_mix_post, m_norm_mlp_pre, m_norm_mlp_post, m_even_w_in, m_even_conv,
                          m_even_a_log, m_even_dt_bias, m_even_dn_norm, m_even_pool_w, m_even_pool_scale, m_even_w_out,
                          m_odd_w_in, m_odd_dw, m_odd_dw_b, m_odd_ln_g, m_odd_ln_b, m_odd_w_out, m_mlp_w_up,
                          m_mlp_w_down)))
    Vo = dict(zip(names, (v_norm_mix_pre, v_norm_mix_post, v_norm_mlp_pre, v_norm_mlp_post, v_even_w_in, v_even_conv,
                          v_even_a_log, v_even_dt_bias, v_even_dn_norm, v_even_pool_w, v_even_pool_scale, v_even_w_out,
                          v_odd_w_in, v_odd_dw, v_odd_dw_b, v_odd_ln_g, v_odd_ln_b, v_odd_w_out, v_mlp_w_up,
                          v_mlp_w_down)))
    shard_axis = {"even_w_in": 2, "even_conv": 2, "even_pool_w": 2, "even_w_out": 1, "odd_w_in": 2, "odd_dw": 2,
                  "odd_dw_b": 1, "odd_ln_g": 1, "odd_ln_b": 1, "odd_w_out": 1, "mlp_w_up": 2, "mlp_w_down": 1}

    S, D = x.shape[1], x.shape[2]
    depth = norm_mix_pre.shape[0]
    H = even_a_log.shape[1]
    dh = even_dn_norm.shape[1]
    DNW = H * dh
    PW = even_pool_scale.shape[1]
    G = len(POOL_WINDOWS)
    PG = PW // G
    KC = even_conv.shape[1]
    BAW = 2 * LANE
    P_COLS = 4 * DNW + PW + BAW
    x2 = x.reshape(S, D)
    tgt = loss_target.reshape(S, D)

    small = ["even_conv", "odd_dw", "odd_dw_b", "odd_ln_g", "odd_ln_b"]
    plug_s, unpack_s = _gather_unit([W[n] for n in small], [shard_axis[n] for n in small], F32)
    full = dict(zip(small, unpack_s(plug_s.after(_run_plug("gather_small", plug_s)))))
    CW = odd_w_out.shape[1] * N_CHIPS
    wfull = {}

    def mixer_weights(i):
        return [(n, i // 2) for n in (("even_w_in", "even_pool_w", "even_w_out") if i % 2 == 0 else ("odd_w_in", "odd_w_out"))]

    riders = []

    def hosted(fn, *a, **k):
        if not riders:
            return fn(*a, **k)
        plug, done = riders.pop(0)
        outs, landed = fn(*a, comm=[plug], **k)
        done(plug.after(landed))
        return outs

    def queue_gather(keys):
        plug, unpack = _gather_unit([W[n][l] for n, l in keys], [shard_axis[n] - 1 for n, _ in keys], BF16)
        riders.append((plug, lambda full_: wfull.update(zip(keys, unpack(full_)))))

    def weight(key):
        while key not in wfull:
            plug, done = riders.pop(0)
            done(plug.after(_run_plug(f"alone_{len(wfull)}_{len(riders)}", plug)))
        return wfull[key]

    queue_gather(mixer_weights(0)[:1])
    queue_gather(mixer_weights(0)[1:])
    for i in range(depth):
        queue_gather([("mlp_w_up", i)])
        queue_gather([("mlp_w_down", i)])
        if i + 1 < depth:
            queue_gather(mixer_weights(i + 1))

    def even_w_in_layout(w):
        o1 = 4 * DNW
        return jnp.concatenate([w[:, :o1], w[:, o1 + 2 * H:], w[:, o1:o1 + 2 * H],
                                jnp.zeros((w.shape[0], BAW - 2 * H), w.dtype)], axis=1)

    def even_w_in_unlayout(g):
        o1 = 4 * DNW
        return jnp.concatenate([g[:, :o1], g[:, o1 + PW:o1 + PW + 2 * H], g[:, o1:o1 + PW]], axis=1)

    def pool_blockdiag(pw):
        return jnp.concatenate([jnp.pad(pw[gi], ((0, 0), (gi * PG, PW - (gi + 1) * PG))) for gi in range(G)], axis=0)

    pool_taps = max(POOL_WINDOWS)
    tap = jnp.arange(pool_taps)[:, None]
    win_c = jnp.repeat(jnp.asarray(POOL_WINDOWS, F32), PG)[None, :]
    pool_mask = (tap >= pool_taps - win_c).astype(F32)

    grads = {}
    tr_full = 128 if D > 1024 else 256

    saved = []
    xc = x2
    tr_fwd = 256
    (h,) = _rowwise("l0_rms_in", _f_rms, [(xc, 0, D)], [(W["norm_mix_pre"][0:1], None, D)], [(D, BF16)], S=S, tr=tr_fwd)
    for i in range(depth):
        jl = i // 2
        sv = {"x_in": xc}
        g1, g2, g3, g4 = (W[n][i:i + 1] for n in ("norm_mix_pre", "norm_mix_post", "norm_mlp_pre", "norm_mlp_post"))
        sv["h"] = h
        if i % 2 == 0:
            w_in = even_w_in_layout(weight(("even_w_in", jl)))
            p = hosted(_matmul, f"l{i}_w_in", h, w_in, "nn", tn=768)
            conv_w = full["even_conv"][jl]
            c = _dwconv_fwd(f"l{i}_conv", p, 0, conv_w, S=S, C=3 * DNW)
            (qk,) = _rowwise(f"l{i}_qk", _f_l2silu, [(c, 0, dh)], [], [(dh, F32)], S=S, ncb=2 * H, tr=1024)
            (vv,) = _rowwise(f"l{i}_v", _f_silu, [(c, 2 * DNW // dh, dh)], [], [(dh, F32)], S=S, ncb=H, tr=1024)
            alog = _lane_row(W["even_a_log"][jl], H)
            dtb = _lane_row(W["even_dt_bias"][jl], H)
            ba_off = (4 * DNW + PW) // LANE
            (gb,) = _rowwise(f"l{i}_gates", _make_gates(H), [(p, ba_off, LANE)], [(alog, None, LANE), (dtb, None, LANE)],
                             [(LANE, F32)], S=S, tr=1024)
            Cn = min(DN_CHUNK, S)
            gT = gb[:, H:2 * H].reshape(S // Cn, Cn, H).transpose(0, 2, 1)
            o, sp, Tm = hosted(_delta_fwd, f"l{i}_delta", qk, vv, gb, gT, S=S, H=H, dh=dh)
            dn = W["even_dn_norm"][jl][None, :]
            (on,) = _rowwise(f"l{i}_outgate", _f_outgate, [(o, 0, dh), (p, 3 * DNW // dh, dh)], [(dn, None, dh)],
                             [(dh, BF16)], S=S, ncb=H, tr=1024)
            pcb = _pick(PW, 512)
            pooled = _dwconv_fwd(f"l{i}_pool", p, 4 * DNW // pcb, pool_mask, S=S, C=PW, win=win_c, out_dtype=BF16, cb=pcb)
            wbd = pool_blockdiag(weight(("even_pool_w", jl)))
            ypre = _matmul(f"l{i}_pool_w", pooled, wbd, "nn")
            psc = W["even_pool_scale"][jl][None, :]
            (ypool,) = _rowwise(f"l{i}_pool_scale", _f_scale, [(ypre, 0, PW)], [(psc, None, PW)], [(PW, BF16)], S=S)
            mixin = jnp.concatenate([on, ypool], axis=1)
            mix = _matmul(f"l{i}_w_out", mixin, weight(("even_w_out", jl)), "nn")
            sv.update(p=p, c=c, qk=qk, v=vv, gb=gb, gT=gT, o=o, sp=sp, Tm=Tm, pooled=pooled, ypre=ypre, mixin=mixin,
                      w_in=w_in, wbd=wbd, alog=alog, dtb=dtb, dn=dn, psc=psc, conv_w=conv_w)
        else:
            p = hosted(_matmul, f"l{i}_w_in", h, weight(("odd_w_in", jl)), "nn")
            ocb = _pick(CW, 1024)
            (u0,) = _rowwise(f"l{i}_glu", _f_glu, [(p, 0, ocb), (p, CW // ocb, ocb)], [], [(ocb, F32)], S=S,
                             ncb=CW // ocb)
            dw_w, dw_b = full["odd_dw"][jl], full["odd_dw_b"][jl][None, :]
            u1 = hosted(_dwconv_fwd, f"l{i}_dwconv", u0, 0, dw_w, S=S, C=CW, bias=dw_b)
            lg, lb = full["odd_ln_g"][jl][None, :], full["odd_ln_b"][jl][None, :]
            (u2,) = _rowwise(f"l{i}_lnsilu", _f_lnsilu, [(u1, 0, CW)], [(lg, None, CW), (lb, None, CW)], [(CW, BF16)],
                             S=S, tr=tr_full)
            mix = _matmul(f"l{i}_w_out", u2, weight(("odd_w_out", jl)), "nn")
            sv.update(p=p, u0=u0, u1=u1, mixin=u2, dw_w=dw_w, lg=lg, lb=lb)
        x_mid, h2 = _rowwise(f"l{i}_mid", _f_mid, [(xc, 0, D), (mix, 0, D)], [(g2, None, D), (g3, None, D)],
                             [(D, F32), (D, BF16)], S=S, tr=tr_fwd)
        act = hosted(_matmul, f"l{i}_w_up", h2, weight(("mlp_w_up", i)), "nn", epi=_relu2, out_dtypes=[BF16])
        ff = hosted(_matmul, f"l{i}_w_down", act, weight(("mlp_w_down", i)), "nn", tk=TK_LONG)
        if i + 1 < depth:
            x_out, h = _rowwise(f"l{i}_out", _f_mid, [(x_mid, 0, D), (ff, 0, D)],
                                [(g4, None, D), (W["norm_mix_pre"][i + 1:i + 2], None, D)], [(D, F32), (D, BF16)], S=S,
                                tr=tr_fwd)
        else:
            (x_out,) = _rowwise(f"l{i}_out", _f_resid, [(x_mid, 0, D), (ff, 0, D)], [(g4, None, D)], [(D, F32)], S=S,
                                tr=tr_fwd)
        sv.update(mix=mix, x_mid=x_mid, h2=h2, act=act, ff=ff, g=(g1, g2, g3, g4))
        saved.append(sv)
        xc = x_out

    dy, loss_row = _rowwise("loss", _f_loss, [(xc, 0, D), (tgt, 0, D)], [], [(D, F32)], [(1, LANE, False)], S=S,
                            tr=tr_full)
    loss = lax.psum(loss_row[0, 0], ("x", "y", "c"))

    def mixer_params(i):
        ns = (("even_w_in", "even_conv", "even_a_log", "even_dt_bias", "even_dn_norm", "even_pool_w", "even_pool_scale",
               "even_w_out") if i % 2 == 0 else ("odd_w_in", "odd_dw", "odd_dw_b", "odd_ln_g", "odd_ln_b", "odd_w_out"))
        return [(n, i // 2) for n in ns] + [("norm_mix_pre", i)]

    def mlp_params(i):
        return [(n, i) for n in ("mlp_w_up", "mlp_w_down", "norm_mlp_pre", "norm_mlp_post", "norm_mix_post")]

    units = []

    def queue_reduce(name, keys):
        axes = [shard_axis[n] - 1 if n in shard_axis else None for n, _ in keys]
        pair, offs, shapes = _reduce_unit(name, [grads[k] for k in keys], axes)
        parts = []
        units.append((keys, offs, shapes, parts))
        for t, (r0, n) in enumerate(_row_chunks(pair.shape[1], EXCHANGE_PARTS)):
            plug = _exchange_plug(pair, r0, n)
            riders.append((plug, lambda q, t=t: parts.append((t, _sum_chips(f"{name}_sum{t}", q)))))

    dx = dy
    for i in reversed(range(depth)):
        jl = i // 2
        sv = saved[i]
        g1, g2, g3, g4 = sv["g"]
        if i + 1 < depth:
            dx, d_ff, dg4, dg1 = _rowwise_bwd(
                f"l{i}_out_b", _f_mid, [(sv["x_mid"], 0, D), (sv["ff"], 0, D)],
                [(g4, None, D), (saved[i + 1]["g"][0], None, D)], [(dx, 0, D), (dh_, 0, D)], [F32, BF16], S=S, tr=tr_full)
            grads["norm_mix_pre", i + 1] = dg1[0]
        else:
            d_ff, dg4 = _rowwise_bwd(f"l{i}_out_b", _f_rms, [(sv["ff"], 0, D)], [(g4, None, D)], [(dx, 0, D)], [BF16],
                                     S=S, tr=tr_full)
        grads["norm_mlp_post", i] = dg4[0]
        du = hosted(_matmul, f"l{i}_w_down_bx", d_ff, wfull["mlp_w_down", i], "nt", epi=_relu2_bwd, extras=[sv["act"]],
                    out_dtypes=[BF16])
        grads["mlp_w_down", i] = hosted(_matmul, f"l{i}_w_down_bw", sv["act"], d_ff, "tn", out_dtype=BF16, tk=TK_LONG)
        dh2 = hosted(_matmul, f"l{i}_w_up_bx", du, wfull["mlp_w_up", i], "nt", tk=TK_LONG)
        grads["mlp_w_up", i] = hosted(_matmul, f"l{i}_w_up_bw", sv["h2"], du, "tn", out_dtype=BF16, tk=TK_LONG)
        dx, d_mix, dg2, dg3 = _rowwise_bwd(
            f"l{i}_mid_b", _f_mid, [(sv["x_in"], 0, D), (sv["mix"], 0, D)], [(g2, None, D), (g3, None, D)],
            [(dx, 0, D), (dh2, 0, D)], [F32, BF16], S=S, tr=tr_full)
        grads["norm_mix_post", i], grads["norm_mlp_pre", i] = dg2[0], dg3[0]
        queue_reduce(f"red{i}", (mixer_params(i + 1) if i + 1 < depth else []) + mlp_params(i))
        if i % 2 == 0:
            d_mixin = _matmul(f"l{i}_w_out_bx", d_mix, wfull["even_w_out", jl], "nt")
            grads["even_w_out", jl] = _matmul(f"l{i}_w_out_bw", sv["mixin"], d_mix, "tn", out_dtype=BF16)
            p = sv["p"]
            pcb = _pick(PW, 512)
            d_ypre, dpsc = _rowwise_bwd(f"l{i}_pool_scale_b", _f_scale, [(sv["ypre"], 0, PW)], [(sv["psc"], None, PW)],
                                        [(d_mixin, DNW // PW, PW)], [BF16], S=S)
            grads["even_pool_scale", jl] = dpsc[0]
            d_pooled = _matmul(f"l{i}_pool_w_bx", d_ypre, sv["wbd"], "nt")
            dwbd = _matmul(f"l{i}_pool_w_bw", sv["pooled"], d_ypre, "tn")
            grads["even_pool_w", jl] = jnp.stack([dwbd[gi * PG:(gi + 1) * PG, gi * PG:(gi + 1) * PG] for gi in range(G)])
            d_xp = _dwconv_bwd(f"l{i}_pool_b", None, 0, d_pooled, pool_mask, S=S, C=PW, win=win_c, want_dw=False,
                               cb=pcb)[0]
            d_o, d_z, ddn = _rowwise_bwd(f"l{i}_outgate_b", _f_outgate, [(sv["o"], 0, dh), (p, 3 * DNW // dh, dh)],
                                         [(sv["dn"], None, dh)], [(d_mixin, 0, dh)], [F32, F32], S=S, ncb=H, tr=1024)
            grads["even_dn_norm", jl] = ddn[0]
            dqk, dv, dgb = hosted(_delta_bwd, f"l{i}_delta_b", sv["qk"], sv["v"], sv["gb"], sv["gT"], sv["sp"],
                                  sv["Tm"], d_o, S=S, H=H, dh=dh)
            ba_off = (4 * DNW + PW) // LANE
            d_ba, dalog, ddtb = _rowwise_bwd(f"l{i}_gates_b", _make_gates(H), [(p, ba_off, LANE)],
                                             [(sv["alog"], None, LANE), (sv["dtb"], None, LANE)], [(dgb, 0, LANE)],
                                             [F32], S=S, tr=1024)
            grads["even_a_log", jl], grads["even_dt_bias", jl] = dalog[0, H:2 * H], ddtb[0, H:2 * H]
            (dc_qk,) = _rowwise_bwd(f"l{i}_qk_b", _f_l2silu, [(sv["c"], 0, dh)], [], [(dqk, 0, dh)], [F32], S=S,
                                    ncb=2 * H, tr=1024)
            (dc_v,) = _rowwise_bwd(f"l{i}_v_b", _f_silu, [(sv["c"], 2 * DNW // dh, dh)], [], [(dv, 0, dh)], [F32], S=S,
                                   ncb=H, tr=1024)
            dc = jnp.concatenate([dc_qk, dc_v], axis=1)
            d_qkv, dconv, _ = _dwconv_bwd(f"l{i}_conv_b", p, 0, dc, sv["conv_w"], S=S, C=3 * DNW)
            grads["even_conv", jl] = dconv
            dp = jnp.concatenate([d_qkv.astype(BF16), d_z.astype(BF16), d_xp.astype(BF16), d_ba.astype(BF16),
                                  jnp.zeros((S, BAW - LANE), BF16)], axis=1)
            dh_ = hosted(_matmul, f"l{i}_w_in_bx", dp, sv["w_in"], "nt", tk=768)
            grads["even_w_in", jl] = even_w_in_unlayout(hosted(_matmul, f"l{i}_w_in_bw", sv["h"], dp, "tn", tn=768,
                                                               out_dtype=BF16))
        else:
            d_u2 = _matmul(f"l{i}_w_out_bx", d_mix, wfull["odd_w_out", jl], "nt")
            grads["odd_w_out", jl] = _matmul(f"l{i}_w_out_bw", sv["mixin"], d_mix, "tn", out_dtype=BF16)
            d_u1, dlg, dlb = _rowwise_bwd(f"l{i}_lnsilu_b", _f_lnsilu, [(sv["u1"], 0, CW)],
                                          [(sv["lg"], None, CW), (sv["lb"], None, CW)], [(d_u2, 0, CW)], [F32], S=S,
                                          tr=tr_full)
            grads["odd_ln_g", jl], grads["odd_ln_b", jl] = dlg[0], dlb[0]
            d_u0, ddw, ddb = hosted(_dwconv_bwd, f"l{i}_dwconv_b", sv["u0"], 0, d_u1, sv["dw_w"], S=S, C=CW)
            grads["odd_dw", jl], grads["odd_dw_b", jl] = ddw, ddb[0]
            p = sv["p"]
            ocb = _pick(CW, 1024)
            da, dgate = _rowwise_bwd(f"l{i}_glu_b", _f_glu, [(p, 0, ocb), (p, CW // ocb, ocb)], [], [(d_u0, 0, ocb)],
                                     [BF16, BF16], S=S, ncb=CW // ocb)
            dp = jnp.concatenate([da, dgate], axis=1)
            dh_ = hosted(_matmul, f"l{i}_w_in_bx", dp, wfull["odd_w_in", jl], "nt")
            grads["odd_w_in", jl] = hosted(_matmul, f"l{i}_w_in_bw", sv["h"], dp, "tn", out_dtype=BF16)
    dx, dg1 = _rowwise_bwd("l0_rms_in_b", _f_rms, [(saved[0]["x_in"], 0, D)], [(saved[0]["g"][0], None, D)],
                           [(dh_, 0, D)], [F32], adds=[(dx, 0, D)], S=S, tr=tr_full)
    grads["norm_mix_pre", 0] = dg1[0]
    grad_x = dx.reshape(x.shape)
    queue_reduce("red_last", mixer_params(0))
    while riders:
        plug, done = riders.pop(0)
        done(plug.after(_run_plug(f"alone_last_{len(riders)}", plug)))

    halves = [jnp.concatenate([q for _, q in sorted(parts, key=lambda tq: tq[0])], axis=0) for *_, parts in units]
    theirs = _sibling_swap("grad_sibling_swap", jnp.concatenate(halves, axis=0))
    south = lax.axis_index("c") == 0
    gshard, r = {}, 0
    for (keys, offs, shapes, _), half in zip(units, halves):
        rh = half.shape[0]
        other = theirs[r:r + rh]
        whole = jnp.concatenate([jnp.where(south, half, other), jnp.where(south, other, half)], axis=0)
        for k, off, shp in zip(keys, offs, shapes):
            gshard[k] = _segment(whole, off, shp)
        r += rh

    outs_g, outs_d, outs_m, outs_v = [], [], [], []
    for n in names:
        g_n = jnp.stack([gshard[n, l] for l in range(W[n].shape[0])])
        g_o, d_o, m_o, v_o = _adamw(f"adamw_{n}", W[n], Mo[n], Vo[n], g_n)
        outs_g.append(g_o)
        outs_d.append(d_o)
        outs_m.append(m_o)
        outs_v.append(v_o)
    return (loss, grad_x, *outs_g, *outs_d, *outs_m, *outs_v)
```

```python
import functools
import math

import jax
import jax.numpy as jnp
from jax import lax
from jax.experimental import pallas as pl
from jax.experimental.pallas import tpu as pltpu

F32 = jnp.float32
BF16 = jnp.bfloat16
EPS = 1e-6
DN_CHUNK = 64
POOL_WINDOWS = (2, 4, 8, 16)
ADAM_LR, ADAM_B1, ADAM_B2, ADAM_EPS, ADAM_WD, ADAM_STEP = 0.001, 0.9, 0.999, 1e-08, 0.01, 10
LANE = 128
FLAT_L = 2048
ADAM_ROWS = 128
VMEM_LIMIT = 56 * 1024 * 1024
TK_LONG = 4096
N_CHIPS = 4
HI = lax.Precision.HIGHEST
MESH = pl.DeviceIdType.MESH


def _cparams(sem):
    return pltpu.CompilerParams(dimension_semantics=sem, vmem_limit_bytes=VMEM_LIMIT)


def _pallas(body, args, *, name, grid, in_specs, out_specs, out_shape, scratch_shapes=(), sem=None, comm=()):
    n_in, n_out, n_scr = len(in_specs), len(out_specs), len(scratch_shapes)
    if not comm:
        return pl.pallas_call(body, name=name, grid=grid, in_specs=in_specs, out_specs=out_specs, out_shape=out_shape,
                              scratch_shapes=list(scratch_shapes), compiler_params=_cparams(sem))(*args)
    ci = [len(p.ins) for p in comm]
    co = [len(p.outs) for p in comm]
    cs = [len(p.sems) for p in comm]

    def wrapped(*refs):
        ins, pos = refs[:n_in], n_in
        cins = refs[pos:pos + sum(ci)]
        pos += sum(ci)
        outs = refs[pos:pos + n_out]
        pos += n_out
        couts = refs[pos:pos + sum(co)]
        pos += sum(co)
        scr = refs[pos:pos + n_scr]
        csems = refs[pos + n_scr:]
        ids = [pl.program_id(a) for a in range(len(grid))]
        first, last = ids[0] == 0, ids[0] == grid[0] - 1
        for a in range(1, len(grid)):
            first = jnp.logical_and(first, ids[a] == 0)
            last = jnp.logical_and(last, ids[a] == grid[a] - 1)
        parts, a, b, c = [], 0, 0, 0
        for p, na, nb, nc in zip(comm, ci, co, cs):
            parts.append((p, cins[a:a + na], couts[b:b + nb], csems[c:c + nc]))
            a, b, c = a + na, b + nb, c + nc

        @pl.when(first)
        def _():
            for p, pi, po, ps in parts:
                p.start(pi, po, ps)

        body(*ins, *outs, *scr)

        @pl.when(last)
        def _():
            for p, pi, po, ps in parts:
                p.finish(pi, po, ps)

    any_spec = pl.BlockSpec(memory_space=pl.ANY)
    res = pl.pallas_call(
        wrapped, name=name, grid=grid,
        in_specs=list(in_specs) + [any_spec] * sum(ci), out_specs=list(out_specs) + [any_spec] * sum(co),
        out_shape=list(out_shape) + [s for p in comm for s in p.outs],
        scratch_shapes=list(scratch_shapes) + [s for p in comm for s in p.sems],
        compiler_params=_cparams(("arbitrary",) * len(grid)),
    )(*args, *[a for p in comm for a in p.ins])
    return res


def _pick(dim, target, mult=LANE):
    if dim <= target:
        return dim
    t = (target // mult) * mult
    while t >= mult:
        if dim % t == 0:
            return t
        t -= mult
    return dim


def _sigmoid(x):
    return 1.0 / (1.0 + jnp.exp(-x))


def _silu(x):
    return x * _sigmoid(x)


def _softplus(x):
    return jnp.maximum(x, 0.0) + jnp.log(1.0 + jnp.exp(-jnp.abs(x)))


def _rms(x, g):
    return x * lax.rsqrt(jnp.mean(x * x, axis=-1, keepdims=True) + EPS) * g


def _matmul(name, a, b, mode, out_dtype=F32, tm=1024, tn=1024, tk=2048, epi=None, extras=(), out_dtypes=None, comm=()):
    if mode == "nn":
        (M, K), (K2, N) = a.shape, b.shape
    elif mode == "nt":
        (M, K), (N, K2) = a.shape, b.shape
    else:
        (K, M), (K2, N) = a.shape, b.shape
    assert K == K2, (name, a.shape, b.shape, mode)
    tm, tn, tk = _pick(M, tm), _pick(N, tn), _pick(K, tk)
    nk = K // tk
    if mode == "nn":
        a_spec = pl.BlockSpec((tm, tk), lambda i, j, k: (i, k))
        b_spec = pl.BlockSpec((tk, tn), lambda i, j, k: (k, j))
        dims = (((1,), (0,)), ((), ()))
    elif mode == "nt":
        a_spec = pl.BlockSpec((tm, tk), lambda i, j, k: (i, k))
        b_spec = pl.BlockSpec((tn, tk), lambda i, j, k: (j, k))
        dims = (((1,), (1,)), ((), ()))
    else:
        a_spec = pl.BlockSpec((tk, tm), lambda i, j, k: (k, i))
        b_spec = pl.BlockSpec((tk, tn), lambda i, j, k: (k, j))
        dims = (((0,), (0,)), ((), ()))
    out_dtypes = list(out_dtypes) if out_dtypes is not None else [out_dtype]
    ne, no = len(extras), len(out_dtypes)
    in_place = epi is None and out_dtypes == [F32]
    use_acc = nk > 1 and not in_place

    def finish(acc, extra_refs, out_refs):
        res = acc if epi is None else epi(acc, *[r[...] for r in extra_refs])
        res = res if isinstance(res, (tuple, list)) else (res,)
        for r, v in zip(out_refs, res):
            r[...] = v.astype(r.dtype)

    def body(a_ref, b_ref, *rest):
        extra_refs, out_refs = rest[:ne], rest[ne:ne + no]
        part = lax.dot_general(a_ref[...].astype(BF16), b_ref[...].astype(BF16), dims, preferred_element_type=F32)
        if nk == 1:
            finish(part, extra_refs, out_refs)
            return
        k = pl.program_id(2)
        acc_ref = rest[-1] if use_acc else out_refs[0]

        @pl.when(k == 0)
        def _():
            acc_ref[...] = part

        @pl.when(k > 0)
        def _():
            acc_ref[...] += part

        if use_acc:
            @pl.when(k == nk - 1)
            def _():
                finish(acc_ref[...], extra_refs, out_refs)

    o_spec = pl.BlockSpec((tm, tn), lambda i, j, k: (i, j))
    res = _pallas(body, (a, b, *extras), name=name, grid=(M // tm, N // tn, nk),
                  in_specs=[a_spec, b_spec] + [o_spec] * ne, out_specs=[o_spec] * no,
                  out_shape=[jax.ShapeDtypeStruct((M, N), dt) for dt in out_dtypes],
                  scratch_shapes=[pltpu.VMEM((tm, tn), F32)] if use_acc else [],
                  sem=("parallel", "parallel", "arbitrary"), comm=comm)
    outs = res[0] if no == 1 else res[:no]
    return (outs, res[no:]) if comm else outs


def _row_spec(tr, C, off):
    return pl.BlockSpec((tr, C), lambda j, i: (i, off + j))


def _par_spec(k, C, off):
    if off is None:
        return pl.BlockSpec((k, C), lambda j, i: (0, 0))
    return pl.BlockSpec((k, C), lambda j, i: (0, off + j))


def _rowwise(name, fn, rows, params, outs, reds=(), *, S, ncb=1, tr=256):
    tr = min(tr, S)
    nr, npar, no = len(rows), len(params), len(outs)

    def body(*refs):
        j, i = pl.program_id(0), pl.program_id(1)
        ins = [r[...].astype(F32) for r in refs[:nr + npar]]
        res = fn(i * tr, j, *ins)
        res = res if isinstance(res, (tuple, list)) else (res,)
        out_refs = refs[nr + npar:]
        for r, v in zip(out_refs[:no], res[:no]):
            r[...] = v.astype(r.dtype)
        for (k, C, per_j), r, v in zip(reds, out_refs[no:], res[no:]):
            first = (i == 0) if per_j else jnp.logical_and(i == 0, j == 0)

            @pl.when(first)
            def _(r=r):
                r[...] = jnp.zeros_like(r)

            r[...] += v

    in_specs = [_row_spec(tr, C, off) for (_, off, C) in rows] + [_par_spec(a.shape[0], C, off) for (a, off, C) in params]
    out_specs = [pl.BlockSpec((tr, C), lambda j, i: (i, j)) for (C, _) in outs]
    out_specs += [pl.BlockSpec((k, C), (lambda j, i: (0, j)) if per_j else (lambda j, i: (0, 0))) for (k, C, per_j) in reds]
    out_shape = [jax.ShapeDtypeStruct((S, ncb * C), dt) for (C, dt) in outs]
    out_shape += [jax.ShapeDtypeStruct((k, C * (ncb if per_j else 1)), F32) for (k, C, per_j) in reds]
    res = pl.pallas_call(
        body, name=name, grid=(ncb, S // tr), in_specs=in_specs, out_specs=out_specs, out_shape=out_shape,
        compiler_params=_cparams(("arbitrary", "arbitrary")),
    )(*[a for (a, _, _) in rows], *[a for (a, _, _) in params])
    return res


def _rowwise_bwd(name, fn, rows, params, cots, drow, adds=None, *, S, ncb=1, tr=128):
    tr = min(tr, S)
    nr, npar, nc = len(rows), len(params), len(cots)
    adds = adds or [None] * nr
    add_list = [a for a in adds if a is not None]
    na = len(add_list)

    def body(*refs):
        j, i = pl.program_id(0), pl.program_id(1)
        ins = [r[...].astype(F32) for r in refs[:nr + npar]]
        cts = [r[...].astype(F32) for r in refs[nr + npar:nr + npar + nc]]
        add_refs = list(refs[nr + npar + nc:nr + npar + nc + na])
        out_refs = list(refs[nr + npar + nc + na:])

        def f(*a):
            res = fn(i * tr, j, *a)
            return tuple(res) if isinstance(res, (tuple, list)) else (res,)

        _, vjp = jax.vjp(f, *ins)
        grads = vjp(tuple(cts))
        for idx in range(nr):
            if drow[idx] is None:
                continue
            g = grads[idx]
            if adds[idx] is not None:
                g = g + add_refs.pop(0)[...].astype(F32)
            r = out_refs.pop(0)
            r[...] = g.astype(r.dtype)
        for idx in range(npar):
            per_j = params[idx][1] is not None
            first = (i == 0) if per_j else jnp.logical_and(i == 0, j == 0)
            r = out_refs.pop(0)

            @pl.when(first)
            def _(r=r):
                r[...] = jnp.zeros_like(r)

            r[...] += grads[nr + idx]

    in_specs = [_row_spec(tr, C, off) for (_, off, C) in rows]
    in_specs += [_par_spec(a.shape[0], C, off) for (a, off, C) in params]
    in_specs += [_row_spec(tr, C, off) for (_, off, C) in cots]
    in_specs += [_row_spec(tr, C, off) for (_, off, C) in add_list]
    out_specs, out_shape = [], []
    for idx in range(nr):
        if drow[idx] is not None:
            C = rows[idx][2]
            out_specs.append(pl.BlockSpec((tr, C), lambda j, i: (i, j)))
            out_shape.append(jax.ShapeDtypeStruct((S, ncb * C), drow[idx]))
    for (a, off, C) in params:
        per_j = off is not None
        out_specs.append(pl.BlockSpec((a.shape[0], C), (lambda j, i: (0, j)) if per_j else (lambda j, i: (0, 0))))
        out_shape.append(jax.ShapeDtypeStruct((a.shape[0], C * (ncb if per_j else 1)), F32))
    return pl.pallas_call(
        body, name=name, grid=(ncb, S // tr), in_specs=in_specs, out_specs=out_specs, out_shape=out_shape,
        compiler_params=_cparams(("arbitrary", "arbitrary")),
    )(*[a for (a, _, _) in rows], *[a for (a, _, _) in params], *[a for (a, _, _) in cots], *[a for (a, _, _) in add_list])


def _halo_rows(K):
    return 8 * ((K - 1 + 7) // 8)


def _inv_count(row0, tr, win):
    t = (row0 + lax.broadcasted_iota(jnp.int32, (tr, 1), 0)).astype(F32)
    return 1.0 / jnp.minimum(t + 1.0, win)


CONV_ROWS = 32


def _shifted_down(xp, rows, K):
    for p in sorted({s % 8 for s in range(K)} - {0}):
        xp[p, 8:rows, :] = xp[0, 8 - p:rows - p, :]


def _shifted_up(yp, rows, K):
    for p in sorted({s % 8 for s in range(K)} - {0}):
        yp[p, 0:rows - 8, :] = yp[0, p:rows - 8 + p, :]


def _dwconv_fwd(name, x, x_off, w, *, S, C, bias=None, win=None, out_dtype=F32, cb=512, tr=256, comm=()):
    K = w.shape[0]
    cb, tr = _pick(C, cb), min(tr, S)
    HB = min(_halo_rows(K), tr)
    assert K - 1 <= HB and tr % HB == 0 and C % cb == 0
    nb = tr // HB
    RB = min(CONV_ROWS, tr)
    extra = [a for a in (bias, win) if a is not None]

    def body(xh_ref, x_ref, w_ref, *rest):
        y_ref, xp = rest[-2], rest[-1]
        i = pl.program_id(1)
        xp[0, 0:HB, :] = jnp.where(i > 0, xh_ref[...].astype(F32), 0.0)
        xp[0, HB:HB + tr, :] = x_ref[...].astype(F32)
        _shifted_down(xp, HB + tr, K)

        def sub(rb, carry):
            r0 = rb * RB
            acc = jnp.zeros((RB, cb), F32)
            for jj in range(K):
                s = K - 1 - jj
                start = pl.multiple_of(HB - 8 * (s // 8) + r0, 8)
                acc = acc + w_ref[jj:jj + 1, :] * xp[s % 8, pl.ds(start, RB), :]
            if bias is not None:
                acc = acc + rest[0][...]
            if win is not None:
                rows = pl.ds(pl.multiple_of(r0, 8), RB)
                acc = acc * _inv_count(i * tr + r0, RB, rest[0][...]) - x_ref[rows, :].astype(F32)
            y_ref[pl.ds(pl.multiple_of(r0, 8), RB), :] = acc.astype(y_ref.dtype)
            return carry

        lax.fori_loop(0, tr // RB, sub, 0)

    in_specs = [pl.BlockSpec((HB, cb), lambda j, i: (jnp.maximum(i * nb - 1, 0), x_off + j)),
                pl.BlockSpec((tr, cb), lambda j, i: (i, x_off + j)),
                pl.BlockSpec((K, cb), lambda j, i: (0, j))]
    in_specs += [pl.BlockSpec((1, cb), lambda j, i: (0, j)) for _ in extra]
    res = _pallas(body, (x, x, w, *extra), name=name, grid=(C // cb, S // tr), in_specs=in_specs,
                  out_specs=[pl.BlockSpec((tr, cb), lambda j, i: (i, j))],
                  out_shape=[jax.ShapeDtypeStruct((S, C), out_dtype)],
                  scratch_shapes=[pltpu.VMEM((8, HB + tr, cb), F32)], sem=("parallel", "arbitrary"), comm=comm)
    return (res[0], res[1:]) if comm else res[0]


def _dwconv_bwd(name, x, x_off, dy, w, *, S, C, win=None, want_dw=True, cb=512, tr=256, comm=()):
    K = w.shape[0]
    cb, tr = _pick(C, cb), min(tr, S)
    HB = min(_halo_rows(K), tr)
    nb, nt = tr // HB, S // tr
    RB = min(CONV_ROWS, tr)

    def body(*refs):
        if want_dw:
            xh_ref, x_ref, dy_ref, dyn_ref, w_ref = refs[:5]
            rest = refs[5:]
        else:
            dy_ref, dyn_ref, w_ref = refs[:3]
            rest = refs[3:]
        i = pl.program_id(1)
        dyt = dy_ref[...].astype(F32)
        dyn = jnp.where(i < nt - 1, dyn_ref[...].astype(F32), 0.0)
        if win is not None:
            win_v = rest[0][...]
            rest = rest[1:]
            yy_t = dyt * _inv_count(i * tr, tr, win_v)
            dyn = dyn * _inv_count((i + 1) * tr, HB, win_v)
        else:
            yy_t = dyt
        if want_dw:
            dx_ref, dw_ref, db_ref, yp, xp, dw8, db8 = rest
        else:
            dx_ref, yp = rest
        yp[0, 0:tr, :] = yy_t
        yp[0, tr:tr + HB, :] = dyn
        _shifted_up(yp, tr + HB, K)

        def sub(rb, carry):
            r0 = rb * RB
            acc = jnp.zeros((RB, cb), F32)
            for jj in range(K):
                s = K - 1 - jj
                start = pl.multiple_of(8 * (s // 8) + r0, 8)
                acc = acc + w_ref[jj:jj + 1, :] * yp[s % 8, pl.ds(start, RB), :]
            rows = pl.ds(pl.multiple_of(r0, 8), RB)
            if win is not None:
                acc = acc - dy_ref[rows, :].astype(F32)
            dx_ref[rows, :] = acc.astype(dx_ref.dtype)
            return carry

        lax.fori_loop(0, tr // RB, sub, 0)
        if want_dw:
            xp[0, 0:HB, :] = jnp.where(i > 0, xh_ref[...].astype(F32), 0.0)
            xp[0, HB:HB + tr, :] = x_ref[...].astype(F32)
            _shifted_down(xp, HB + tr, K)

            @pl.when(i == 0)
            def _():
                dw8[...] = jnp.zeros_like(dw8)
                db8[...] = jnp.zeros_like(db8)

            def sub_w(rb, carry):
                r0 = rb * RB
                dyb = dy_ref[pl.ds(pl.multiple_of(r0, 8), RB), :].astype(F32)
                for jj in range(K):
                    s = K - 1 - jj
                    start = pl.multiple_of(HB - 8 * (s // 8) + r0, 8)
                    prod = dyb * xp[s % 8, pl.ds(start, RB), :]
                    dw8[jj] += jnp.sum(prod.reshape(RB // 8, 8, cb), axis=0)
                db8[...] += jnp.sum(dyb.reshape(RB // 8, 8, cb), axis=0)
                return carry

            lax.fori_loop(0, tr // RB, sub_w, 0)

            @pl.when(i == nt - 1)
            def _():
                for jj in range(K):
                    dw_ref[jj:jj + 1, :] = jnp.sum(dw8[jj], axis=0, keepdims=True)
                db_ref[...] = jnp.sum(db8[...], axis=0, keepdims=True)

    last = S // HB - 1
    in_specs, args = [], []
    if want_dw:
        in_specs += [pl.BlockSpec((HB, cb), lambda j, i: (jnp.maximum(i * nb - 1, 0), x_off + j)),
                     pl.BlockSpec((tr, cb), lambda j, i: (i, x_off + j))]
        args += [x, x]
    in_specs += [pl.BlockSpec((tr, cb), lambda j, i: (i, j)),
                 pl.BlockSpec((HB, cb), lambda j, i: (jnp.minimum((i + 1) * nb, last), j)),
                 pl.BlockSpec((K, cb), lambda j, i: (0, j))]
    args += [dy, dy, w]
    if win is not None:
        in_specs.append(pl.BlockSpec((1, cb), lambda j, i: (0, j)))
        args.append(win)
    out_specs = [pl.BlockSpec((tr, cb), lambda j, i: (i, j))]
    out_shape = [jax.ShapeDtypeStruct((S, C), F32)]
    scratch = [pltpu.VMEM((8, tr + HB, cb), F32)]
    if want_dw:
        out_specs += [pl.BlockSpec((K, cb), lambda j, i: (0, j)), pl.BlockSpec((1, cb), lambda j, i: (0, j))]
        out_shape += [jax.ShapeDtypeStruct((K, C), F32), jax.ShapeDtypeStruct((1, C), F32)]
        scratch += [pltpu.VMEM((8, HB + tr, cb), F32), pltpu.VMEM((K, 8, cb), F32), pltpu.VMEM((8, cb), F32)]
    res = _pallas(body, args, name=name, grid=(C // cb, S // tr), in_specs=in_specs, out_specs=out_specs,
                  out_shape=out_shape, scratch_shapes=scratch, sem=("parallel", "arbitrary"), comm=comm)
    return (res[:len(out_specs)], res[len(out_specs):]) if comm else res


def _dot(a, b, dims, hi=False):
    if hi:
        return lax.dot_general(a, b, (dims, ((), ())), precision=HI, preferred_element_type=F32)
    return lax.dot_general(a.astype(BF16), b.astype(BF16), (dims, ((), ())), preferred_element_type=F32)


_NN, _NT, _TN = ((1,), (0,)), ((1,), (1,)), ((0,), (0,))


def _col(m, idx):
    lane = lax.broadcasted_iota(jnp.int32, m.shape, 1)
    return jnp.sum(jnp.where(lane == idx, m, 0.0), axis=1, keepdims=True)


def _row(m, idx):
    sub = lax.broadcasted_iota(jnp.int32, m.shape, 0)
    return jnp.sum(jnp.where(sub == idx, m, 0.0), axis=0, keepdims=True)


def _delta_chunk(q, k, v, beta, gcc, gcr, causal, strict, eye, scale, C):
    d = {}
    gam = jnp.where(causal, jnp.exp(jnp.where(causal, gcc - gcr, 0.0)), 0.0)
    eg = jnp.exp(gcc)
    g_last = _row(gcc, C - 1)
    d["gam"], d["eg"], d["g_last"] = gam, eg, g_last
    d["ek"] = jnp.exp(g_last - gcc)
    d["decay"] = jnp.exp(g_last)
    qs = q * scale
    kb = k * beta
    d["qs"], d["kb"] = qs, kb
    d["kk"] = _dot(kb, k, _NT)
    d["A"] = jnp.where(strict, d["kk"] * gam, 0.0)
    d["qk"] = _dot(qs, k, _NT)
    d["attn"] = jnp.where(causal, d["qk"] * gam, 0.0)
    d["vb"] = v * beta
    d["kbg"] = kb * eg
    d["qg"] = qs * eg
    d["kd"] = k * d["ek"]
    return d


def _split(m):
    hi = m.astype(BF16)
    return hi, (m - hi.astype(F32)).astype(BF16)


def _dot3(a, b, dims):
    return _dot(a[0], b[0], dims) + (_dot(a[0], b[1], dims) + _dot(a[1], b[0], dims))


def _tri_inverse(As, eye):
    P = [-A for A in As]
    T = [eye + p for p in P]
    n = 1
    while 2 * n < As[0].shape[0]:
        Ps = [_split(p) for p in P]
        P = [_dot3(ps, ps, _NN) for ps in Ps]
        Ts = [_split(t) for t in T]
        Ps = [_split(p) for p in P]
        T = [t + _dot3(ts, ps, _NN) for t, ts, ps in zip(T, Ts, Ps)]
        n *= 2
    return T


def _delta_fwd(name, qk, v, gb, gT, *, S, H, dh, comm=()):
    C = min(DN_CHUNK, S)
    N, W = S // C, H * dh
    scale = dh ** -0.5

    def body(qk_ref, v_ref, gb_ref, gT_ref, o_ref, sp_ref, T_ref, st):
        n = pl.program_id(0)

        @pl.when(n == 0)
        def _():
            st[...] = jnp.zeros_like(st)

        r = lax.broadcasted_iota(jnp.int32, (C, C), 0)
        c = lax.broadcasted_iota(jnp.int32, (C, C), 1)
        causal, strict = r >= c, r > c
        eye = (r == c).astype(F32)
        Lt = causal.astype(F32)
        gbv = gb_ref[...]
        gcum = _dot(Lt, gbv, _NN, hi=True)
        gcumT = _dot(gT_ref[0], Lt, _NT, hi=True)
        hs = range(H)
        sl = [slice(h * dh, (h + 1) * dh) for h in hs]
        d = [_delta_chunk(qk_ref[:, sl[h]], qk_ref[:, W + h * dh:W + (h + 1) * dh], v_ref[:, sl[h]], _col(gbv, h),
                          _col(gcum, H + h), _row(gcumT, h), causal, strict, eye, scale, C) for h in hs]
        T = _tri_inverse([d[h]["A"] for h in hs], eye)
        u = [_dot(T[h], d[h]["vb"], _NN) for h in hs]
        w = [_dot(T[h], d[h]["kbg"], _NN) for h in hs]
        s0 = [st[h] for h in hs]
        ws = [_dot(w[h], s0[h], _NN) for h in hs]
        qs0 = [_dot(d[h]["qg"], s0[h], _NN) for h in hs]
        v_new = [u[h] - ws[h] for h in hs]
        av = [_dot(d[h]["attn"], v_new[h], _NN) for h in hs]
        kv = [_dot(d[h]["kd"], v_new[h], _TN) for h in hs]
        for h in hs:
            sp_ref[0, h] = s0[h]
            T_ref[0, h] = T[h]
            o_ref[:, sl[h]] = qs0[h] + av[h]
            st[h] = s0[h] * d[h]["decay"] + kv[h]

    res = _pallas(
        body, (qk, v, gb, gT), name=name, grid=(N,),
        in_specs=[pl.BlockSpec((C, 2 * W), lambda n: (n, 0)), pl.BlockSpec((C, W), lambda n: (n, 0)),
                  pl.BlockSpec((C, LANE), lambda n: (n, 0)), pl.BlockSpec((1, H, C), lambda n: (n, 0, 0))],
        out_specs=[pl.BlockSpec((C, W), lambda n: (n, 0)), pl.BlockSpec((1, H, dh, dh), lambda n: (n, 0, 0, 0)),
                   pl.BlockSpec((1, H, C, C), lambda n: (n, 0, 0, 0))],
        out_shape=[jax.ShapeDtypeStruct((S, W), F32), jax.ShapeDtypeStruct((N, H, dh, dh), F32),
                   jax.ShapeDtypeStruct((N, H, C, C), F32)],
        scratch_shapes=[pltpu.VMEM((H, dh, dh), F32)], sem=("arbitrary",), comm=comm)
    return (res[:3], res[3:]) if comm else res[:3]


def _delta_bwd(name, qk, v, gb, gT, sp, Tm, do, *, S, H, dh, comm=()):
    C = min(DN_CHUNK, S)
    N, W = S // C, H * dh
    scale = dh ** -0.5

    def body(qk_ref, v_ref, gb_ref, gT_ref, sp_ref, T_ref, do_ref, dqk_ref, dv_ref, dgb_ref, ds):
        n = pl.program_id(0)

        @pl.when(n == 0)
        def _():
            ds[...] = jnp.zeros_like(ds)

        r = lax.broadcasted_iota(jnp.int32, (C, C), 0)
        c = lax.broadcasted_iota(jnp.int32, (C, C), 1)
        causal, strict = r >= c, r > c
        eye = (r == c).astype(F32)
        Lt = causal.astype(F32)
        ones = jnp.ones((C, LANE), F32)
        lane = lax.broadcasted_iota(jnp.int32, (C, LANE), 1)
        rowi = lax.broadcasted_iota(jnp.int32, (C, 1), 0)
        gbv = gb_ref[...]
        gcum = _dot(Lt, gbv, _NN, hi=True)
        gcumT = _dot(gT_ref[0], Lt, _NT, hi=True)
        dgc_all = jnp.zeros((C, LANE), F32)
        dbeta_all = jnp.zeros((C, LANE), F32)
        hs = range(H)
        sl = [slice(h * dh, (h + 1) * dh) for h in hs]
        ksl = [slice(W + h * dh, W + (h + 1) * dh) for h in hs]
        k = [qk_ref[:, ksl[h]] for h in hs]
        vv = [v_ref[:, sl[h]] for h in hs]
        beta = [_col(gbv, h) for h in hs]
        d = [_delta_chunk(qk_ref[:, sl[h]], k[h], vv[h], beta[h], _col(gcum, H + h), _row(gcumT, h), causal, strict,
                          eye, scale, C) for h in hs]
        T = [T_ref[0, h] for h in hs]
        s0 = [sp_ref[0, h] for h in hs]
        dO = [do_ref[:, sl[h]] for h in hs]
        dS = [ds[h] for h in hs]
        u = [_dot(T[h], d[h]["vb"], _NN) for h in hs]
        w = [_dot(T[h], d[h]["kbg"], _NN) for h in hs]
        ws = [_dot(w[h], s0[h], _NN) for h in hs]
        v_new = [u[h] - ws[h] for h in hs]
        dv_new = [_dot(d[h]["attn"], dO[h], _TN) + _dot(d[h]["kd"], dS[h], _NN) for h in hs]
        dattn = [jnp.where(causal, _dot(dO[h], v_new[h], _NT), 0.0) for h in hs]
        dqg = [_dot(dO[h], s0[h], _NT) for h in hs]
        dkd = [_dot(v_new[h], dS[h], _NT) for h in hs]
        ddecay = [jnp.sum(jnp.sum(s0[h] * dS[h], axis=1, keepdims=True), axis=0, keepdims=True) for h in hs]
        ds_new = [_dot(d[h]["qg"], dO[h], _TN) + d[h]["decay"] * dS[h] - _dot(w[h], dv_new[h], _TN) for h in hs]
        dw = [-_dot(dv_new[h], s0[h], _NT) for h in hs]
        for h in hs:
            ds[h] = ds_new[h]
        dT = [_dot(dv_new[h], d[h]["vb"], _NT) + _dot(dw[h], d[h]["kbg"], _NT) for h in hs]
        dvb = [_dot(T[h], dv_new[h], _TN) for h in hs]
        dkbg = [_dot(T[h], dw[h], _TN) for h in hs]
        Ts = [_split(T[h]) for h in hs]
        x1 = [_dot3(Ts[h], _split(dT[h]), _TN) for h in hs]
        dA = [jnp.where(strict, -_dot3(_split(x1[h]), Ts[h], _NT), 0.0) for h in hs]
        dkk = [dA[h] * d[h]["gam"] for h in hs]
        dqk_m = [dattn[h] * d[h]["gam"] for h in hs]
        m = [_split(dA[h] * d[h]["A"] + dattn[h] * d[h]["attn"]) for h in hs]
        msum = [jnp.sum(dA[h] * d[h]["A"] + dattn[h] * d[h]["attn"], axis=1, keepdims=True) for h in hs]
        mcol = [jnp.max(_dot(m[h][0], ones, _TN) + _dot(m[h][1], ones, _TN), axis=1, keepdims=True) for h in hs]
        dkb = [_dot(dkk[h], k[h], _NN) + dkbg[h] * d[h]["eg"] for h in hs]
        dk = [_dot(dkk[h], d[h]["kb"], _TN) + _dot(dqk_m[h], d[h]["qs"], _TN) + dkd[h] * d[h]["ek"] + dkb[h] * beta[h]
              for h in hs]
        dqs = [_dot(dqk_m[h], k[h], _NN) + dqg[h] * d[h]["eg"] for h in hs]
        for h in hs:
            r_kd = jnp.sum(dkd[h] * d[h]["kd"], axis=1, keepdims=True)
            dgc = (msum[h] - mcol[h] + jnp.sum(dqg[h] * d[h]["qg"], axis=1, keepdims=True) - r_kd
                   + jnp.sum(dkbg[h] * d[h]["kbg"], axis=1, keepdims=True))
            dg_last = jnp.sum(r_kd, axis=0, keepdims=True) + ddecay[h] * d[h]["decay"]
            dgc = dgc + jnp.where(rowi == C - 1, dg_last, 0.0)
            dbeta = jnp.sum(dkb[h] * k[h], axis=1, keepdims=True) + jnp.sum(dvb[h] * vv[h], axis=1, keepdims=True)
            dqk_ref[:, sl[h]] = dqs[h] * scale
            dqk_ref[:, ksl[h]] = dk[h]
            dv_ref[:, sl[h]] = dvb[h] * beta[h]
            dgc_all = dgc_all + jnp.where(lane == H + h, dgc, 0.0)
            dbeta_all = dbeta_all + jnp.where(lane == h, dbeta, 0.0)
        dgb_ref[...] = _dot(Lt, dgc_all, _TN, hi=True) + dbeta_all

    rev = lambda n: N - 1 - n
    res = _pallas(
        body, (qk, v, gb, gT, sp, Tm, do), name=name, grid=(N,),
        in_specs=[pl.BlockSpec((C, 2 * W), lambda n: (rev(n), 0)), pl.BlockSpec((C, W), lambda n: (rev(n), 0)),
                  pl.BlockSpec((C, LANE), lambda n: (rev(n), 0)), pl.BlockSpec((1, H, C), lambda n: (rev(n), 0, 0)),
                  pl.BlockSpec((1, H, dh, dh), lambda n: (rev(n), 0, 0, 0)),
                  pl.BlockSpec((1, H, C, C), lambda n: (rev(n), 0, 0, 0)),
                  pl.BlockSpec((C, W), lambda n: (rev(n), 0))],
        out_specs=[pl.BlockSpec((C, 2 * W), lambda n: (rev(n), 0)), pl.BlockSpec((C, W), lambda n: (rev(n), 0)),
                   pl.BlockSpec((C, LANE), lambda n: (rev(n), 0))],
        out_shape=[jax.ShapeDtypeStruct((S, 2 * W), F32), jax.ShapeDtypeStruct((S, W), F32),
                   jax.ShapeDtypeStruct((S, LANE), F32)],
        scratch_shapes=[pltpu.VMEM((H, dh, dh), F32)], sem=("arbitrary",), comm=comm)
    return (res[:3], res[3:]) if comm else res[:3]


_ANY = pl.BlockSpec(memory_space=pl.ANY)


ICI_CHUNKS = 4
D2D_CHUNKS = 4
EXCHANGE_PARTS = 3


def _place():
    return lax.axis_index("x"), lax.axis_index("y"), lax.axis_index("c")


def _row_chunks(rows, n):
    n = max(1, min(n, rows // 8))
    while n > 1 and (rows % n or (rows // n) % 8):
        n -= 1
    return [(k * (rows // n), rows // n) for k in range(n)]


class _Plug:
    def __init__(self, ins, outs, sems, start, finish, after):
        self.ins, self.outs, self.sems, self.start, self.finish, self.after = ins, outs, sems, start, finish, after


def _run_plug(name, plug):
    def body(*refs):
        ni, no = len(plug.ins), len(plug.outs)
        plug.start(refs[:ni], refs[ni:ni + no], refs[ni + no:])
        plug.finish(refs[:ni], refs[ni:ni + no], refs[ni + no:])

    return pl.pallas_call(body, name=name, in_specs=[_ANY] * len(plug.ins), out_specs=[_ANY] * len(plug.outs),
                          out_shape=list(plug.outs), scratch_shapes=list(plug.sems))(*plug.ins)


def _gather_plug(flat):
    R, L = flat.shape
    Rh = R // 2
    ici = _row_chunks(Rh, ICI_CHUNKS)
    sub = _row_chunks(ici[0][1], D2D_CHUNKS)
    ni, ns = len(ici), len(sub)

    def parts(ins, outs, sems):
        (x_ref,), (out_ref,), (send_sems, recv_sems) = ins, outs, sems
        x, y, c = _place()
        chips = [(1 - x, y), (x, 1 - y), (1 - x, 1 - y)]

        def rows(px, py, pc, r0, n):
            return out_ref.at[2 * px + py, pl.ds(pc * Rh + r0, n), :]

        def copy(k, src, dst, to):
            return pltpu.make_async_remote_copy(src_ref=src, dst_ref=dst, send_sem=send_sems.at[k],
                                                recv_sem=recv_sems.at[k], device_id=to, device_id_type=MESH)

        first = [copy(k * ni + q, x_ref.at[pl.ds(c * Rh + r0, n), :], rows(x, y, c, r0, n), (*chip, c))
                 for k, chip in enumerate(chips) for q, (r0, n) in enumerate(ici)]
        return x_ref, (x, y, c), chips, rows, copy, first

    def start(ins, outs, sems):
        for cp in parts(ins, outs, sems)[-1]:
            cp.start()

    def finish(ins, outs, sems):
        x_ref, (x, y, c), chips, rows, copy, first = parts(ins, outs, sems)
        sibling = (x, y, 1 - c)
        passed = []
        for k, chip in enumerate(chips):
            for q, (r0, n) in enumerate(ici):
                copy(k * ni + q, x_ref.at[pl.ds(r0, n), :], rows(*chip, c, r0, n), (*chip, c)).wait_recv()
                for t, (s0, m) in enumerate(sub):
                    cp = copy(3 * ni + (k * ni + q) * ns + t, rows(*chip, c, r0 + s0, m), rows(*chip, c, r0 + s0, m), sibling)
                    cp.start()
                    passed.append(cp)
        for k, chip in enumerate(chips):
            for q, (r0, n) in enumerate(ici):
                for t, (s0, m) in enumerate(sub):
                    copy(3 * ni + (k * ni + q) * ns + t, x_ref.at[pl.ds(r0, m), :], rows(*chip, 1 - c, r0 + s0, m),
                         sibling).wait_recv()
        for cp in first + passed:
            cp.wait_send()

    def after(res):
        return lax.dynamic_update_slice(res[0], flat[None], (2 * lax.axis_index("x") + lax.axis_index("y"), 0, 0))

    nsem = 3 * ni * (1 + ns)
    return _Plug([flat], [jax.ShapeDtypeStruct((N_CHIPS, R, L), flat.dtype)],
                 [pltpu.SemaphoreType.DMA((nsem,)), pltpu.SemaphoreType.DMA((nsem,))], start, finish, after)


def _sibling_split(name, g):
    _, R, L = g.shape
    Rh = R // 2

    chunks = [(j, r0, n) for j in range(N_CHIPS) for (r0, n) in _row_chunks(Rh, D2D_CHUNKS)]

    def body(g_ref, got_ref, send_sems, recv_sems):
        x, y, c = _place()
        cps = [pltpu.make_async_remote_copy(src_ref=g_ref.at[j, pl.ds((1 - c) * Rh + r0, n), :],
                                            dst_ref=got_ref.at[j, pl.ds(r0, n), :], send_sem=send_sems.at[k],
                                            recv_sem=recv_sems.at[k], device_id=(x, y, 1 - c), device_id_type=MESH)
               for k, (j, r0, n) in enumerate(chunks)]
        for cp in cps:
            cp.start()
        for cp in cps:
            cp.wait()

    sems = pltpu.SemaphoreType.DMA((len(chunks),))
    got = pl.pallas_call(
        body, name=name, in_specs=[_ANY], out_specs=_ANY, out_shape=jax.ShapeDtypeStruct((N_CHIPS, Rh, L), g.dtype),
        scratch_shapes=[sems, sems],
    )(g)
    own = lax.dynamic_slice(g, (0, lax.axis_index("c") * Rh, 0), (N_CHIPS, Rh, L))
    return own, got


def _exchange_plug(p, row0, nrows):
    ici = _row_chunks(nrows, ICI_CHUNKS)
    ni = len(ici)

    def sends(ins, outs, sems):
        (p_ref,), (q_ref,), (send_sems, recv_sems) = ins, outs, sems
        x, y, c = _place()
        me = 2 * x + y
        chips = [(1 - x, y), (x, 1 - y), (1 - x, 1 - y)]
        return [pltpu.make_async_remote_copy(src_ref=p_ref.at[2 * cx + cy, pl.ds(row0 + r0, n), :],
                                             dst_ref=q_ref.at[me, pl.ds(r0, n), :],
                                             send_sem=send_sems.at[k * ni + q], recv_sem=recv_sems.at[k * ni + q],
                                             device_id=(cx, cy, c), device_id_type=MESH)
                for k, (cx, cy) in enumerate(chips) for q, (r0, n) in enumerate(ici)]

    def start(ins, outs, sems):
        for cp in sends(ins, outs, sems):
            cp.start()

    def finish(ins, outs, sems):
        (p_ref,), (q_ref,), (send_sems, recv_sems) = ins, outs, sems
        x, y, c = _place()
        me = 2 * x + y
        chips = [(1 - x, y), (x, 1 - y), (1 - x, 1 - y)]
        for k, (cx, cy) in enumerate(chips):
            for q, (r0, n) in enumerate(ici):
                pltpu.make_async_remote_copy(src_ref=p_ref.at[me, pl.ds(r0, n), :],
                                             dst_ref=q_ref.at[2 * cx + cy, pl.ds(r0, n), :],
                                             send_sem=send_sems.at[k * ni + q], recv_sem=recv_sems.at[k * ni + q],
                                             device_id=(cx, cy, c), device_id_type=MESH).wait_recv()
        for cp in sends(ins, outs, sems):
            cp.wait_send()

    def after(res):
        me = 2 * lax.axis_index("x") + lax.axis_index("y")
        mine = lax.dynamic_slice(p, (me, row0, 0), (1, nrows, p.shape[2]))
        return lax.dynamic_update_slice(res[0], mine, (me, 0, 0))

    return _Plug([p], [jax.ShapeDtypeStruct((N_CHIPS, nrows, p.shape[2]), p.dtype)],
                 [pltpu.SemaphoreType.DMA((3 * ni,)), pltpu.SemaphoreType.DMA((3 * ni,))], start, finish, after)


def _sibling_swap(name, half):
    Rh, L = half.shape
    chunks = _row_chunks(Rh, 2 * D2D_CHUNKS)

    def body(h_ref, out_ref, send_sems, recv_sems):
        x, y, c = _place()
        cps = [pltpu.make_async_remote_copy(src_ref=h_ref.at[pl.ds(r0, n), :], dst_ref=out_ref.at[pl.ds(r0, n), :],
                                            send_sem=send_sems.at[k], recv_sem=recv_sems.at[k], device_id=(x, y, 1 - c),
                                            device_id_type=MESH)
               for k, (r0, n) in enumerate(chunks)]
        for cp in cps:
            cp.start()
        for cp in cps:
            cp.wait()

    sems = pltpu.SemaphoreType.DMA((len(chunks),))
    return pl.pallas_call(
        body, name=name, in_specs=[_ANY], out_specs=_ANY, out_shape=jax.ShapeDtypeStruct((Rh, L), half.dtype),
        scratch_shapes=[sems, sems],
    )(half)


def _add_pairs(name, a, b, out_dtype):
    n, Rh, L = a.shape
    tr = _pick(Rh, 512, 8)

    def body(a_ref, b_ref, o_ref):
        o_ref[...] = (a_ref[...].astype(F32) + b_ref[...].astype(F32)).astype(o_ref.dtype)

    spec = pl.BlockSpec((1, tr, L), lambda j, i: (j, i, 0))
    return pl.pallas_call(body, name=name, grid=(n, Rh // tr), in_specs=[spec, spec], out_specs=spec,
                          out_shape=jax.ShapeDtypeStruct(a.shape, out_dtype),
                          compiler_params=_cparams(("parallel", "parallel")))(a, b)


def _sum_chips(name, q):
    n, Rh, L = q.shape
    tr = _pick(Rh, 512, 8)

    def body(q_ref, o_ref):
        acc = q_ref[0].astype(F32)
        for s in range(1, n):
            acc = acc + q_ref[s].astype(F32)
        o_ref[...] = acc

    return pl.pallas_call(body, name=name, grid=(Rh // tr,),
                          in_specs=[pl.BlockSpec((n, tr, L), lambda i: (0, i, 0))],
                          out_specs=pl.BlockSpec((tr, L), lambda i: (i, 0)),
                          out_shape=jax.ShapeDtypeStruct((Rh, L), F32),
                          compiler_params=_cparams(("parallel",)))(q)


def _adamw(name, w, m, v, g):
    shape, size = w.shape, w.size
    rows = -(-size // FLAT_L)
    rows_p = -(-rows // ADAM_ROWS) * ADAM_ROWS
    pad = rows_p * FLAT_L - size

    def flat2d(a):
        a = a.reshape(-1)
        if pad:
            a = jnp.pad(a, (0, pad), constant_values=1.0)
        return a.reshape(rows_p, FLAT_L)

    c1 = 1.0 / (1.0 - ADAM_B1 ** ADAM_STEP)
    c2 = 1.0 / (1.0 - ADAM_B2 ** ADAM_STEP)

    def body(w_ref, m_ref, v_ref, g_ref, go_ref, d_ref, mo_ref, vo_ref):
        g = g_ref[...]
        wv = w_ref[...]
        mn = ADAM_B1 * m_ref[...] + (1.0 - ADAM_B1) * g
        vn = ADAM_B2 * v_ref[...] + (1.0 - ADAM_B2) * (g * g)
        go_ref[...] = g
        mo_ref[...] = mn
        vo_ref[...] = vn
        d_ref[...] = -ADAM_LR * ((mn * c1) / (jnp.sqrt(vn * c2) + ADAM_EPS) + ADAM_WD * wv)

    spec = pl.BlockSpec((ADAM_ROWS, FLAT_L), lambda i: (i, 0))
    shp = jax.ShapeDtypeStruct((rows_p, FLAT_L), F32)
    outs = pl.pallas_call(
        body, name=name, grid=(rows_p // ADAM_ROWS,),
        in_specs=[spec] * 4, out_specs=[spec] * 4, out_shape=[shp] * 4, compiler_params=_cparams(("parallel",)),
    )(flat2d(w), flat2d(m), flat2d(v), flat2d(g))

    def back(a):
        a = a.reshape(-1)
        if pad:
            a = a[:size]
        return a.reshape(shape)

    return tuple(back(a) for a in outs)


SEG_ROWS = 16


def _keeps_rows(shape):
    return len(shape) >= 2 and shape[-1] < FLAT_L and FLAT_L % shape[-1] != 0


def _seg_rows(shape):
    rows = math.prod(shape[:-1]) if _keeps_rows(shape) else -(-math.prod(shape) // FLAT_L)
    return -(-rows // SEG_ROWS) * SEG_ROWS


def _to_rows(a, dtype):
    rows = _seg_rows(a.shape)
    if _keeps_rows(a.shape):
        r = a.reshape(-1, a.shape[-1]).astype(dtype)
        return jnp.pad(r, ((0, rows - r.shape[0]), (0, FLAT_L - a.shape[-1])))
    flat = jnp.pad(a.reshape(-1).astype(dtype), (0, rows * FLAT_L - a.size))
    return flat.reshape(rows, FLAT_L)


def _from_rows(seg, shape):
    if _keeps_rows(shape):
        return seg[:math.prod(shape[:-1]), :shape[-1]].reshape(shape)
    return seg.reshape(-1)[:math.prod(shape)].reshape(shape)


def _pack(pieces, dtype, row_mult):
    segs, offs, r = [], [], 0
    for a in pieces:
        segs.append(_to_rows(a, dtype))
        offs.append(r)
        r += segs[-1].shape[0]
    tail = -r % row_mult
    if tail:
        segs.append(jnp.zeros((tail, FLAT_L), dtype))
    return jnp.concatenate(segs, axis=0), offs


def _segment(flat, off, shape):
    return _from_rows(flat[off:off + _seg_rows(shape)], shape)


def _gather_unit(shards, axes, dtype):
    flat, offs = _pack(shards, dtype, 64)
    plug = _gather_plug(flat)

    def unpack(full):
        return [jnp.concatenate([_segment(full[j], off, a.shape) for j in range(N_CHIPS)], axis=ax)
                for a, ax, off in zip(shards, axes, offs)]

    return plug, unpack


def _reduce_unit(name, grads, axes):
    pieces = [[] for _ in range(N_CHIPS)]
    for g, ax in zip(grads, axes):
        parts = jnp.split(g, N_CHIPS, axis=ax) if ax is not None else [g] * N_CHIPS
        for jc in range(N_CHIPS):
            pieces[jc].append(parts[jc])
    packed = [_pack(pieces[jc], BF16, 512) for jc in range(N_CHIPS)]
    gsend = jnp.stack([pk[0] for pk in packed])
    own, got = _sibling_split(name + "_split", gsend)
    pair = _add_pairs(name + "_add", own, got, BF16)
    return pair, packed[0][1], [a.shape for a in pieces[0]]


def _f_rms(row0, j, x, g):
    return _rms(x, g)


def _f_mid(row0, j, x, y, g_a, g_b):
    xn = x + _rms(y, g_a)
    return xn, _rms(xn, g_b)


def _f_resid(row0, j, x, y, g):
    return x + _rms(y, g)


def _relu2(u):
    r = jnp.maximum(u, 0.0)
    return r * r


def _relu2_bwd(d_act, act):
    return d_act * (2.0 * jnp.sqrt(act.astype(F32)))


def _f_l2silu(row0, j, c):
    a = _silu(c)
    return a * lax.rsqrt(jnp.sum(a * a, axis=-1, keepdims=True) + EPS)


def _f_silu(row0, j, c):
    return _silu(c)


def _f_scale(row0, j, y, s):
    return y * s


def _f_glu(row0, j, a, gate):
    return a * _sigmoid(gate)


def _f_lnsilu(row0, j, u, g, b):
    mu = jnp.mean(u, axis=-1, keepdims=True)
    uc = u - mu
    return _silu(uc * lax.rsqrt(jnp.mean(uc * uc, axis=-1, keepdims=True) + EPS) * g + b)


def _f_outgate(row0, j, o, z, g):
    return _rms(o, g) * _silu(z)


def _make_gates(H):
    def f(row0, j, ba, alog, dt):
        lane = lax.broadcasted_iota(jnp.int32, ba.shape, 1)
        beta = _sigmoid(ba)
        g = -jnp.exp(alog) * _softplus(ba + dt)
        return jnp.where(lane < H, beta, jnp.where(lane < 2 * H, g, 0.0))
    return f


def _f_loss(row0, j, y, t):
    e = y - t
    loss = 0.5 * jnp.sum(jnp.mean(e * e, axis=-1, keepdims=True), axis=0, keepdims=True)
    return e * (1.0 / y.shape[-1]), jnp.broadcast_to(loss, (1, LANE))


def _lane_row(vec, start):
    return jnp.pad(vec.astype(F32)[None, :], ((0, 0), (start, LANE - start - vec.shape[0])))


def kernel(x, norm_mix_pre, norm_mix_post, norm_mlp_pre, norm_mlp_post, even_w_in, even_conv, even_a_log, even_dt_bias, even_dn_norm, even_pool_w, even_pool_scale, even_w_out, odd_w_in, odd_dw, odd_dw_b, odd_ln_g, odd_ln_b, odd_w_out, mlp_w_up, mlp_w_down, loss_target, m_norm_mix_pre, m_norm_mix_post, m_norm_mlp_pre, m_norm_mlp_post, m_even_w_in, m_even_conv, m_even_a_log, m_even_dt_bias, m_even_dn_norm, m_even_pool_w, m_even_pool_scale, m_even_w_out, m_odd_w_in, m_odd_dw, m_odd_dw_b, m_odd_ln_g, m_odd_ln_b, m_odd_w_out, m_mlp_w_up, m_mlp_w_down, v_norm_mix_pre, v_norm_mix_post, v_norm_mlp_pre, v_norm_mlp_post, v_even_w_in, v_even_conv, v_even_a_log, v_even_dt_bias, v_even_dn_norm, v_even_pool_w, v_even_pool_scale, v_even_w_out, v_odd_w_in, v_odd_dw, v_odd_dw_b, v_odd_ln_g, v_odd_ln_b, v_odd_w_out, v_mlp_w_up, v_mlp_w_down):
    names = ["norm_mix_pre", "norm_mix_post", "norm_mlp_pre", "norm_mlp_post", "even_w_in", "even_conv", "even_a_log",
             "even_dt_bias", "even_dn_norm", "even_pool_w", "even_pool_scale", "even_w_out", "odd_w_in", "odd_dw",
             "odd_dw_b", "odd_ln_g", "odd_ln_b", "odd_w_out", "mlp_w_up", "mlp_w_down"]
    W = dict(zip(names, (norm_mix_pre, norm_mix_post, norm_mlp_pre, norm_mlp_post, even_w_in, even_conv, even_a_log,
                         even_dt_bias, even_dn_norm, even_pool_w, even_pool_scale, even_w_out, odd_w_in, odd_dw,
                         odd_dw_b, odd_ln_g, odd_ln_b, odd_w_out, mlp_w_up, mlp_w_down)))
    Mo = dict(zip(names, (m_norm_mix_pre, m_norm_mix_post, m_norm_mlp_pre, m_norm_mlp_post, m_even_w_in, m_even_conv,
                          m_even_a_log, m_even_dt_bias, m_even_dn_norm, m_even_pool_w, m_even_pool_scale, m_even_w_out,
                          m_odd_w_in, m_odd_dw, m_odd_dw_b, m_odd_ln_g, m_odd_ln_b, m_odd_w_out, m_mlp_w_up,
                          m_mlp_w_down)))
    Vo = dict(zip(names, (v_norm_mix_pre, v_norm_mix_post, v_norm_mlp_pre, v_norm_mlp_post, v_even_w_in, v_even_conv,
                          v_even_a_log, v_even_dt_bias, v_even_dn_norm, v_even_pool_w, v_even_pool_scale, v_even_w_out,
                          v_odd_w_in, v_odd_dw, v_odd_dw_b, v_odd_ln_g, v_odd_ln_b, v_odd_w_out, v_mlp_w_up,
                          v_mlp_w_down)))
    shard_axis = {"even_w_in": 2, "even_conv": 2, "even_pool_w": 2, "even_w_out": 1, "odd_w_in": 2, "odd_dw": 2,
                  "odd_dw_b": 1, "odd_ln_g": 1, "odd_ln_b": 1, "odd_w_out": 1, "mlp_w_up": 2, "mlp_w_down": 1}

    S, D = x.shape[1], x.shape[2]
    depth = norm_mix_pre.shape[0]
    H = even_a_log.shape[1]
    dh = even_dn_norm.shape[1]
    DNW = H * dh
    PW = even_pool_scale.shape[1]
    G = len(POOL_WINDOWS)
    PG = PW // G
    KC = even_conv.shape[1]
    BAW = 2 * LANE
    P_COLS = 4 * DNW + PW + BAW
    x2 = x.reshape(S, D)
    tgt = loss_target.reshape(S, D)

    small = ["even_conv", "odd_dw", "odd_dw_b", "odd_ln_g", "odd_ln_b"]
    plug_s, unpack_s = _gather_unit([W[n] for n in small], [shard_axis[n] for n in small], F32)
    full = dict(zip(small, unpack_s(plug_s.after(_run_plug("gather_small", plug_s)))))
    CW = odd_w_out.shape[1] * N_CHIPS
    wfull = {}

    def mixer_weights(i):
        return [(n, i // 2) for n in (("even_w_in", "even_pool_w", "even_w_out") if i % 2 == 0 else ("odd_w_in", "odd_w_out"))]

    riders = []

    def hosted(fn, *a, **k):
        if not riders:
            return fn(*a, **k)
        plug, done = riders.pop(0)
        outs, landed = fn(*a, comm=[plug], **k)
        done(plug.after(landed))
        return outs

    def queue_gather(keys):
        plug, unpack = _gather_unit([W[n][l] for n, l in keys], [shard_axis[n] - 1 for n, _ in keys], BF16)
        riders.append((plug, lambda full_: wfull.update(zip(keys, unpack(full_)))))

    def weight(key):
        while key not in wfull:
            plug, done = riders.pop(0)
            done(plug.after(_run_plug(f"alone_{len(wfull)}_{len(riders)}", plug)))
        return wfull[key]

    queue_gather(mixer_weights(0)[:1])
    queue_gather(mixer_weights(0)[1:])
    for i in range(depth):
        queue_gather([("mlp_w_up", i)])
        queue_gather([("mlp_w_down", i)])
        if i + 1 < depth:
            queue_gather(mixer_weights(i + 1))

    def even_w_in_layout(w):
        o1 = 4 * DNW
        return jnp.concatenate([w[:, :o1], w[:, o1 + 2 * H:], w[:, o1:o1 + 2 * H],
                                jnp.zeros((w.shape[0], BAW - 2 * H), w.dtype)], axis=1)

    def even_w_in_unlayout(g):
        o1 = 4 * DNW
        return jnp.concatenate([g[:, :o1], g[:, o1 + PW:o1 + PW + 2 * H], g[:, o1:o1 + PW]], axis=1)

    def pool_blockdiag(pw):
        return jnp.concatenate([jnp.pad(pw[gi], ((0, 0), (gi * PG, PW - (gi + 1) * PG))) for gi in range(G)], axis=0)

    pool_taps = max(POOL_WINDOWS)
    tap = jnp.arange(pool_taps)[:, None]
    win_c = jnp.repeat(jnp.asarray(POOL_WINDOWS, F32), PG)[None, :]
    pool_mask = (tap >= pool_taps - win_c).astype(F32)

    grads = {}
    tr_full = 128 if D > 1024 else 256

    saved = []
    xc = x2
    tr_fwd = 256
    (h,) = _rowwise("l0_rms_in", _f_rms, [(xc, 0, D)], [(W["norm_mix_pre"][0:1], None, D)], [(D, BF16)], S=S, tr=tr_fwd)
    for i in range(depth):
        jl = i // 2
        sv = {"x_in": xc}
        g1, g2, g3, g4 = (W[n][i:i + 1] for n in ("norm_mix_pre", "norm_mix_post", "norm_mlp_pre", "norm_mlp_post"))
        sv["h"] = h
        if i % 2 == 0:
            w_in = even_w_in_layout(weight(("even_w_in", jl)))
            p = hosted(_matmul, f"l{i}_w_in", h, w_in, "nn", tn=768)
            conv_w = full["even_conv"][jl]
            c = _dwconv_fwd(f"l{i}_conv", p, 0, conv_w, S=S, C=3 * DNW)
            (qk,) = _rowwise(f"l{i}_qk", _f_l2silu, [(c, 0, dh)], [], [(dh, F32)], S=S, ncb=2 * H, tr=1024)
            (vv,) = _rowwise(f"l{i}_v", _f_silu, [(c, 2 * DNW // dh, dh)], [], [(dh, F32)], S=S, ncb=H, tr=1024)
            alog = _lane_row(W["even_a_log"][jl], H)
            dtb = _lane_row(W["even_dt_bias"][jl], H)
            ba_off = (4 * DNW + PW) // LANE
            (gb,) = _rowwise(f"l{i}_gates", _make_gates(H), [(p, ba_off, LANE)], [(alog, None, LANE), (dtb, None, LANE)],
                             [(LANE, F32)], S=S, tr=1024)
            Cn = min(DN_CHUNK, S)
            gT = gb[:, H:2 * H].reshape(S // Cn, Cn, H).transpose(0, 2, 1)
            o, sp, Tm = hosted(_delta_fwd, f"l{i}_delta", qk, vv, gb, gT, S=S, H=H, dh=dh)
            dn = W["even_dn_norm"][jl][None, :]
            (on,) = _rowwise(f"l{i}_outgate", _f_outgate, [(o, 0, dh), (p, 3 * DNW // dh, dh)], [(dn, None, dh)],
                             [(dh, BF16)], S=S, ncb=H, tr=1024)
            pcb = _pick(PW, 512)
            pooled = _dwconv_fwd(f"l{i}_pool", p, 4 * DNW // pcb, pool_mask, S=S, C=PW, win=win_c, out_dtype=BF16, cb=pcb)
            wbd = pool_blockdiag(weight(("even_pool_w", jl)))
            ypre = _matmul(f"l{i}_pool_w", pooled, wbd, "nn")
            psc = W["even_pool_scale"][jl][None, :]
            (ypool,) = _rowwise(f"l{i}_pool_scale", _f_scale, [(ypre, 0, PW)], [(psc, None, PW)], [(PW, BF16)], S=S)
            mixin = jnp.concatenate([on, ypool], axis=1)
            mix = _matmul(f"l{i}_w_out", mixin, weight(("even_w_out", jl)), "nn")
            sv.update(p=p, c=c, qk=qk, v=vv, gb=gb, gT=gT, o=o, sp=sp, Tm=Tm, pooled=pooled, ypre=ypre, mixin=mixin,
                      w_in=w_in, wbd=wbd, alog=alog, dtb=dtb, dn=dn, psc=psc, conv_w=conv_w)
        else:
            p = hosted(_matmul, f"l{i}_w_in", h, weight(("odd_w_in", jl)), "nn")
            ocb = _pick(CW, 1024)
            (u0,) = _rowwise(f"l{i}_glu", _f_glu, [(p, 0, ocb), (p, CW // ocb, ocb)], [], [(ocb, F32)], S=S,
                             ncb=CW // ocb)
            dw_w, dw_b = full["odd_dw"][jl], full["odd_dw_b"][jl][None, :]
            u1 = hosted(_dwconv_fwd, f"l{i}_dwconv", u0, 0, dw_w, S=S, C=CW, bias=dw_b)
            lg, lb = full["odd_ln_g"][jl][None, :], full["odd_ln_b"][jl][None, :]
            (u2,) = _rowwise(f"l{i}_lnsilu", _f_lnsilu, [(u1, 0, CW)], [(lg, None, CW), (lb, None, CW)], [(CW, BF16)],
                             S=S, tr=tr_full)
            mix = _matmul(f"l{i}_w_out", u2, weight(("odd_w_out", jl)), "nn")
            sv.update(p=p, u0=u0, u1=u1, mixin=u2, dw_w=dw_w, lg=lg, lb=lb)
        x_mid, h2 = _rowwise(f"l{i}_mid", _f_mid, [(xc, 0, D), (mix, 0, D)], [(g2, None, D), (g3, None, D)],
                             [(D, F32), (D, BF16)], S=S, tr=tr_fwd)
        act = hosted(_matmul, f"l{i}_w_up", h2, weight(("mlp_w_up", i)), "nn", epi=_relu2, out_dtypes=[BF16])
        ff = hosted(_matmul, f"l{i}_w_down", act, weight(("mlp_w_down", i)), "nn", tk=TK_LONG)
        if i + 1 < depth:
            x_out, h = _rowwise(f"l{i}_out", _f_mid, [(x_mid, 0, D), (ff, 0, D)],
                                [(g4, None, D), (W["norm_mix_pre"][i + 1:i + 2], None, D)], [(D, F32), (D, BF16)], S=S,
                                tr=tr_fwd)
        else:
            (x_out,) = _rowwise(f"l{i}_out", _f_resid, [(x_mid, 0, D), (ff, 0, D)], [(g4, None, D)], [(D, F32)], S=S,
                                tr=tr_fwd)
        sv.update(mix=mix, x_mid=x_mid, h2=h2, act=act, ff=ff, g=(g1, g2, g3, g4))
        saved.append(sv)
        xc = x_out

    dy, loss_row = _rowwise("loss", _f_loss, [(xc, 0, D), (tgt, 0, D)], [], [(D, F32)], [(1, LANE, False)], S=S,
                            tr=tr_full)
    loss = lax.psum(loss_row[0, 0], ("x", "y", "c"))

    def mixer_params(i):
        ns = (("even_w_in", "even_conv", "even_a_log", "even_dt_bias", "even_dn_norm", "even_pool_w", "even_pool_scale",
               "even_w_out") if i % 2 == 0 else ("odd_w_in", "odd_dw", "odd_dw_b", "odd_ln_g", "odd_ln_b", "odd_w_out"))
        return [(n, i // 2) for n in ns] + [("norm_mix_pre", i)]

    def mlp_params(i):
        return [(n, i) for n in ("mlp_w_up", "mlp_w_down", "norm_mlp_pre", "norm_mlp_post", "norm_mix_post")]

    units = []

    def queue_reduce(name, keys):
        axes = [shard_axis[n] - 1 if n in shard_axis else None for n, _ in keys]
        pair, offs, shapes = _reduce_unit(name, [grads[k] for k in keys], axes)
        parts = []
        units.append((keys, offs, shapes, parts))
        for t, (r0, n) in enumerate(_row_chunks(pair.shape[1], EXCHANGE_PARTS)):
            plug = _exchange_plug(pair, r0, n)
            riders.append((plug, lambda q, t=t: parts.append((t, _sum_chips(f"{name}_sum{t}", q)))))

    dx = dy
    for i in reversed(range(depth)):
        jl = i // 2
        sv = saved[i]
        g1, g2, g3, g4 = sv["g"]
        if i + 1 < depth:
            dx, d_ff, dg4, dg1 = _rowwise_bwd(
                f"l{i}_out_b", _f_mid, [(sv["x_mid"], 0, D), (sv["ff"], 0, D)],
                [(g4, None, D), (saved[i + 1]["g"][0], None, D)], [(dx, 0, D), (dh_, 0, D)], [F32, BF16], S=S, tr=tr_full)
            grads["norm_mix_pre", i + 1] = dg1[0]
        else:
            d_ff, dg4 = _rowwise_bwd(f"l{i}_out_b", _f_rms, [(sv["ff"], 0, D)], [(g4, None, D)], [(dx, 0, D)], [BF16],
                                     S=S, tr=tr_full)
        grads["norm_mlp_post", i] = dg4[0]
        du = hosted(_matmul, f"l{i}_w_down_bx", d_ff, wfull["mlp_w_down", i], "nt", epi=_relu2_bwd, extras=[sv["act"]],
                    out_dtypes=[BF16])
        grads["mlp_w_down", i] = hosted(_matmul, f"l{i}_w_down_bw", sv["act"], d_ff, "tn", out_dtype=BF16, tk=TK_LONG)
        dh2 = hosted(_matmul, f"l{i}_w_up_bx", du, wfull["mlp_w_up", i], "nt", tk=TK_LONG)
        grads["mlp_w_up", i] = hosted(_matmul, f"l{i}_w_up_bw", sv["h2"], du, "tn", out_dtype=BF16, tk=TK_LONG)
        dx, d_mix, dg2, dg3 = _rowwise_bwd(
            f"l{i}_mid_b", _f_mid, [(sv["x_in"], 0, D), (sv["mix"], 0, D)], [(g2, None, D), (g3, None, D)],
            [(dx, 0, D), (dh2, 0, D)], [F32, BF16], S=S, tr=tr_full)
        grads["norm_mix_post", i], grads["norm_mlp_pre", i] = dg2[0], dg3[0]
        queue_reduce(f"red{i}", (mixer_params(i + 1) if i + 1 < depth else []) + mlp_params(i))
        if i % 2 == 0:
            d_mixin = _matmul(f"l{i}_w_out_bx", d_mix, wfull["even_w_out", jl], "nt")
            grads["even_w_out", jl] = _matmul(f"l{i}_w_out_bw", sv["mixin"], d_mix, "tn", out_dtype=BF16)
            p = sv["p"]
            pcb = _pick(PW, 512)
            d_ypre, dpsc = _rowwise_bwd(f"l{i}_pool_scale_b", _f_scale, [(sv["ypre"], 0, PW)], [(sv["psc"], None, PW)],
                                        [(d_mixin, DNW // PW, PW)], [BF16], S=S)
            grads["even_pool_scale", jl] = dpsc[0]
            d_pooled = _matmul(f"l{i}_pool_w_bx", d_ypre, sv["wbd"], "nt")
            dwbd = _matmul(f"l{i}_pool_w_bw", sv["pooled"], d_ypre, "tn")
            grads["even_pool_w", jl] = jnp.stack([dwbd[gi * PG:(gi + 1) * PG, gi * PG:(gi + 1) * PG] for gi in range(G)])
            d_xp = _dwconv_bwd(f"l{i}_pool_b", None, 0, d_pooled, pool_mask, S=S, C=PW, win=win_c, want_dw=False,
                               cb=pcb)[0]
            d_o, d_z, ddn = _rowwise_bwd(f"l{i}_outgate_b", _f_outgate, [(sv["o"], 0, dh), (p, 3 * DNW // dh, dh)],
                                         [(sv["dn"], None, dh)], [(d_mixin, 0, dh)], [F32, F32], S=S, ncb=H, tr=1024)
            grads["even_dn_norm", jl] = ddn[0]
            dqk, dv, dgb = hosted(_delta_bwd, f"l{i}_delta_b", sv["qk"], sv["v"], sv["gb"], sv["gT"], sv["sp"],
                                  sv["Tm"], d_o, S=S, H=H, dh=dh)
            ba_off = (4 * DNW + PW) // LANE
            d_ba, dalog, ddtb = _rowwise_bwd(f"l{i}_gates_b", _make_gates(H), [(p, ba_off, LANE)],
                                             [(sv["alog"], None, LANE), (sv["dtb"], None, LANE)], [(dgb, 0, LANE)],
                                             [F32], S=S, tr=1024)
            grads["even_a_log", jl], grads["even_dt_bias", jl] = dalog[0, H:2 * H], ddtb[0, H:2 * H]
            (dc_qk,) = _rowwise_bwd(f"l{i}_qk_b", _f_l2silu, [(sv["c"], 0, dh)], [], [(dqk, 0, dh)], [F32], S=S,
                                    ncb=2 * H, tr=1024)
            (dc_v,) = _rowwise_bwd(f"l{i}_v_b", _f_silu, [(sv["c"], 2 * DNW // dh, dh)], [], [(dv, 0, dh)], [F32], S=S,
                                   ncb=H, tr=1024)
            dc = jnp.concatenate([dc_qk, dc_v], axis=1)
            d_qkv, dconv, _ = _dwconv_bwd(f"l{i}_conv_b", p, 0, dc, sv["conv_w"], S=S, C=3 * DNW)
            grads["even_conv", jl] = dconv
            dp = jnp.concatenate([d_qkv.astype(BF16), d_z.astype(BF16), d_xp.astype(BF16), d_ba.astype(BF16),
                                  jnp.zeros((S, BAW - LANE), BF16)], axis=1)
            dh_ = hosted(_matmul, f"l{i}_w_in_bx", dp, sv["w_in"], "nt", tk=P_COLS // 2)
            grads["even_w_in", jl] = even_w_in_unlayout(hosted(_matmul, f"l{i}_w_in_bw", sv["h"], dp, "tn", tn=768,
                                                               out_dtype=BF16, tk=TK_LONG))
        else:
            d_u2 = _matmul(f"l{i}_w_out_bx", d_mix, wfull["odd_w_out", jl], "nt")
            grads["odd_w_out", jl] = _matmul(f"l{i}_w_out_bw", sv["mixin"], d_mix, "tn", out_dtype=BF16)
            d_u1, dlg, dlb = _rowwise_bwd(f"l{i}_lnsilu_b", _f_lnsilu, [(sv["u1"], 0, CW)],
                                          [(sv["lg"], None, CW), (sv["lb"], None, CW)], [(d_u2, 0, CW)], [F32], S=S,
                                          tr=tr_full)
            grads["odd_ln_g", jl], grads["odd_ln_b", jl] = dlg[0], dlb[0]
            d_u0, ddw, ddb = hosted(_dwconv_bwd, f"l{i}_dwconv_b", sv["u0"], 0, d_u1, sv["dw_w"], S=S, C=CW)
            grads["odd_dw", jl], grads["odd_dw_b", jl] = ddw, ddb[0]
            p = sv["p"]
            ocb = _pick(CW, 1024)
            da, dgate = _rowwise_bwd(f"l{i}_glu_b", _f_glu, [(p, 0, ocb), (p, CW // ocb, ocb)], [], [(d_u0, 0, ocb)],
                                     [BF16, BF16], S=S, ncb=CW // ocb)
            dp = jnp.concatenate([da, dgate], axis=1)
            dh_ = hosted(_matmul, f"l{i}_w_in_bx", dp, wfull["odd_w_in", jl], "nt", tk=TK_LONG)
            grads["odd_w_in", jl] = hosted(_matmul, f"l{i}_w_in_bw", sv["h"], dp, "tn", out_dtype=BF16, tk=TK_LONG)
    dx, dg1 = _rowwise_bwd("l0_rms_in_b", _f_rms, [(saved[0]["x_in"], 0, D)], [(saved[0]["g"][0], None, D)],
                           [(dh_, 0, D)], [F32], adds=[(dx, 0, D)], S=S, tr=tr_full)
    grads["norm_mix_pre", 0] = dg1[0]
    grad_x = dx.reshape(x.shape)
    queue_reduce("red_last", mixer_params(0))
    while riders:
        plug, done = riders.pop(0)
        done(plug.after(_run_plug(f"alone_last_{len(riders)}", plug)))

    halves = [jnp.concatenate([q for _, q in sorted(parts, key=lambda tq: tq[0])], axis=0) for *_, parts in units]
    theirs = _sibling_swap("grad_sibling_swap", jnp.concatenate(halves, axis=0))
    south = lax.axis_index("c") == 0
    gshard, r = {}, 0
    for (keys, offs, shapes, _), half in zip(units, halves):
        rh = half.shape[0]
        other = theirs[r:r + rh]
        whole = jnp.concatenate([jnp.where(south, half, other), jnp.where(south, other, half)], axis=0)
        for k, off, shp in zip(keys, offs, shapes):
            gshard[k] = _segment(whole, off, shp)
        r += rh

    outs_g, outs_d, outs_m, outs_v = [], [], [], []
    for n in names:
        g_n = jnp.stack([gshard[n, l] for l in range(W[n].shape[0])])
        g_o, d_o, m_o, v_o = _adamw(f"adamw_{n}", W[n], Mo[n], Vo[n], g_n)
        outs_g.append(g_o)
        outs_d.append(d_o)
        outs_m.append(m_o)
        outs_v.append(v_o)
    return (loss, grad_x, *outs_g, *outs_d, *outs_m, *outs_v)
```
